```python
import math
import jax, jax.numpy as jnp
from jax import lax
import numpy as np

D_MODEL = 1024
BATCH = 2
SEQ = 8192
DEPTH = 1

D_RNN = D_MODEL // 2
N_RNN_BLOCKS = 8
RNN_BLOCK = D_RNN // N_RNN_BLOCKS
CONV_WIDTH = 4
CONV_LEFT = 2
LRU_C = 8.0
N_ATTN_HEADS = 8
HEAD_DIM = 64
D_ATTN = N_ATTN_HEADS * HEAD_DIM
DILATED_PATTERNS = ((128, 1), (512, 4), (2048, 16))
Q_BLOCK = 128
N_BUCKETS = 32
MAX_DISTANCE = 1024
D_MIX = D_RNN + D_ATTN
D_IN = 2 * D_RNN + 3 * D_ATTN
D_FF = 4 * D_MODEL
EPS = 1e-6
NEG_INF = -1e30

kernel_name = "hybrid_rglru_dilated_attn_block"


def rms_norm(x, g):
    xf = x.astype(jnp.float32)
    y = xf * lax.rsqrt(jnp.mean(xf * xf, axis=-1, keepdims=True) + EPS)
    return (y * g.astype(jnp.float32)).astype(x.dtype)


def t5_bucket(rel):
    nb = N_BUCKETS // 2
    max_exact = nb // 2
    ret = jnp.where(rel > 0, nb, 0)
    n = jnp.abs(rel)
    nf = jnp.maximum(n, 1).astype(jnp.float32)
    large = max_exact + (jnp.log(nf / max_exact) / math.log(MAX_DISTANCE / max_exact)
                         * (nb - max_exact)).astype(jnp.int32)
    large = jnp.minimum(large, nb - 1)
    return ret + jnp.where(n < max_exact, n, large)


def centred_depthwise_conv(x, w, b):
    S = x.shape[1]
    xp = jnp.pad(x, ((0, 0), (CONV_LEFT, CONV_WIDTH - 1 - CONV_LEFT), (0, 0)))
    y = b
    for k in range(CONV_WIDTH):
        y = y + xp[:, k:k + S] * w[k]
    return y


def rg_lru(x, wa, ba, wx, bx, lam, reverse):
    B, S, _ = x.shape
    xb = x.reshape(B, S, N_RNN_BLOCKS, RNN_BLOCK)
    r = jax.nn.sigmoid((jnp.einsum('bsnc,ncd->bsnd', xb, wa).reshape(B, S, D_RNN) + ba).astype(jnp.float32))
    i = jax.nn.sigmoid((jnp.einsum('bsnc,ncd->bsnd', xb, wx).reshape(B, S, D_RNN) + bx).astype(jnp.float32))
    log_a = -LRU_C * jax.nn.softplus(-lam.astype(jnp.float32)) * r
    a = jnp.exp(log_a)
    b_in = jnp.sqrt(-jnp.expm1(2.0 * log_a)) * (i * x.astype(jnp.float32))

    def combine(c1, c2):
        a1, b1 = c1
        a2, b2 = c2
        return a1 * a2, a2 * b1 + b2

    _, h = lax.associative_scan(combine, (a, b_in), reverse=reverse, axis=1)
    return h


def dilated_attention(q, k, v, rel_bias):
    B, S, H, Dh = q.shape
    nb = S // Q_BLOCK
    scale = Dh ** -0.5
    pats = []
    for window, dil in DILATED_PATTERNS:
        half = window // (2 * dil)
        offs = jnp.arange(-half, half + 1, dtype=jnp.int32) * dil
        bias = rel_bias[t5_bucket(offs)].astype(jnp.float32).T
        pats.append((offs, bias))
    q_blocks = q.reshape(B, nb, Q_BLOCK, H, Dh).transpose(1, 0, 2, 3, 4)

    def one_block(args):
        qb, n = args
        pos = n * Q_BLOCK + jnp.arange(Q_BLOCK, dtype=jnp.int32)
        outs, lses = [], []
        for offs, bias in pats:
            kpos = pos[:, None] + offs[None, :]
            valid = (kpos >= 0) & (kpos < S)
            kidx = jnp.clip(kpos, 0, S - 1)
            kg = k[:, kidx]
            vg = v[:, kidx]
            logits = jnp.einsum('bqhd,bqjhd->bhqj', qb, kg).astype(jnp.float32) * scale
            logits = logits + bias[None, :, None, :]
            logits = jnp.where(valid[None, None], logits, NEG_INF)
            lse = jax.nn.logsumexp(logits, axis=-1)
            p = jnp.exp(logits - lse[..., None])
            outs.append(jnp.einsum('bhqj,bqjhd->bqhd', p.astype(v.dtype), vg).astype(jnp.float32))
            lses.append(lse)
        w = jax.nn.softmax(jnp.stack(lses, axis=0), axis=0)
        w = jnp.transpose(w, (0, 1, 3, 2))[..., None]
        o = jnp.sum(w * jnp.stack(outs, axis=0), axis=0)
        return o.astype(q.dtype)

    out = lax.map(one_block, (q_blocks, jnp.arange(nb, dtype=jnp.int32)))
    return out.transpose(1, 0, 2, 3, 4).reshape(B, S, H * Dh)


def setup_inputs(seed: int = 0) -> dict:
    key = jax.random.key(seed)
    ks = jax.random.split(key, 24)
    f32 = jnp.float32

    def nrm(k, shape, s):
        return jax.random.normal(k, shape, f32) * s

    def gain(k, shape):
        return 1.0 + 0.02 * jax.random.normal(k, shape, f32)

    def lam(k):
        u = jax.random.uniform(k, (DEPTH, D_RNN), f32, minval=0.9, maxval=0.999)
        s = u ** (1.0 / LRU_C)
        return jnp.log(s) - jnp.log1p(-s)

    return {
        "x": jax.random.normal(ks[0], (BATCH, SEQ, D_MODEL), f32),
        "attn_norm_g": gain(ks[1], (DEPTH, D_MODEL)),
        "w_in": nrm(ks[2], (DEPTH, D_MODEL, D_IN), D_MODEL ** -0.5),
        "conv_w": nrm(ks[3], (DEPTH, CONV_WIDTH, D_RNN), CONV_WIDTH ** -0.5),
        "conv_b": nrm(ks[4], (DEPTH, D_RNN), 0.01),
        "lru_wa_fwd": nrm(ks[5], (DEPTH, N_RNN_BLOCKS, RNN_BLOCK, RNN_BLOCK), RNN_BLOCK ** -0.5),
        "lru_ba_fwd": nrm(ks[6], (DEPTH, D_RNN), 0.01),
        "lru_wx_fwd": nrm(ks[7], (DEPTH, N_RNN_BLOCKS, RNN_BLOCK, RNN_BLOCK), RNN_BLOCK ** -0.5),
        "lru_bx_fwd": nrm(ks[8], (DEPTH, D_RNN), 0.01),
        "lru_lam_fwd": lam(ks[9]),
        "lru_wa_bwd": nrm(ks[10], (DEPTH, N_RNN_BLOCKS, RNN_BLOCK, RNN_BLOCK), RNN_BLOCK ** -0.5),
        "lru_ba_bwd": nrm(ks[11], (DEPTH, D_RNN), 0.01),
        "lru_wx_bwd": nrm(ks[12], (DEPTH, N_RNN_BLOCKS, RNN_BLOCK, RNN_BLOCK), RNN_BLOCK ** -0.5),
        "lru_bx_bwd": nrm(ks[13], (DEPTH, D_RNN), 0.01),
        "lru_lam_bwd": lam(ks[14]),
        "rel_bias": nrm(ks[15], (N_BUCKETS, N_ATTN_HEADS), 0.5),
        "norm_rnn_g": gain(ks[16], (DEPTH, D_RNN)),
        "norm_attn_g": gain(ks[17], (DEPTH, D_ATTN)),
        "w_out": nrm(ks[18], (DEPTH, D_MIX, D_MODEL), D_MIX ** -0.5),
        "mlp_norm_g": gain(ks[19], (DEPTH, D_MODEL)),
        "w_up": nrm(ks[20], (DEPTH, D_MODEL, D_FF), D_MODEL ** -0.5),
        "w_down": nrm(ks[21], (DEPTH, D_FF, D_MODEL), D_FF ** -0.5),
        "final_norm_g": gain(ks[22], (D_MODEL,)),
    }


def reference(x, attn_norm_g, w_in, conv_w, conv_b,
              lru_wa_fwd, lru_ba_fwd, lru_wx_fwd, lru_bx_fwd, lru_lam_fwd,
              lru_wa_bwd, lru_ba_bwd, lru_wx_bwd, lru_bx_bwd, lru_lam_bwd,
              rel_bias, norm_rnn_g, norm_attn_g, w_out,
              mlp_norm_g, w_up, w_down, final_norm_g):
    B, S, _ = x.shape
    for l in range(DEPTH):
        h = rms_norm(x, attn_norm_g[l])
        proj = h @ w_in[l]
        xr, gate, q, k, v = jnp.split(
            proj, [D_RNN, 2 * D_RNN, 2 * D_RNN + D_ATTN, 2 * D_RNN + 2 * D_ATTN], axis=-1)
        xr = centred_depthwise_conv(xr, conv_w[l], conv_b[l])
        h_f = rg_lru(xr, lru_wa_fwd[l], lru_ba_fwd[l], lru_wx_fwd[l], lru_bx_fwd[l], lru_lam_fwd[l], False)
        h_b = rg_lru(xr, lru_wa_bwd[l], lru_ba_bwd[l], lru_wx_bwd[l], lru_bx_bwd[l], lru_lam_bwd[l], True)
        y_rnn = (h_f + h_b).astype(x.dtype) * jax.nn.gelu(gate)
        qh = q.reshape(B, S, N_ATTN_HEADS, HEAD_DIM)
        kh = k.reshape(B, S, N_ATTN_HEADS, HEAD_DIM)
        vh = v.reshape(B, S, N_ATTN_HEADS, HEAD_DIM)
        y_attn = dilated_attention(qh, kh, vh, rel_bias)
        mix = jnp.concatenate([rms_norm(y_rnn, norm_rnn_g[l]), rms_norm(y_attn, norm_attn_g[l])], axis=-1)
        x = x + mix @ w_out[l]
        h = rms_norm(x, mlp_norm_g[l])
        x = x + jnp.square(jax.nn.relu(h @ w_up[l])) @ w_down[l]
    return rms_norm(x, final_norm_g)
```

```python
import functools
import math

import numpy as np
import jax
import jax.numpy as jnp
from jax import lax
from jax.experimental import pallas as pl
from jax.experimental.pallas import tpu as pltpu

F32 = jnp.float32
BF16 = jnp.bfloat16

D_MODEL = 1024
D_RNN = 512
N_RNN_BLOCKS = 8
RNN_BLOCK = D_RNN // N_RNN_BLOCKS
CONV_WIDTH = 4
CONV_LEFT = 2
LRU_C = 8.0
N_HEADS = 8
HEAD_DIM = 64
D_ATTN = N_HEADS * HEAD_DIM
PATTERNS = ((128, 1), (512, 4), (2048, 16))
HALF_STEPS = 64
N_BUCKETS = 32
MAX_DISTANCE = 1024
D_IN = 2 * D_RNN + 3 * D_ATTN
D_FF = 4 * D_MODEL
EPS = 1e-6
NEG_INF = -1e30

LANES = 128
SUBLANES = 8
MXU_DIM = 256

Q_TILE = 128
K_TILE = 64
K_WIN = Q_TILE + 2 * HALF_STEPS
GATE_BLK = MXU_DIM

VMEM_LIMIT = 48 * 1024 * 1024


def _rms(x, g):
    ms = jnp.mean(x * x, axis=-1, keepdims=True)
    return x * lax.rsqrt(ms + EPS) * g


def _t5_bucket_np(rel):
    nb = N_BUCKETS // 2
    max_exact = nb // 2
    ret = np.where(rel > 0, nb, 0)
    n = np.abs(rel)
    nf = np.maximum(n, 1).astype(np.float32)
    large = max_exact + (np.log(nf / np.float32(max_exact)) / np.float32(math.log(MAX_DISTANCE / max_exact))
                         * np.float32(nb - max_exact)).astype(np.int32)
    large = np.minimum(large, nb - 1)
    return ret + np.where(n < max_exact, n, large)


def _bucket_index_tables():
    qi = np.arange(Q_TILE)[:, None]
    ci = np.arange(K_WIN)[None, :]
    step = ci - qi - HALF_STEPS
    inside = np.abs(step) <= HALF_STEPS
    tabs = []
    for _, dil in PATTERNS:
        b = _t5_bucket_np((step * dil).astype(np.int32))
        tabs.append(np.where(inside, b, -1).astype(np.int32))
    return np.stack(tabs, axis=0)


def _bias_table_kernel(idx_ref, rb_ref, out_ref):
    h = pl.program_id(1)
    idx = idx_ref[0]
    acc = jnp.full(idx.shape, NEG_INF, F32)
    for b in range(N_BUCKETS):
        acc = jnp.where(idx == b, rb_ref[b, h], acc)
    out_ref[0, 0] = acc


def _bias_tables(rel_bias):
    idx = jnp.asarray(_bucket_index_tables())
    n_pat = len(PATTERNS)
    return pl.pallas_call(
        _bias_table_kernel,
        grid=(n_pat, N_HEADS),
        in_specs=[
            pl.BlockSpec((1, Q_TILE, K_WIN), lambda g, h: (g, 0, 0)),
            pl.BlockSpec(memory_space=pltpu.SMEM),
        ],
        out_specs=pl.BlockSpec((1, 1, Q_TILE, K_WIN), lambda g, h: (g, h, 0, 0)),
        out_shape=jax.ShapeDtypeStruct((n_pat, N_HEADS, Q_TILE, K_WIN), F32),
        name="bias_table",
    )(idx, rel_bias)


def _in_proj_kernel(x_ref, g_ref, w_ref, xr_ref, gate_ref, q_ref, k_ref, v_ref):
    h = _rms(x_ref[...], g_ref[...]).astype(BF16)

    def seg(lo, width):
        return jnp.dot(h, w_ref[:, lo:lo + width], preferred_element_type=F32)

    xr_ref[...] = seg(0, D_RNN)
    gate_ref[...] = seg(D_RNN, D_RNN)
    q_ref[...] = (seg(2 * D_RNN, D_ATTN) * (HEAD_DIM ** -0.5)).astype(BF16)
    k_ref[...] = seg(2 * D_RNN + D_ATTN, D_ATTN).astype(BF16)
    v_ref[...] = seg(2 * D_RNN + 2 * D_ATTN, D_ATTN).astype(BF16)


def _in_proj(x2, g, w_in_bf, tm):
    n = x2.shape[0]
    row = lambda i: (i, 0)
    const = lambda i: (0, 0)
    return pl.pallas_call(
        _in_proj_kernel,
        grid=(n // tm,),
        in_specs=[
            pl.BlockSpec((tm, D_MODEL), row),
            pl.BlockSpec((1, D_MODEL), const),
            pl.BlockSpec((D_MODEL, D_IN), const),
        ],
        out_specs=[
            pl.BlockSpec((tm, D_RNN), row),
            pl.BlockSpec((tm, D_RNN), row),
            pl.BlockSpec((tm, D_ATTN), row),
            pl.BlockSpec((tm, D_ATTN), row),
            pl.BlockSpec((tm, D_ATTN), row),
        ],
        out_shape=[
            jax.ShapeDtypeStruct((n, D_RNN), F32),
            jax.ShapeDtypeStruct((n, D_RNN), F32),
            jax.ShapeDtypeStruct((n, D_ATTN), BF16),
            jax.ShapeDtypeStruct((n, D_ATTN), BF16),
            jax.ShapeDtypeStruct((n, D_ATTN), BF16),
        ],
        compiler_params=pltpu.CompilerParams(
            dimension_semantics=("arbitrary",), vmem_limit_bytes=VMEM_LIMIT),
        name="in_proj",
    )(x2, g, w_in_bf)


SCAN_ROWS = 128


def _scan_block(a, b, reverse):
    n = a.shape[0]
    row = lax.broadcasted_iota(jnp.int32, a.shape, 0)
    s = 1
    while s < n:
        if reverse:
            ra = pltpu.roll(a, n - s, 0)
            rb = pltpu.roll(b, n - s, 0)
            m = row < n - s
        else:
            ra = pltpu.roll(a, s, 0)
            rb = pltpu.roll(b, s, 0)
            m = row >= s
        b = jnp.where(m, a * rb + b, b)
        a = jnp.where(m, a * ra, a)
        s *= 2
    return a, b


def _rnn_kernel(*refs, reverse, final, tc, nchunks):
    if final:
        (prev_ref, cur_ref, next_ref, hf_ref, gate_ref, cw_ref, cb_ref, wa_ref, wx_ref,
         ba_ref, bx_ref, lam_ref, gn_ref, out_ref, xext, a_scr, b_scr, h_scr, carry) = refs
    else:
        (prev_ref, cur_ref, next_ref, cw_ref, cb_ref, wa_ref, wx_ref,
         ba_ref, bx_ref, lam_ref, out_ref, xext, a_scr, b_scr, h_scr, carry) = refs

    step = pl.program_id(1)
    chunk = (nchunks - 1 - step) if reverse else step

    @pl.when(step == 0)
    def _():
        carry[...] = jnp.zeros_like(carry)

    xext[0:SUBLANES, :] = jnp.where(chunk > 0, prev_ref[0], 0.0)
    xext[SUBLANES:SUBLANES + tc, :] = cur_ref[0]
    xext[SUBLANES + tc:2 * SUBLANES + tc, :] = jnp.where(chunk < nchunks - 1, next_ref[0], 0.0)

    xc = cb_ref[...]
    for k in range(CONV_WIDTH):
        xc = xc + xext[pl.ds(SUBLANES - CONV_LEFT + k, tc), :] * cw_ref[k:k + 1, :]
    xcb = xc.astype(BF16)

    za, zx = [], []
    for j in range(D_RNN // GATE_BLK):
        xj = xcb[:, j * GATE_BLK:(j + 1) * GATE_BLK]
        za.append(jnp.dot(xj, wa_ref[j], preferred_element_type=F32))
        zx.append(jnp.dot(xj, wx_ref[j], preferred_element_type=F32))
    za = jnp.concatenate(za, axis=-1) + ba_ref[...]
    zx = jnp.concatenate(zx, axis=-1) + bx_ref[...]
    r = 1.0 / (1.0 + jnp.exp(-za))
    i = 1.0 / (1.0 + jnp.exp(-zx))
    nlam = -lam_ref[...]
    softplus = jnp.maximum(nlam, 0.0) + jnp.log1p(jnp.exp(-jnp.abs(nlam)))
    log_a = (-LRU_C * softplus) * r
    a = jnp.exp(log_a)
    a_scr[...] = a
    b_scr[...] = jnp.sqrt(-jnp.tanh(log_a) * (1.0 + a * a)) * (i * xc)

    nblk = tc // SCAN_ROWS
    order = range(nblk - 1, -1, -1) if reverse else range(nblk)
    last = 0 if reverse else SCAN_ROWS - 1
    for j in range(D_RNN // LANES):
        lanes = slice(j * LANES, (j + 1) * LANES)
        c = carry[0:1, lanes]
        for rb in order:
            rows = slice(rb * SCAN_ROWS, (rb + 1) * SCAN_ROWS)
            ap, h0 = _scan_block(a_scr[rows, lanes], b_scr[rows, lanes], reverse)
            h = h0 + ap * c
            h_scr[rows, lanes] = h
            c = h[last:last + 1, :]
        carry[0:1, lanes] = c

    if final:
        g = gate_ref[0]
        gelu = g * (0.5 * (1.0 + jnp.tanh(math.sqrt(2.0 / math.pi) * (g + 0.044715 * (g * g * g)))))
        y = (hf_ref[0] + h_scr[...]) * gelu
        out_ref[0] = _rms(y, gn_ref[...]).astype(BF16)
    else:
        out_ref[0] = h_scr[...]


def _rnn_pass(xr3, extra, params, reverse, final, tc):
    bsz, s, _ = xr3.shape
    nchunks = s // tc
    hb = tc // SUBLANES
    nhalo = s // SUBLANES

    def cidx(c):
        return (nchunks - 1 - c) if reverse else c

    cur_map = lambda b, c: (b, cidx(c), 0)
    prev_map = lambda b, c: (b, jnp.maximum(cidx(c) * hb - 1, 0), 0)
    next_map = lambda b, c: (b, jnp.minimum((cidx(c) + 1) * hb, nhalo - 1), 0)
    const2 = lambda b, c: (0, 0)
    const3 = lambda b, c: (0, 0, 0)

    cw, cb, wa, wx, ba, bx, lam, gn = params
    in_specs = [
        pl.BlockSpec((1, SUBLANES, D_RNN), prev_map),
        pl.BlockSpec((1, tc, D_RNN), cur_map),
        pl.BlockSpec((1, SUBLANES, D_RNN), next_map),
    ]
    args = [xr3, xr3, xr3]
    if final:
        in_specs += [pl.BlockSpec((1, tc, D_RNN), cur_map), pl.BlockSpec((1, tc, D_RNN), cur_map)]
        args += list(extra)
    in_specs += [
        pl.BlockSpec((CONV_WIDTH, D_RNN), const2),
        pl.BlockSpec((1, D_RNN), const2),
        pl.BlockSpec((D_RNN // GATE_BLK, GATE_BLK, GATE_BLK), const3),
        pl.BlockSpec((D_RNN // GATE_BLK, GATE_BLK, GATE_BLK), const3),
        pl.BlockSpec((1, D_RNN), const2),
        pl.BlockSpec((1, D_RNN), const2),
        pl.BlockSpec((1, D_RNN), const2),
    ]
    args += [cw, cb, wa, wx, ba, bx, lam]
    if final:
        in_specs.append(pl.BlockSpec((1, D_RNN), const2))
        args.append(gn)
    out_dtype = BF16 if final else F32
    return pl.pallas_call(
        functools.partial(_rnn_kernel, reverse=reverse, final=final, tc=tc, nchunks=nchunks),
        grid=(bsz, nchunks),
        in_specs=in_specs,
        out_specs=pl.BlockSpec((1, tc, D_RNN), cur_map),
        out_shape=jax.ShapeDtypeStruct((bsz, s, D_RNN), out_dtype),
        scratch_shapes=[
            pltpu.VMEM((tc + 2 * SUBLANES, D_RNN), F32),
            pltpu.VMEM((tc, D_RNN), F32),
            pltpu.VMEM((tc, D_RNN), F32),
            pltpu.VMEM((tc, D_RNN), F32),
            pltpu.VMEM((SUBLANES, D_RNN), F32),
        ],
        compiler_params=pltpu.CompilerParams(
            dimension_semantics=("arbitrary", "arbitrary"), vmem_limit_bytes=VMEM_LIMIT),
        name="rnn_bwd" if reverse else "rnn_fwd",
    )(*args)


def _gate_weights(w):
    per = GATE_BLK // RNN_BLOCK
    out = jnp.zeros((D_RNN // GATE_BLK, GATE_BLK, GATE_BLK), w.dtype)
    for n in range(N_RNN_BLOCKS):
        j, o = divmod(n, per)
        out = out.at[j, o * RNN_BLOCK:(o + 1) * RNN_BLOCK, o * RNN_BLOCK:(o + 1) * RNN_BLOCK].set(w[n])
    return out.astype(BF16)


def _attn_kernel(q_ref, k0, k1, k2, k3, v0, v1, v2, v3, bias_ref, o_ref, lse_ref, *, sub_len):
    i = pl.program_id(2)
    q = q_ref[0]
    kk = jnp.concatenate([k0[0], k1[0], k2[0], k3[0]], axis=0)
    vv = jnp.concatenate([v0[0], v1[0], v2[0], v3[0]], axis=0)
    col = lax.broadcasted_iota(jnp.int32, (Q_TILE, K_WIN), 1)
    kidx = col + (i * Q_TILE - HALF_STEPS)
    valid = (kidx >= 0) & (kidx < sub_len)
    lane = lax.broadcasted_iota(jnp.int32, (Q_TILE, LANES), 1)
    heads_per_slab = LANES // HEAD_DIM
    for p in range(D_ATTN // LANES):
        sl = slice(p * LANES, (p + 1) * LANES)
        qp, kp, vp = q[:, sl], kk[:, sl], vv[:, sl]
        o_slab = jnp.zeros((Q_TILE, LANES), F32)
        lse_slab = jnp.zeros((Q_TILE, LANES), F32)
        for hh in range(heads_per_slab):
            head = p * heads_per_slab + hh
            mine = (lane >= hh * HEAD_DIM) & (lane < (hh + 1) * HEAD_DIM)
            qm = jnp.where(mine, qp, jnp.zeros_like(qp))
            s = lax.dot_general(qm, kp, (((1,), (1,)), ((), ())), preferred_element_type=F32)
            s = jnp.where(valid, s + bias_ref[0, head], NEG_INF)
            m = jnp.max(s, axis=-1, keepdims=True)
            e = jnp.exp(s - m)
            l = jnp.sum(e, axis=-1, keepdims=True)
            o = jnp.dot(e.astype(BF16), vp, preferred_element_type=F32) / l
            o_slab = jnp.where(mine, o, o_slab)
            lse_slab = jnp.where(mine, m + jnp.log(l), lse_slab)
        o_ref[0, :, sl] = o_slab
        lse_ref[0, :, sl] = lse_slab


def _attn_pattern(q3, k3, v3, bias, pat, dil):
    bsz, s, _ = q3.shape
    sub_len = s // dil
    nq = sub_len // Q_TILE
    nk = sub_len // K_TILE
    qv = q3.reshape(bsz, sub_len, dil * D_ATTN)
    kv = k3.reshape(bsz, sub_len, dil * D_ATTN)
    vv = v3.reshape(bsz, sub_len, dil * D_ATTN)
    ratio = Q_TILE // K_TILE

    def kmap(t):
        return lambda b, r, i: (b, jnp.clip(i * ratio - HALF_STEPS // K_TILE + t, 0, nk - 1), r)

    qmap = lambda b, r, i: (b, i, r)
    n_kblk = K_WIN // K_TILE
    in_specs = [pl.BlockSpec((1, Q_TILE, D_ATTN), qmap)]
    in_specs += [pl.BlockSpec((1, K_TILE, D_ATTN), kmap(t)) for t in range(n_kblk)]
    in_specs += [pl.BlockSpec((1, K_TILE, D_ATTN), kmap(t)) for t in range(n_kblk)]
    in_specs.append(pl.BlockSpec((1, N_HEADS, Q_TILE, K_WIN), lambda b, r, i: (pat, 0, 0, 0)))
    o, lse = pl.pallas_call(
        functools.partial(_attn_kernel, sub_len=sub_len),
        grid=(bsz, dil, nq),
        in_specs=in_specs,
        out_specs=[pl.BlockSpec((1, Q_TILE, D_ATTN), qmap), pl.BlockSpec((1, Q_TILE, D_ATTN), qmap)],
        out_shape=[jax.ShapeDtypeStruct((bsz, sub_len, dil * D_ATTN), F32)] * 2,
        compiler_params=pltpu.CompilerParams(
            dimension_semantics=("arbitrary",) * 3, vmem_limit_bytes=VMEM_LIMIT),
        name=f"attn_d{dil}",
    )(qv, *([kv] * n_kblk), *([vv] * n_kblk), bias)
    return o.reshape(bsz * s, D_ATTN), lse.reshape(bsz * s, D_ATTN)


FF_CHUNK = 1024


def _out_mlp_kernel(x_ref, mr_ref, o1, o2, o3, l1, l2, l3, ga_ref, wo_ref, gm_ref, wu_ref, wd_ref,
                    gf_ref, out_ref):
    a1, a2, a3 = l1[...], l2[...], l3[...]
    mx = jnp.maximum(jnp.maximum(a1, a2), a3)
    e1, e2, e3 = jnp.exp(a1 - mx), jnp.exp(a2 - mx), jnp.exp(a3 - mx)
    y_attn = (e1 * o1[...] + e2 * o2[...] + e3 * o3[...]) / (e1 + e2 + e3)
    mix_attn = _rms(y_attn, ga_ref[...]).astype(BF16)
    proj = jnp.dot(mr_ref[...], wo_ref[0:D_RNN, :], preferred_element_type=F32)
    proj = proj + jnp.dot(mix_attn, wo_ref[D_RNN:D_RNN + D_ATTN, :], preferred_element_type=F32)
    x1 = proj + x_ref[...]
    h = _rms(x1, gm_ref[...]).astype(BF16)
    ff = None
    for c in range(D_FF // FF_CHUNK):
        u = jnp.dot(h, wu_ref[:, c * FF_CHUNK:(c + 1) * FF_CHUNK], preferred_element_type=F32)
        u = jnp.square(jnp.maximum(u, 0.0)).astype(BF16)
        d = jnp.dot(u, wd_ref[c * FF_CHUNK:(c + 1) * FF_CHUNK, :], preferred_element_type=F32)
        ff = d if ff is None else ff + d
    out_ref[...] = _rms(ff + x1, gf_ref[...])


def _out_mlp(x2, mix_rnn, os_, lses, ga, wo, gm, wu, wd, gf, tm):
    n = x2.shape[0]
    row = lambda i: (i, 0)
    const = lambda i: (0, 0)
    once = pl.Buffered(1)
    in_specs = [pl.BlockSpec((tm, D_MODEL), row), pl.BlockSpec((tm, D_RNN), row)]
    in_specs += [pl.BlockSpec((tm, D_ATTN), row)] * 6
    in_specs += [
        pl.BlockSpec((1, D_ATTN), const),
        pl.BlockSpec((D_RNN + D_ATTN, D_MODEL), const, pipeline_mode=once),
        pl.BlockSpec((1, D_MODEL), const),
        pl.BlockSpec((D_MODEL, D_FF), const, pipeline_mode=once),
        pl.BlockSpec((D_FF, D_MODEL), const, pipeline_mode=once),
        pl.BlockSpec((1, D_MODEL), const),
    ]
    return pl.pallas_call(
        _out_mlp_kernel,
        grid=(n // tm,),
        in_specs=in_specs,
        out_specs=pl.BlockSpec((tm, D_MODEL), row),
        out_shape=jax.ShapeDtypeStruct((n, D_MODEL), F32),
        compiler_params=pltpu.CompilerParams(
            dimension_semantics=("arbitrary",), vmem_limit_bytes=VMEM_LIMIT),
        name="out_mlp",
    )(x2, mix_rnn, *os_, *lses, ga, wo, gm, wu, wd, gf)


def kernel(x, attn_norm_g, w_in, conv_w, conv_b, lru_wa_fwd, lru_ba_fwd, lru_wx_fwd, lru_bx_fwd, lru_lam_fwd, lru_wa_bwd, lru_ba_bwd, lru_wx_bwd, lru_bx_bwd, lru_lam_bwd, rel_bias, norm_rnn_g, norm_attn_g, w_out, mlp_norm_g, w_up, w_down, final_norm_g):
    bsz, s, _ = x.shape
    n = bsz * s
    depth = w_in.shape[0]
    bias = _bias_tables(rel_bias)
    x2 = x.reshape(n, D_MODEL)
    row = lambda v: v.reshape(1, -1)
    for l in range(depth):
        xr, gate, q, k, v = _in_proj(x2, row(attn_norm_g[l]), w_in[l].astype(BF16), tm=512)
        xr3 = xr.reshape(bsz, s, D_RNN)
        gate3 = gate.reshape(bsz, s, D_RNN)
        fwd = (conv_w[l], row(conv_b[l]), _gate_weights(lru_wa_fwd[l]), _gate_weights(lru_wx_fwd[l]),
               row(lru_ba_fwd[l]), row(lru_bx_fwd[l]), row(lru_lam_fwd[l]), None)
        bwd = (conv_w[l], row(conv_b[l]), _gate_weights(lru_wa_bwd[l]), _gate_weights(lru_wx_bwd[l]),
               row(lru_ba_bwd[l]), row(lru_bx_bwd[l]), row(lru_lam_bwd[l]), row(norm_rnn_g[l]))
        h_f = _rnn_pass(xr3, (), fwd, reverse=False, final=False, tc=512)
        mix_rnn = _rnn_pass(xr3, (h_f, gate3), bwd, reverse=True, final=True, tc=512)
        q3 = q.reshape(bsz, s, D_ATTN)
        k3 = k.reshape(bsz, s, D_ATTN)
        v3 = v.reshape(bsz, s, D_ATTN)
        os_, lses = [], []
        for pat, (_, dil) in enumerate(PATTERNS):
            o, lse = _attn_pattern(q3, k3, v3, bias, pat, dil)
            os_.append(o)
            lses.append(lse)
        assert depth == 1, "the final RMSNorm is fused into the single layer's last call"
        x2 = _out_mlp(x2, mix_rnn.reshape(n, D_RNN), os_, lses, row(norm_attn_g[l]),
                      w_out[l].astype(BF16), row(mlp_norm_g[l]), w_up[l].astype(BF16),
                      w_down[l].astype(BF16), row(final_norm_g), tm=256)
    return x2.reshape(bsz, s, D_MODEL)
```

```python
import functools
import math

import numpy as np
import jax
import jax.numpy as jnp
from jax import lax
from jax.experimental import pallas as pl
from jax.experimental.pallas import tpu as pltpu

F32 = jnp.float32
BF16 = jnp.bfloat16

D_MODEL = 1024
D_RNN = 512
N_RNN_BLOCKS = 8
RNN_BLOCK = D_RNN // N_RNN_BLOCKS
CONV_WIDTH = 4
CONV_LEFT = 2
LRU_C = 8.0
N_HEADS = 8
HEAD_DIM = 64
D_ATTN = N_HEADS * HEAD_DIM
PATTERNS = ((128, 1), (512, 4), (2048, 16))
HALF_STEPS = 64
N_BUCKETS = 32
MAX_DISTANCE = 1024
D_IN = 2 * D_RNN + 3 * D_ATTN
D_FF = 4 * D_MODEL
EPS = 1e-6
NEG_INF = -1e30

LANES = 128
SUBLANES = 8
MXU_DIM = 256

Q_TILE = 128
K_TILE = 64
K_WIN = Q_TILE + 2 * HALF_STEPS
GATE_BLK = MXU_DIM

VMEM_LIMIT = 48 * 1024 * 1024


def _rms(x, g):
    ms = jnp.mean(x * x, axis=-1, keepdims=True)
    return x * lax.rsqrt(ms + EPS) * g


def _t5_bucket_np(rel):
    nb = N_BUCKETS // 2
    max_exact = nb // 2
    ret = np.where(rel > 0, nb, 0)
    n = np.abs(rel)
    nf = np.maximum(n, 1).astype(np.float32)
    large = max_exact + (np.log(nf / np.float32(max_exact)) / np.float32(math.log(MAX_DISTANCE / max_exact))
                         * np.float32(nb - max_exact)).astype(np.int32)
    large = np.minimum(large, nb - 1)
    return ret + np.where(n < max_exact, n, large)


def _bucket_index_tables():
    qi = np.arange(Q_TILE)[:, None]
    ci = np.arange(K_WIN)[None, :]
    step = ci - qi - HALF_STEPS
    inside = np.abs(step) <= HALF_STEPS
    tabs = []
    for _, dil in PATTERNS:
        b = _t5_bucket_np((step * dil).astype(np.int32))
        tabs.append(np.where(inside, b, -1).astype(np.int32))
    return np.stack(tabs, axis=0)


def _bias_table_kernel(idx_ref, rb_ref, out_ref):
    h = pl.program_id(1)
    idx = idx_ref[0]
    acc = jnp.full(idx.shape, NEG_INF, F32)
    for b in range(N_BUCKETS):
        acc = jnp.where(idx == b, rb_ref[b, h], acc)
    out_ref[0, 0] = acc


def _bias_tables(rel_bias):
    idx = jnp.asarray(_bucket_index_tables())
    n_pat = len(PATTERNS)
    return pl.pallas_call(
        _bias_table_kernel,
        grid=(n_pat, N_HEADS),
        in_specs=[
            pl.BlockSpec((1, Q_TILE, K_WIN), lambda g, h: (g, 0, 0)),
            pl.BlockSpec(memory_space=pltpu.SMEM),
        ],
        out_specs=pl.BlockSpec((1, 1, Q_TILE, K_WIN), lambda g, h: (g, h, 0, 0)),
        out_shape=jax.ShapeDtypeStruct((n_pat, N_HEADS, Q_TILE, K_WIN), F32),
        name="bias_table",
    )(idx, rel_bias)


def _in_proj_kernel(x_ref, g_ref, w_ref, xr_ref, gate_ref, *rest, tm):
    n_pat = len(PATTERNS)
    outs = [rest[a * n_pat:(a + 1) * n_pat] for a in range(3)]
    slab = rest[3 * n_pat]
    h = _rms(x_ref[0], g_ref[...]).astype(BF16)

    def seg(lo, width):
        return jnp.dot(h, w_ref[:, lo:lo + width], preferred_element_type=F32)

    xr_ref[0] = seg(0, D_RNN)
    gate_ref[0] = seg(D_RNN, D_RNN)
    n_slab = D_ATTN // LANES
    for a, scale in enumerate((HEAD_DIM ** -0.5, 1.0, 1.0)):
        p = seg(2 * D_RNN + a * D_ATTN, D_ATTN)
        if scale != 1.0:
            p = p * scale
        for (_, dil), o_ref in zip(PATTERNS, outs[a]):
            if dil == 1:
                o_ref[0, 0] = p.astype(BF16)
        for j in range(n_slab):
            slab[j] = p[:, j * LANES:(j + 1) * LANES]
        for (_, dil), o_ref in zip(PATTERNS, outs[a]):
            if dil == 1:
                continue
            for r in range(dil):
                for j in range(n_slab):
                    o_ref[0, r, :, j * LANES:(j + 1) * LANES] = (
                        slab[j, pl.ds(r, tm // dil, stride=dil), :].astype(BF16))


def _in_proj(x3, g, w_in_bf, tm):
    bsz, s, _ = x3.shape
    row = lambda b, i: (b, i, 0)
    cls = lambda b, i: (b, 0, i, 0)
    const = lambda b, i: (0, 0)
    qkv_specs, qkv_shapes = [], []
    for _ in range(3):
        for _, dil in PATTERNS:
            qkv_specs.append(pl.BlockSpec((1, dil, tm // dil, D_ATTN), cls))
            qkv_shapes.append(jax.ShapeDtypeStruct((bsz, dil, s // dil, D_ATTN), BF16))
    return pl.pallas_call(
        functools.partial(_in_proj_kernel, tm=tm),
        grid=(bsz, s // tm),
        in_specs=[
            pl.BlockSpec((1, tm, D_MODEL), row),
            pl.BlockSpec((1, D_MODEL), const),
            pl.BlockSpec((D_MODEL, D_IN), const),
        ],
        out_specs=[pl.BlockSpec((1, tm, D_RNN), row), pl.BlockSpec((1, tm, D_RNN), row)] + qkv_specs,
        out_shape=[jax.ShapeDtypeStruct((bsz, s, D_RNN), F32)] * 2 + qkv_shapes,
        scratch_shapes=[pltpu.VMEM((D_ATTN // LANES, tm, LANES), F32)],
        compiler_params=pltpu.CompilerParams(
            dimension_semantics=("arbitrary", "arbitrary"), vmem_limit_bytes=VMEM_LIMIT),
        name="in_proj",
    )(x3, g, w_in_bf)


SCAN_ROWS = 128


def _scan_block(a, b, reverse):
    n = a.shape[0]
    row = lax.broadcasted_iota(jnp.int32, a.shape, 0)
    s = 1
    while s < n:
        if reverse:
            ra = pltpu.roll(a, n - s, 0)
            rb = pltpu.roll(b, n - s, 0)
            m = row < n - s
        else:
            ra = pltpu.roll(a, s, 0)
            rb = pltpu.roll(b, s, 0)
            m = row >= s
        b = jnp.where(m, a * rb + b, b)
        a = jnp.where(m, a * ra, a)
        s *= 2
    return a, b


def _rnn_kernel(*refs, reverse, final, tc, nchunks):
    if final:
        (prev_ref, cur_ref, next_ref, hf_ref, gate_ref, cw_ref, cb_ref, wa_ref, wx_ref,
         ba_ref, bx_ref, lam_ref, gn_ref, out_ref, xext, a_scr, b_scr, h_scr, carry) = refs
    else:
        (prev_ref, cur_ref, next_ref, cw_ref, cb_ref, wa_ref, wx_ref,
         ba_ref, bx_ref, lam_ref, out_ref, xext, a_scr, b_scr, h_scr, carry) = refs

    step = pl.program_id(1)
    chunk = (nchunks - 1 - step) if reverse else step

    @pl.when(step == 0)
    def _():
        carry[...] = jnp.zeros_like(carry)

    xext[0:SUBLANES, :] = jnp.where(chunk > 0, prev_ref[0], 0.0)
    xext[SUBLANES:SUBLANES + tc, :] = cur_ref[0]
    xext[SUBLANES + tc:2 * SUBLANES + tc, :] = jnp.where(chunk < nchunks - 1, next_ref[0], 0.0)

    xc = cb_ref[...]
    for k in range(CONV_WIDTH):
        xc = xc + xext[pl.ds(SUBLANES - CONV_LEFT + k, tc), :] * cw_ref[k:k + 1, :]
    xcb = xc.astype(BF16)

    za, zx = [], []
    for j in range(D_RNN // GATE_BLK):
        xj = xcb[:, j * GATE_BLK:(j + 1) * GATE_BLK]
        za.append(jnp.dot(xj, wa_ref[j], preferred_element_type=F32))
        zx.append(jnp.dot(xj, wx_ref[j], preferred_element_type=F32))
    za = jnp.concatenate(za, axis=-1) + ba_ref[...]
    zx = jnp.concatenate(zx, axis=-1) + bx_ref[...]
    r = 1.0 / (1.0 + jnp.exp(-za))
    i = 1.0 / (1.0 + jnp.exp(-zx))
    nlam = -lam_ref[...]
    softplus = jnp.maximum(nlam, 0.0) + jnp.log1p(jnp.exp(-jnp.abs(nlam)))
    log_a = (-LRU_C * softplus) * r
    a = jnp.exp(log_a)
    a_scr[...] = a
    b_scr[...] = jnp.sqrt(-jnp.tanh(log_a) * (1.0 + a * a)) * (i * xc)

    nblk = tc // SCAN_ROWS
    order = range(nblk - 1, -1, -1) if reverse else range(nblk)
    last = 0 if reverse else SCAN_ROWS - 1
    for j in range(D_RNN // LANES):
        lanes = slice(j * LANES, (j + 1) * LANES)
        c = carry[0:1, lanes]
        for rb in order:
            rows = slice(rb * SCAN_ROWS, (rb + 1) * SCAN_ROWS)
            ap, h0 = _scan_block(a_scr[rows, lanes], b_scr[rows, lanes], reverse)
            h = h0 + ap * c
            h_scr[rows, lanes] = h
            c = h[last:last + 1, :]
        carry[0:1, lanes] = c

    if final:
        g = gate_ref[0]
        gelu = g * (0.5 * (1.0 + jnp.tanh(math.sqrt(2.0 / math.pi) * (g + 0.044715 * (g * g * g)))))
        y = (hf_ref[0] + h_scr[...]) * gelu
        out_ref[0] = _rms(y, gn_ref[...]).astype(BF16)
    else:
        out_ref[0] = h_scr[...]


def _rnn_pass(xr3, extra, params, reverse, final, tc):
    bsz, s, _ = xr3.shape
    nchunks = s // tc
    hb = tc // SUBLANES
    nhalo = s // SUBLANES

    def cidx(c):
        return (nchunks - 1 - c) if reverse else c

    cur_map = lambda b, c: (b, cidx(c), 0)
    prev_map = lambda b, c: (b, jnp.maximum(cidx(c) * hb - 1, 0), 0)
    next_map = lambda b, c: (b, jnp.minimum((cidx(c) + 1) * hb, nhalo - 1), 0)
    const2 = lambda b, c: (0, 0)
    const3 = lambda b, c: (0, 0, 0)

    cw, cb, wa, wx, ba, bx, lam, gn = params
    in_specs = [
        pl.BlockSpec((1, SUBLANES, D_RNN), prev_map),
        pl.BlockSpec((1, tc, D_RNN), cur_map),
        pl.BlockSpec((1, SUBLANES, D_RNN), next_map),
    ]
    args = [xr3, xr3, xr3]
    if final:
        in_specs += [pl.BlockSpec((1, tc, D_RNN), cur_map), pl.BlockSpec((1, tc, D_RNN), cur_map)]
        args += list(extra)
    in_specs += [
        pl.BlockSpec((CONV_WIDTH, D_RNN), const2),
        pl.BlockSpec((1, D_RNN), const2),
        pl.BlockSpec((D_RNN // GATE_BLK, GATE_BLK, GATE_BLK), const3),
        pl.BlockSpec((D_RNN // GATE_BLK, GATE_BLK, GATE_BLK), const3),
        pl.BlockSpec((1, D_RNN), const2),
        pl.BlockSpec((1, D_RNN), const2),
        pl.BlockSpec((1, D_RNN), const2),
    ]
    args += [cw, cb, wa, wx, ba, bx, lam]
    if final:
        in_specs.append(pl.BlockSpec((1, D_RNN), const2))
        args.append(gn)
    out_dtype = BF16 if final else F32
    return pl.pallas_call(
        functools.partial(_rnn_kernel, reverse=reverse, final=final, tc=tc, nchunks=nchunks),
        grid=(bsz, nchunks),
        in_specs=in_specs,
        out_specs=pl.BlockSpec((1, tc, D_RNN), cur_map),
        out_shape=jax.ShapeDtypeStruct((bsz, s, D_RNN), out_dtype),
        scratch_shapes=[
            pltpu.VMEM((tc + 2 * SUBLANES, D_RNN), F32),
            pltpu.VMEM((tc, D_RNN), F32),
            pltpu.VMEM((tc, D_RNN), F32),
            pltpu.VMEM((tc, D_RNN), F32),
            pltpu.VMEM((SUBLANES, D_RNN), F32),
        ],
        compiler_params=pltpu.CompilerParams(
            dimension_semantics=("arbitrary", "arbitrary"), vmem_limit_bytes=VMEM_LIMIT),
        name="rnn_bwd" if reverse else "rnn_fwd",
    )(*args)


def _gate_weights(w):
    per = GATE_BLK // RNN_BLOCK
    out = jnp.zeros((D_RNN // GATE_BLK, GATE_BLK, GATE_BLK), w.dtype)
    for n in range(N_RNN_BLOCKS):
        j, o = divmod(n, per)
        out = out.at[j, o * RNN_BLOCK:(o + 1) * RNN_BLOCK, o * RNN_BLOCK:(o + 1) * RNN_BLOCK].set(w[n])
    return out.astype(BF16)


def _attn_kernel(q_ref, k0, k1, k2, k3, v0, v1, v2, v3, bias_ref, o_ref, lse_ref, *, sub_len):
    i = pl.program_id(2)
    q = q_ref[0]
    kk = jnp.concatenate([k0[0], k1[0], k2[0], k3[0]], axis=0)
    vv = jnp.concatenate([v0[0], v1[0], v2[0], v3[0]], axis=0)
    col = lax.broadcasted_iota(jnp.int32, (Q_TILE, K_WIN), 1)
    kidx = col + (i * Q_TILE - HALF_STEPS)
    valid = (kidx >= 0) & (kidx < sub_len)
    lane = lax.broadcasted_iota(jnp.int32, (Q_TILE, LANES), 1)
    heads_per_slab = LANES // HEAD_DIM
    for p in range(D_ATTN // LANES):
        sl = slice(p * LANES, (p + 1) * LANES)
        qp, kp, vp = q[:, sl], kk[:, sl], vv[:, sl]
        o_slab = jnp.zeros((Q_TILE, LANES), F32)
        lse_slab = jnp.zeros((Q_TILE, LANES), F32)
        for hh in range(heads_per_slab):
            head = p * heads_per_slab + hh
            mine = (lane >= hh * HEAD_DIM) & (lane < (hh + 1) * HEAD_DIM)
            qm = jnp.where(mine, qp, jnp.zeros_like(qp))
            s = lax.dot_general(qm, kp, (((1,), (1,)), ((), ())), preferred_element_type=F32)
            s = jnp.where(valid, s + bias_ref[0, head], NEG_INF)
            m = jnp.max(s, axis=-1, keepdims=True)
            e = jnp.exp(s - m)
            l = jnp.sum(e, axis=-1, keepdims=True)
            o = jnp.dot(e.astype(BF16), vp, preferred_element_type=F32) / l
            o_slab = jnp.where(mine, o, o_slab)
            lse_slab = jnp.where(mine, m + jnp.log(l), lse_slab)
        o_ref[0, :, sl] = o_slab
        lse_ref[0, :, sl] = lse_slab


def _attn_pattern(qc, kc, vc, bias, pat):
    bsz, dil, sub_len, _ = qc.shape
    nq = sub_len // Q_TILE
    nk = sub_len // K_TILE
    ratio = Q_TILE // K_TILE

    def kmap(t):
        return lambda b, r, i: (b, r, jnp.clip(i * ratio - HALF_STEPS // K_TILE + t, 0, nk - 1), 0)

    qmap = lambda b, r, i: (b, r, i, 0)
    n_kblk = K_WIN // K_TILE
    in_specs = [pl.BlockSpec((1, None, Q_TILE, D_ATTN), qmap)]
    in_specs += [pl.BlockSpec((1, None, K_TILE, D_ATTN), kmap(t)) for t in range(n_kblk)]
    in_specs += [pl.BlockSpec((1, None, K_TILE, D_ATTN), kmap(t)) for t in range(n_kblk)]
    in_specs.append(pl.BlockSpec((1, N_HEADS, Q_TILE, K_WIN), lambda b, r, i: (pat, 0, 0, 0)))
    return pl.pallas_call(
        functools.partial(_attn_kernel, sub_len=sub_len),
        grid=(bsz, dil, nq),
        in_specs=in_specs,
        out_specs=[pl.BlockSpec((1, None, Q_TILE, D_ATTN), qmap)] * 2,
        out_shape=[jax.ShapeDtypeStruct((bsz, dil, sub_len, D_ATTN), F32)] * 2,
        compiler_params=pltpu.CompilerParams(
            dimension_semantics=("arbitrary",) * 3, vmem_limit_bytes=VMEM_LIMIT),
        name=f"attn_d{dil}",
    )(qc, *([kc] * n_kblk), *([vc] * n_kblk), bias)


FF_CHUNK = 1024


def _natural_order(ref, scr, tm):
    dil = ref.shape[1]
    if dil == 1:
        return ref[0, 0]
    n_slab = D_ATTN // LANES
    for r in range(dil):
        for j in range(n_slab):
            scr[j, pl.ds(r, tm // dil, stride=dil), :] = ref[0, r, :, j * LANES:(j + 1) * LANES]
    return jnp.concatenate([scr[j] for j in range(n_slab)], axis=-1)


def _out_mlp_kernel(x_ref, mr_ref, o1, o2, o3, l1, l2, l3, ga_ref, wo_ref, gm_ref, wu_ref, wd_ref,
                    gf_ref, out_ref, s_o2, s_o3, s_l2, s_l3, *, tm):
    a1 = _natural_order(l1, None, tm)
    a2 = _natural_order(l2, s_l2, tm)
    a3 = _natural_order(l3, s_l3, tm)
    mx = jnp.maximum(jnp.maximum(a1, a2), a3)
    e1, e2, e3 = jnp.exp(a1 - mx), jnp.exp(a2 - mx), jnp.exp(a3 - mx)
    y_attn = (e1 * _natural_order(o1, None, tm) + e2 * _natural_order(o2, s_o2, tm)
              + e3 * _natural_order(o3, s_o3, tm)) / (e1 + e2 + e3)
    mix_attn = _rms(y_attn, ga_ref[...]).astype(BF16)
    proj = jnp.dot(mr_ref[0], wo_ref[0:D_RNN, :], preferred_element_type=F32)
    proj = proj + jnp.dot(mix_attn, wo_ref[D_RNN:D_RNN + D_ATTN, :], preferred_element_type=F32)
    x1 = proj + x_ref[0]
    h = _rms(x1, gm_ref[...]).astype(BF16)
    ff = None
    for c in range(D_FF // FF_CHUNK):
        u = jnp.dot(h, wu_ref[:, c * FF_CHUNK:(c + 1) * FF_CHUNK], preferred_element_type=F32)
        u = jnp.square(jnp.maximum(u, 0.0)).astype(BF16)
        d = jnp.dot(u, wd_ref[c * FF_CHUNK:(c + 1) * FF_CHUNK, :], preferred_element_type=F32)
        ff = d if ff is None else ff + d
    out_ref[0] = _rms(ff + x1, gf_ref[...])


def _out_mlp(x3, mix_rnn, os_, lses, ga, wo, gm, wu, wd, gf, tm):
    bsz, s, _ = x3.shape
    row = lambda b, i: (b, i, 0)
    cls = lambda b, i: (b, 0, i, 0)
    const = lambda b, i: (0, 0)
    once = pl.Buffered(1)
    cls_specs = [pl.BlockSpec((1, dil, tm // dil, D_ATTN), cls) for _, dil in PATTERNS]
    in_specs = [pl.BlockSpec((1, tm, D_MODEL), row), pl.BlockSpec((1, tm, D_RNN), row)]
    in_specs += cls_specs + cls_specs
    in_specs += [
        pl.BlockSpec((1, D_ATTN), const),
        pl.BlockSpec((D_RNN + D_ATTN, D_MODEL), const, pipeline_mode=once),
        pl.BlockSpec((1, D_MODEL), const),
        pl.BlockSpec((D_MODEL, D_FF), const, pipeline_mode=once),
        pl.BlockSpec((D_FF, D_MODEL), const, pipeline_mode=once),
        pl.BlockSpec((1, D_MODEL), const),
    ]
    return pl.pallas_call(
        functools.partial(_out_mlp_kernel, tm=tm),
        grid=(bsz, s // tm),
        in_specs=in_specs,
        out_specs=pl.BlockSpec((1, tm, D_MODEL), row),
        out_shape=jax.ShapeDtypeStruct((bsz, s, D_MODEL), F32),
        scratch_shapes=[pltpu.VMEM((D_ATTN // LANES, tm, LANES), F32)] * 4,
        compiler_params=pltpu.CompilerParams(
            dimension_semantics=("arbitrary", "arbitrary"), vmem_limit_bytes=VMEM_LIMIT),
        name="out_mlp",
    )(x3, mix_rnn, *os_, *lses, ga, wo, gm, wu, wd, gf)


def kernel(x, attn_norm_g, w_in, conv_w, conv_b, lru_wa_fwd, lru_ba_fwd, lru_wx_fwd, lru_bx_fwd, lru_lam_fwd, lru_wa_bwd, lru_ba_bwd, lru_wx_bwd, lru_bx_bwd, lru_lam_bwd, rel_bias, norm_rnn_g, norm_attn_g, w_out, mlp_norm_g, w_up, w_down, final_norm_g):
    depth = w_in.shape[0]
    assert depth == 1, "the final RMSNorm is fused into the single layer's last call"
    l = 0
    n_pat = len(PATTERNS)
    row = lambda v: v.reshape(1, -1)
    bias = _bias_tables(rel_bias)
    proj = _in_proj(x, row(attn_norm_g[l]), w_in[l].astype(BF16), tm=512)
    xr, gate = proj[0], proj[1]
    qs, ks, vs = (proj[2 + a * n_pat:2 + (a + 1) * n_pat] for a in range(3))
    fwd = (conv_w[l], row(conv_b[l]), _gate_weights(lru_wa_fwd[l]), _gate_weights(lru_wx_fwd[l]),
           row(lru_ba_fwd[l]), row(lru_bx_fwd[l]), row(lru_lam_fwd[l]), None)
    bwd = (conv_w[l], row(conv_b[l]), _gate_weights(lru_wa_bwd[l]), _gate_weights(lru_wx_bwd[l]),
           row(lru_ba_bwd[l]), row(lru_bx_bwd[l]), row(lru_lam_bwd[l]), row(norm_rnn_g[l]))
    h_f = _rnn_pass(xr, (), fwd, reverse=False, final=False, tc=512)
    mix_rnn = _rnn_pass(xr, (h_f, gate), bwd, reverse=True, final=True, tc=512)
    os_, lses = [], []
    for pat in range(n_pat):
        o, lse = _attn_pattern(qs[pat], ks[pat], vs[pat], bias, pat)
        os_.append(o)
        lses.append(lse)
    return _out_mlp(x, mix_rnn, os_, lses, row(norm_attn_g[l]), w_out[l].astype(BF16),
                    row(mlp_norm_g[l]), w_up[l].astype(BF16), w_down[l].astype(BF16),
                    row(final_norm_g), tm=256)
```

```python
import functools
import math

import numpy as np
import jax
import jax.numpy as jnp
from jax import lax
from jax.experimental import pallas as pl
from jax.experimental.pallas import tpu as pltpu

F32 = jnp.float32
BF16 = jnp.bfloat16

D_MODEL = 1024
D_RNN = 512
N_RNN_BLOCKS = 8
RNN_BLOCK = D_RNN // N_RNN_BLOCKS
CONV_WIDTH = 4
CONV_LEFT = 2
LRU_C = 8.0
N_HEADS = 8
HEAD_DIM = 64
D_ATTN = N_HEADS * HEAD_DIM
PATTERNS = ((128, 1), (512, 4), (2048, 16))
HALF_STEPS = 64
N_BUCKETS = 32
MAX_DISTANCE = 1024
D_IN = 2 * D_RNN + 3 * D_ATTN
D_FF = 4 * D_MODEL
EPS = 1e-6
NEG_INF = -1e30

LANES = 128
SUBLANES = 8
MXU_DIM = 256

Q_TILE = 128
K_TILE = 64
K_WIN = Q_TILE + 2 * HALF_STEPS
TILES_PER_STEP = 4
LOG2E = math.log2(math.e)
INTERIOR, FIRST, LAST = 0, 1, 2
GATE_BLK = MXU_DIM

VMEM_LIMIT = 48 * 1024 * 1024
ATTN_VMEM_LIMIT = 56 * 1024 * 1024


def _rms(x, g):
    ms = jnp.mean(x * x, axis=-1, keepdims=True)
    return x * lax.rsqrt(ms + EPS) * g


def _t5_bucket_np(rel):
    nb = N_BUCKETS // 2
    max_exact = nb // 2
    ret = np.where(rel > 0, nb, 0)
    n = np.abs(rel)
    nf = np.maximum(n, 1).astype(np.float32)
    large = max_exact + (np.log(nf / np.float32(max_exact)) / np.float32(math.log(MAX_DISTANCE / max_exact))
                         * np.float32(nb - max_exact)).astype(np.int32)
    large = np.minimum(large, nb - 1)
    return ret + np.where(n < max_exact, n, large)


def _bucket_index_tables():
    qi = np.arange(Q_TILE)[:, None]
    ci = np.arange(K_WIN)[None, :]
    window_lead = {INTERIOR: HALF_STEPS, FIRST: 0, LAST: 2 * HALF_STEPS}
    tabs = []
    for _, dil in PATTERNS:
        variants = []
        for v in (INTERIOR, FIRST, LAST):
            step = ci - qi - window_lead[v]
            b = _t5_bucket_np((step * dil).astype(np.int32))
            variants.append(np.where(np.abs(step) <= HALF_STEPS, b, -1))
        tabs.append(np.stack(variants))
    return np.stack(tabs, axis=0).astype(np.int32)


def _bias_table_kernel(idx_ref, rb_ref, out_ref):
    h = pl.program_id(1)
    idx = idx_ref[0]
    acc = jnp.full(idx.shape, NEG_INF, F32)
    for b in range(N_BUCKETS):
        acc = jnp.where(idx == b, rb_ref[b, h] * LOG2E, acc)
    out_ref[0, 0] = acc


def _bias_tables(rel_bias):
    idx = jnp.asarray(_bucket_index_tables()).reshape(-1, Q_TILE, K_WIN)
    n_tab = idx.shape[0]
    tabs = pl.pallas_call(
        _bias_table_kernel,
        grid=(n_tab, N_HEADS),
        in_specs=[
            pl.BlockSpec((1, Q_TILE, K_WIN), lambda g, h: (g, 0, 0)),
            pl.BlockSpec(memory_space=pltpu.SMEM),
        ],
        out_specs=pl.BlockSpec((1, 1, Q_TILE, K_WIN), lambda g, h: (g, h, 0, 0)),
        out_shape=jax.ShapeDtypeStruct((n_tab, N_HEADS, Q_TILE, K_WIN), F32),
        name="bias_table",
    )(idx, rel_bias)
    return tabs.reshape(len(PATTERNS), 3, N_HEADS, Q_TILE, K_WIN)


def _in_proj_kernel(x_ref, g_ref, w_ref, xr_ref, gate_ref, *rest, tm):
    n_pat = len(PATTERNS)
    outs = [rest[a * n_pat:(a + 1) * n_pat] for a in range(3)]
    slab = rest[3 * n_pat]
    h = _rms(x_ref[0], g_ref[...]).astype(BF16)

    def seg(lo, width):
        return jnp.dot(h, w_ref[:, lo:lo + width], preferred_element_type=F32)

    xr_ref[0] = seg(0, D_RNN)
    gate_ref[0] = seg(D_RNN, D_RNN)
    n_slab = D_ATTN // LANES
    for a, scale in enumerate((HEAD_DIM ** -0.5 * LOG2E, 1.0, 1.0)):
        p = seg(2 * D_RNN + a * D_ATTN, D_ATTN)
        if scale != 1.0:
            p = p * scale
        for (_, dil), o_ref in zip(PATTERNS, outs[a]):
            if dil == 1:
                o_ref[0, 0] = p.astype(BF16)
        for j in range(n_slab):
            slab[j] = p[:, j * LANES:(j + 1) * LANES]
        for (_, dil), o_ref in zip(PATTERNS, outs[a]):
            if dil == 1:
                continue
            for r in range(dil):
                for j in range(n_slab):
                    o_ref[0, r, :, j * LANES:(j + 1) * LANES] = (
                        slab[j, pl.ds(r, tm // dil, stride=dil), :].astype(BF16))


def _in_proj(x3, g, w_in_bf, tm):
    bsz, s, _ = x3.shape
    row = lambda b, i: (b, i, 0)
    cls = lambda b, i: (b, 0, i, 0)
    const = lambda b, i: (0, 0)
    qkv_specs, qkv_shapes = [], []
    for _ in range(3):
        for _, dil in PATTERNS:
            qkv_specs.append(pl.BlockSpec((1, dil, tm // dil, D_ATTN), cls))
            qkv_shapes.append(jax.ShapeDtypeStruct((bsz, dil, s // dil, D_ATTN), BF16))
    return pl.pallas_call(
        functools.partial(_in_proj_kernel, tm=tm),
        grid=(bsz, s // tm),
        in_specs=[
            pl.BlockSpec((1, tm, D_MODEL), row),
            pl.BlockSpec((1, D_MODEL), const),
            pl.BlockSpec((D_MODEL, D_IN), const),
        ],
        out_specs=[pl.BlockSpec((1, tm, D_RNN), row), pl.BlockSpec((1, tm, D_RNN), row)] + qkv_specs,
        out_shape=[jax.ShapeDtypeStruct((bsz, s, D_RNN), F32)] * 2 + qkv_shapes,
        scratch_shapes=[pltpu.VMEM((D_ATTN // LANES, tm, LANES), F32)],
        compiler_params=pltpu.CompilerParams(
            dimension_semantics=("arbitrary", "arbitrary"), vmem_limit_bytes=VMEM_LIMIT),
        name="in_proj",
    )(x3, g, w_in_bf)


SCAN_ROWS = 128


def _scan_block(a, b, reverse):
    n = a.shape[0]
    row = lax.broadcasted_iota(jnp.int32, a.shape, 0)
    s = 1
    while s < n:
        if reverse:
            ra = pltpu.roll(a, n - s, 0)
            rb = pltpu.roll(b, n - s, 0)
            m = row < n - s
        else:
            ra = pltpu.roll(a, s, 0)
            rb = pltpu.roll(b, s, 0)
            m = row >= s
        b = jnp.where(m, a * rb + b, b)
        a = jnp.where(m, a * ra, a)
        s *= 2
    return a, b


def _rnn_kernel(*refs, reverse, final, tc, nchunks):
    if final:
        (prev_ref, cur_ref, next_ref, hf_ref, gate_ref, cw_ref, cb_ref, wa_ref, wx_ref,
         ba_ref, bx_ref, lam_ref, gn_ref, out_ref, xext, a_scr, b_scr, h_scr, carry) = refs
    else:
        (prev_ref, cur_ref, next_ref, cw_ref, cb_ref, wa_ref, wx_ref,
         ba_ref, bx_ref, lam_ref, out_ref, xext, a_scr, b_scr, h_scr, carry) = refs

    step = pl.program_id(1)
    chunk = (nchunks - 1 - step) if reverse else step

    @pl.when(step == 0)
    def _():
        carry[...] = jnp.zeros_like(carry)

    xext[0:SUBLANES, :] = jnp.where(chunk > 0, prev_ref[0], 0.0)
    xext[SUBLANES:SUBLANES + tc, :] = cur_ref[0]
    xext[SUBLANES + tc:2 * SUBLANES + tc, :] = jnp.where(chunk < nchunks - 1, next_ref[0], 0.0)

    xc = cb_ref[...]
    for k in range(CONV_WIDTH):
        xc = xc + xext[pl.ds(SUBLANES - CONV_LEFT + k, tc), :] * cw_ref[k:k + 1, :]
    xcb = xc.astype(BF16)

    za, zx = [], []
    for j in range(D_RNN // GATE_BLK):
        xj = xcb[:, j * GATE_BLK:(j + 1) * GATE_BLK]
        za.append(jnp.dot(xj, wa_ref[j], preferred_element_type=F32))
        zx.append(jnp.dot(xj, wx_ref[j], preferred_element_type=F32))
    za = jnp.concatenate(za, axis=-1) + ba_ref[...]
    zx = jnp.concatenate(zx, axis=-1) + bx_ref[...]
    r = 1.0 / (1.0 + jnp.exp(-za))
    i = 1.0 / (1.0 + jnp.exp(-zx))
    nlam = -lam_ref[...]
    softplus = jnp.maximum(nlam, 0.0) + jnp.log1p(jnp.exp(-jnp.abs(nlam)))
    log_a = (-LRU_C * softplus) * r
    a = jnp.exp(log_a)
    a_scr[...] = a
    b_scr[...] = jnp.sqrt(-jnp.tanh(log_a) * (1.0 + a * a)) * (i * xc)

    nblk = tc // SCAN_ROWS
    order = range(nblk - 1, -1, -1) if reverse else range(nblk)
    last = 0 if reverse else SCAN_ROWS - 1
    for j in range(D_RNN // LANES):
        lanes = slice(j * LANES, (j + 1) * LANES)
        c = carry[0:1, lanes]
        for rb in order:
            rows = slice(rb * SCAN_ROWS, (rb + 1) * SCAN_ROWS)
            ap, h0 = _scan_block(a_scr[rows, lanes], b_scr[rows, lanes], reverse)
            h = h0 + ap * c
            h_scr[rows, lanes] = h
            c = h[last:last + 1, :]
        carry[0:1, lanes] = c

    if final:
        g = gate_ref[0]
        gelu = g * (0.5 * (1.0 + jnp.tanh(math.sqrt(2.0 / math.pi) * (g + 0.044715 * (g * g * g)))))
        y = (hf_ref[0] + h_scr[...]) * gelu
        out_ref[0] = _rms(y, gn_ref[...]).astype(BF16)
    else:
        out_ref[0] = h_scr[...]


def _rnn_pass(xr3, extra, params, reverse, final, tc):
    bsz, s, _ = xr3.shape
    nchunks = s // tc
    hb = tc // SUBLANES
    nhalo = s // SUBLANES

    def cidx(c):
        return (nchunks - 1 - c) if reverse else c

    cur_map = lambda b, c: (b, cidx(c), 0)
    prev_map = lambda b, c: (b, jnp.maximum(cidx(c) * hb - 1, 0), 0)
    next_map = lambda b, c: (b, jnp.minimum((cidx(c) + 1) * hb, nhalo - 1), 0)
    const2 = lambda b, c: (0, 0)
    const3 = lambda b, c: (0, 0, 0)

    cw, cb, wa, wx, ba, bx, lam, gn = params
    in_specs = [
        pl.BlockSpec((1, SUBLANES, D_RNN), prev_map),
        pl.BlockSpec((1, tc, D_RNN), cur_map),
        pl.BlockSpec((1, SUBLANES, D_RNN), next_map),
    ]
    args = [xr3, xr3, xr3]
    if final:
        in_specs += [pl.BlockSpec((1, tc, D_RNN), cur_map), pl.BlockSpec((1, tc, D_RNN), cur_map)]
        args += list(extra)
    in_specs += [
        pl.BlockSpec((CONV_WIDTH, D_RNN), const2),
        pl.BlockSpec((1, D_RNN), const2),
        pl.BlockSpec((D_RNN // GATE_BLK, GATE_BLK, GATE_BLK), const3),
        pl.BlockSpec((D_RNN // GATE_BLK, GATE_BLK, GATE_BLK), const3),
        pl.BlockSpec((1, D_RNN), const2),
        pl.BlockSpec((1, D_RNN), const2),
        pl.BlockSpec((1, D_RNN), const2),
    ]
    args += [cw, cb, wa, wx, ba, bx, lam]
    if final:
        in_specs.append(pl.BlockSpec((1, D_RNN), const2))
        args.append(gn)
    out_dtype = BF16 if final else F32
    return pl.pallas_call(
        functools.partial(_rnn_kernel, reverse=reverse, final=final, tc=tc, nchunks=nchunks),
        grid=(bsz, nchunks),
        in_specs=in_specs,
        out_specs=pl.BlockSpec((1, tc, D_RNN), cur_map),
        out_shape=jax.ShapeDtypeStruct((bsz, s, D_RNN), out_dtype),
        scratch_shapes=[
            pltpu.VMEM((tc + 2 * SUBLANES, D_RNN), F32),
            pltpu.VMEM((tc, D_RNN), F32),
            pltpu.VMEM((tc, D_RNN), F32),
            pltpu.VMEM((tc, D_RNN), F32),
            pltpu.VMEM((SUBLANES, D_RNN), F32),
        ],
        compiler_params=pltpu.CompilerParams(
            dimension_semantics=("arbitrary", "arbitrary"), vmem_limit_bytes=VMEM_LIMIT),
        name="rnn_bwd" if reverse else "rnn_fwd",
    )(*args)


def _gate_weights(w):
    per = GATE_BLK // RNN_BLOCK
    out = jnp.zeros((D_RNN // GATE_BLK, GATE_BLK, GATE_BLK), w.dtype)
    for n in range(N_RNN_BLOCKS):
        j, o = divmod(n, per)
        out = out.at[j, o * RNN_BLOCK:(o + 1) * RNN_BLOCK, o * RNN_BLOCK:(o + 1) * RNN_BLOCK].set(w[n])
    return out.astype(BF16)


def _attn_kernel(q_ref, k_ref, v_ref, bias_ref, o_ref, lse_ref, *, sub_len):
    i = pl.program_id(2)
    n_tiles = sub_len // Q_TILE
    lane = lax.broadcasted_iota(jnp.int32, (Q_TILE, LANES), 1)
    heads_per_slab = LANES // HEAD_DIM
    first_half = lane < HEAD_DIM
    for t in range(TILES_PER_STEP):
        rows = slice(t * Q_TILE, (t + 1) * Q_TILE)
        q = q_ref[0, rows, :]
        tile = i * TILES_PER_STEP + t
        variant = jnp.where(tile == 0, FIRST, jnp.where(tile == n_tiles - 1, LAST, INTERIOR))
        start = pl.multiple_of(jnp.clip(tile * Q_TILE - HALF_STEPS, 0, sub_len - K_WIN), K_TILE)
        kk = k_ref[0, pl.ds(start, K_WIN), :]
        vv = v_ref[0, pl.ds(start, K_WIN), :]
        for p in range(D_ATTN // LANES):
            sl = slice(p * LANES, (p + 1) * LANES)
            qp, kp, vp = q[:, sl], kk[:, sl], vv[:, sl]
            zero = jnp.zeros_like(qp)
            q2 = jnp.concatenate([jnp.where(first_half, qp, zero), jnp.where(first_half, zero, qp)], axis=0)
            s = lax.dot_general(q2, kp, (((1,), (1,)), ((), ())), preferred_element_type=F32)
            h0 = p * heads_per_slab
            s = s + bias_ref[variant, h0:h0 + heads_per_slab].reshape(heads_per_slab * Q_TILE, K_WIN)
            m = jnp.max(s, axis=-1, keepdims=True)
            e = jnp.exp2(s - m)
            l = jnp.sum(e, axis=-1, keepdims=True)
            o2 = jnp.dot(e.astype(BF16), vp, preferred_element_type=F32) * (1.0 / l)
            lse2 = jnp.broadcast_to(m + jnp.log(l) * LOG2E, (heads_per_slab * Q_TILE, LANES))
            o_ref[0, rows, sl] = jnp.where(first_half, o2[:Q_TILE], o2[Q_TILE:])
            lse_ref[0, rows, sl] = jnp.where(first_half, lse2[:Q_TILE], lse2[Q_TILE:])


def _attn_pattern(qc, kc, vc, bias, pat):
    bsz, dil, sub_len, _ = qc.shape
    q_rows = TILES_PER_STEP * Q_TILE
    qmap = lambda b, r, i: (b, r, i, 0)
    seq_map = lambda b, r, i: (b, r, 0, 0)
    in_specs = [
        pl.BlockSpec((1, None, q_rows, D_ATTN), qmap),
        pl.BlockSpec((1, None, sub_len, D_ATTN), seq_map),
        pl.BlockSpec((1, None, sub_len, D_ATTN), seq_map),
        pl.BlockSpec((None, 3, N_HEADS, Q_TILE, K_WIN), lambda b, r, i: (pat, 0, 0, 0, 0)),
    ]
    return pl.pallas_call(
        functools.partial(_attn_kernel, sub_len=sub_len),
        grid=(bsz, dil, sub_len // q_rows),
        in_specs=in_specs,
        out_specs=[pl.BlockSpec((1, None, q_rows, D_ATTN), qmap)] * 2,
        out_shape=[jax.ShapeDtypeStruct((bsz, dil, sub_len, D_ATTN), F32)] * 2,
        compiler_params=pltpu.CompilerParams(
            dimension_semantics=("arbitrary",) * 3, vmem_limit_bytes=ATTN_VMEM_LIMIT),
        name=f"attn_d{dil}",
    )(qc, kc, vc, bias)


FF_CHUNK = 1024


def _natural_order(ref, scr, tm):
    dil = ref.shape[1]
    if dil == 1:
        return ref[0, 0]
    n_slab = D_ATTN // LANES
    for r in range(dil):
        for j in range(n_slab):
            scr[j, pl.ds(r, tm // dil, stride=dil), :] = ref[0, r, :, j * LANES:(j + 1) * LANES]
    return jnp.concatenate([scr[j] for j in range(n_slab)], axis=-1)


def _out_mlp_kernel(x_ref, mr_ref, o1, o2, o3, l1, l2, l3, ga_ref, wo_ref, gm_ref, wu_ref, wd_ref,
                    gf_ref, out_ref, s_o2, s_o3, s_l2, s_l3, *, tm):
    a1 = _natural_order(l1, None, tm)
    a2 = _natural_order(l2, s_l2, tm)
    a3 = _natural_order(l3, s_l3, tm)
    mx = jnp.maximum(jnp.maximum(a1, a2), a3)
    e1, e2, e3 = jnp.exp2(a1 - mx), jnp.exp2(a2 - mx), jnp.exp2(a3 - mx)
    y_attn = (e1 * _natural_order(o1, None, tm) + e2 * _natural_order(o2, s_o2, tm)
              + e3 * _natural_order(o3, s_o3, tm)) / (e1 + e2 + e3)
    mix_attn = _rms(y_attn, ga_ref[...]).astype(BF16)
    proj = jnp.dot(mr_ref[0], wo_ref[0:D_RNN, :], preferred_element_type=F32)
    proj = proj + jnp.dot(mix_attn, wo_ref[D_RNN:D_RNN + D_ATTN, :], preferred_element_type=F32)
    x1 = proj + x_ref[0]
    h = _rms(x1, gm_ref[...]).astype(BF16)
    ff = None
    for c in range(D_FF // FF_CHUNK):
        u = jnp.dot(h, wu_ref[:, c * FF_CHUNK:(c + 1) * FF_CHUNK], preferred_element_type=F32)
        u = jnp.square(jnp.maximum(u, 0.0)).astype(BF16)
        d = jnp.dot(u, wd_ref[c * FF_CHUNK:(c + 1) * FF_CHUNK, :], preferred_element_type=F32)
        ff = d if ff is None else ff + d
    out_ref[0] = _rms(ff + x1, gf_ref[...])


def _out_mlp(x3, mix_rnn, os_, lses, ga, wo, gm, wu, wd, gf, tm):
    bsz, s, _ = x3.shape
    row = lambda b, i: (b, i, 0)
    cls = lambda b, i: (b, 0, i, 0)
    const = lambda b, i: (0, 0)
    once = pl.Buffered(1)
    cls_specs = [pl.BlockSpec((1, dil, tm // dil, D_ATTN), cls) for _, dil in PATTERNS]
    in_specs = [pl.BlockSpec((1, tm, D_MODEL), row), pl.BlockSpec((1, tm, D_RNN), row)]
    in_specs += cls_specs + cls_specs
    in_specs += [
        pl.BlockSpec((1, D_ATTN), const),
        pl.BlockSpec((D_RNN + D_ATTN, D_MODEL), const, pipeline_mode=once),
        pl.BlockSpec((1, D_MODEL), const),
        pl.BlockSpec((D_MODEL, D_FF), const, pipeline_mode=once),
        pl.BlockSpec((D_FF, D_MODEL), const, pipeline_mode=once),
        pl.BlockSpec((1, D_MODEL), const),
    ]
    return pl.pallas_call(
        functools.partial(_out_mlp_kernel, tm=tm),
        grid=(bsz, s // tm),
        in_specs=in_specs,
        out_specs=pl.BlockSpec((1, tm, D_MODEL), row),
        out_shape=jax.ShapeDtypeStruct((bsz, s, D_MODEL), F32),
        scratch_shapes=[pltpu.VMEM((D_ATTN // LANES, tm, LANES), F32)] * 4,
        compiler_params=pltpu.CompilerParams(
            dimension_semantics=("arbitrary", "arbitrary"), vmem_limit_bytes=VMEM_LIMIT),
        name="out_mlp",
    )(x3, mix_rnn, *os_, *lses, ga, wo, gm, wu, wd, gf)


def kernel(x, attn_norm_g, w_in, conv_w, conv_b, lru_wa_fwd, lru_ba_fwd, lru_wx_fwd, lru_bx_fwd, lru_lam_fwd, lru_wa_bwd, lru_ba_bwd, lru_wx_bwd, lru_bx_bwd, lru_lam_bwd, rel_bias, norm_rnn_g, norm_attn_g, w_out, mlp_norm_g, w_up, w_down, final_norm_g):
    depth = w_in.shape[0]
    assert depth == 1, "the final RMSNorm is fused into the single layer's last call"
    l = 0
    n_pat = len(PATTERNS)
    row = lambda v: v.reshape(1, -1)
    bias = _bias_tables(rel_bias)
    proj = _in_proj(x, row(attn_norm_g[l]), w_in[l].astype(BF16), tm=512)
    xr, gate = proj[0], proj[1]
    qs, ks, vs = (proj[2 + a * n_pat:2 + (a + 1) * n_pat] for a in range(3))
    fwd = (conv_w[l], row(conv_b[l]), _gate_weights(lru_wa_fwd[l]), _gate_weights(lru_wx_fwd[l]),
           row(lru_ba_fwd[l]), row(lru_bx_fwd[l]), row(lru_lam_fwd[l]), None)
    bwd = (conv_w[l], row(conv_b[l]), _gate_weights(lru_wa_bwd[l]), _gate_weights(lru_wx_bwd[l]),
           row(lru_ba_bwd[l]), row(lru_bx_bwd[l]), row(lru_lam_bwd[l]), row(norm_rnn_g[l]))
    h_f = _rnn_pass(xr, (), fwd, reverse=False, final=False, tc=512)
    mix_rnn = _rnn_pass(xr, (h_f, gate), bwd, reverse=True, final=True, tc=512)
    os_, lses = [], []
    for pat in range(n_pat):
        o, lse = _attn_pattern(qs[pat], ks[pat], vs[pat], bias, pat)
        os_.append(o)
        lses.append(lse)
    return _out_mlp(x, mix_rnn, os_, lses, row(norm_attn_g[l]), w_out[l].astype(BF16),
                    row(mlp_norm_g[l]), w_up[l].astype(BF16), w_down[l].astype(BF16),
                    row(final_norm_g), tm=256)
```

```python
import functools
import math

import numpy as np
import jax
import jax.numpy as jnp
from jax import lax
from jax.experimental import pallas as pl
from jax.experimental.pallas import tpu as pltpu

F32 = jnp.float32
BF16 = jnp.bfloat16

D_MODEL = 1024
D_RNN = 512
N_RNN_BLOCKS = 8
RNN_BLOCK = D_RNN // N_RNN_BLOCKS
CONV_WIDTH = 4
CONV_LEFT = 2
LRU_C = 8.0
N_HEADS = 8
HEAD_DIM = 64
D_ATTN = N_HEADS * HEAD_DIM
PATTERNS = ((128, 1), (512, 4), (2048, 16))
HALF_STEPS = 64
N_BUCKETS = 32
MAX_DISTANCE = 1024
D_IN = 2 * D_RNN + 3 * D_ATTN
D_FF = 4 * D_MODEL
EPS = 1e-6
NEG_INF = -1e30

LANES = 128
SUBLANES = 8
MXU_DIM = 256

Q_TILE = 128
K_TILE = 64
K_WIN = Q_TILE + 2 * HALF_STEPS
TILES_PER_STEP = 4
LOG2E = math.log2(math.e)
INTERIOR, FIRST, LAST = 0, 1, 2
GATE_BLK = MXU_DIM

VMEM_LIMIT = 48 * 1024 * 1024
ATTN_VMEM_LIMIT = 56 * 1024 * 1024


def _rms(x, g):
    ms = jnp.mean(x * x, axis=-1, keepdims=True)
    return x * lax.rsqrt(ms + EPS) * g


def _t5_bucket_np(rel):
    nb = N_BUCKETS // 2
    max_exact = nb // 2
    ret = np.where(rel > 0, nb, 0)
    n = np.abs(rel)
    nf = np.maximum(n, 1).astype(np.float32)
    large = max_exact + (np.log(nf / np.float32(max_exact)) / np.float32(math.log(MAX_DISTANCE / max_exact))
                         * np.float32(nb - max_exact)).astype(np.int32)
    large = np.minimum(large, nb - 1)
    return ret + np.where(n < max_exact, n, large)


def _bucket_index_tables():
    qi = np.arange(Q_TILE)[:, None]
    ci = np.arange(K_WIN)[None, :]
    window_lead = {INTERIOR: HALF_STEPS, FIRST: 0, LAST: 2 * HALF_STEPS}
    tabs = []
    for _, dil in PATTERNS:
        variants = []
        for v in (INTERIOR, FIRST, LAST):
            step = ci - qi - window_lead[v]
            b = _t5_bucket_np((step * dil).astype(np.int32))
            variants.append(np.where(np.abs(step) <= HALF_STEPS, b, -1))
        tabs.append(np.stack(variants))
    return np.stack(tabs, axis=0).astype(np.int32)


def _bias_table_kernel(idx_ref, rb_ref, out_ref):
    idx = idx_ref[0]
    hit = [idx == b for b in range(N_BUCKETS)]
    for h in range(N_HEADS):
        acc = jnp.full(idx.shape, NEG_INF, F32)
        for b in range(N_BUCKETS):
            acc = jnp.where(hit[b], rb_ref[b, h] * LOG2E, acc)
        out_ref[0, h] = acc


def _bias_tables(rel_bias):
    idx = jnp.asarray(_bucket_index_tables()).reshape(-1, Q_TILE, K_WIN)
    n_tab = idx.shape[0]
    tabs = pl.pallas_call(
        _bias_table_kernel,
        grid=(n_tab,),
        in_specs=[
            pl.BlockSpec((1, Q_TILE, K_WIN), lambda g: (g, 0, 0)),
            pl.BlockSpec(memory_space=pltpu.SMEM),
        ],
        out_specs=pl.BlockSpec((1, N_HEADS, Q_TILE, K_WIN), lambda g: (g, 0, 0, 0)),
        out_shape=jax.ShapeDtypeStruct((n_tab, N_HEADS, Q_TILE, K_WIN), F32),
        name="bias_table",
    )(idx, rel_bias)
    return tabs.reshape(len(PATTERNS), 3, N_HEADS, Q_TILE, K_WIN)


def _in_proj_kernel(x_ref, g_ref, w_ref, xr_ref, gate_ref, *rest, tm):
    n_pat = len(PATTERNS)
    outs = [rest[a * n_pat:(a + 1) * n_pat] for a in range(3)]
    slabs = rest[3 * n_pat:3 * n_pat + 3]
    h_scr = rest[3 * n_pat + 3]
    h_scr[...] = _rms(x_ref[0], g_ref[...]).astype(BF16)

    def seg(lo, width):
        return jnp.dot(h_scr[...], w_ref[:, lo:lo + width], preferred_element_type=F32)

    n_slab = D_ATTN // LANES
    for a, scale in enumerate((HEAD_DIM ** -0.5 * LOG2E, 1.0, 1.0)):
        p = seg(2 * D_RNN + a * D_ATTN, D_ATTN)
        if scale != 1.0:
            p = p * scale
        slab = slabs[a]
        for (_, dil), o_ref in zip(PATTERNS, outs[a]):
            if dil == 1:
                o_ref[0, 0] = p.astype(BF16)
        for j in range(n_slab):
            slab[j] = p[:, j * LANES:(j + 1) * LANES]
        for (_, dil), o_ref in zip(PATTERNS, outs[a]):
            if dil == 1:
                continue
            for r in range(dil):
                for j in range(n_slab):
                    o_ref[0, r, :, j * LANES:(j + 1) * LANES] = (
                        slab[j, pl.ds(r, tm // dil, stride=dil), :].astype(BF16))
    xr_ref[0] = seg(0, D_RNN)
    gate_ref[0] = seg(D_RNN, D_RNN)


def _in_proj(x3, g, w_in_bf, tm):
    bsz, s, _ = x3.shape
    row = lambda b, i: (b, i, 0)
    cls = lambda b, i: (b, 0, i, 0)
    const = lambda b, i: (0, 0)
    qkv_specs, qkv_shapes = [], []
    for _ in range(3):
        for _, dil in PATTERNS:
            qkv_specs.append(pl.BlockSpec((1, dil, tm // dil, D_ATTN), cls))
            qkv_shapes.append(jax.ShapeDtypeStruct((bsz, dil, s // dil, D_ATTN), BF16))
    return pl.pallas_call(
        functools.partial(_in_proj_kernel, tm=tm),
        grid=(bsz, s // tm),
        in_specs=[
            pl.BlockSpec((1, tm, D_MODEL), row),
            pl.BlockSpec((1, D_MODEL), const),
            pl.BlockSpec((D_MODEL, D_IN), const),
        ],
        out_specs=[pl.BlockSpec((1, tm, D_RNN), row), pl.BlockSpec((1, tm, D_RNN), row)] + qkv_specs,
        out_shape=[jax.ShapeDtypeStruct((bsz, s, D_RNN), F32)] * 2 + qkv_shapes,
        scratch_shapes=[pltpu.VMEM((D_ATTN // LANES, tm, LANES), F32)] * 3 + [pltpu.VMEM((tm, D_MODEL), BF16)],
        compiler_params=pltpu.CompilerParams(
            dimension_semantics=("arbitrary", "arbitrary"), vmem_limit_bytes=VMEM_LIMIT),
        name="in_proj",
    )(x3, g, w_in_bf)


SCAN_ROWS = 128


def _scan_block(a, b, reverse):
    n = a.shape[0]
    row = lax.broadcasted_iota(jnp.int32, a.shape, 0)
    s = 1
    while s < n:
        if reverse:
            ra = pltpu.roll(a, n - s, 0)
            rb = pltpu.roll(b, n - s, 0)
            m = row < n - s
        else:
            ra = pltpu.roll(a, s, 0)
            rb = pltpu.roll(b, s, 0)
            m = row >= s
        b = jnp.where(m, a * rb + b, b)
        a = jnp.where(m, a * ra, a)
        s *= 2
    return a, b


def _rnn_kernel(*refs, reverse, final, tc, nchunks):
    if final:
        (prev_ref, cur_ref, next_ref, hf_ref, gate_ref, cw_ref, cb_ref, wa_ref, wx_ref,
         ba_ref, bx_ref, lam_ref, gn_ref, out_ref, xext, a_scr, b_scr, h_scr, carry) = refs
    else:
        (prev_ref, cur_ref, next_ref, cw_ref, cb_ref, wa_ref, wx_ref,
         ba_ref, bx_ref, lam_ref, out_ref, xext, a_scr, b_scr, h_scr, carry) = refs

    step = pl.program_id(1)
    chunk = (nchunks - 1 - step) if reverse else step

    @pl.when(step == 0)
    def _():
        carry[...] = jnp.zeros_like(carry)

    xext[0:SUBLANES, :] = jnp.where(chunk > 0, prev_ref[0], 0.0)
    xext[SUBLANES:SUBLANES + tc, :] = cur_ref[0]
    xext[SUBLANES + tc:2 * SUBLANES + tc, :] = jnp.where(chunk < nchunks - 1, next_ref[0], 0.0)

    xc = cb_ref[...]
    for k in range(CONV_WIDTH):
        xc = xc + xext[pl.ds(SUBLANES - CONV_LEFT + k, tc), :] * cw_ref[k:k + 1, :]
    xcb = xc.astype(BF16)

    za, zx = [], []
    for j in range(D_RNN // GATE_BLK):
        xj = xcb[:, j * GATE_BLK:(j + 1) * GATE_BLK]
        za.append(jnp.dot(xj, wa_ref[j], preferred_element_type=F32))
        zx.append(jnp.dot(xj, wx_ref[j], preferred_element_type=F32))
    za = jnp.concatenate(za, axis=-1) + ba_ref[...]
    zx = jnp.concatenate(zx, axis=-1) + bx_ref[...]
    r = 1.0 / (1.0 + jnp.exp(-za))
    i = 1.0 / (1.0 + jnp.exp(-zx))
    nlam = -lam_ref[...]
    softplus = jnp.maximum(nlam, 0.0) + jnp.log1p(jnp.exp(-jnp.abs(nlam)))
    log_a = (-LRU_C * softplus) * r
    a = jnp.exp(log_a)
    a_scr[...] = a
    b_scr[...] = jnp.sqrt(-jnp.tanh(log_a) * (1.0 + a * a)) * (i * xc)

    nblk = tc // SCAN_ROWS
    order = range(nblk - 1, -1, -1) if reverse else range(nblk)
    last = 0 if reverse else SCAN_ROWS - 1
    for j in range(D_RNN // LANES):
        lanes = slice(j * LANES, (j + 1) * LANES)
        c = carry[0:1, lanes]
        for rb in order:
            rows = slice(rb * SCAN_ROWS, (rb + 1) * SCAN_ROWS)
            ap, h0 = _scan_block(a_scr[rows, lanes], b_scr[rows, lanes], reverse)
            h = h0 + ap * c
            h_scr[rows, lanes] = h
            c = h[last:last + 1, :]
        carry[0:1, lanes] = c

    if final:
        g = gate_ref[0]
        gelu = g * (0.5 * (1.0 + jnp.tanh(math.sqrt(2.0 / math.pi) * (g + 0.044715 * (g * g * g)))))
        y = (hf_ref[0] + h_scr[...]) * gelu
        out_ref[0] = _rms(y, gn_ref[...]).astype(BF16)
    else:
        out_ref[0] = h_scr[...]


def _rnn_pass(xr3, extra, params, reverse, final, tc):
    bsz, s, _ = xr3.shape
    nchunks = s // tc
    hb = tc // SUBLANES
    nhalo = s // SUBLANES

    def cidx(c):
        return (nchunks - 1 - c) if reverse else c

    cur_map = lambda b, c: (b, cidx(c), 0)
    prev_map = lambda b, c: (b, jnp.maximum(cidx(c) * hb - 1, 0), 0)
    next_map = lambda b, c: (b, jnp.minimum((cidx(c) + 1) * hb, nhalo - 1), 0)
    const2 = lambda b, c: (0, 0)
    const3 = lambda b, c: (0, 0, 0)

    cw, cb, wa, wx, ba, bx, lam, gn = params
    in_specs = [
        pl.BlockSpec((1, SUBLANES, D_RNN), prev_map),
        pl.BlockSpec((1, tc, D_RNN), cur_map),
        pl.BlockSpec((1, SUBLANES, D_RNN), next_map),
    ]
    args = [xr3, xr3, xr3]
    if final:
        in_specs += [pl.BlockSpec((1, tc, D_RNN), cur_map), pl.BlockSpec((1, tc, D_RNN), cur_map)]
        args += list(extra)
    in_specs += [
        pl.BlockSpec((CONV_WIDTH, D_RNN), const2),
        pl.BlockSpec((1, D_RNN), const2),
        pl.BlockSpec((D_RNN // GATE_BLK, GATE_BLK, GATE_BLK), const3),
        pl.BlockSpec((D_RNN // GATE_BLK, GATE_BLK, GATE_BLK), const3),
        pl.BlockSpec((1, D_RNN), const2),
        pl.BlockSpec((1, D_RNN), const2),
        pl.BlockSpec((1, D_RNN), const2),
    ]
    args += [cw, cb, wa, wx, ba, bx, lam]
    if final:
        in_specs.append(pl.BlockSpec((1, D_RNN), const2))
        args.append(gn)
    out_dtype = BF16 if final else F32
    return pl.pallas_call(
        functools.partial(_rnn_kernel, reverse=reverse, final=final, tc=tc, nchunks=nchunks),
        grid=(bsz, nchunks),
        in_specs=in_specs,
        out_specs=pl.BlockSpec((1, tc, D_RNN), cur_map),
        out_shape=jax.ShapeDtypeStruct((bsz, s, D_RNN), out_dtype),
        scratch_shapes=[
            pltpu.VMEM((tc + 2 * SUBLANES, D_RNN), F32),
            pltpu.VMEM((tc, D_RNN), F32),
            pltpu.VMEM((tc, D_RNN), F32),
            pltpu.VMEM((tc, D_RNN), F32),
            pltpu.VMEM((SUBLANES, D_RNN), F32),
        ],
        compiler_params=pltpu.CompilerParams(
            dimension_semantics=("arbitrary", "arbitrary"), vmem_limit_bytes=VMEM_LIMIT),
        name="rnn_bwd" if reverse else "rnn_fwd",
    )(*args)


def _gate_weights(w):
    per = GATE_BLK // RNN_BLOCK
    out = jnp.zeros((D_RNN // GATE_BLK, GATE_BLK, GATE_BLK), w.dtype)
    for n in range(N_RNN_BLOCKS):
        j, o = divmod(n, per)
        out = out.at[j, o * RNN_BLOCK:(o + 1) * RNN_BLOCK, o * RNN_BLOCK:(o + 1) * RNN_BLOCK].set(w[n])
    return out.astype(BF16)


def _attn_kernel(q_ref, k_ref, v_ref, bias_ref, o_ref, lse_ref, *, sub_len):
    i = pl.program_id(2)
    n_tiles = sub_len // Q_TILE
    lane = lax.broadcasted_iota(jnp.int32, (Q_TILE, LANES), 1)
    heads_per_slab = LANES // HEAD_DIM
    first_half = lane < HEAD_DIM
    for t in range(TILES_PER_STEP):
        rows = slice(t * Q_TILE, (t + 1) * Q_TILE)
        q = q_ref[0, rows, :]
        tile = i * TILES_PER_STEP + t
        variant = jnp.where(tile == 0, FIRST, jnp.where(tile == n_tiles - 1, LAST, INTERIOR))
        start = pl.multiple_of(jnp.clip(tile * Q_TILE - HALF_STEPS, 0, sub_len - K_WIN), K_TILE)
        kk = k_ref[0, pl.ds(start, K_WIN), :]
        vv = v_ref[0, pl.ds(start, K_WIN), :]
        for p in range(D_ATTN // LANES):
            sl = slice(p * LANES, (p + 1) * LANES)
            qp, kp, vp = q[:, sl], kk[:, sl], vv[:, sl]
            zero = jnp.zeros_like(qp)
            q2 = jnp.concatenate([jnp.where(first_half, qp, zero), jnp.where(first_half, zero, qp)], axis=0)
            s = lax.dot_general(q2, kp, (((1,), (1,)), ((), ())), preferred_element_type=F32)
            h0 = p * heads_per_slab
            s = s + bias_ref[variant, h0:h0 + heads_per_slab].reshape(heads_per_slab * Q_TILE, K_WIN)
            m = jnp.max(s, axis=-1, keepdims=True)
            e = jnp.exp2(s - m)
            l = jnp.sum(e, axis=-1, keepdims=True)
            o2 = jnp.dot(e.astype(BF16), vp, preferred_element_type=F32) * (1.0 / l)
            lse2 = jnp.broadcast_to(m + jnp.log(l) * LOG2E, (heads_per_slab * Q_TILE, LANES))
            o_ref[0, rows, sl] = jnp.where(first_half, o2[:Q_TILE], o2[Q_TILE:])
            lse_ref[0, rows, sl] = jnp.where(first_half, lse2[:Q_TILE], lse2[Q_TILE:])


def _attn_pattern(qc, kc, vc, bias, pat):
    bsz, dil, sub_len, _ = qc.shape
    q_rows = TILES_PER_STEP * Q_TILE
    qmap = lambda b, r, i: (b, r, i, 0)
    seq_map = lambda b, r, i: (b, r, 0, 0)
    in_specs = [
        pl.BlockSpec((1, None, q_rows, D_ATTN), qmap),
        pl.BlockSpec((1, None, sub_len, D_ATTN), seq_map),
        pl.BlockSpec((1, None, sub_len, D_ATTN), seq_map),
        pl.BlockSpec((None, 3, N_HEADS, Q_TILE, K_WIN), lambda b, r, i: (pat, 0, 0, 0, 0)),
    ]
    return pl.pallas_call(
        functools.partial(_attn_kernel, sub_len=sub_len),
        grid=(bsz, dil, sub_len // q_rows),
        in_specs=in_specs,
        out_specs=[pl.BlockSpec((1, None, q_rows, D_ATTN), qmap)] * 2,
        out_shape=[jax.ShapeDtypeStruct((bsz, dil, sub_len, D_ATTN), F32)] * 2,
        compiler_params=pltpu.CompilerParams(
            dimension_semantics=("arbitrary",) * 3, vmem_limit_bytes=ATTN_VMEM_LIMIT),
        name=f"attn_d{dil}",
    )(qc, kc, vc, bias)


FF_CHUNK = 1024


SUB_ROWS = 256


def _natural_order(ref, scr, u):
    dil = ref.shape[1]
    per = SUB_ROWS // dil
    if dil == 1:
        return ref[0, 0, u * per:(u + 1) * per, :]
    n_slab = D_ATTN // LANES
    for r in range(dil):
        for j in range(n_slab):
            scr[u, j, pl.ds(r, per, stride=dil), :] = ref[0, r, u * per:(u + 1) * per, j * LANES:(j + 1) * LANES]
    return jnp.concatenate([scr[u, j] for j in range(n_slab)], axis=-1)


def _out_mlp_kernel(x_ref, mr_ref, o1, o2, o3, l1, l2, l3, ga_ref, wo_ref, gm_ref, wu_ref, wd_ref,
                    gf_ref, out_ref, s_o2, s_o3, s_l2, s_l3, *, tm):
    n_sub = tm // SUB_ROWS

    def merge(u):
        a1 = _natural_order(l1, None, u)
        a2 = _natural_order(l2, s_l2, u)
        a3 = _natural_order(l3, s_l3, u)
        mx = jnp.maximum(jnp.maximum(a1, a2), a3)
        e1, e2, e3 = jnp.exp2(a1 - mx), jnp.exp2(a2 - mx), jnp.exp2(a3 - mx)
        y_attn = (e1 * _natural_order(o1, None, u) + e2 * _natural_order(o2, s_o2, u)
                  + e3 * _natural_order(o3, s_o3, u)) / (e1 + e2 + e3)
        return _rms(y_attn, ga_ref[...]).astype(BF16)

    def out_proj(u, mix_attn):
        rows = slice(u * SUB_ROWS, (u + 1) * SUB_ROWS)
        proj = jnp.dot(mr_ref[0, rows, :], wo_ref[0:D_RNN, :], preferred_element_type=F32)
        proj = proj + jnp.dot(mix_attn, wo_ref[D_RNN:D_RNN + D_ATTN, :], preferred_element_type=F32)
        x1 = proj + x_ref[0, rows, :]
        return x1, _rms(x1, gm_ref[...]).astype(BF16)

    def ff_chunk(h, c):
        z = jnp.dot(h, wu_ref[:, c * FF_CHUNK:(c + 1) * FF_CHUNK], preferred_element_type=F32)
        z = jnp.square(jnp.maximum(z, 0.0)).astype(BF16)
        return jnp.dot(z, wd_ref[c * FF_CHUNK:(c + 1) * FF_CHUNK, :], preferred_element_type=F32)

    def finish(u, ff, x1):
        rows = slice(u * SUB_ROWS, (u + 1) * SUB_ROWS)
        out_ref[0, rows, :] = _rms(ff + x1, gf_ref[...])

    n_chunk = D_FF // FF_CHUNK
    x1, h = out_proj(0, merge(0))
    pending = None
    for u in range(n_sub):
        ff = None
        nxt = None
        for c in range(n_chunk):
            d = ff_chunk(h, c)
            ff = d if ff is None else ff + d
            if c == 0 and pending is not None:
                finish(*pending)
            if c == 1 and u + 1 < n_sub:
                mix_next = merge(u + 1)
            if c == 2 and u + 1 < n_sub:
                nxt = out_proj(u + 1, mix_next)
        pending = (u, ff, x1)
        if nxt is not None:
            x1, h = nxt
    finish(*pending)


def _out_mlp(x3, mix_rnn, os_, lses, ga, wo, gm, wu, wd, gf, tm):
    bsz, s, _ = x3.shape
    row = lambda b, i: (b, i, 0)
    cls = lambda b, i: (b, 0, i, 0)
    const = lambda b, i: (0, 0)
    once = pl.Buffered(1)
    cls_specs = [pl.BlockSpec((1, dil, tm // dil, D_ATTN), cls) for _, dil in PATTERNS]
    in_specs = [pl.BlockSpec((1, tm, D_MODEL), row), pl.BlockSpec((1, tm, D_RNN), row)]
    in_specs += cls_specs + cls_specs
    in_specs += [
        pl.BlockSpec((1, D_ATTN), const),
        pl.BlockSpec((D_RNN + D_ATTN, D_MODEL), const, pipeline_mode=once),
        pl.BlockSpec((1, D_MODEL), const),
        pl.BlockSpec((D_MODEL, D_FF), const, pipeline_mode=once),
        pl.BlockSpec((D_FF, D_MODEL), const, pipeline_mode=once),
        pl.BlockSpec((1, D_MODEL), const),
    ]
    return pl.pallas_call(
        functools.partial(_out_mlp_kernel, tm=tm),
        grid=(bsz, s // tm),
        in_specs=in_specs,
        out_specs=pl.BlockSpec((1, tm, D_MODEL), row),
        out_shape=jax.ShapeDtypeStruct((bsz, s, D_MODEL), F32),
        scratch_shapes=[pltpu.VMEM((tm // SUB_ROWS, D_ATTN // LANES, SUB_ROWS, LANES), F32)] * 4,
        compiler_params=pltpu.CompilerParams(
            dimension_semantics=("arbitrary", "arbitrary"), vmem_limit_bytes=ATTN_VMEM_LIMIT),
        name="out_mlp",
    )(x3, mix_rnn, *os_, *lses, ga, wo, gm, wu, wd, gf)


def kernel(x, attn_norm_g, w_in, conv_w, conv_b, lru_wa_fwd, lru_ba_fwd, lru_wx_fwd, lru_bx_fwd, lru_lam_fwd, lru_wa_bwd, lru_ba_bwd, lru_wx_bwd, lru_bx_bwd, lru_lam_bwd, rel_bias, norm_rnn_g, norm_attn_g, w_out, mlp_norm_g, w_up, w_down, final_norm_g):
    depth = w_in.shape[0]
    assert depth == 1, "the final RMSNorm is fused into the single layer's last call"
    l = 0
    n_pat = len(PATTERNS)
    row = lambda v: v.reshape(1, -1)
    bias = _bias_tables(rel_bias)
    proj = _in_proj(x, row(attn_norm_g[l]), w_in[l].astype(BF16), tm=512)
    xr, gate = proj[0], proj[1]
    qs, ks, vs = (proj[2 + a * n_pat:2 + (a + 1) * n_pat] for a in range(3))
    fwd = (conv_w[l], row(conv_b[l]), _gate_weights(lru_wa_fwd[l]), _gate_weights(lru_wx_fwd[l]),
           row(lru_ba_fwd[l]), row(lru_bx_fwd[l]), row(lru_lam_fwd[l]), None)
    bwd = (conv_w[l], row(conv_b[l]), _gate_weights(lru_wa_bwd[l]), _gate_weights(lru_wx_bwd[l]),
           row(lru_ba_bwd[l]), row(lru_bx_bwd[l]), row(lru_lam_bwd[l]), row(norm_rnn_g[l]))
    h_f = _rnn_pass(xr, (), fwd, reverse=False, final=False, tc=512)
    mix_rnn = _rnn_pass(xr, (h_f, gate), bwd, reverse=True, final=True, tc=512)
    os_, lses = [], []
    for pat in range(n_pat):
        o, lse = _attn_pattern(qs[pat], ks[pat], vs[pat], bias, pat)
        os_.append(o)
        lses.append(lse)
    return _out_mlp(x, mix_rnn, os_, lses, row(norm_attn_g[l]), w_out[l].astype(BF16),
                    row(mlp_norm_g[l]), w_up[l].astype(BF16), w_down[l].astype(BF16),
                    row(final_norm_g), tm=512)
```

```python
import functools
import math

import numpy as np
import jax
import jax.numpy as jnp
from jax import lax
from jax.experimental import pallas as pl
from jax.experimental.pallas import tpu as pltpu

F32 = jnp.float32
BF16 = jnp.bfloat16

D_MODEL = 1024
D_RNN = 512
N_RNN_BLOCKS = 8
RNN_BLOCK = D_RNN // N_RNN_BLOCKS
CONV_WIDTH = 4
CONV_LEFT = 2
LRU_C = 8.0
N_HEADS = 8
HEAD_DIM = 64
D_ATTN = N_HEADS * HEAD_DIM
PATTERNS = ((128, 1), (512, 4), (2048, 16))
HALF_STEPS = 64
N_BUCKETS = 32
MAX_DISTANCE = 1024
D_IN = 2 * D_RNN + 3 * D_ATTN
D_FF = 4 * D_MODEL
EPS = 1e-6
NEG_INF = -1e30

LANES = 128
SUBLANES = 8
MXU_DIM = 256

Q_TILE = 128
K_TILE = 64
K_WIN = Q_TILE + 2 * HALF_STEPS
TILES_PER_STEP = 4
LOG2E = math.log2(math.e)
INTERIOR, FIRST, LAST = 0, 1, 2
GATE_BLK = MXU_DIM

VMEM_LIMIT = 48 * 1024 * 1024
ATTN_VMEM_LIMIT = 56 * 1024 * 1024


def _rms(x, g):
    ms = jnp.mean(x * x, axis=-1, keepdims=True)
    return x * lax.rsqrt(ms + EPS) * g


def _t5_bucket_np(rel):
    nb = N_BUCKETS // 2
    max_exact = nb // 2
    ret = np.where(rel > 0, nb, 0)
    n = np.abs(rel)
    nf = np.maximum(n, 1).astype(np.float32)
    large = max_exact + (np.log(nf / np.float32(max_exact)) / np.float32(math.log(MAX_DISTANCE / max_exact))
                         * np.float32(nb - max_exact)).astype(np.int32)
    large = np.minimum(large, nb - 1)
    return ret + np.where(n < max_exact, n, large)


def _bucket_index_tables():
    qi = np.arange(Q_TILE)[:, None]
    ci = np.arange(K_WIN)[None, :]
    window_lead = {INTERIOR: HALF_STEPS, FIRST: 0, LAST: 2 * HALF_STEPS}
    tabs = []
    for _, dil in PATTERNS:
        variants = []
        for v in (INTERIOR, FIRST, LAST):
            step = ci - qi - window_lead[v]
            b = _t5_bucket_np((step * dil).astype(np.int32))
            variants.append(np.where(np.abs(step) <= HALF_STEPS, b, -1))
        tabs.append(np.stack(variants))
    return np.stack(tabs, axis=0).astype(np.int32)


def _bias_table_kernel(idx_ref, rb_ref, out_ref):
    idx = idx_ref[0]
    hit = [idx == b for b in range(N_BUCKETS)]
    for h in range(N_HEADS):
        acc = jnp.full(idx.shape, NEG_INF, F32)
        for b in range(N_BUCKETS):
            acc = jnp.where(hit[b], rb_ref[b, h] * LOG2E, acc)
        out_ref[0, h] = acc


def _bias_tables(rel_bias):
    idx = jnp.asarray(_bucket_index_tables()).reshape(-1, Q_TILE, K_WIN)
    n_tab = idx.shape[0]
    tabs = pl.pallas_call(
        _bias_table_kernel,
        grid=(n_tab,),
        in_specs=[
            pl.BlockSpec((1, Q_TILE, K_WIN), lambda g: (g, 0, 0)),
            pl.BlockSpec(memory_space=pltpu.SMEM),
        ],
        out_specs=pl.BlockSpec((1, N_HEADS, Q_TILE, K_WIN), lambda g: (g, 0, 0, 0)),
        out_shape=jax.ShapeDtypeStruct((n_tab, N_HEADS, Q_TILE, K_WIN), F32),
        name="bias_table",
    )(idx, rel_bias)
    return tabs.reshape(len(PATTERNS), 3, N_HEADS, Q_TILE, K_WIN)


def _stripe_store(ref, p):
    for st in range(SUBLANES):
        for j in range(D_RNN // LANES):
            ref[0, j, pl.ds(st, STRIPE_ROWS, stride=SUBLANES), :] = (
                p[st * STRIPE_ROWS:(st + 1) * STRIPE_ROWS, j * LANES:(j + 1) * LANES])


def _in_proj_kernel(x_ref, g_ref, w_ref, xr_ref, gate_ref, *rest, tm):
    n_pat = len(PATTERNS)
    outs = [rest[a * n_pat:(a + 1) * n_pat] for a in range(3)]
    slabs = rest[3 * n_pat:3 * n_pat + 3]
    h_scr = rest[3 * n_pat + 3]
    h_scr[...] = _rms(x_ref[0], g_ref[...]).astype(BF16)

    def seg(lo, width):
        return jnp.dot(h_scr[...], w_ref[:, lo:lo + width], preferred_element_type=F32)

    n_slab = D_ATTN // LANES
    for a, scale in enumerate((HEAD_DIM ** -0.5 * LOG2E, 1.0, 1.0)):
        p = seg(2 * D_RNN + a * D_ATTN, D_ATTN)
        if scale != 1.0:
            p = p * scale
        slab = slabs[a]
        for (_, dil), o_ref in zip(PATTERNS, outs[a]):
            if dil == 1:
                o_ref[0, 0] = p.astype(BF16)
        for j in range(n_slab):
            slab[j] = p[:, j * LANES:(j + 1) * LANES]
        for (_, dil), o_ref in zip(PATTERNS, outs[a]):
            if dil == 1:
                continue
            for r in range(dil):
                for j in range(n_slab):
                    o_ref[0, r, :, j * LANES:(j + 1) * LANES] = (
                        slab[j, pl.ds(r, tm // dil, stride=dil), :].astype(BF16))
    _stripe_store(xr_ref, seg(0, D_RNN))
    _stripe_store(gate_ref, seg(D_RNN, D_RNN))


def _in_proj(x3, g, w_in_bf, tm):
    assert tm == RNN_CHUNK
    bsz, s, _ = x3.shape
    row = lambda b, i: (b, i, 0)
    cls = lambda b, i: (b, 0, i, 0)
    const = lambda b, i: (0, 0)
    qkv_specs, qkv_shapes = [], []
    for _ in range(3):
        for _, dil in PATTERNS:
            qkv_specs.append(pl.BlockSpec((1, dil, tm // dil, D_ATTN), cls))
            qkv_shapes.append(jax.ShapeDtypeStruct((bsz, dil, s // dil, D_ATTN), BF16))
    return pl.pallas_call(
        functools.partial(_in_proj_kernel, tm=tm),
        grid=(bsz, s // tm),
        in_specs=[
            pl.BlockSpec((1, tm, D_MODEL), row),
            pl.BlockSpec((1, D_MODEL), const),
            pl.BlockSpec((D_MODEL, D_IN), const),
        ],
        out_specs=[pl.BlockSpec((1, D_RNN // LANES, tm, LANES), cls)] * 2 + qkv_specs,
        out_shape=[jax.ShapeDtypeStruct((bsz, D_RNN // LANES, s, LANES), F32)] * 2 + qkv_shapes,
        scratch_shapes=[pltpu.VMEM((D_ATTN // LANES, tm, LANES), F32)] * 3 + [pltpu.VMEM((tm, D_MODEL), BF16)],
        compiler_params=pltpu.CompilerParams(
            dimension_semantics=("arbitrary", "arbitrary"), vmem_limit_bytes=VMEM_LIMIT),
        name="in_proj",
    )(x3, g, w_in_bf)


RNN_CHUNK = 512
STRIPE_ROWS = RNN_CHUNK // SUBLANES


def _scan_block(a, b, reverse):
    n = a.shape[0]
    row = lax.broadcasted_iota(jnp.int32, a.shape, 0)
    s = 1
    while s < n:
        if reverse:
            ra = pltpu.roll(a, n - s, 0)
            rb = pltpu.roll(b, n - s, 0)
            m = row < n - s
        else:
            ra = pltpu.roll(a, s, 0)
            rb = pltpu.roll(b, s, 0)
            m = row >= s
        b = jnp.where(m, a * rb + b, b)
        a = jnp.where(m, a * ra, a)
        s *= 2
    return a, b


def _rnn_kernel(*refs, reverse, final, nchunks):
    if final:
        (pa_ref, pb_ref, cur_ref, nx_ref, hf_ref, gate_ref, cw_ref, cb_ref, wa_ref, wx_ref,
         ba_ref, bx_ref, lam_ref, gn_ref, out_ref, a_scr, b_scr, h_scr, carry) = refs
    else:
        (pa_ref, pb_ref, cur_ref, nx_ref, cw_ref, cb_ref, wa_ref, wx_ref,
         ba_ref, bx_ref, lam_ref, out_ref, a_scr, b_scr, carry) = refs
        h_scr = None

    step = pl.program_id(1)
    chunk = (nchunks - 1 - step) if reverse else step
    n_slab = D_RNN // LANES
    nv = STRIPE_ROWS
    sub = lax.broadcasted_iota(jnp.int32, (SUBLANES, LANES), 0)
    top, bot = SUBLANES - 1, 0

    @pl.when(step == 0)
    def _():
        carry[...] = jnp.zeros_like(carry)

    xc = []
    for j in range(n_slab):
        lanes = slice(j * LANES, (j + 1) * LANES)
        x = cur_ref[0, j].reshape(nv, SUBLANES, LANES)
        pa = jnp.where(chunk > 0, pa_ref[0, j], 0.0)
        pb = jnp.where(chunk > 0, pb_ref[0, j], 0.0)
        nx = jnp.where(chunk < nchunks - 1, nx_ref[0, j], 0.0)
        xm2 = pltpu.roll(jnp.where(sub == top, pa, x[nv - 2]), 1, 0)
        xm1 = pltpu.roll(jnp.where(sub == top, pb, x[nv - 1]), 1, 0)
        xp1 = pltpu.roll(jnp.where(sub == bot, nx, x[0]), SUBLANES - 1, 0)
        xext = jnp.concatenate([xm2[None], xm1[None], x, xp1[None]], axis=0)
        acc = cb_ref[:, lanes] + xext[0:nv] * cw_ref[0:1, lanes]
        for k in range(1, CONV_WIDTH):
            acc = acc + xext[k:k + nv] * cw_ref[k:k + 1, lanes]
        xc.append(acc.reshape(RNN_CHUNK, LANES))

    nlam = -lam_ref[...]
    softplus = jnp.maximum(nlam, 0.0) + jnp.log1p(jnp.exp(-jnp.abs(nlam)))
    coef = -LRU_C * softplus
    slabs_per_blk = GATE_BLK // LANES
    for jj in range(D_RNN // GATE_BLK):
        blk = slice(jj * GATE_BLK, (jj + 1) * GATE_BLK)
        xj = jnp.concatenate(xc[jj * slabs_per_blk:(jj + 1) * slabs_per_blk], axis=-1)
        xjb = xj.astype(BF16)
        za = jnp.dot(xjb, wa_ref[jj], preferred_element_type=F32) + ba_ref[:, blk]
        zx = jnp.dot(xjb, wx_ref[jj], preferred_element_type=F32) + bx_ref[:, blk]
        r = 1.0 / (1.0 + jnp.exp(-za))
        i = 1.0 / (1.0 + jnp.exp(-zx))
        log_a = coef[:, blk] * r
        a = jnp.exp(log_a)
        b = jnp.sqrt(-jnp.tanh(log_a) * (1.0 + a * a)) * (i * xj)
        for t in range(slabs_per_blk):
            a_scr[jj * slabs_per_blk + t] = a[:, t * LANES:(t + 1) * LANES]
            b_scr[jj * slabs_per_blk + t] = b[:, t * LANES:(t + 1) * LANES]

    order = range(nv - 1, -1, -1) if reverse else range(nv)
    group = lambda v: slice(v * SUBLANES, (v + 1) * SUBLANES)

    h = [jnp.zeros((SUBLANES, LANES), F32)] * n_slab
    p = [jnp.ones((SUBLANES, LANES), F32)] * n_slab
    for v in order:
        for j in range(n_slab):
            av = a_scr[j, group(v), :]
            h[j] = av * h[j] + b_scr[j, group(v), :]
            p[j] = av * p[j]

    start = []
    for j in range(n_slab):
        pc, hc = _scan_block(p[j], h[j], reverse)
        prev = carry[j]
        if reverse:
            seed = jnp.broadcast_to(prev[bot:bot + 1, :], (SUBLANES, LANES))
            ends = hc + pc * seed
            start.append(jnp.where(sub == top, pltpu.roll(prev, SUBLANES - 1, 0),
                                   pltpu.roll(ends, SUBLANES - 1, 0)))
        else:
            seed = jnp.broadcast_to(prev[top:top + 1, :], (SUBLANES, LANES))
            ends = hc + pc * seed
            start.append(jnp.where(sub == bot, pltpu.roll(prev, 1, 0), pltpu.roll(ends, 1, 0)))
        carry[j] = ends

    dst = h_scr if final else None
    h = start
    for v in order:
        for j in range(n_slab):
            h[j] = a_scr[j, group(v), :] * h[j] + b_scr[j, group(v), :]
            if final:
                dst[j, group(v), :] = h[j]
            else:
                out_ref[0, j, group(v), :] = h[j]

    if final:
        ys = []
        for j in range(n_slab):
            g = gate_ref[0, j]
            gelu = g * (0.5 * (1.0 + jnp.tanh(math.sqrt(2.0 / math.pi) * (g + 0.044715 * (g * g * g)))))
            ys.append((hf_ref[0, j] + h_scr[j]) * gelu)
        ss = ys[0] * ys[0]
        for j in range(1, n_slab):
            ss = ss + ys[j] * ys[j]
        inv = lax.rsqrt(jnp.sum(ss, axis=-1, keepdims=True) * (1.0 / D_RNN) + EPS)
        for j in range(n_slab):
            lanes = slice(j * LANES, (j + 1) * LANES)
            h_scr[j] = ys[j] * inv * gn_ref[:, lanes]
        for st in range(SUBLANES):
            for j in range(n_slab):
                out_ref[0, st * nv:(st + 1) * nv, j * LANES:(j + 1) * LANES] = (
                    h_scr[j, pl.ds(st, nv, stride=SUBLANES), :].astype(BF16))


def _rnn_pass(xr4, extra, params, reverse, final):
    bsz, n_slab, s, _ = xr4.shape
    nchunks = s // RNN_CHUNK
    hb = RNN_CHUNK // SUBLANES
    nhalo = s // SUBLANES

    def cidx(c):
        return (nchunks - 1 - c) if reverse else c

    cur_map = lambda b, c: (b, 0, cidx(c), 0)
    pa_map = lambda b, c: (b, 0, jnp.maximum(cidx(c) * hb - 2, 0), 0)
    pb_map = lambda b, c: (b, 0, jnp.maximum(cidx(c) * hb - 1, 0), 0)
    nx_map = lambda b, c: (b, 0, jnp.minimum((cidx(c) + 1) * hb, nhalo - 1), 0)
    const2 = lambda b, c: (0, 0)
    const3 = lambda b, c: (0, 0, 0)

    cw, cb, wa, wx, ba, bx, lam, gn = params
    halo = (1, n_slab, SUBLANES, LANES)
    full = (1, n_slab, RNN_CHUNK, LANES)
    in_specs = [pl.BlockSpec(halo, pa_map), pl.BlockSpec(halo, pb_map), pl.BlockSpec(full, cur_map),
                pl.BlockSpec(halo, nx_map)]
    args = [xr4, xr4, xr4, xr4]
    if final:
        in_specs += [pl.BlockSpec(full, cur_map), pl.BlockSpec(full, cur_map)]
        args += list(extra)
    in_specs += [
        pl.BlockSpec((CONV_WIDTH, D_RNN), const2),
        pl.BlockSpec((1, D_RNN), const2),
        pl.BlockSpec((D_RNN // GATE_BLK, GATE_BLK, GATE_BLK), const3),
        pl.BlockSpec((D_RNN // GATE_BLK, GATE_BLK, GATE_BLK), const3),
        pl.BlockSpec((1, D_RNN), const2),
        pl.BlockSpec((1, D_RNN), const2),
        pl.BlockSpec((1, D_RNN), const2),
    ]
    args += [cw, cb, wa, wx, ba, bx, lam]
    slab_scr = pltpu.VMEM((n_slab, RNN_CHUNK, LANES), F32)
    scratch = [slab_scr, slab_scr]
    if final:
        in_specs.append(pl.BlockSpec((1, D_RNN), const2))
        args.append(gn)
        scratch.append(slab_scr)
        out_spec = pl.BlockSpec((1, RNN_CHUNK, D_RNN), lambda b, c: (b, cidx(c), 0))
        out_shape = jax.ShapeDtypeStruct((bsz, s, D_RNN), BF16)
    else:
        out_spec = pl.BlockSpec(full, cur_map)
        out_shape = jax.ShapeDtypeStruct(xr4.shape, F32)
    scratch.append(pltpu.VMEM((n_slab, SUBLANES, LANES), F32))
    return pl.pallas_call(
        functools.partial(_rnn_kernel, reverse=reverse, final=final, nchunks=nchunks),
        grid=(bsz, nchunks),
        in_specs=in_specs,
        out_specs=out_spec,
        out_shape=out_shape,
        scratch_shapes=scratch,
        compiler_params=pltpu.CompilerParams(
            dimension_semantics=("arbitrary", "arbitrary"), vmem_limit_bytes=VMEM_LIMIT),
        name="rnn_bwd" if reverse else "rnn_fwd",
    )(*args)


def _gate_weights(w):
    per = GATE_BLK // RNN_BLOCK
    out = jnp.zeros((D_RNN // GATE_BLK, GATE_BLK, GATE_BLK), w.dtype)
    for n in range(N_RNN_BLOCKS):
        j, o = divmod(n, per)
        out = out.at[j, o * RNN_BLOCK:(o + 1) * RNN_BLOCK, o * RNN_BLOCK:(o + 1) * RNN_BLOCK].set(w[n])
    return out.astype(BF16)


def _attn_kernel(q_ref, k_ref, v_ref, bias_ref, o_ref, lse_ref, *, sub_len):
    i = pl.program_id(2)
    n_tiles = sub_len // Q_TILE
    lane = lax.broadcasted_iota(jnp.int32, (Q_TILE, LANES), 1)
    heads_per_slab = LANES // HEAD_DIM
    first_half = lane < HEAD_DIM
    for t in range(TILES_PER_STEP):
        rows = slice(t * Q_TILE, (t + 1) * Q_TILE)
        q = q_ref[0, rows, :]
        tile = i * TILES_PER_STEP + t
        variant = jnp.where(tile == 0, FIRST, jnp.where(tile == n_tiles - 1, LAST, INTERIOR))
        start = pl.multiple_of(jnp.clip(tile * Q_TILE - HALF_STEPS, 0, sub_len - K_WIN), K_TILE)
        kk = k_ref[0, pl.ds(start, K_WIN), :]
        vv = v_ref[0, pl.ds(start, K_WIN), :]
        for p in range(D_ATTN // LANES):
            sl = slice(p * LANES, (p + 1) * LANES)
            qp, kp, vp = q[:, sl], kk[:, sl], vv[:, sl]
            zero = jnp.zeros_like(qp)
            q2 = jnp.concatenate([jnp.where(first_half, qp, zero), jnp.where(first_half, zero, qp)], axis=0)
            s = lax.dot_general(q2, kp, (((1,), (1,)), ((), ())), preferred_element_type=F32)
            h0 = p * heads_per_slab
            s = s + bias_ref[variant, h0:h0 + heads_per_slab].reshape(heads_per_slab * Q_TILE, K_WIN)
            m = jnp.max(s, axis=-1, keepdims=True)
            e = jnp.exp2(s - m)
            l = jnp.sum(e, axis=-1, keepdims=True)
            o2 = jnp.dot(e.astype(BF16), vp, preferred_element_type=F32) * (1.0 / l)
            lse2 = jnp.broadcast_to(m + jnp.log(l) * LOG2E, (heads_per_slab * Q_TILE, LANES))
            o_ref[0, rows, sl] = jnp.where(first_half, o2[:Q_TILE], o2[Q_TILE:])
            lse_ref[0, rows, sl] = jnp.where(first_half, lse2[:Q_TILE], lse2[Q_TILE:])


def _attn_pattern(qc, kc, vc, bias, pat):
    bsz, dil, sub_len, _ = qc.shape
    q_rows = TILES_PER_STEP * Q_TILE
    qmap = lambda b, r, i: (b, r, i, 0)
    seq_map = lambda b, r, i: (b, r, 0, 0)
    in_specs = [
        pl.BlockSpec((1, None, q_rows, D_ATTN), qmap),
        pl.BlockSpec((1, None, sub_len, D_ATTN), seq_map),
        pl.BlockSpec((1, None, sub_len, D_ATTN), seq_map),
        pl.BlockSpec((None, 3, N_HEADS, Q_TILE, K_WIN), lambda b, r, i: (pat, 0, 0, 0, 0)),
    ]
    return pl.pallas_call(
        functools.partial(_attn_kernel, sub_len=sub_len),
        grid=(bsz, dil, sub_len // q_rows),
        in_specs=in_specs,
        out_specs=[pl.BlockSpec((1, None, q_rows, D_ATTN), qmap)] * 2,
        out_shape=[jax.ShapeDtypeStruct((bsz, dil, sub_len, D_ATTN), F32)] * 2,
        compiler_params=pltpu.CompilerParams(
            dimension_semantics=("arbitrary",) * 3, vmem_limit_bytes=ATTN_VMEM_LIMIT),
        name=f"attn_d{dil}",
    )(qc, kc, vc, bias)


FF_CHUNK = 1024


SUB_ROWS = 256


def _natural_order(ref, scr, u):
    dil = ref.shape[1]
    per = SUB_ROWS // dil
    if dil == 1:
        return ref[0, 0, u * per:(u + 1) * per, :]
    n_slab = D_ATTN // LANES
    for r in range(dil):
        for j in range(n_slab):
            scr[u, j, pl.ds(r, per, stride=dil), :] = ref[0, r, u * per:(u + 1) * per, j * LANES:(j + 1) * LANES]
    return jnp.concatenate([scr[u, j] for j in range(n_slab)], axis=-1)


def _out_mlp_kernel(x_ref, mr_ref, o1, o2, o3, l1, l2, l3, ga_ref, wo_ref, gm_ref, wu_ref, wd_ref,
                    gf_ref, out_ref, s_o2, s_o3, s_l2, s_l3, *, tm):
    n_sub = tm // SUB_ROWS

    def merge(u):
        a1 = _natural_order(l1, None, u)
        a2 = _natural_order(l2, s_l2, u)
        a3 = _natural_order(l3, s_l3, u)
        mx = jnp.maximum(jnp.maximum(a1, a2), a3)
        e1, e2, e3 = jnp.exp2(a1 - mx), jnp.exp2(a2 - mx), jnp.exp2(a3 - mx)
        y_attn = (e1 * _natural_order(o1, None, u) + e2 * _natural_order(o2, s_o2, u)
                  + e3 * _natural_order(o3, s_o3, u)) / (e1 + e2 + e3)
        return _rms(y_attn, ga_ref[...]).astype(BF16)

    def out_proj(u, mix_attn):
        rows = slice(u * SUB_ROWS, (u + 1) * SUB_ROWS)
        proj = jnp.dot(mr_ref[0, rows, :], wo_ref[0:D_RNN, :], preferred_element_type=F32)
        proj = proj + jnp.dot(mix_attn, wo_ref[D_RNN:D_RNN + D_ATTN, :], preferred_element_type=F32)
        x1 = proj + x_ref[0, rows, :]
        return x1, _rms(x1, gm_ref[...]).astype(BF16)

    def ff_chunk(h, c):
        z = jnp.dot(h, wu_ref[:, c * FF_CHUNK:(c + 1) * FF_CHUNK], preferred_element_type=F32)
        z = jnp.square(jnp.maximum(z, 0.0)).astype(BF16)
        return jnp.dot(z, wd_ref[c * FF_CHUNK:(c + 1) * FF_CHUNK, :], preferred_element_type=F32)

    def finish(u, ff, x1):
        rows = slice(u * SUB_ROWS, (u + 1) * SUB_ROWS)
        out_ref[0, rows, :] = _rms(ff + x1, gf_ref[...])

    n_chunk = D_FF // FF_CHUNK
    x1, h = out_proj(0, merge(0))
    pending = None
    for u in range(n_sub):
        ff = None
        nxt = None
        for c in range(n_chunk):
            d = ff_chunk(h, c)
            ff = d if ff is None else ff + d
            if c == 0 and pending is not None:
                finish(*pending)
            if c == 1 and u + 1 < n_sub:
                mix_next = merge(u + 1)
            if c == 2 and u + 1 < n_sub:
                nxt = out_proj(u + 1, mix_next)
        pending = (u, ff, x1)
        if nxt is not None:
            x1, h = nxt
    finish(*pending)


def _out_mlp(x3, mix_rnn, os_, lses, ga, wo, gm, wu, wd, gf, tm):
    bsz, s, _ = x3.shape
    row = lambda b, i: (b, i, 0)
    cls = lambda b, i: (b, 0, i, 0)
    const = lambda b, i: (0, 0)
    once = pl.Buffered(1)
    cls_specs = [pl.BlockSpec((1, dil, tm // dil, D_ATTN), cls) for _, dil in PATTERNS]
    in_specs = [pl.BlockSpec((1, tm, D_MODEL), row), pl.BlockSpec((1, tm, D_RNN), row)]
    in_specs += cls_specs + cls_specs
    in_specs += [
        pl.BlockSpec((1, D_ATTN), const),
        pl.BlockSpec((D_RNN + D_ATTN, D_MODEL), const, pipeline_mode=once),
        pl.BlockSpec((1, D_MODEL), const),
        pl.BlockSpec((D_MODEL, D_FF), const, pipeline_mode=once),
        pl.BlockSpec((D_FF, D_MODEL), const, pipeline_mode=once),
        pl.BlockSpec((1, D_MODEL), const),
    ]
    return pl.pallas_call(
        functools.partial(_out_mlp_kernel, tm=tm),
        grid=(bsz, s // tm),
        in_specs=in_specs,
        out_specs=pl.BlockSpec((1, tm, D_MODEL), row),
        out_shape=jax.ShapeDtypeStruct((bsz, s, D_MODEL), F32),
        scratch_shapes=[pltpu.VMEM((tm // SUB_ROWS, D_ATTN // LANES, SUB_ROWS, LANES), F32)] * 4,
        compiler_params=pltpu.CompilerParams(
            dimension_semantics=("arbitrary", "arbitrary"), vmem_limit_bytes=ATTN_VMEM_LIMIT),
        name="out_mlp",
    )(x3, mix_rnn, *os_, *lses, ga, wo, gm, wu, wd, gf)


def kernel(x, attn_norm_g, w_in, conv_w, conv_b, lru_wa_fwd, lru_ba_fwd, lru_wx_fwd, lru_bx_fwd, lru_lam_fwd, lru_wa_bwd, lru_ba_bwd, lru_wx_bwd, lru_bx_bwd, lru_lam_bwd, rel_bias, norm_rnn_g, norm_attn_g, w_out, mlp_norm_g, w_up, w_down, final_norm_g):
    depth = w_in.shape[0]
    assert depth == 1, "the final RMSNorm is fused into the single layer's last call"
    l = 0
    n_pat = len(PATTERNS)
    row = lambda v: v.reshape(1, -1)
    bias = _bias_tables(rel_bias)
    proj = _in_proj(x, row(attn_norm_g[l]), w_in[l].astype(BF16), tm=RNN_CHUNK)
    xr, gate = proj[0], proj[1]
    qs, ks, vs = (proj[2 + a * n_pat:2 + (a + 1) * n_pat] for a in range(3))
    fwd = (conv_w[l], row(conv_b[l]), _gate_weights(lru_wa_fwd[l]), _gate_weights(lru_wx_fwd[l]),
           row(lru_ba_fwd[l]), row(lru_bx_fwd[l]), row(lru_lam_fwd[l]), None)
    bwd = (conv_w[l], row(conv_b[l]), _gate_weights(lru_wa_bwd[l]), _gate_weights(lru_wx_bwd[l]),
           row(lru_ba_bwd[l]), row(lru_bx_bwd[l]), row(lru_lam_bwd[l]), row(norm_rnn_g[l]))
    h_f = _rnn_pass(xr, (), fwd, reverse=False, final=False)
    mix_rnn = _rnn_pass(xr, (h_f, gate), bwd, reverse=True, final=True)
    os_, lses = [], []
    for pat in range(n_pat):
        o, lse = _attn_pattern(qs[pat], ks[pat], vs[pat], bias, pat)
        os_.append(o)
        lses.append(lse)
    return _out_mlp(x, mix_rnn, os_, lses, row(norm_attn_g[l]), w_out[l].astype(BF16),
                    row(mlp_norm_g[l]), w_up[l].astype(BF16), w_down[l].astype(BF16),
                    row(final_norm_g), tm=512)
```

```python
import functools
import math

import numpy as np
import jax
import jax.numpy as jnp
from jax import lax
from jax.experimental import pallas as pl
from jax.experimental.pallas import tpu as pltpu

F32 = jnp.float32
BF16 = jnp.bfloat16

D_MODEL = 1024
D_RNN = 512
N_RNN_BLOCKS = 8
RNN_BLOCK = D_RNN // N_RNN_BLOCKS
CONV_WIDTH = 4
CONV_LEFT = 2
LRU_C = 8.0
N_HEADS = 8
HEAD_DIM = 64
D_ATTN = N_HEADS * HEAD_DIM
PATTERNS = ((128, 1), (512, 4), (2048, 16))
HALF_STEPS = 64
N_BUCKETS = 32
MAX_DISTANCE = 1024
D_IN = 2 * D_RNN + 3 * D_ATTN
D_FF = 4 * D_MODEL
EPS = 1e-6
NEG_INF = -1e30

LANES = 128
SUBLANES = 8
MXU_DIM = 256

Q_TILE = 128
K_TILE = 64
K_WIN = Q_TILE + 2 * HALF_STEPS
TILES_PER_STEP = 4
LOG2E = math.log2(math.e)
INTERIOR, FIRST, LAST = 0, 1, 2
GATE_BLK = MXU_DIM

VMEM_LIMIT = 48 * 1024 * 1024
ATTN_VMEM_LIMIT = 56 * 1024 * 1024


def _rms(x, g):
    ms = jnp.mean(x * x, axis=-1, keepdims=True)
    return x * lax.rsqrt(ms + EPS) * g


def _t5_bucket_np(rel):
    nb = N_BUCKETS // 2
    max_exact = nb // 2
    ret = np.where(rel > 0, nb, 0)
    n = np.abs(rel)
    nf = np.maximum(n, 1).astype(np.float32)
    large = max_exact + (np.log(nf / np.float32(max_exact)) / np.float32(math.log(MAX_DISTANCE / max_exact))
                         * np.float32(nb - max_exact)).astype(np.int32)
    large = np.minimum(large, nb - 1)
    return ret + np.where(n < max_exact, n, large)


def _bucket_index_tables():
    qi = np.arange(Q_TILE)[:, None]
    ci = np.arange(K_WIN)[None, :]
    step = ci - qi - HALF_STEPS
    tabs = []
    for _, dil in PATTERNS:
        b = _t5_bucket_np((step * dil).astype(np.int32))
        tabs.append(np.where(np.abs(step) <= HALF_STEPS, b, -1))
    return np.stack(tabs, axis=0).astype(np.int32)


def _bias_table_kernel(idx_ref, rb_ref, out_ref):
    idx = idx_ref[0]
    col = lax.broadcasted_iota(jnp.int32, idx.shape, 1)
    hit = [idx == b for b in range(N_BUCKETS)]
    for h in range(N_HEADS):
        acc = jnp.full(idx.shape, NEG_INF, F32)
        for b in range(N_BUCKETS):
            acc = jnp.where(hit[b], rb_ref[b, h] * LOG2E, acc)
        out_ref[0, INTERIOR, h] = acc
        out_ref[0, FIRST, h] = jnp.where(col < K_WIN - HALF_STEPS, pltpu.roll(acc, K_WIN - HALF_STEPS, 1), NEG_INF)
        out_ref[0, LAST, h] = jnp.where(col >= HALF_STEPS, pltpu.roll(acc, HALF_STEPS, 1), NEG_INF)


def _bias_tables(rel_bias):
    idx = jnp.asarray(_bucket_index_tables())
    n_pat = idx.shape[0]
    return pl.pallas_call(
        _bias_table_kernel,
        grid=(n_pat,),
        in_specs=[
            pl.BlockSpec((1, Q_TILE, K_WIN), lambda g: (g, 0, 0)),
            pl.BlockSpec(memory_space=pltpu.SMEM),
        ],
        out_specs=pl.BlockSpec((1, 3, N_HEADS, Q_TILE, K_WIN), lambda g: (g, 0, 0, 0, 0)),
        out_shape=jax.ShapeDtypeStruct((n_pat, 3, N_HEADS, Q_TILE, K_WIN), F32),
        name="bias_table",
    )(idx, rel_bias)


def _stripe_store(ref, p):
    for st in range(SUBLANES):
        for j in range(D_RNN // LANES):
            ref[0, j, pl.ds(st, STRIPE_ROWS, stride=SUBLANES), :] = (
                p[st * STRIPE_ROWS:(st + 1) * STRIPE_ROWS, j * LANES:(j + 1) * LANES])


PERM_ROWS = MXU_DIM


def _class_permutations():
    mats = []
    for _, dil in PATTERNS:
        if dil == 1:
            continue
        per = PERM_ROWS // dil
        p = np.zeros((PERM_ROWS, PERM_ROWS), np.float32)
        for r in range(dil):
            for m in range(per):
                p[r * per + m, m * dil + r] = 1.0
        mats.append(p)
    return np.stack(mats)


def _in_proj_kernel(x_ref, g_ref, w_ref, perm_ref, xr_ref, gate_ref, *rest, tm):
    n_pat = len(PATTERNS)
    outs = [rest[a * n_pat:(a + 1) * n_pat] for a in range(3)]
    h_scr = rest[3 * n_pat]
    h_scr[...] = _rms(x_ref[0], g_ref[...]).astype(BF16)

    def seg(lo, width):
        return jnp.dot(h_scr[...], w_ref[:, lo:lo + width], preferred_element_type=F32)

    for a, scale in enumerate((HEAD_DIM ** -0.5 * LOG2E, 1.0, 1.0)):
        p = seg(2 * D_RNN + a * D_ATTN, D_ATTN)
        if scale != 1.0:
            p = p * scale
        pb = p.astype(BF16)
        dilated = 0
        for (_, dil), o_ref in zip(PATTERNS, outs[a]):
            if dil == 1:
                o_ref[0, 0] = pb
                continue
            per = PERM_ROWS // dil
            for blk in range(tm // PERM_ROWS):
                y = jnp.dot(perm_ref[dilated], pb[blk * PERM_ROWS:(blk + 1) * PERM_ROWS, :],
                            preferred_element_type=F32).astype(BF16)
                for r in range(dil):
                    o_ref[0, r, blk * per:(blk + 1) * per, :] = y[r * per:(r + 1) * per, :]
            dilated += 1
    _stripe_store(xr_ref, seg(0, D_RNN))
    _stripe_store(gate_ref, seg(D_RNN, D_RNN))


def _in_proj(x3, g, w_in_bf, tm):
    assert tm == RNN_CHUNK
    bsz, s, _ = x3.shape
    row = lambda b, i: (b, i, 0)
    cls = lambda b, i: (b, 0, i, 0)
    const = lambda b, i: (0, 0)
    perms = jnp.asarray(_class_permutations(), BF16)
    qkv_specs, qkv_shapes = [], []
    for _ in range(3):
        for _, dil in PATTERNS:
            qkv_specs.append(pl.BlockSpec((1, dil, tm // dil, D_ATTN), cls))
            qkv_shapes.append(jax.ShapeDtypeStruct((bsz, dil, s // dil, D_ATTN), BF16))
    return pl.pallas_call(
        functools.partial(_in_proj_kernel, tm=tm),
        grid=(bsz, s // tm),
        in_specs=[
            pl.BlockSpec((1, tm, D_MODEL), row),
            pl.BlockSpec((1, D_MODEL), const),
            pl.BlockSpec((D_MODEL, D_IN), const),
            pl.BlockSpec(perms.shape, lambda b, i: (0, 0, 0)),
        ],
        out_specs=[pl.BlockSpec((1, D_RNN // LANES, tm, LANES), cls)] * 2 + qkv_specs,
        out_shape=[jax.ShapeDtypeStruct((bsz, D_RNN // LANES, s, LANES), F32)] * 2 + qkv_shapes,
        scratch_shapes=[pltpu.VMEM((tm, D_MODEL), BF16)],
        compiler_params=pltpu.CompilerParams(
            dimension_semantics=("arbitrary", "arbitrary"), vmem_limit_bytes=VMEM_LIMIT),
        name="in_proj",
    )(x3, g, w_in_bf, perms)


RNN_CHUNK = 512
STRIPE_ROWS = RNN_CHUNK // SUBLANES


def _scan_block(a, b, reverse):
    n = a.shape[0]
    row = lax.broadcasted_iota(jnp.int32, a.shape, 0)
    s = 1
    while s < n:
        if reverse:
            ra = pltpu.roll(a, n - s, 0)
            rb = pltpu.roll(b, n - s, 0)
            m = row < n - s
        else:
            ra = pltpu.roll(a, s, 0)
            rb = pltpu.roll(b, s, 0)
            m = row >= s
        b = jnp.where(m, a * rb + b, b)
        a = jnp.where(m, a * ra, a)
        s *= 2
    return a, b


def _rnn_kernel(*refs, reverse, final, nchunks):
    if final:
        (pa_ref, pb_ref, cur_ref, nx_ref, hf_ref, gate_ref, cw_ref, cb_ref, wa_ref, wx_ref,
         ba_ref, bx_ref, lam_ref, gn_ref, out_ref, a_scr, b_scr, h_scr, carry) = refs
    else:
        (pa_ref, pb_ref, cur_ref, nx_ref, cw_ref, cb_ref, wa_ref, wx_ref,
         ba_ref, bx_ref, lam_ref, out_ref, a_scr, b_scr, carry) = refs
        h_scr = None

    step = pl.program_id(1)
    chunk = (nchunks - 1 - step) if reverse else step
    n_slab = D_RNN // LANES
    nv = STRIPE_ROWS
    sub = lax.broadcasted_iota(jnp.int32, (SUBLANES, LANES), 0)
    top, bot = SUBLANES - 1, 0

    @pl.when(step == 0)
    def _():
        carry[...] = jnp.zeros_like(carry)

    xc = []
    for j in range(n_slab):
        lanes = slice(j * LANES, (j + 1) * LANES)
        x = cur_ref[0, j].reshape(nv, SUBLANES, LANES)
        pa = jnp.where(chunk > 0, pa_ref[0, j], 0.0)
        pb = jnp.where(chunk > 0, pb_ref[0, j], 0.0)
        nx = jnp.where(chunk < nchunks - 1, nx_ref[0, j], 0.0)
        xm2 = pltpu.roll(jnp.where(sub == top, pa, x[nv - 2]), 1, 0)
        xm1 = pltpu.roll(jnp.where(sub == top, pb, x[nv - 1]), 1, 0)
        xp1 = pltpu.roll(jnp.where(sub == bot, nx, x[0]), SUBLANES - 1, 0)
        xext = jnp.concatenate([xm2[None], xm1[None], x, xp1[None]], axis=0)
        acc = cb_ref[:, lanes] + xext[0:nv] * cw_ref[0:1, lanes]
        for k in range(1, CONV_WIDTH):
            acc = acc + xext[k:k + nv] * cw_ref[k:k + 1, lanes]
        xc.append(acc.reshape(RNN_CHUNK, LANES))

    nlam = -lam_ref[...]
    softplus = jnp.maximum(nlam, 0.0) + jnp.log1p(jnp.exp(-jnp.abs(nlam)))
    coef = -LRU_C * softplus
    slabs_per_blk = GATE_BLK // LANES
    for jj in range(D_RNN // GATE_BLK):
        blk = slice(jj * GATE_BLK, (jj + 1) * GATE_BLK)
        xj = jnp.concatenate(xc[jj * slabs_per_blk:(jj + 1) * slabs_per_blk], axis=-1)
        xjb = xj.astype(BF16)
        za = jnp.dot(xjb, wa_ref[jj], preferred_element_type=F32) + ba_ref[:, blk]
        zx = jnp.dot(xjb, wx_ref[jj], preferred_element_type=F32) + bx_ref[:, blk]
        r = 1.0 / (1.0 + jnp.exp(-za))
        i = 1.0 / (1.0 + jnp.exp(-zx))
        log_a = coef[:, blk] * r
        a = jnp.exp(log_a)
        b = jnp.sqrt(-jnp.tanh(log_a) * (1.0 + a * a)) * (i * xj)
        for t in range(slabs_per_blk):
            a_scr[jj * slabs_per_blk + t] = a[:, t * LANES:(t + 1) * LANES]
            b_scr[jj * slabs_per_blk + t] = b[:, t * LANES:(t + 1) * LANES]

    order = range(nv - 1, -1, -1) if reverse else range(nv)
    group = lambda v: slice(v * SUBLANES, (v + 1) * SUBLANES)

    h = [jnp.zeros((SUBLANES, LANES), F32)] * n_slab
    p = [jnp.ones((SUBLANES, LANES), F32)] * n_slab
    for v in order:
        for j in range(n_slab):
            av = a_scr[j, group(v), :]
            h[j] = av * h[j] + b_scr[j, group(v), :]
            p[j] = av * p[j]

    start = []
    for j in range(n_slab):
        pc, hc = _scan_block(p[j], h[j], reverse)
        prev = carry[j]
        if reverse:
            seed = jnp.broadcast_to(prev[bot:bot + 1, :], (SUBLANES, LANES))
            ends = hc + pc * seed
            start.append(jnp.where(sub == top, pltpu.roll(prev, SUBLANES - 1, 0),
                                   pltpu.roll(ends, SUBLANES - 1, 0)))
        else:
            seed = jnp.broadcast_to(prev[top:top + 1, :], (SUBLANES, LANES))
            ends = hc + pc * seed
            start.append(jnp.where(sub == bot, pltpu.roll(prev, 1, 0), pltpu.roll(ends, 1, 0)))
        carry[j] = ends

    dst = h_scr if final else None
    h = start
    for v in order:
        for j in range(n_slab):
            h[j] = a_scr[j, group(v), :] * h[j] + b_scr[j, group(v), :]
            if final:
                dst[j, group(v), :] = h[j]
            else:
                out_ref[0, j, group(v), :] = h[j]

    if final:
        ys = []
        for j in range(n_slab):
            g = gate_ref[0, j]
            gelu = g * (0.5 * (1.0 + jnp.tanh(math.sqrt(2.0 / math.pi) * (g + 0.044715 * (g * g * g)))))
            ys.append((hf_ref[0, j] + h_scr[j]) * gelu)
        ss = ys[0] * ys[0]
        for j in range(1, n_slab):
            ss = ss + ys[j] * ys[j]
        inv = lax.rsqrt(jnp.sum(ss, axis=-1, keepdims=True) * (1.0 / D_RNN) + EPS)
        for j in range(n_slab):
            lanes = slice(j * LANES, (j + 1) * LANES)
            h_scr[j] = ys[j] * inv * gn_ref[:, lanes]
        for st in range(SUBLANES):
            for j in range(n_slab):
                out_ref[0, st * nv:(st + 1) * nv, j * LANES:(j + 1) * LANES] = (
                    h_scr[j, pl.ds(st, nv, stride=SUBLANES), :].astype(BF16))


def _rnn_pass(xr4, extra, params, reverse, final):
    bsz, n_slab, s, _ = xr4.shape
    nchunks = s // RNN_CHUNK
    hb = RNN_CHUNK // SUBLANES
    nhalo = s // SUBLANES

    def cidx(c):
        return (nchunks - 1 - c) if reverse else c

    cur_map = lambda b, c: (b, 0, cidx(c), 0)
    pa_map = lambda b, c: (b, 0, jnp.maximum(cidx(c) * hb - 2, 0), 0)
    pb_map = lambda b, c: (b, 0, jnp.maximum(cidx(c) * hb - 1, 0), 0)
    nx_map = lambda b, c: (b, 0, jnp.minimum((cidx(c) + 1) * hb, nhalo - 1), 0)
    const2 = lambda b, c: (0, 0)
    const3 = lambda b, c: (0, 0, 0)

    cw, cb, wa, wx, ba, bx, lam, gn = params
    halo = (1, n_slab, SUBLANES, LANES)
    full = (1, n_slab, RNN_CHUNK, LANES)
    in_specs = [pl.BlockSpec(halo, pa_map), pl.BlockSpec(halo, pb_map), pl.BlockSpec(full, cur_map),
                pl.BlockSpec(halo, nx_map)]
    args = [xr4, xr4, xr4, xr4]
    if final:
        in_specs += [pl.BlockSpec(full, cur_map), pl.BlockSpec(full, cur_map)]
        args += list(extra)
    in_specs += [
        pl.BlockSpec((CONV_WIDTH, D_RNN), const2),
        pl.BlockSpec((1, D_RNN), const2),
        pl.BlockSpec((D_RNN // GATE_BLK, GATE_BLK, GATE_BLK), const3),
        pl.BlockSpec((D_RNN // GATE_BLK, GATE_BLK, GATE_BLK), const3),
        pl.BlockSpec((1, D_RNN), const2),
        pl.BlockSpec((1, D_RNN), const2),
        pl.BlockSpec((1, D_RNN), const2),
    ]
    args += [cw, cb, wa, wx, ba, bx, lam]
    slab_scr = pltpu.VMEM((n_slab, RNN_CHUNK, LANES), F32)
    scratch = [slab_scr, slab_scr]
    if final:
        in_specs.append(pl.BlockSpec((1, D_RNN), const2))
        args.append(gn)
        scratch.append(slab_scr)
        out_spec = pl.BlockSpec((1, RNN_CHUNK, D_RNN), lambda b, c: (b, cidx(c), 0))
        out_shape = jax.ShapeDtypeStruct((bsz, s, D_RNN), BF16)
    else:
        out_spec = pl.BlockSpec(full, cur_map)
        out_shape = jax.ShapeDtypeStruct(xr4.shape, F32)
    scratch.append(pltpu.VMEM((n_slab, SUBLANES, LANES), F32))
    return pl.pallas_call(
        functools.partial(_rnn_kernel, reverse=reverse, final=final, nchunks=nchunks),
        grid=(bsz, nchunks),
        in_specs=in_specs,
        out_specs=out_spec,
        out_shape=out_shape,
        scratch_shapes=scratch,
        compiler_params=pltpu.CompilerParams(
            dimension_semantics=("arbitrary", "arbitrary"), vmem_limit_bytes=VMEM_LIMIT),
        name="rnn_bwd" if reverse else "rnn_fwd",
    )(*args)


def _gate_weights(w):
    per = GATE_BLK // RNN_BLOCK
    out = jnp.zeros((D_RNN // GATE_BLK, GATE_BLK, GATE_BLK), w.dtype)
    for n in range(N_RNN_BLOCKS):
        j, o = divmod(n, per)
        out = out.at[j, o * RNN_BLOCK:(o + 1) * RNN_BLOCK, o * RNN_BLOCK:(o + 1) * RNN_BLOCK].set(w[n])
    return out.astype(BF16)


def _attn_kernel(q_ref, k_ref, v_ref, bias_ref, o_ref, lse_ref, *, sub_len):
    i = pl.program_id(2)
    n_tiles = sub_len // Q_TILE
    lane = lax.broadcasted_iota(jnp.int32, (Q_TILE, LANES), 1)
    heads_per_slab = LANES // HEAD_DIM
    first_half = lane < HEAD_DIM
    for t in range(TILES_PER_STEP):
        rows = slice(t * Q_TILE, (t + 1) * Q_TILE)
        q = q_ref[0, rows, :]
        tile = i * TILES_PER_STEP + t
        variant = jnp.where(tile == 0, FIRST, jnp.where(tile == n_tiles - 1, LAST, INTERIOR))
        start = pl.multiple_of(jnp.clip(tile * Q_TILE - HALF_STEPS, 0, sub_len - K_WIN), K_TILE)
        kk = k_ref[0, pl.ds(start, K_WIN), :]
        vv = v_ref[0, pl.ds(start, K_WIN), :]
        for p in range(D_ATTN // LANES):
            sl = slice(p * LANES, (p + 1) * LANES)
            qp, kp, vp = q[:, sl], kk[:, sl], vv[:, sl]
            zero = jnp.zeros_like(qp)
            q2 = jnp.concatenate([jnp.where(first_half, qp, zero), jnp.where(first_half, zero, qp)], axis=0)
            s = lax.dot_general(q2, kp, (((1,), (1,)), ((), ())), preferred_element_type=F32)
            h0 = p * heads_per_slab
            s = s + bias_ref[variant, h0:h0 + heads_per_slab].reshape(heads_per_slab * Q_TILE, K_WIN)
            m = jnp.max(s, axis=-1, keepdims=True)
            e = jnp.exp2(s - m)
            l = jnp.sum(e, axis=-1, keepdims=True)
            o2 = jnp.dot(e.astype(BF16), vp, preferred_element_type=F32) * (1.0 / l)
            lse2 = jnp.broadcast_to(m + jnp.log(l) * LOG2E, (heads_per_slab * Q_TILE, LANES))
            o_ref[0, rows, sl] = jnp.where(first_half, o2[:Q_TILE], o2[Q_TILE:])
            lse_ref[0, rows, sl] = jnp.where(first_half, lse2[:Q_TILE], lse2[Q_TILE:])


def _attn_pattern(qc, kc, vc, bias, pat):
    bsz, dil, sub_len, _ = qc.shape
    q_rows = TILES_PER_STEP * Q_TILE
    qmap = lambda b, r, i: (b, r, i, 0)
    seq_map = lambda b, r, i: (b, r, 0, 0)
    in_specs = [
        pl.BlockSpec((1, None, q_rows, D_ATTN), qmap),
        pl.BlockSpec((1, None, sub_len, D_ATTN), seq_map),
        pl.BlockSpec((1, None, sub_len, D_ATTN), seq_map),
        pl.BlockSpec((None, 3, N_HEADS, Q_TILE, K_WIN), lambda b, r, i: (pat, 0, 0, 0, 0)),
    ]
    return pl.pallas_call(
        functools.partial(_attn_kernel, sub_len=sub_len),
        grid=(bsz, dil, sub_len // q_rows),
        in_specs=in_specs,
        out_specs=[pl.BlockSpec((1, None, q_rows, D_ATTN), qmap)] * 2,
        out_shape=[jax.ShapeDtypeStruct((bsz, dil, sub_len, D_ATTN), F32)] * 2,
        compiler_params=pltpu.CompilerParams(
            dimension_semantics=("arbitrary",) * 3, vmem_limit_bytes=ATTN_VMEM_LIMIT),
        name=f"attn_d{dil}",
    )(qc, kc, vc, bias)


FF_CHUNK = 1024


SUB_ROWS = 256


def _natural_order(ref, scr, u):
    dil = ref.shape[1]
    per = SUB_ROWS // dil
    if dil == 1:
        return ref[0, 0, u * per:(u + 1) * per, :]
    n_slab = D_ATTN // LANES
    for r in range(dil):
        for j in range(n_slab):
            scr[u, j, pl.ds(r, per, stride=dil), :] = ref[0, r, u * per:(u + 1) * per, j * LANES:(j + 1) * LANES]
    return jnp.concatenate([scr[u, j] for j in range(n_slab)], axis=-1)


def _out_mlp_kernel(x_ref, mr_ref, o1, o2, o3, l1, l2, l3, ga_ref, wo_ref, gm_ref, wu_ref, wd_ref,
                    gf_ref, out_ref, s_o2, s_o3, s_l2, s_l3, *, tm):
    n_sub = tm // SUB_ROWS

    def merge(u):
        a1 = _natural_order(l1, None, u)
        a2 = _natural_order(l2, s_l2, u)
        a3 = _natural_order(l3, s_l3, u)
        mx = jnp.maximum(jnp.maximum(a1, a2), a3)
        e1, e2, e3 = jnp.exp2(a1 - mx), jnp.exp2(a2 - mx), jnp.exp2(a3 - mx)
        y_attn = (e1 * _natural_order(o1, None, u) + e2 * _natural_order(o2, s_o2, u)
                  + e3 * _natural_order(o3, s_o3, u)) / (e1 + e2 + e3)
        return _rms(y_attn, ga_ref[...]).astype(BF16)

    def out_proj(u, mix_attn):
        rows = slice(u * SUB_ROWS, (u + 1) * SUB_ROWS)
        proj = jnp.dot(mr_ref[0, rows, :], wo_ref[0:D_RNN, :], preferred_element_type=F32)
        proj = proj + jnp.dot(mix_attn, wo_ref[D_RNN:D_RNN + D_ATTN, :], preferred_element_type=F32)
        x1 = proj + x_ref[0, rows, :]
        return x1, _rms(x1, gm_ref[...]).astype(BF16)

    def ff_chunk(h, c):
        z = jnp.dot(h, wu_ref[:, c * FF_CHUNK:(c + 1) * FF_CHUNK], preferred_element_type=F32)
        z = jnp.square(jnp.maximum(z, 0.0)).astype(BF16)
        return jnp.dot(z, wd_ref[c * FF_CHUNK:(c + 1) * FF_CHUNK, :], preferred_element_type=F32)

    def finish(u, ff, x1):
        rows = slice(u * SUB_ROWS, (u + 1) * SUB_ROWS)
        out_ref[0, rows, :] = _rms(ff + x1, gf_ref[...])

    n_chunk = D_FF // FF_CHUNK
    x1, h = out_proj(0, merge(0))
    pending = None
    for u in range(n_sub):
        ff = None
        nxt = None
        for c in range(n_chunk):
            d = ff_chunk(h, c)
            ff = d if ff is None else ff + d
            if c == 0 and pending is not None:
                finish(*pending)
            if c == 1 and u + 1 < n_sub:
                mix_next = merge(u + 1)
            if c == 2 and u + 1 < n_sub:
                nxt = out_proj(u + 1, mix_next)
        pending = (u, ff, x1)
        if nxt is not None:
            x1, h = nxt
    finish(*pending)


def _out_mlp(x3, mix_rnn, os_, lses, ga, wo, gm, wu, wd, gf, tm):
    bsz, s, _ = x3.shape
    row = lambda b, i: (b, i, 0)
    cls = lambda b, i: (b, 0, i, 0)
    const = lambda b, i: (0, 0)
    once = pl.Buffered(1)
    cls_specs = [pl.BlockSpec((1, dil, tm // dil, D_ATTN), cls) for _, dil in PATTERNS]
    in_specs = [pl.BlockSpec((1, tm, D_MODEL), row), pl.BlockSpec((1, tm, D_RNN), row)]
    in_specs += cls_specs + cls_specs
    in_specs += [
        pl.BlockSpec((1, D_ATTN), const),
        pl.BlockSpec((D_RNN + D_ATTN, D_MODEL), const, pipeline_mode=once),
        pl.BlockSpec((1, D_MODEL), const),
        pl.BlockSpec((D_MODEL, D_FF), const, pipeline_mode=once),
        pl.BlockSpec((D_FF, D_MODEL), const, pipeline_mode=once),
        pl.BlockSpec((1, D_MODEL), const),
    ]
    return pl.pallas_call(
        functools.partial(_out_mlp_kernel, tm=tm),
        grid=(bsz, s // tm),
        in_specs=in_specs,
        out_specs=pl.BlockSpec((1, tm, D_MODEL), row),
        out_shape=jax.ShapeDtypeStruct((bsz, s, D_MODEL), F32),
        scratch_shapes=[pltpu.VMEM((tm // SUB_ROWS, D_ATTN // LANES, SUB_ROWS, LANES), F32)] * 4,
        compiler_params=pltpu.CompilerParams(
            dimension_semantics=("arbitrary", "arbitrary"), vmem_limit_bytes=ATTN_VMEM_LIMIT),
        name="out_mlp",
    )(x3, mix_rnn, *os_, *lses, ga, wo, gm, wu, wd, gf)


def kernel(x, attn_norm_g, w_in, conv_w, conv_b, lru_wa_fwd, lru_ba_fwd, lru_wx_fwd, lru_bx_fwd, lru_lam_fwd, lru_wa_bwd, lru_ba_bwd, lru_wx_bwd, lru_bx_bwd, lru_lam_bwd, rel_bias, norm_rnn_g, norm_attn_g, w_out, mlp_norm_g, w_up, w_down, final_norm_g):
    depth = w_in.shape[0]
    assert depth == 1, "the final RMSNorm is fused into the single layer's last call"
    l = 0
    n_pat = len(PATTERNS)
    row = lambda v: v.reshape(1, -1)
    bias = _bias_tables(rel_bias)
    proj = _in_proj(x, row(attn_norm_g[l]), w_in[l].astype(BF16), tm=RNN_CHUNK)
    xr, gate = proj[0], proj[1]
    qs, ks, vs = (proj[2 + a * n_pat:2 + (a + 1) * n_pat] for a in range(3))
    fwd = (conv_w[l], row(conv_b[l]), _gate_weights(lru_wa_fwd[l]), _gate_weights(lru_wx_fwd[l]),
           row(lru_ba_fwd[l]), row(lru_bx_fwd[l]), row(lru_lam_fwd[l]), None)
    bwd = (conv_w[l], row(conv_b[l]), _gate_weights(lru_wa_bwd[l]), _gate_weights(lru_wx_bwd[l]),
           row(lru_ba_bwd[l]), row(lru_bx_bwd[l]), row(lru_lam_bwd[l]), row(norm_rnn_g[l]))
    h_f = _rnn_pass(xr, (), fwd, reverse=False, final=False)
    mix_rnn = _rnn_pass(xr, (h_f, gate), bwd, reverse=True, final=True)
    os_, lses = [], []
    for pat in range(n_pat):
        o, lse = _attn_pattern(qs[pat], ks[pat], vs[pat], bias, pat)
        os_.append(o)
        lses.append(lse)
    return _out_mlp(x, mix_rnn, os_, lses, row(norm_attn_g[l]), w_out[l].astype(BF16),
                    row(mlp_norm_g[l]), w_up[l].astype(BF16), w_down[l].astype(BF16),
                    row(final_norm_g), tm=512)
```

```python
import functools
import math

import numpy as np
import jax
import jax.numpy as jnp
from jax import lax
from jax.experimental import pallas as pl
from jax.experimental.pallas import tpu as pltpu

F32 = jnp.float32
BF16 = jnp.bfloat16

D_MODEL = 1024
D_RNN = 512
N_RNN_BLOCKS = 8
RNN_BLOCK = D_RNN // N_RNN_BLOCKS
CONV_WIDTH = 4
CONV_LEFT = 2
LRU_C = 8.0
N_HEADS = 8
HEAD_DIM = 64
D_ATTN = N_HEADS * HEAD_DIM
PATTERNS = ((128, 1), (512, 4), (2048, 16))
HALF_STEPS = 64
N_BUCKETS = 32
MAX_DISTANCE = 1024
D_IN = 2 * D_RNN + 3 * D_ATTN
D_FF = 4 * D_MODEL
EPS = 1e-6
NEG_INF = -1e30

LANES = 128
SUBLANES = 8
MXU_DIM = 256

Q_TILE = 128
K_TILE = 64
K_WIN = Q_TILE + 2 * HALF_STEPS
MAX_TILES_PER_STEP = 8
LOG2E = math.log2(math.e)
INTERIOR, FIRST, LAST = 0, 1, 2
GATE_BLK = MXU_DIM

VMEM_LIMIT = 48 * 1024 * 1024
ATTN_VMEM_LIMIT = 56 * 1024 * 1024


def _rms(x, g):
    ms = jnp.mean(x * x, axis=-1, keepdims=True)
    return x * lax.rsqrt(ms + EPS) * g


def _t5_bucket_np(rel):
    nb = N_BUCKETS // 2
    max_exact = nb // 2
    ret = np.where(rel > 0, nb, 0)
    n = np.abs(rel)
    nf = np.maximum(n, 1).astype(np.float32)
    large = max_exact + (np.log(nf / np.float32(max_exact)) / np.float32(math.log(MAX_DISTANCE / max_exact))
                         * np.float32(nb - max_exact)).astype(np.int32)
    large = np.minimum(large, nb - 1)
    return ret + np.where(n < max_exact, n, large)


def _bucket_index_tables():
    qi = np.arange(Q_TILE)[:, None]
    ci = np.arange(K_WIN)[None, :]
    step = ci - qi - HALF_STEPS
    tabs = []
    for _, dil in PATTERNS:
        b = _t5_bucket_np((step * dil).astype(np.int32))
        tabs.append(np.where(np.abs(step) <= HALF_STEPS, b, -1))
    return np.stack(tabs, axis=0).astype(np.int32)


def _bias_table_kernel(idx_ref, rb_ref, out_ref):
    idx = idx_ref[0]
    col = lax.broadcasted_iota(jnp.int32, idx.shape, 1)
    hit = [idx == b for b in range(N_BUCKETS)]
    for h in range(N_HEADS):
        acc = jnp.full(idx.shape, NEG_INF, F32)
        for b in range(N_BUCKETS):
            acc = jnp.where(hit[b], rb_ref[b, h] * LOG2E, acc)
        out_ref[0, INTERIOR, h] = acc
        out_ref[0, FIRST, h] = jnp.where(col < K_WIN - HALF_STEPS, pltpu.roll(acc, K_WIN - HALF_STEPS, 1), NEG_INF)
        out_ref[0, LAST, h] = jnp.where(col >= HALF_STEPS, pltpu.roll(acc, HALF_STEPS, 1), NEG_INF)


def _bias_tables(rel_bias):
    idx = jnp.asarray(_bucket_index_tables())
    n_pat = idx.shape[0]
    return pl.pallas_call(
        _bias_table_kernel,
        grid=(n_pat,),
        in_specs=[
            pl.BlockSpec((1, Q_TILE, K_WIN), lambda g: (g, 0, 0)),
            pl.BlockSpec(memory_space=pltpu.SMEM),
        ],
        out_specs=pl.BlockSpec((1, 3, N_HEADS, Q_TILE, K_WIN), lambda g: (g, 0, 0, 0, 0)),
        out_shape=jax.ShapeDtypeStruct((n_pat, 3, N_HEADS, Q_TILE, K_WIN), F32),
        name="bias_table",
    )(idx, rel_bias)


def _stripe_store(ref, p):
    for st in range(SUBLANES):
        for j in range(D_RNN // LANES):
            ref[0, j, pl.ds(st, STRIPE_ROWS, stride=SUBLANES), :] = (
                p[st * STRIPE_ROWS:(st + 1) * STRIPE_ROWS, j * LANES:(j + 1) * LANES])


PERM_ROWS = MXU_DIM


def _class_permutations():
    mats = []
    for _, dil in PATTERNS:
        if dil == 1:
            continue
        per = PERM_ROWS // dil
        p = np.zeros((PERM_ROWS, PERM_ROWS), np.float32)
        for r in range(dil):
            for m in range(per):
                p[r * per + m, m * dil + r] = 1.0
        mats.append(p)
    return np.stack(mats)


def _in_proj_kernel(x_ref, g_ref, w_ref, perm_ref, xr_ref, gate_ref, *rest, tm):
    n_pat = len(PATTERNS)
    outs = [rest[a * n_pat:(a + 1) * n_pat] for a in range(3)]
    h_scr = rest[3 * n_pat]
    h_scr[...] = _rms(x_ref[0], g_ref[...]).astype(BF16)

    def seg(lo, width):
        return jnp.dot(h_scr[...], w_ref[:, lo:lo + width], preferred_element_type=F32)

    for a, scale in enumerate((HEAD_DIM ** -0.5 * LOG2E, 1.0, 1.0)):
        p = seg(2 * D_RNN + a * D_ATTN, D_ATTN)
        if scale != 1.0:
            p = p * scale
        pb = p.astype(BF16)
        dilated = 0
        for (_, dil), o_ref in zip(PATTERNS, outs[a]):
            if dil == 1:
                o_ref[0, 0] = pb
                continue
            per = PERM_ROWS // dil
            for blk in range(tm // PERM_ROWS):
                y = jnp.dot(perm_ref[dilated], pb[blk * PERM_ROWS:(blk + 1) * PERM_ROWS, :],
                            preferred_element_type=F32).astype(BF16)
                for r in range(dil):
                    o_ref[0, r, blk * per:(blk + 1) * per, :] = y[r * per:(r + 1) * per, :]
            dilated += 1
    _stripe_store(xr_ref, seg(0, D_RNN))
    _stripe_store(gate_ref, seg(D_RNN, D_RNN))


def _in_proj(x3, g, w_in_bf, tm):
    assert tm == RNN_CHUNK
    bsz, s, _ = x3.shape
    row = lambda b, i: (b, i, 0)
    cls = lambda b, i: (b, 0, i, 0)
    const = lambda b, i: (0, 0)
    perms = jnp.asarray(_class_permutations(), BF16)
    qkv_specs, qkv_shapes = [], []
    for _ in range(3):
        for _, dil in PATTERNS:
            qkv_specs.append(pl.BlockSpec((1, dil, tm // dil, D_ATTN), cls))
            qkv_shapes.append(jax.ShapeDtypeStruct((bsz, dil, s // dil, D_ATTN), BF16))
    return pl.pallas_call(
        functools.partial(_in_proj_kernel, tm=tm),
        grid=(bsz, s // tm),
        in_specs=[
            pl.BlockSpec((1, tm, D_MODEL), row),
            pl.BlockSpec((1, D_MODEL), const),
            pl.BlockSpec((D_MODEL, D_IN), const),
            pl.BlockSpec(perms.shape, lambda b, i: (0, 0, 0)),
        ],
        out_specs=[pl.BlockSpec((1, D_RNN // LANES, tm, LANES), cls)] * 2 + qkv_specs,
        out_shape=[jax.ShapeDtypeStruct((bsz, D_RNN // LANES, s, LANES), F32)] * 2 + qkv_shapes,
        scratch_shapes=[pltpu.VMEM((tm, D_MODEL), BF16)],
        compiler_params=pltpu.CompilerParams(
            dimension_semantics=("arbitrary", "arbitrary"), vmem_limit_bytes=VMEM_LIMIT),
        name="in_proj",
    )(x3, g, w_in_bf, perms)


RNN_CHUNK = 512
STRIPE_ROWS = RNN_CHUNK // SUBLANES


def _scan_block(a, b, reverse):
    n = a.shape[0]
    row = lax.broadcasted_iota(jnp.int32, a.shape, 0)
    s = 1
    while s < n:
        if reverse:
            ra = pltpu.roll(a, n - s, 0)
            rb = pltpu.roll(b, n - s, 0)
            m = row < n - s
        else:
            ra = pltpu.roll(a, s, 0)
            rb = pltpu.roll(b, s, 0)
            m = row >= s
        b = jnp.where(m, a * rb + b, b)
        a = jnp.where(m, a * ra, a)
        s *= 2
    return a, b


def _rnn_kernel(*refs, reverse, final, nchunks):
    if final:
        (pa_ref, pb_ref, cur_ref, nx_ref, hf_ref, gate_ref, cw_ref, cb_ref, wa_ref, wx_ref,
         ba_ref, bx_ref, lam_ref, gn_ref, out_ref, a_scr, b_scr, h_scr, carry) = refs
    else:
        (pa_ref, pb_ref, cur_ref, nx_ref, cw_ref, cb_ref, wa_ref, wx_ref,
         ba_ref, bx_ref, lam_ref, out_ref, a_scr, b_scr, carry) = refs
        h_scr = None

    step = pl.program_id(1)
    chunk = (nchunks - 1 - step) if reverse else step
    n_slab = D_RNN // LANES
    nv = STRIPE_ROWS
    sub = lax.broadcasted_iota(jnp.int32, (SUBLANES, LANES), 0)
    top, bot = SUBLANES - 1, 0

    @pl.when(step == 0)
    def _():
        carry[...] = jnp.zeros_like(carry)

    xc = []
    for j in range(n_slab):
        lanes = slice(j * LANES, (j + 1) * LANES)
        x = cur_ref[0, j].reshape(nv, SUBLANES, LANES)
        pa = jnp.where(chunk > 0, pa_ref[0, j], 0.0)
        pb = jnp.where(chunk > 0, pb_ref[0, j], 0.0)
        nx = jnp.where(chunk < nchunks - 1, nx_ref[0, j], 0.0)
        xm2 = pltpu.roll(jnp.where(sub == top, pa, x[nv - 2]), 1, 0)
        xm1 = pltpu.roll(jnp.where(sub == top, pb, x[nv - 1]), 1, 0)
        xp1 = pltpu.roll(jnp.where(sub == bot, nx, x[0]), SUBLANES - 1, 0)
        xext = jnp.concatenate([xm2[None], xm1[None], x, xp1[None]], axis=0)
        acc = cb_ref[:, lanes] + xext[0:nv] * cw_ref[0:1, lanes]
        for k in range(1, CONV_WIDTH):
            acc = acc + xext[k:k + nv] * cw_ref[k:k + 1, lanes]
        xc.append(acc.reshape(RNN_CHUNK, LANES))

    nlam = -lam_ref[...]
    softplus = jnp.maximum(nlam, 0.0) + jnp.log1p(jnp.exp(-jnp.abs(nlam)))
    coef = -LRU_C * softplus
    slabs_per_blk = GATE_BLK // LANES
    for jj in range(D_RNN // GATE_BLK):
        blk = slice(jj * GATE_BLK, (jj + 1) * GATE_BLK)
        xj = jnp.concatenate(xc[jj * slabs_per_blk:(jj + 1) * slabs_per_blk], axis=-1)
        xjb = xj.astype(BF16)
        za = jnp.dot(xjb, wa_ref[jj], preferred_element_type=F32) + ba_ref[:, blk]
        zx = jnp.dot(xjb, wx_ref[jj], preferred_element_type=F32) + bx_ref[:, blk]
        r = 1.0 / (1.0 + jnp.exp(-za))
        i = 1.0 / (1.0 + jnp.exp(-zx))
        log_a = coef[:, blk] * r
        a = jnp.exp(log_a)
        b = jnp.sqrt(-jnp.tanh(log_a) * (1.0 + a * a)) * (i * xj)
        for t in range(slabs_per_blk):
            a_scr[jj * slabs_per_blk + t] = a[:, t * LANES:(t + 1) * LANES]
            b_scr[jj * slabs_per_blk + t] = b[:, t * LANES:(t + 1) * LANES]

    order = range(nv - 1, -1, -1) if reverse else range(nv)
    group = lambda v: slice(v * SUBLANES, (v + 1) * SUBLANES)

    h = [jnp.zeros((SUBLANES, LANES), F32)] * n_slab
    p = [jnp.ones((SUBLANES, LANES), F32)] * n_slab
    for v in order:
        for j in range(n_slab):
            av = a_scr[j, group(v), :]
            h[j] = av * h[j] + b_scr[j, group(v), :]
            p[j] = av * p[j]

    start = []
    for j in range(n_slab):
        pc, hc = _scan_block(p[j], h[j], reverse)
        prev = carry[j]
        if reverse:
            seed = jnp.broadcast_to(prev[bot:bot + 1, :], (SUBLANES, LANES))
            ends = hc + pc * seed
            start.append(jnp.where(sub == top, pltpu.roll(prev, SUBLANES - 1, 0),
                                   pltpu.roll(ends, SUBLANES - 1, 0)))
        else:
            seed = jnp.broadcast_to(prev[top:top + 1, :], (SUBLANES, LANES))
            ends = hc + pc * seed
            start.append(jnp.where(sub == bot, pltpu.roll(prev, 1, 0), pltpu.roll(ends, 1, 0)))
        carry[j] = ends

    dst = h_scr if final else None
    h = start
    for v in order:
        for j in range(n_slab):
            h[j] = a_scr[j, group(v), :] * h[j] + b_scr[j, group(v), :]
            if final:
                dst[j, group(v), :] = h[j]
            else:
                out_ref[0, j, group(v), :] = h[j]

    if final:
        ys = []
        for j in range(n_slab):
            g = gate_ref[0, j]
            gelu = g * (0.5 * (1.0 + jnp.tanh(math.sqrt(2.0 / math.pi) * (g + 0.044715 * (g * g * g)))))
            ys.append((hf_ref[0, j] + h_scr[j]) * gelu)
        ss = ys[0] * ys[0]
        for j in range(1, n_slab):
            ss = ss + ys[j] * ys[j]
        inv = lax.rsqrt(jnp.sum(ss, axis=-1, keepdims=True) * (1.0 / D_RNN) + EPS)
        for j in range(n_slab):
            lanes = slice(j * LANES, (j + 1) * LANES)
            h_scr[j] = ys[j] * inv * gn_ref[:, lanes]
        for st in range(SUBLANES):
            for j in range(n_slab):
                out_ref[0, st * nv:(st + 1) * nv, j * LANES:(j + 1) * LANES] = (
                    h_scr[j, pl.ds(st, nv, stride=SUBLANES), :].astype(BF16))


def _rnn_pass(xr4, extra, params, reverse, final):
    bsz, n_slab, s, _ = xr4.shape
    nchunks = s // RNN_CHUNK
    hb = RNN_CHUNK // SUBLANES
    nhalo = s // SUBLANES

    def cidx(c):
        return (nchunks - 1 - c) if reverse else c

    cur_map = lambda b, c: (b, 0, cidx(c), 0)
    pa_map = lambda b, c: (b, 0, jnp.maximum(cidx(c) * hb - 2, 0), 0)
    pb_map = lambda b, c: (b, 0, jnp.maximum(cidx(c) * hb - 1, 0), 0)
    nx_map = lambda b, c: (b, 0, jnp.minimum((cidx(c) + 1) * hb, nhalo - 1), 0)
    const2 = lambda b, c: (0, 0)
    const3 = lambda b, c: (0, 0, 0)

    cw, cb, wa, wx, ba, bx, lam, gn = params
    halo = (1, n_slab, SUBLANES, LANES)
    full = (1, n_slab, RNN_CHUNK, LANES)
    in_specs = [pl.BlockSpec(halo, pa_map), pl.BlockSpec(halo, pb_map), pl.BlockSpec(full, cur_map),
                pl.BlockSpec(halo, nx_map)]
    args = [xr4, xr4, xr4, xr4]
    if final:
        in_specs += [pl.BlockSpec(full, cur_map), pl.BlockSpec(full, cur_map)]
        args += list(extra)
    in_specs += [
        pl.BlockSpec((CONV_WIDTH, D_RNN), const2),
        pl.BlockSpec((1, D_RNN), const2),
        pl.BlockSpec((D_RNN // GATE_BLK, GATE_BLK, GATE_BLK), const3),
        pl.BlockSpec((D_RNN // GATE_BLK, GATE_BLK, GATE_BLK), const3),
        pl.BlockSpec((1, D_RNN), const2),
        pl.BlockSpec((1, D_RNN), const2),
        pl.BlockSpec((1, D_RNN), const2),
    ]
    args += [cw, cb, wa, wx, ba, bx, lam]
    slab_scr = pltpu.VMEM((n_slab, RNN_CHUNK, LANES), F32)
    scratch = [slab_scr, slab_scr]
    if final:
        in_specs.append(pl.BlockSpec((1, D_RNN), const2))
        args.append(gn)
        scratch.append(slab_scr)
        out_spec = pl.BlockSpec((1, RNN_CHUNK, D_RNN), lambda b, c: (b, cidx(c), 0))
        out_shape = jax.ShapeDtypeStruct((bsz, s, D_RNN), BF16)
    else:
        out_spec = pl.BlockSpec(full, cur_map)
        out_shape = jax.ShapeDtypeStruct(xr4.shape, F32)
    scratch.append(pltpu.VMEM((n_slab, SUBLANES, LANES), F32))
    return pl.pallas_call(
        functools.partial(_rnn_kernel, reverse=reverse, final=final, nchunks=nchunks),
        grid=(bsz, nchunks),
        in_specs=in_specs,
        out_specs=out_spec,
        out_shape=out_shape,
        scratch_shapes=scratch,
        compiler_params=pltpu.CompilerParams(
            dimension_semantics=("arbitrary", "arbitrary"), vmem_limit_bytes=VMEM_LIMIT),
        name="rnn_bwd" if reverse else "rnn_fwd",
    )(*args)


def _gate_weights(w):
    per = GATE_BLK // RNN_BLOCK
    w5 = w.reshape(D_RNN // GATE_BLK, per, RNN_BLOCK, 1, RNN_BLOCK)
    on_diag = jnp.asarray(np.eye(per, dtype=bool)).reshape(1, per, 1, per, 1)
    dense = jnp.where(on_diag, w5, 0.0)
    return dense.reshape(D_RNN // GATE_BLK, GATE_BLK, GATE_BLK).astype(BF16)


def _attn_kernel(q_ref, k_ref, v_ref, bias_ref, o_ref, lse_ref, *, sub_len, tiles_per_step):
    i = pl.program_id(2)
    n_tiles = sub_len // Q_TILE
    lane = lax.broadcasted_iota(jnp.int32, (Q_TILE, LANES), 1)
    heads_per_slab = LANES // HEAD_DIM
    first_half = lane < HEAD_DIM
    for t in range(tiles_per_step):
        rows = slice(t * Q_TILE, (t + 1) * Q_TILE)
        q = q_ref[0, rows, :]
        tile = i * tiles_per_step + t
        variant = jnp.where(tile == 0, FIRST, jnp.where(tile == n_tiles - 1, LAST, INTERIOR))
        start = pl.multiple_of(jnp.clip(tile * Q_TILE - HALF_STEPS, 0, sub_len - K_WIN), K_TILE)
        kk = k_ref[0, pl.ds(start, K_WIN), :]
        vv = v_ref[0, pl.ds(start, K_WIN), :]
        for p in range(D_ATTN // LANES):
            sl = slice(p * LANES, (p + 1) * LANES)
            qp, kp, vp = q[:, sl], kk[:, sl], vv[:, sl]
            zero = jnp.zeros_like(qp)
            q2 = jnp.concatenate([jnp.where(first_half, qp, zero), jnp.where(first_half, zero, qp)], axis=0)
            s = lax.dot_general(q2, kp, (((1,), (1,)), ((), ())), preferred_element_type=F32)
            h0 = p * heads_per_slab
            s = s + bias_ref[variant, h0:h0 + heads_per_slab].reshape(heads_per_slab * Q_TILE, K_WIN)
            m = jnp.max(s, axis=-1, keepdims=True)
            e = jnp.exp2(s - m)
            l = jnp.sum(e, axis=-1, keepdims=True)
            o2 = jnp.dot(e.astype(BF16), vp, preferred_element_type=F32) * (1.0 / l)
            lse2 = jnp.broadcast_to(m + jnp.log(l) * LOG2E, (heads_per_slab * Q_TILE, LANES))
            o_ref[0, rows, sl] = jnp.where(first_half, o2[:Q_TILE], o2[Q_TILE:])
            lse_ref[0, rows, sl] = jnp.where(first_half, lse2[:Q_TILE], lse2[Q_TILE:])


def _attn_pattern(qc, kc, vc, bias, pat):
    bsz, dil, sub_len, _ = qc.shape
    tiles_per_step = min(MAX_TILES_PER_STEP, sub_len // Q_TILE)
    q_rows = tiles_per_step * Q_TILE
    qmap = lambda b, r, i: (b, r, i, 0)
    seq_map = lambda b, r, i: (b, r, 0, 0)
    in_specs = [
        pl.BlockSpec((1, None, q_rows, D_ATTN), qmap),
        pl.BlockSpec((1, None, sub_len, D_ATTN), seq_map),
        pl.BlockSpec((1, None, sub_len, D_ATTN), seq_map),
        pl.BlockSpec((None, 3, N_HEADS, Q_TILE, K_WIN), lambda b, r, i: (pat, 0, 0, 0, 0)),
    ]
    return pl.pallas_call(
        functools.partial(_attn_kernel, sub_len=sub_len, tiles_per_step=tiles_per_step),
        grid=(bsz, dil, sub_len // q_rows),
        in_specs=in_specs,
        out_specs=[pl.BlockSpec((1, None, q_rows, D_ATTN), qmap)] * 2,
        out_shape=[jax.ShapeDtypeStruct((bsz, dil, sub_len, D_ATTN), F32)] * 2,
        compiler_params=pltpu.CompilerParams(
            dimension_semantics=("arbitrary",) * 3, vmem_limit_bytes=ATTN_VMEM_LIMIT),
        name=f"attn_d{dil}",
    )(qc, kc, vc, bias)


FF_CHUNK = 1024


SUB_ROWS = 256


def _natural_order(ref, scr, u):
    dil = ref.shape[1]
    per = SUB_ROWS // dil
    if dil == 1:
        return ref[0, 0, u * per:(u + 1) * per, :]
    n_slab = D_ATTN // LANES
    for r in range(dil):
        for j in range(n_slab):
            scr[u, j, pl.ds(r, per, stride=dil), :] = ref[0, r, u * per:(u + 1) * per, j * LANES:(j + 1) * LANES]
    return jnp.concatenate([scr[u, j] for j in range(n_slab)], axis=-1)


def _out_mlp_kernel(x_ref, mr_ref, o1, o2, o3, l1, l2, l3, ga_ref, wo_ref, gm_ref, wu_ref, wd_ref,
                    gf_ref, out_ref, s_o2, s_o3, s_l2, s_l3, *, tm):
    n_sub = tm // SUB_ROWS

    def merge(u):
        a1 = _natural_order(l1, None, u)
        a2 = _natural_order(l2, s_l2, u)
        a3 = _natural_order(l3, s_l3, u)
        mx = jnp.maximum(jnp.maximum(a1, a2), a3)
        e1, e2, e3 = jnp.exp2(a1 - mx), jnp.exp2(a2 - mx), jnp.exp2(a3 - mx)
        y_attn = (e1 * _natural_order(o1, None, u) + e2 * _natural_order(o2, s_o2, u)
                  + e3 * _natural_order(o3, s_o3, u)) / (e1 + e2 + e3)
        return _rms(y_attn, ga_ref[...]).astype(BF16)

    def out_proj(u, mix_attn):
        rows = slice(u * SUB_ROWS, (u + 1) * SUB_ROWS)
        proj = jnp.dot(mr_ref[0, rows, :], wo_ref[0:D_RNN, :], preferred_element_type=F32)
        proj = proj + jnp.dot(mix_attn, wo_ref[D_RNN:D_RNN + D_ATTN, :], preferred_element_type=F32)
        x1 = proj + x_ref[0, rows, :]
        return x1, _rms(x1, gm_ref[...]).astype(BF16)

    def ff_chunk(h, c):
        z = jnp.dot(h, wu_ref[:, c * FF_CHUNK:(c + 1) * FF_CHUNK], preferred_element_type=F32)
        z = jnp.square(jnp.maximum(z, 0.0)).astype(BF16)
        return jnp.dot(z, wd_ref[c * FF_CHUNK:(c + 1) * FF_CHUNK, :], preferred_element_type=F32)

    def finish(u, ff, x1):
        rows = slice(u * SUB_ROWS, (u + 1) * SUB_ROWS)
        out_ref[0, rows, :] = _rms(ff + x1, gf_ref[...])

    n_chunk = D_FF // FF_CHUNK
    x1, h = out_proj(0, merge(0))
    pending = None
    for u in range(n_sub):
        ff = None
        nxt = None
        for c in range(n_chunk):
            d = ff_chunk(h, c)
            ff = d if ff is None else ff + d
            if c == 0 and pending is not None:
                finish(*pending)
            if c == 1 and u + 1 < n_sub:
                mix_next = merge(u + 1)
            if c == 2 and u + 1 < n_sub:
                nxt = out_proj(u + 1, mix_next)
        pending = (u, ff, x1)
        if nxt is not None:
            x1, h = nxt
    finish(*pending)


def _out_mlp(x3, mix_rnn, os_, lses, ga, wo, gm, wu, wd, gf, tm):
    bsz, s, _ = x3.shape
    row = lambda b, i: (b, i, 0)
    cls = lambda b, i: (b, 0, i, 0)
    const = lambda b, i: (0, 0)
    once = pl.Buffered(1)
    cls_specs = [pl.BlockSpec((1, dil, tm // dil, D_ATTN), cls) for _, dil in PATTERNS]
    in_specs = [pl.BlockSpec((1, tm, D_MODEL), row), pl.BlockSpec((1, tm, D_RNN), row)]
    in_specs += cls_specs + cls_specs
    in_specs += [
        pl.BlockSpec((1, D_ATTN), const),
        pl.BlockSpec((D_RNN + D_ATTN, D_MODEL), const, pipeline_mode=once),
        pl.BlockSpec((1, D_MODEL), const),
        pl.BlockSpec((D_MODEL, D_FF), const, pipeline_mode=once),
        pl.BlockSpec((D_FF, D_MODEL), const, pipeline_mode=once),
        pl.BlockSpec((1, D_MODEL), const),
    ]
    return pl.pallas_call(
        functools.partial(_out_mlp_kernel, tm=tm),
        grid=(bsz, s // tm),
        in_specs=in_specs,
        out_specs=pl.BlockSpec((1, tm, D_MODEL), row),
        out_shape=jax.ShapeDtypeStruct((bsz, s, D_MODEL), F32),
        scratch_shapes=[pltpu.VMEM((tm // SUB_ROWS, D_ATTN // LANES, SUB_ROWS, LANES), F32)] * 4,
        compiler_params=pltpu.CompilerParams(
            dimension_semantics=("arbitrary", "arbitrary"), vmem_limit_bytes=ATTN_VMEM_LIMIT),
        name="out_mlp",
    )(x3, mix_rnn, *os_, *lses, ga, wo, gm, wu, wd, gf)


def kernel(x, attn_norm_g, w_in, conv_w, conv_b, lru_wa_fwd, lru_ba_fwd, lru_wx_fwd, lru_bx_fwd, lru_lam_fwd, lru_wa_bwd, lru_ba_bwd, lru_wx_bwd, lru_bx_bwd, lru_lam_bwd, rel_bias, norm_rnn_g, norm_attn_g, w_out, mlp_norm_g, w_up, w_down, final_norm_g):
    depth = w_in.shape[0]
    assert depth == 1, "the final RMSNorm is fused into the single layer's last call"
    l = 0
    n_pat = len(PATTERNS)
    row = lambda v: v.reshape(1, -1)
    bias = _bias_tables(rel_bias)
    proj = _in_proj(x, row(attn_norm_g[l]), w_in[l].astype(BF16), tm=RNN_CHUNK)
    xr, gate = proj[0], proj[1]
    qs, ks, vs = (proj[2 + a * n_pat:2 + (a + 1) * n_pat] for a in range(3))
    fwd = (conv_w[l], row(conv_b[l]), _gate_weights(lru_wa_fwd[l]), _gate_weights(lru_wx_fwd[l]),
           row(lru_ba_fwd[l]), row(lru_bx_fwd[l]), row(lru_lam_fwd[l]), None)
    bwd = (conv_w[l], row(conv_b[l]), _gate_weights(lru_wa_bwd[l]), _gate_weights(lru_wx_bwd[l]),
           row(lru_ba_bwd[l]), row(lru_bx_bwd[l]), row(lru_lam_bwd[l]), row(norm_rnn_g[l]))
    h_f = _rnn_pass(xr, (), fwd, reverse=False, final=False)
    mix_rnn = _rnn_pass(xr, (h_f, gate), bwd, reverse=True, final=True)
    os_, lses = [], []
    for pat in range(n_pat):
        o, lse = _attn_pattern(qs[pat], ks[pat], vs[pat], bias, pat)
        os_.append(o)
        lses.append(lse)
    return _out_mlp(x, mix_rnn, os_, lses, row(norm_attn_g[l]), w_out[l].astype(BF16),
                    row(mlp_norm_g[l]), w_up[l].astype(BF16), w_down[l].astype(BF16),
                    row(final_norm_g), tm=512)
```

```python
import functools
import math

import numpy as np
import jax
import jax.numpy as jnp
from jax import lax
from jax.experimental import pallas as pl
from jax.experimental.pallas import tpu as pltpu

F32 = jnp.float32
BF16 = jnp.bfloat16

D_MODEL = 1024
D_RNN = 512
N_RNN_BLOCKS = 8
RNN_BLOCK = D_RNN // N_RNN_BLOCKS
CONV_WIDTH = 4
CONV_LEFT = 2
LRU_C = 8.0
N_HEADS = 8
HEAD_DIM = 64
D_ATTN = N_HEADS * HEAD_DIM
PATTERNS = ((128, 1), (512, 4), (2048, 16))
HALF_STEPS = 64
N_BUCKETS = 32
MAX_DISTANCE = 1024
D_IN = 2 * D_RNN + 3 * D_ATTN
D_FF = 4 * D_MODEL
EPS = 1e-6
NEG_INF = -1e30

LANES = 128
SUBLANES = 8
MXU_DIM = 256

Q_TILE = 128
K_TILE = 64
K_WIN = Q_TILE + 2 * HALF_STEPS
MAX_TILES_PER_STEP = 8
LOG2E = math.log2(math.e)
INTERIOR, FIRST, LAST = 0, 1, 2
GATE_BLK = MXU_DIM

VMEM_LIMIT = 48 * 1024 * 1024
ATTN_VMEM_LIMIT = 56 * 1024 * 1024


def _rms(x, g):
    ms = jnp.mean(x * x, axis=-1, keepdims=True)
    return x * lax.rsqrt(ms + EPS) * g


def _t5_bucket_np(rel):
    nb = N_BUCKETS // 2
    max_exact = nb // 2
    ret = np.where(rel > 0, nb, 0)
    n = np.abs(rel)
    nf = np.maximum(n, 1).astype(np.float32)
    large = max_exact + (np.log(nf / np.float32(max_exact)) / np.float32(math.log(MAX_DISTANCE / max_exact))
                         * np.float32(nb - max_exact)).astype(np.int32)
    large = np.minimum(large, nb - 1)
    return ret + np.where(n < max_exact, n, large)


def _bucket_index_tables():
    qi = np.arange(Q_TILE)[:, None]
    ci = np.arange(K_WIN)[None, :]
    step = ci - qi - HALF_STEPS
    tabs = []
    for _, dil in PATTERNS:
        b = _t5_bucket_np((step * dil).astype(np.int32))
        tabs.append(np.where(np.abs(step) <= HALF_STEPS, b, -1))
    return np.stack(tabs, axis=0).astype(np.int32)


def _bias_table_kernel(idx_ref, rb_ref, out_ref):
    idx = idx_ref[0]
    col = lax.broadcasted_iota(jnp.int32, idx.shape, 1)
    hit = [idx == b for b in range(N_BUCKETS)]
    for h in range(N_HEADS):
        acc = jnp.full(idx.shape, NEG_INF, F32)
        for b in range(N_BUCKETS):
            acc = jnp.where(hit[b], rb_ref[b, h] * LOG2E, acc)
        out_ref[0, INTERIOR, h] = acc
        out_ref[0, FIRST, h] = jnp.where(col < K_WIN - HALF_STEPS, pltpu.roll(acc, K_WIN - HALF_STEPS, 1), NEG_INF)
        out_ref[0, LAST, h] = jnp.where(col >= HALF_STEPS, pltpu.roll(acc, HALF_STEPS, 1), NEG_INF)


def _bias_tables(rel_bias):
    idx = jnp.asarray(_bucket_index_tables())
    n_pat = idx.shape[0]
    return pl.pallas_call(
        _bias_table_kernel,
        grid=(n_pat,),
        in_specs=[
            pl.BlockSpec((1, Q_TILE, K_WIN), lambda g: (g, 0, 0)),
            pl.BlockSpec(memory_space=pltpu.SMEM),
        ],
        out_specs=pl.BlockSpec((1, 3, N_HEADS, Q_TILE, K_WIN), lambda g: (g, 0, 0, 0, 0)),
        out_shape=jax.ShapeDtypeStruct((n_pat, 3, N_HEADS, Q_TILE, K_WIN), F32),
        name="bias_table",
    )(idx, rel_bias)


def _stripe_store(ref, p):
    for st in range(SUBLANES):
        for j in range(D_RNN // LANES):
            ref[0, j, pl.ds(st, STRIPE_ROWS, stride=SUBLANES), :] = (
                p[st * STRIPE_ROWS:(st + 1) * STRIPE_ROWS, j * LANES:(j + 1) * LANES])


PERM_ROWS = MXU_DIM


def _class_permutations():
    mats = []
    for _, dil in PATTERNS:
        if dil == 1:
            continue
        per = PERM_ROWS // dil
        p = np.zeros((PERM_ROWS, PERM_ROWS), np.float32)
        for r in range(dil):
            for m in range(per):
                p[r * per + m, m * dil + r] = 1.0
        mats.append(p)
    return np.stack(mats)


def _in_proj_kernel(x_ref, g_ref, w_ref, perm_ref, xr_ref, gate_ref, *rest, tm):
    n_pat = len(PATTERNS)
    outs = [rest[a * n_pat:(a + 1) * n_pat] for a in range(3)]
    h_scr = rest[3 * n_pat]
    h_scr[...] = _rms(x_ref[0], g_ref[...]).astype(BF16)

    def seg(lo, width):
        return jnp.dot(h_scr[...], w_ref[:, lo:lo + width], preferred_element_type=F32)

    for a, scale in enumerate((HEAD_DIM ** -0.5 * LOG2E, 1.0, 1.0)):
        p = seg(2 * D_RNN + a * D_ATTN, D_ATTN)
        if scale != 1.0:
            p = p * scale
        pb = p.astype(BF16)
        dilated = 0
        for (_, dil), o_ref in zip(PATTERNS, outs[a]):
            if dil == 1:
                o_ref[0, 0] = pb
                continue
            per = PERM_ROWS // dil
            for blk in range(tm // PERM_ROWS):
                y = jnp.dot(perm_ref[dilated], pb[blk * PERM_ROWS:(blk + 1) * PERM_ROWS, :],
                            preferred_element_type=F32).astype(BF16)
                for r in range(dil):
                    o_ref[0, r, blk * per:(blk + 1) * per, :] = y[r * per:(r + 1) * per, :]
            dilated += 1
    _stripe_store(xr_ref, seg(0, D_RNN))
    _stripe_store(gate_ref, seg(D_RNN, D_RNN))


def _in_proj(x3, g, w_in_bf, tm):
    assert tm == RNN_CHUNK
    bsz, s, _ = x3.shape
    row = lambda b, i: (b, i, 0)
    cls = lambda b, i: (b, 0, i, 0)
    const = lambda b, i: (0, 0)
    perms = jnp.asarray(_class_permutations(), BF16)
    qkv_specs, qkv_shapes = [], []
    for _ in range(3):
        for _, dil in PATTERNS:
            qkv_specs.append(pl.BlockSpec((1, dil, tm // dil, D_ATTN), cls))
            qkv_shapes.append(jax.ShapeDtypeStruct((bsz, dil, s // dil, D_ATTN), BF16))
    return pl.pallas_call(
        functools.partial(_in_proj_kernel, tm=tm),
        grid=(bsz, s // tm),
        in_specs=[
            pl.BlockSpec((1, tm, D_MODEL), row),
            pl.BlockSpec((1, D_MODEL), const),
            pl.BlockSpec((D_MODEL, D_IN), const),
            pl.BlockSpec(perms.shape, lambda b, i: (0, 0, 0)),
        ],
        out_specs=[pl.BlockSpec((1, D_RNN // LANES, tm, LANES), cls)] * 2 + qkv_specs,
        out_shape=[jax.ShapeDtypeStruct((bsz, D_RNN // LANES, s, LANES), F32)] * 2 + qkv_shapes,
        scratch_shapes=[pltpu.VMEM((tm, D_MODEL), BF16)],
        compiler_params=pltpu.CompilerParams(
            dimension_semantics=("arbitrary", "arbitrary"), vmem_limit_bytes=VMEM_LIMIT),
        name="in_proj",
    )(x3, g, w_in_bf, perms)


RNN_CHUNK = 512
STRIPE_ROWS = RNN_CHUNK // SUBLANES


def _scan_block(a, b, reverse):
    n = a.shape[0]
    row = lax.broadcasted_iota(jnp.int32, a.shape, 0)
    s = 1
    while s < n:
        if reverse:
            ra = pltpu.roll(a, n - s, 0)
            rb = pltpu.roll(b, n - s, 0)
            m = row < n - s
        else:
            ra = pltpu.roll(a, s, 0)
            rb = pltpu.roll(b, s, 0)
            m = row >= s
        b = jnp.where(m, a * rb + b, b)
        a = jnp.where(m, a * ra, a)
        s *= 2
    return a, b


def _rnn_kernel(*refs, reverse, final, nchunks):
    if final:
        (pa_ref, pb_ref, cur_ref, nx_ref, hf_ref, gate_ref, cw_ref, cb_ref, wa_ref, wx_ref,
         ba_ref, bx_ref, lam_ref, gn_ref, out_ref, a_scr, b_scr, h_scr, carry) = refs
    else:
        (pa_ref, pb_ref, cur_ref, nx_ref, cw_ref, cb_ref, wa_ref, wx_ref,
         ba_ref, bx_ref, lam_ref, out_ref, a_scr, b_scr, carry) = refs
        h_scr = None

    step = pl.program_id(1)
    chunk = (nchunks - 1 - step) if reverse else step
    n_slab = D_RNN // LANES
    nv = STRIPE_ROWS
    sub = lax.broadcasted_iota(jnp.int32, (SUBLANES, LANES), 0)
    top, bot = SUBLANES - 1, 0

    @pl.when(step == 0)
    def _():
        carry[...] = jnp.zeros_like(carry)

    xc = []
    for j in range(n_slab):
        lanes = slice(j * LANES, (j + 1) * LANES)
        x = cur_ref[0, j].reshape(nv, SUBLANES, LANES)
        pa = jnp.where(chunk > 0, pa_ref[0, j], 0.0)
        pb = jnp.where(chunk > 0, pb_ref[0, j], 0.0)
        nx = jnp.where(chunk < nchunks - 1, nx_ref[0, j], 0.0)
        xm2 = pltpu.roll(jnp.where(sub == top, pa, x[nv - 2]), 1, 0)
        xm1 = pltpu.roll(jnp.where(sub == top, pb, x[nv - 1]), 1, 0)
        xp1 = pltpu.roll(jnp.where(sub == bot, nx, x[0]), SUBLANES - 1, 0)
        xext = jnp.concatenate([xm2[None], xm1[None], x, xp1[None]], axis=0)
        acc = cb_ref[:, lanes] + xext[0:nv] * cw_ref[0:1, lanes]
        for k in range(1, CONV_WIDTH):
            acc = acc + xext[k:k + nv] * cw_ref[k:k + 1, lanes]
        xc.append(acc.reshape(RNN_CHUNK, LANES))

    nlam = -lam_ref[...]
    softplus = jnp.maximum(nlam, 0.0) + jnp.log1p(jnp.exp(-jnp.abs(nlam)))
    coef = -LRU_C * softplus
    slabs_per_blk = GATE_BLK // LANES
    for jj in range(D_RNN // GATE_BLK):
        blk = slice(jj * GATE_BLK, (jj + 1) * GATE_BLK)
        xj = jnp.concatenate(xc[jj * slabs_per_blk:(jj + 1) * slabs_per_blk], axis=-1)
        xjb = xj.astype(BF16)
        za = jnp.dot(xjb, wa_ref[jj], preferred_element_type=F32) + ba_ref[:, blk]
        zx = jnp.dot(xjb, wx_ref[jj], preferred_element_type=F32) + bx_ref[:, blk]
        r = 1.0 / (1.0 + jnp.exp(-za))
        i = 1.0 / (1.0 + jnp.exp(-zx))
        log_a = coef[:, blk] * r
        a = jnp.exp(log_a)
        b = jnp.sqrt(-jnp.tanh(log_a) * (1.0 + a * a)) * (i * xj)
        for t in range(slabs_per_blk):
            a_scr[jj * slabs_per_blk + t] = a[:, t * LANES:(t + 1) * LANES]
            b_scr[jj * slabs_per_blk + t] = b[:, t * LANES:(t + 1) * LANES]

    order = range(nv - 1, -1, -1) if reverse else range(nv)
    group = lambda v: slice(v * SUBLANES, (v + 1) * SUBLANES)

    h = [jnp.zeros((SUBLANES, LANES), F32)] * n_slab
    p = [jnp.ones((SUBLANES, LANES), F32)] * n_slab
    for v in order:
        for j in range(n_slab):
            av = a_scr[j, group(v), :]
            h[j] = av * h[j] + b_scr[j, group(v), :]
            p[j] = av * p[j]

    start = []
    for j in range(n_slab):
        pc, hc = _scan_block(p[j], h[j], reverse)
        prev = carry[j]
        if reverse:
            seed = jnp.broadcast_to(prev[bot:bot + 1, :], (SUBLANES, LANES))
            ends = hc + pc * seed
            start.append(jnp.where(sub == top, pltpu.roll(prev, SUBLANES - 1, 0),
                                   pltpu.roll(ends, SUBLANES - 1, 0)))
        else:
            seed = jnp.broadcast_to(prev[top:top + 1, :], (SUBLANES, LANES))
            ends = hc + pc * seed
            start.append(jnp.where(sub == bot, pltpu.roll(prev, 1, 0), pltpu.roll(ends, 1, 0)))
        carry[j] = ends

    dst = h_scr if final else None
    h = start
    for v in order:
        for j in range(n_slab):
            h[j] = a_scr[j, group(v), :] * h[j] + b_scr[j, group(v), :]
            if final:
                dst[j, group(v), :] = h[j]
            else:
                out_ref[0, j, group(v), :] = h[j]

    if final:
        ys = []
        for j in range(n_slab):
            g = gate_ref[0, j]
            gelu = g * (0.5 * (1.0 + jnp.tanh(math.sqrt(2.0 / math.pi) * (g + 0.044715 * (g * g * g)))))
            ys.append((hf_ref[0, j] + h_scr[j]) * gelu)
        ss = ys[0] * ys[0]
        for j in range(1, n_slab):
            ss = ss + ys[j] * ys[j]
        inv = lax.rsqrt(jnp.sum(ss, axis=-1, keepdims=True) * (1.0 / D_RNN) + EPS)
        for j in range(n_slab):
            lanes = slice(j * LANES, (j + 1) * LANES)
            h_scr[j] = ys[j] * inv * gn_ref[:, lanes]
        for st in range(SUBLANES):
            for j in range(n_slab):
                out_ref[0, st * nv:(st + 1) * nv, j * LANES:(j + 1) * LANES] = (
                    h_scr[j, pl.ds(st, nv, stride=SUBLANES), :].astype(BF16))


def _rnn_pass(xr4, extra, params, reverse, final):
    bsz, n_slab, s, _ = xr4.shape
    nchunks = s // RNN_CHUNK
    hb = RNN_CHUNK // SUBLANES
    nhalo = s // SUBLANES

    def cidx(c):
        return (nchunks - 1 - c) if reverse else c

    cur_map = lambda b, c: (b, 0, cidx(c), 0)
    pa_map = lambda b, c: (b, 0, jnp.maximum(cidx(c) * hb - 2, 0), 0)
    pb_map = lambda b, c: (b, 0, jnp.maximum(cidx(c) * hb - 1, 0), 0)
    nx_map = lambda b, c: (b, 0, jnp.minimum((cidx(c) + 1) * hb, nhalo - 1), 0)
    const2 = lambda b, c: (0, 0)
    const3 = lambda b, c: (0, 0, 0)

    cw, cb, wa, wx, ba, bx, lam, gn = params
    halo = (1, n_slab, SUBLANES, LANES)
    full = (1, n_slab, RNN_CHUNK, LANES)
    in_specs = [pl.BlockSpec(halo, pa_map), pl.BlockSpec(halo, pb_map), pl.BlockSpec(full, cur_map),
                pl.BlockSpec(halo, nx_map)]
    args = [xr4, xr4, xr4, xr4]
    if final:
        in_specs += [pl.BlockSpec(full, cur_map), pl.BlockSpec(full, cur_map)]
        args += list(extra)
    in_specs += [
        pl.BlockSpec((CONV_WIDTH, D_RNN), const2),
        pl.BlockSpec((1, D_RNN), const2),
        pl.BlockSpec((D_RNN // GATE_BLK, GATE_BLK, GATE_BLK), const3),
        pl.BlockSpec((D_RNN // GATE_BLK, GATE_BLK, GATE_BLK), const3),
        pl.BlockSpec((1, D_RNN), const2),
        pl.BlockSpec((1, D_RNN), const2),
        pl.BlockSpec((1, D_RNN), const2),
    ]
    args += [cw, cb, wa, wx, ba, bx, lam]
    slab_scr = pltpu.VMEM((n_slab, RNN_CHUNK, LANES), F32)
    scratch = [slab_scr, slab_scr]
    if final:
        in_specs.append(pl.BlockSpec((1, D_RNN), const2))
        args.append(gn)
        scratch.append(slab_scr)
        out_spec = pl.BlockSpec((1, RNN_CHUNK, D_RNN), lambda b, c: (b, cidx(c), 0))
        out_shape = jax.ShapeDtypeStruct((bsz, s, D_RNN), BF16)
    else:
        out_spec = pl.BlockSpec(full, cur_map)
        out_shape = jax.ShapeDtypeStruct(xr4.shape, F32)
    scratch.append(pltpu.VMEM((n_slab, SUBLANES, LANES), F32))
    return pl.pallas_call(
        functools.partial(_rnn_kernel, reverse=reverse, final=final, nchunks=nchunks),
        grid=(bsz, nchunks),
        in_specs=in_specs,
        out_specs=out_spec,
        out_shape=out_shape,
        scratch_shapes=scratch,
        compiler_params=pltpu.CompilerParams(
            dimension_semantics=("arbitrary", "arbitrary"), vmem_limit_bytes=VMEM_LIMIT),
        name="rnn_bwd" if reverse else "rnn_fwd",
    )(*args)


def _gate_weights(w):
    per = GATE_BLK // RNN_BLOCK
    w5 = w.reshape(D_RNN // GATE_BLK, per, RNN_BLOCK, 1, RNN_BLOCK)
    on_diag = jnp.asarray(np.eye(per, dtype=bool)).reshape(1, per, 1, per, 1)
    dense = jnp.where(on_diag, w5, 0.0)
    return dense.reshape(D_RNN // GATE_BLK, GATE_BLK, GATE_BLK).astype(BF16)


def _attn_kernel(q_ref, k_ref, v_ref, bias_ref, o_ref, lse_ref, *, sub_len, tiles_per_step):
    i = pl.program_id(2)
    n_tiles = sub_len // Q_TILE
    lane = lax.broadcasted_iota(jnp.int32, (Q_TILE, LANES), 1)
    heads_per_slab = LANES // HEAD_DIM
    first_half = lane < HEAD_DIM
    for t in range(tiles_per_step):
        rows = slice(t * Q_TILE, (t + 1) * Q_TILE)
        q = q_ref[0, rows, :]
        tile = i * tiles_per_step + t
        variant = jnp.where(tile == 0, FIRST, jnp.where(tile == n_tiles - 1, LAST, INTERIOR))
        start = pl.multiple_of(jnp.clip(tile * Q_TILE - HALF_STEPS, 0, sub_len - K_WIN), K_TILE)
        kk = k_ref[0, pl.ds(start, K_WIN), :]
        vv = v_ref[0, pl.ds(start, K_WIN), :]
        for p in range(D_ATTN // LANES):
            sl = slice(p * LANES, (p + 1) * LANES)
            qp, kp, vp = q[:, sl], kk[:, sl], vv[:, sl]
            zero = jnp.zeros_like(qp)
            q2 = jnp.concatenate([jnp.where(first_half, qp, zero), jnp.where(first_half, zero, qp)], axis=0)
            s = lax.dot_general(q2, kp, (((1,), (1,)), ((), ())), preferred_element_type=F32)
            h0 = p * heads_per_slab
            s = s + bias_ref[variant, h0:h0 + heads_per_slab].reshape(heads_per_slab * Q_TILE, K_WIN)
            m = jnp.max(s, axis=-1, keepdims=True)
            e = jnp.exp2(s - m)
            l = jnp.sum(e, axis=-1, keepdims=True)
            o2 = jnp.dot(e.astype(BF16), vp, preferred_element_type=F32) * (1.0 / l)
            lse2 = jnp.broadcast_to(m + jnp.log(l) * LOG2E, (heads_per_slab * Q_TILE, LANES))
            o_ref[0, rows, sl] = jnp.where(first_half, o2[:Q_TILE], o2[Q_TILE:])
            lse_ref[0, rows, sl] = jnp.where(first_half, lse2[:Q_TILE], lse2[Q_TILE:])


def _attn_pattern(qc, kc, vc, bias, pat):
    bsz, dil, sub_len, _ = qc.shape
    tiles_per_step = min(MAX_TILES_PER_STEP, sub_len // Q_TILE)
    q_rows = tiles_per_step * Q_TILE
    qmap = lambda b, r, i: (b, r, i, 0)
    seq_map = lambda b, r, i: (b, r, 0, 0)
    in_specs = [
        pl.BlockSpec((1, None, q_rows, D_ATTN), qmap),
        pl.BlockSpec((1, None, sub_len, D_ATTN), seq_map),
        pl.BlockSpec((1, None, sub_len, D_ATTN), seq_map),
        pl.BlockSpec((None, 3, N_HEADS, Q_TILE, K_WIN), lambda b, r, i: (pat, 0, 0, 0, 0)),
    ]
    return pl.pallas_call(
        functools.partial(_attn_kernel, sub_len=sub_len, tiles_per_step=tiles_per_step),
        grid=(bsz, dil, sub_len // q_rows),
        in_specs=in_specs,
        out_specs=[pl.BlockSpec((1, None, q_rows, D_ATTN), qmap)] * 2,
        out_shape=[jax.ShapeDtypeStruct((bsz, dil, sub_len, D_ATTN), F32)] * 2,
        compiler_params=pltpu.CompilerParams(
            dimension_semantics=("arbitrary",) * 3, vmem_limit_bytes=ATTN_VMEM_LIMIT),
        name=f"attn_d{dil}",
    )(qc, kc, vc, bias)


FF_CHUNK = 1024


SUB_ROWS = 512


def _natural_order(ref, scr, u):
    dil = ref.shape[1]
    per = SUB_ROWS // dil
    if dil == 1:
        return ref[0, 0, u * per:(u + 1) * per, :]
    n_slab = D_ATTN // LANES
    for r in range(dil):
        for j in range(n_slab):
            scr[u, j, pl.ds(r, per, stride=dil), :] = ref[0, r, u * per:(u + 1) * per, j * LANES:(j + 1) * LANES]
    return jnp.concatenate([scr[u, j] for j in range(n_slab)], axis=-1)


def _out_mlp_kernel(x_ref, mr_ref, o1, o2, o3, l1, l2, l3, ga_ref, wo_ref, gm_ref, wu_ref, wd_ref,
                    gf_ref, out_ref, s_o2, s_o3, s_l2, s_l3, *, tm):
    n_sub = tm // SUB_ROWS

    def merge(u):
        a1 = _natural_order(l1, None, u)
        a2 = _natural_order(l2, s_l2, u)
        a3 = _natural_order(l3, s_l3, u)
        mx = jnp.maximum(jnp.maximum(a1, a2), a3)
        e1, e2, e3 = jnp.exp2(a1 - mx), jnp.exp2(a2 - mx), jnp.exp2(a3 - mx)
        y_attn = (e1 * _natural_order(o1, None, u) + e2 * _natural_order(o2, s_o2, u)
                  + e3 * _natural_order(o3, s_o3, u)) / (e1 + e2 + e3)
        return _rms(y_attn, ga_ref[...]).astype(BF16)

    def out_proj(u, mix_attn):
        rows = slice(u * SUB_ROWS, (u + 1) * SUB_ROWS)
        proj = jnp.dot(mr_ref[0, rows, :], wo_ref[0:D_RNN, :], preferred_element_type=F32)
        proj = proj + jnp.dot(mix_attn, wo_ref[D_RNN:D_RNN + D_ATTN, :], preferred_element_type=F32)
        x1 = proj + x_ref[0, rows, :]
        return x1, _rms(x1, gm_ref[...]).astype(BF16)

    def ff_chunk(h, c):
        z = jnp.dot(h, wu_ref[:, c * FF_CHUNK:(c + 1) * FF_CHUNK], preferred_element_type=F32)
        z = jnp.square(jnp.maximum(z, 0.0)).astype(BF16)
        return jnp.dot(z, wd_ref[c * FF_CHUNK:(c + 1) * FF_CHUNK, :], preferred_element_type=F32)

    def finish(u, ff, x1):
        rows = slice(u * SUB_ROWS, (u + 1) * SUB_ROWS)
        out_ref[0, rows, :] = _rms(ff + x1, gf_ref[...])

    n_chunk = D_FF // FF_CHUNK
    x1, h = out_proj(0, merge(0))
    pending = None
    for u in range(n_sub):
        ff = None
        nxt = None
        for c in range(n_chunk):
            d = ff_chunk(h, c)
            ff = d if ff is None else ff + d
            if c == 0 and pending is not None:
                finish(*pending)
            if c == 1 and u + 1 < n_sub:
                mix_next = merge(u + 1)
            if c == 2 and u + 1 < n_sub:
                nxt = out_proj(u + 1, mix_next)
        pending = (u, ff, x1)
        if nxt is not None:
            x1, h = nxt
    finish(*pending)


def _out_mlp(x3, mix_rnn, os_, lses, ga, wo, gm, wu, wd, gf, tm):
    bsz, s, _ = x3.shape
    row = lambda b, i: (b, i, 0)
    cls = lambda b, i: (b, 0, i, 0)
    const = lambda b, i: (0, 0)
    once = pl.Buffered(1)
    cls_specs = [pl.BlockSpec((1, dil, tm // dil, D_ATTN), cls) for _, dil in PATTERNS]
    in_specs = [pl.BlockSpec((1, tm, D_MODEL), row), pl.BlockSpec((1, tm, D_RNN), row)]
    in_specs += cls_specs + cls_specs
    in_specs += [
        pl.BlockSpec((1, D_ATTN), const),
        pl.BlockSpec((D_RNN + D_ATTN, D_MODEL), const, pipeline_mode=once),
        pl.BlockSpec((1, D_MODEL), const),
        pl.BlockSpec((D_MODEL, D_FF), const, pipeline_mode=once),
        pl.BlockSpec((D_FF, D_MODEL), const, pipeline_mode=once),
        pl.BlockSpec((1, D_MODEL), const),
    ]
    return pl.pallas_call(
        functools.partial(_out_mlp_kernel, tm=tm),
        grid=(bsz, s // tm),
        in_specs=in_specs,
        out_specs=pl.BlockSpec((1, tm, D_MODEL), row),
        out_shape=jax.ShapeDtypeStruct((bsz, s, D_MODEL), F32),
        scratch_shapes=[pltpu.VMEM((tm // SUB_ROWS, D_ATTN // LANES, SUB_ROWS, LANES), F32)] * 4,
        compiler_params=pltpu.CompilerParams(
            dimension_semantics=("arbitrary", "arbitrary"), vmem_limit_bytes=ATTN_VMEM_LIMIT),
        name="out_mlp",
    )(x3, mix_rnn, *os_, *lses, ga, wo, gm, wu, wd, gf)


def kernel(x, attn_norm_g, w_in, conv_w, conv_b, lru_wa_fwd, lru_ba_fwd, lru_wx_fwd, lru_bx_fwd, lru_lam_fwd, lru_wa_bwd, lru_ba_bwd, lru_wx_bwd, lru_bx_bwd, lru_lam_bwd, rel_bias, norm_rnn_g, norm_attn_g, w_out, mlp_norm_g, w_up, w_down, final_norm_g):
    depth = w_in.shape[0]
    assert depth == 1, "the final RMSNorm is fused into the single layer's last call"
    l = 0
    n_pat = len(PATTERNS)
    row = lambda v: v.reshape(1, -1)
    bias = _bias_tables(rel_bias)
    proj = _in_proj(x, row(attn_norm_g[l]), w_in[l].astype(BF16), tm=RNN_CHUNK)
    xr, gate = proj[0], proj[1]
    qs, ks, vs = (proj[2 + a * n_pat:2 + (a + 1) * n_pat] for a in range(3))
    fwd = (conv_w[l], row(conv_b[l]), _gate_weights(lru_wa_fwd[l]), _gate_weights(lru_wx_fwd[l]),
           row(lru_ba_fwd[l]), row(lru_bx_fwd[l]), row(lru_lam_fwd[l]), None)
    bwd = (conv_w[l], row(conv_b[l]), _gate_weights(lru_wa_bwd[l]), _gate_weights(lru_wx_bwd[l]),
           row(lru_ba_bwd[l]), row(lru_bx_bwd[l]), row(lru_lam_bwd[l]), row(norm_rnn_g[l]))
    h_f = _rnn_pass(xr, (), fwd, reverse=False, final=False)
    mix_rnn = _rnn_pass(xr, (h_f, gate), bwd, reverse=True, final=True)
    os_, lses = [], []
    for pat in range(n_pat):
        o, lse = _attn_pattern(qs[pat], ks[pat], vs[pat], bias, pat)
        os_.append(o)
        lses.append(lse)
    return _out_mlp(x, mix_rnn, os_, lses, row(norm_attn_g[l]), w_out[l].astype(BF16),
                    row(mlp_norm_g[l]), w_up[l].astype(BF16), w_down[l].astype(BF16),
                    row(final_norm_g), tm=512)
```

```python
import functools
import math

import numpy as np
import jax
import jax.numpy as jnp
from jax import lax
from jax.experimental import pallas as pl
from jax.experimental.pallas import tpu as pltpu

F32 = jnp.float32
BF16 = jnp.bfloat16

D_MODEL = 1024
D_RNN = 512
N_RNN_BLOCKS = 8
RNN_BLOCK = D_RNN // N_RNN_BLOCKS
CONV_WIDTH = 4
CONV_LEFT = 2
LRU_C = 8.0
N_HEADS = 8
HEAD_DIM = 64
D_ATTN = N_HEADS * HEAD_DIM
PATTERNS = ((128, 1), (512, 4), (2048, 16))
HALF_STEPS = 64
N_BUCKETS = 32
MAX_DISTANCE = 1024
D_IN = 2 * D_RNN + 3 * D_ATTN
D_FF = 4 * D_MODEL
EPS = 1e-6
NEG_INF = -1e30

LANES = 128
SUBLANES = 8
MXU_DIM = 256

Q_TILE = 128
K_TILE = 64
K_WIN = Q_TILE + 2 * HALF_STEPS
MAX_TILES_PER_STEP = 8
LOG2E = math.log2(math.e)
INTERIOR, FIRST, LAST = 0, 1, 2
GATE_BLK = MXU_DIM

VMEM_LIMIT = 48 * 1024 * 1024
ATTN_VMEM_LIMIT = 56 * 1024 * 1024


def _rms(x, g):
    ms = jnp.mean(x * x, axis=-1, keepdims=True)
    return x * lax.rsqrt(ms + EPS) * g


def _t5_bucket_np(rel):
    nb = N_BUCKETS // 2
    max_exact = nb // 2
    ret = np.where(rel > 0, nb, 0)
    n = np.abs(rel)
    nf = np.maximum(n, 1).astype(np.float32)
    large = max_exact + (np.log(nf / np.float32(max_exact)) / np.float32(math.log(MAX_DISTANCE / max_exact))
                         * np.float32(nb - max_exact)).astype(np.int32)
    large = np.minimum(large, nb - 1)
    return ret + np.where(n < max_exact, n, large)


def _bucket_index_tables():
    qi = np.arange(Q_TILE)[:, None]
    ci = np.arange(K_WIN)[None, :]
    step = ci - qi - HALF_STEPS
    tabs = []
    for _, dil in PATTERNS:
        b = _t5_bucket_np((step * dil).astype(np.int32))
        tabs.append(np.where(np.abs(step) <= HALF_STEPS, b, -1))
    return np.stack(tabs, axis=0).astype(np.int32)


def _bias_table_kernel(idx_ref, rb_ref, out_ref):
    idx = idx_ref[0]
    col = lax.broadcasted_iota(jnp.int32, idx.shape, 1)
    hit = [idx == b for b in range(N_BUCKETS)]
    for h in range(N_HEADS):
        acc = jnp.full(idx.shape, NEG_INF, F32)
        for b in range(N_BUCKETS):
            acc = jnp.where(hit[b], rb_ref[b, h] * LOG2E, acc)
        out_ref[0, INTERIOR, h] = acc
        out_ref[0, FIRST, h] = jnp.where(col < K_WIN - HALF_STEPS, pltpu.roll(acc, K_WIN - HALF_STEPS, 1), NEG_INF)
        out_ref[0, LAST, h] = jnp.where(col >= HALF_STEPS, pltpu.roll(acc, HALF_STEPS, 1), NEG_INF)


def _bias_tables(rel_bias):
    idx = jnp.asarray(_bucket_index_tables())
    n_pat = idx.shape[0]
    return pl.pallas_call(
        _bias_table_kernel,
        grid=(n_pat,),
        in_specs=[
            pl.BlockSpec((1, Q_TILE, K_WIN), lambda g: (g, 0, 0)),
            pl.BlockSpec(memory_space=pltpu.SMEM),
        ],
        out_specs=pl.BlockSpec((1, 3, N_HEADS, Q_TILE, K_WIN), lambda g: (g, 0, 0, 0, 0)),
        out_shape=jax.ShapeDtypeStruct((n_pat, 3, N_HEADS, Q_TILE, K_WIN), F32),
        name="bias_table",
    )(idx, rel_bias)


def _stripe_store(ref, p):
    for st in range(SUBLANES):
        for j in range(D_RNN // LANES):
            ref[0, j, pl.ds(st, STRIPE_ROWS, stride=SUBLANES), :] = (
                p[st * STRIPE_ROWS:(st + 1) * STRIPE_ROWS, j * LANES:(j + 1) * LANES])


PERM_ROWS = MXU_DIM


def _class_permutations():
    mats = []
    for _, dil in PATTERNS:
        if dil == 1:
            continue
        per = PERM_ROWS // dil
        p = np.zeros((PERM_ROWS, PERM_ROWS), np.float32)
        for r in range(dil):
            for m in range(per):
                p[r * per + m, m * dil + r] = 1.0
        mats.append(p)
    return np.stack(mats)


def _in_proj_kernel(x_ref, g_ref, w_ref, perm_ref, xr_ref, gate_ref, *rest, tm):
    n_pat = len(PATTERNS)
    outs = [rest[a * n_pat:(a + 1) * n_pat] for a in range(3)]
    h_scr = rest[3 * n_pat]
    h_scr[...] = _rms(x_ref[0], g_ref[...]).astype(BF16)

    def seg(lo, width):
        return jnp.dot(h_scr[...], w_ref[:, lo:lo + width], preferred_element_type=F32)

    for a, scale in enumerate((HEAD_DIM ** -0.5 * LOG2E, 1.0, 1.0)):
        p = seg(2 * D_RNN + a * D_ATTN, D_ATTN)
        if scale != 1.0:
            p = p * scale
        pb = p.astype(BF16)
        dilated = 0
        for (_, dil), o_ref in zip(PATTERNS, outs[a]):
            if dil == 1:
                o_ref[0, 0] = pb
                continue
            per = PERM_ROWS // dil
            for blk in range(tm // PERM_ROWS):
                y = jnp.dot(perm_ref[dilated], pb[blk * PERM_ROWS:(blk + 1) * PERM_ROWS, :],
                            preferred_element_type=F32).astype(BF16)
                for r in range(dil):
                    o_ref[0, r, blk * per:(blk + 1) * per, :] = y[r * per:(r + 1) * per, :]
            dilated += 1
    g = seg(D_RNN, D_RNN)
    gelu = g * (0.5 * (1.0 + jnp.tanh(math.sqrt(2.0 / math.pi) * (g + 0.044715 * (g * g * g)))))
    _stripe_store(gate_ref, gelu)
    _stripe_store(xr_ref, seg(0, D_RNN))


def _in_proj(x3, g, w_in_bf, tm):
    assert tm == RNN_CHUNK
    bsz, s, _ = x3.shape
    row = lambda b, i: (b, i, 0)
    cls = lambda b, i: (b, 0, i, 0)
    const = lambda b, i: (0, 0)
    perms = jnp.asarray(_class_permutations(), BF16)
    qkv_specs, qkv_shapes = [], []
    for _ in range(3):
        for _, dil in PATTERNS:
            qkv_specs.append(pl.BlockSpec((1, dil, tm // dil, D_ATTN), cls))
            qkv_shapes.append(jax.ShapeDtypeStruct((bsz, dil, s // dil, D_ATTN), BF16))
    return pl.pallas_call(
        functools.partial(_in_proj_kernel, tm=tm),
        grid=(bsz, s // tm),
        in_specs=[
            pl.BlockSpec((1, tm, D_MODEL), row),
            pl.BlockSpec((1, D_MODEL), const),
            pl.BlockSpec((D_MODEL, D_IN), const),
            pl.BlockSpec(perms.shape, lambda b, i: (0, 0, 0)),
        ],
        out_specs=[pl.BlockSpec((1, D_RNN // LANES, tm, LANES), cls)] * 2 + qkv_specs,
        out_shape=[jax.ShapeDtypeStruct((bsz, D_RNN // LANES, s, LANES), F32)] * 2 + qkv_shapes,
        scratch_shapes=[pltpu.VMEM((tm, D_MODEL), BF16)],
        compiler_params=pltpu.CompilerParams(
            dimension_semantics=("arbitrary", "arbitrary"), vmem_limit_bytes=VMEM_LIMIT),
        name="in_proj",
    )(x3, g, w_in_bf, perms)


RNN_CHUNK = 512
STRIPE_ROWS = RNN_CHUNK // SUBLANES


def _scan_block(a, b, reverse):
    n = a.shape[0]
    row = lax.broadcasted_iota(jnp.int32, a.shape, 0)
    s = 1
    while s < n:
        if reverse:
            ra = pltpu.roll(a, n - s, 0)
            rb = pltpu.roll(b, n - s, 0)
            m = row < n - s
        else:
            ra = pltpu.roll(a, s, 0)
            rb = pltpu.roll(b, s, 0)
            m = row >= s
        b = jnp.where(m, a * rb + b, b)
        a = jnp.where(m, a * ra, a)
        s *= 2
    return a, b


def _rnn_kernel(*refs, reverse, final, nchunks):
    if final:
        (pa_ref, pb_ref, cur_ref, nx_ref, hf_ref, gate_ref, cw_ref, cb_ref, wa_ref, wx_ref,
         ba_ref, bx_ref, lam_ref, gn_ref, out_ref, a_scr, b_scr, h_scr, carry) = refs
    else:
        (pa_ref, pb_ref, cur_ref, nx_ref, cw_ref, cb_ref, wa_ref, wx_ref,
         ba_ref, bx_ref, lam_ref, out_ref, a_scr, b_scr, carry) = refs
        h_scr = None

    step = pl.program_id(1)
    chunk = (nchunks - 1 - step) if reverse else step
    n_slab = D_RNN // LANES
    nv = STRIPE_ROWS
    sub = lax.broadcasted_iota(jnp.int32, (SUBLANES, LANES), 0)
    top, bot = SUBLANES - 1, 0

    @pl.when(step == 0)
    def _():
        carry[...] = jnp.zeros_like(carry)

    xc = []
    for j in range(n_slab):
        lanes = slice(j * LANES, (j + 1) * LANES)
        x = cur_ref[0, j].reshape(nv, SUBLANES, LANES)
        pa = jnp.where(chunk > 0, pa_ref[0, j], 0.0)
        pb = jnp.where(chunk > 0, pb_ref[0, j], 0.0)
        nx = jnp.where(chunk < nchunks - 1, nx_ref[0, j], 0.0)
        xm2 = pltpu.roll(jnp.where(sub == top, pa, x[nv - 2]), 1, 0)
        xm1 = pltpu.roll(jnp.where(sub == top, pb, x[nv - 1]), 1, 0)
        xp1 = pltpu.roll(jnp.where(sub == bot, nx, x[0]), SUBLANES - 1, 0)
        xext = jnp.concatenate([xm2[None], xm1[None], x, xp1[None]], axis=0)
        acc = cb_ref[:, lanes] + xext[0:nv] * cw_ref[0:1, lanes]
        for k in range(1, CONV_WIDTH):
            acc = acc + xext[k:k + nv] * cw_ref[k:k + 1, lanes]
        xc.append(acc.reshape(RNN_CHUNK, LANES))

    nlam = -lam_ref[...]
    softplus = jnp.maximum(nlam, 0.0) + jnp.log1p(jnp.exp(-jnp.abs(nlam)))
    coef = -LRU_C * softplus
    slabs_per_blk = GATE_BLK // LANES
    for jj in range(D_RNN // GATE_BLK):
        blk = slice(jj * GATE_BLK, (jj + 1) * GATE_BLK)
        xj = jnp.concatenate(xc[jj * slabs_per_blk:(jj + 1) * slabs_per_blk], axis=-1)
        xjb = xj.astype(BF16)
        za = jnp.dot(xjb, wa_ref[jj], preferred_element_type=F32) + ba_ref[:, blk]
        zx = jnp.dot(xjb, wx_ref[jj], preferred_element_type=F32) + bx_ref[:, blk]
        r = 1.0 / (1.0 + jnp.exp(-za))
        i = 1.0 / (1.0 + jnp.exp(-zx))
        log_a = coef[:, blk] * r
        a = jnp.exp(log_a)
        u = -jnp.tanh(log_a) * (1.0 + a * a)
        root = jnp.where(u > 0.0, u * lax.rsqrt(u), 0.0)
        b = root * (i * xj)
        for t in range(slabs_per_blk):
            a_scr[jj * slabs_per_blk + t] = a[:, t * LANES:(t + 1) * LANES]
            b_scr[jj * slabs_per_blk + t] = b[:, t * LANES:(t + 1) * LANES]

    order = range(nv - 1, -1, -1) if reverse else range(nv)
    group = lambda v: slice(v * SUBLANES, (v + 1) * SUBLANES)

    h = [jnp.zeros((SUBLANES, LANES), F32)] * n_slab
    p = [jnp.ones((SUBLANES, LANES), F32)] * n_slab
    for v in order:
        for j in range(n_slab):
            av = a_scr[j, group(v), :]
            h[j] = av * h[j] + b_scr[j, group(v), :]
            p[j] = av * p[j]

    start = []
    for j in range(n_slab):
        pc, hc = _scan_block(p[j], h[j], reverse)
        prev = carry[j]
        if reverse:
            seed = jnp.broadcast_to(prev[bot:bot + 1, :], (SUBLANES, LANES))
            ends = hc + pc * seed
            start.append(jnp.where(sub == top, pltpu.roll(prev, SUBLANES - 1, 0),
                                   pltpu.roll(ends, SUBLANES - 1, 0)))
        else:
            seed = jnp.broadcast_to(prev[top:top + 1, :], (SUBLANES, LANES))
            ends = hc + pc * seed
            start.append(jnp.where(sub == bot, pltpu.roll(prev, 1, 0), pltpu.roll(ends, 1, 0)))
        carry[j] = ends

    dst = h_scr if final else None
    h = start
    for v in order:
        for j in range(n_slab):
            h[j] = a_scr[j, group(v), :] * h[j] + b_scr[j, group(v), :]
            if final:
                dst[j, group(v), :] = h[j]
            else:
                out_ref[0, j, group(v), :] = h[j]

    if final:
        ys = []
        for j in range(n_slab):
            ys.append((hf_ref[0, j] + h_scr[j]) * gate_ref[0, j])
        ss = ys[0] * ys[0]
        for j in range(1, n_slab):
            ss = ss + ys[j] * ys[j]
        inv = lax.rsqrt(jnp.sum(ss, axis=-1, keepdims=True) * (1.0 / D_RNN) + EPS)
        for j in range(n_slab):
            lanes = slice(j * LANES, (j + 1) * LANES)
            h_scr[j] = ys[j] * inv * gn_ref[:, lanes]
        for st in range(SUBLANES):
            for j in range(n_slab):
                out_ref[0, st * nv:(st + 1) * nv, j * LANES:(j + 1) * LANES] = (
                    h_scr[j, pl.ds(st, nv, stride=SUBLANES), :].astype(BF16))


def _rnn_pass(xr4, extra, params, reverse, final):
    bsz, n_slab, s, _ = xr4.shape
    nchunks = s // RNN_CHUNK
    hb = RNN_CHUNK // SUBLANES
    nhalo = s // SUBLANES

    def cidx(c):
        return (nchunks - 1 - c) if reverse else c

    cur_map = lambda b, c: (b, 0, cidx(c), 0)
    pa_map = lambda b, c: (b, 0, jnp.maximum(cidx(c) * hb - 2, 0), 0)
    pb_map = lambda b, c: (b, 0, jnp.maximum(cidx(c) * hb - 1, 0), 0)
    nx_map = lambda b, c: (b, 0, jnp.minimum((cidx(c) + 1) * hb, nhalo - 1), 0)
    const2 = lambda b, c: (0, 0)
    const3 = lambda b, c: (0, 0, 0)

    cw, cb, wa, wx, ba, bx, lam, gn = params
    halo = (1, n_slab, SUBLANES, LANES)
    full = (1, n_slab, RNN_CHUNK, LANES)
    in_specs = [pl.BlockSpec(halo, pa_map), pl.BlockSpec(halo, pb_map), pl.BlockSpec(full, cur_map),
                pl.BlockSpec(halo, nx_map)]
    args = [xr4, xr4, xr4, xr4]
    if final:
        in_specs += [pl.BlockSpec(full, cur_map), pl.BlockSpec(full, cur_map)]
        args += list(extra)
    in_specs += [
        pl.BlockSpec((CONV_WIDTH, D_RNN), const2),
        pl.BlockSpec((1, D_RNN), const2),
        pl.BlockSpec((D_RNN // GATE_BLK, GATE_BLK, GATE_BLK), const3),
        pl.BlockSpec((D_RNN // GATE_BLK, GATE_BLK, GATE_BLK), const3),
        pl.BlockSpec((1, D_RNN), const2),
        pl.BlockSpec((1, D_RNN), const2),
        pl.BlockSpec((1, D_RNN), const2),
    ]
    args += [cw, cb, wa, wx, ba, bx, lam]
    slab_scr = pltpu.VMEM((n_slab, RNN_CHUNK, LANES), F32)
    scratch = [slab_scr, slab_scr]
    if final:
        in_specs.append(pl.BlockSpec((1, D_RNN), const2))
        args.append(gn)
        scratch.append(slab_scr)
        out_spec = pl.BlockSpec((1, RNN_CHUNK, D_RNN), lambda b, c: (b, cidx(c), 0))
        out_shape = jax.ShapeDtypeStruct((bsz, s, D_RNN), BF16)
    else:
        out_spec = pl.BlockSpec(full, cur_map)
        out_shape = jax.ShapeDtypeStruct(xr4.shape, F32)
    scratch.append(pltpu.VMEM((n_slab, SUBLANES, LANES), F32))
    return pl.pallas_call(
        functools.partial(_rnn_kernel, reverse=reverse, final=final, nchunks=nchunks),
        grid=(bsz, nchunks),
        in_specs=in_specs,
        out_specs=out_spec,
        out_shape=out_shape,
        scratch_shapes=scratch,
        compiler_params=pltpu.CompilerParams(
            dimension_semantics=("arbitrary", "arbitrary"), vmem_limit_bytes=VMEM_LIMIT),
        name="rnn_bwd" if reverse else "rnn_fwd",
    )(*args)


def _gate_weights(w):
    per = GATE_BLK // RNN_BLOCK
    w5 = w.reshape(D_RNN // GATE_BLK, per, RNN_BLOCK, 1, RNN_BLOCK)
    on_diag = jnp.asarray(np.eye(per, dtype=bool)).reshape(1, per, 1, per, 1)
    dense = jnp.where(on_diag, w5, 0.0)
    return dense.reshape(D_RNN // GATE_BLK, GATE_BLK, GATE_BLK).astype(BF16)


def _attn_kernel(q_ref, k_ref, v_ref, bias_ref, o_ref, lse_ref, *, sub_len, tiles_per_step):
    i = pl.program_id(2)
    n_tiles = sub_len // Q_TILE
    lane = lax.broadcasted_iota(jnp.int32, (Q_TILE, LANES), 1)
    heads_per_slab = LANES // HEAD_DIM
    first_half = lane < HEAD_DIM
    for cl in range(q_ref.shape[1]):
        for t in range(tiles_per_step):
            rows = slice(t * Q_TILE, (t + 1) * Q_TILE)
            q = q_ref[0, cl, rows, :]
            tile = i * tiles_per_step + t
            variant = jnp.where(tile == 0, FIRST, jnp.where(tile == n_tiles - 1, LAST, INTERIOR))
            start = pl.multiple_of(jnp.clip(tile * Q_TILE - HALF_STEPS, 0, sub_len - K_WIN), K_TILE)
            kk = k_ref[0, cl, pl.ds(start, K_WIN), :]
            vv = v_ref[0, cl, pl.ds(start, K_WIN), :]
            for p in range(D_ATTN // LANES):
                sl = slice(p * LANES, (p + 1) * LANES)
                qp, kp, vp = q[:, sl], kk[:, sl], vv[:, sl]
                zero = jnp.zeros_like(qp)
                q2 = jnp.concatenate([jnp.where(first_half, qp, zero), jnp.where(first_half, zero, qp)], axis=0)
                s = lax.dot_general(q2, kp, (((1,), (1,)), ((), ())), preferred_element_type=F32)
                h0 = p * heads_per_slab
                s = s + bias_ref[variant, h0:h0 + heads_per_slab].reshape(heads_per_slab * Q_TILE, K_WIN)
                m = jnp.max(s, axis=-1, keepdims=True)
                e = jnp.exp2(s - m)
                l = jnp.sum(e, axis=-1, keepdims=True)
                o2 = jnp.dot(e.astype(BF16), vp, preferred_element_type=F32) * (1.0 / l)
                lse2 = jnp.broadcast_to(m + jnp.log(l) * LOG2E, (heads_per_slab * Q_TILE, LANES))
                o_ref[0, cl, rows, sl] = jnp.where(first_half, o2[:Q_TILE], o2[Q_TILE:])
                lse_ref[0, cl, rows, sl] = jnp.where(first_half, lse2[:Q_TILE], lse2[Q_TILE:])


def _attn_pattern(qc, kc, vc, bias, pat):
    bsz, dil, sub_len, _ = qc.shape
    tiles_per_step = min(MAX_TILES_PER_STEP, sub_len // Q_TILE)
    classes_per_step = MAX_TILES_PER_STEP // tiles_per_step
    q_rows = tiles_per_step * Q_TILE
    qmap = lambda b, r, i: (b, r, i, 0)
    seq_map = lambda b, r, i: (b, r, 0, 0)
    in_specs = [
        pl.BlockSpec((1, classes_per_step, q_rows, D_ATTN), qmap),
        pl.BlockSpec((1, classes_per_step, sub_len, D_ATTN), seq_map),
        pl.BlockSpec((1, classes_per_step, sub_len, D_ATTN), seq_map),
        pl.BlockSpec((None, 3, N_HEADS, Q_TILE, K_WIN), lambda b, r, i: (pat, 0, 0, 0, 0)),
    ]
    return pl.pallas_call(
        functools.partial(_attn_kernel, sub_len=sub_len, tiles_per_step=tiles_per_step),
        grid=(bsz, dil // classes_per_step, sub_len // q_rows),
        in_specs=in_specs,
        out_specs=[pl.BlockSpec((1, classes_per_step, q_rows, D_ATTN), qmap)] * 2,
        out_shape=[jax.ShapeDtypeStruct((bsz, dil, sub_len, D_ATTN), F32)] * 2,
        compiler_params=pltpu.CompilerParams(
            dimension_semantics=("arbitrary",) * 3, vmem_limit_bytes=ATTN_VMEM_LIMIT),
        name=f"attn_d{dil}",
    )(qc, kc, vc, bias)


FF_CHUNK = 1024


SUB_ROWS = 512


def _natural_order(ref, scr, u):
    dil = ref.shape[1]
    per = SUB_ROWS // dil
    if dil == 1:
        return ref[0, 0, u * per:(u + 1) * per, :]
    n_slab = D_ATTN // LANES
    for r in range(dil):
        for j in range(n_slab):
            scr[u, j, pl.ds(r, per, stride=dil), :] = ref[0, r, u * per:(u + 1) * per, j * LANES:(j + 1) * LANES]
    return jnp.concatenate([scr[u, j] for j in range(n_slab)], axis=-1)


def _out_mlp_kernel(x_ref, mr_ref, o1, o2, o3, l1, l2, l3, ga_ref, wo_ref, gm_ref, wu_ref, wd_ref,
                    gf_ref, out_ref, s_o2, s_o3, s_l2, s_l3, *, tm):
    n_sub = tm // SUB_ROWS

    def merge(u):
        a1 = _natural_order(l1, None, u)
        a2 = _natural_order(l2, s_l2, u)
        a3 = _natural_order(l3, s_l3, u)
        mx = jnp.maximum(jnp.maximum(a1, a2), a3)
        e1, e2, e3 = jnp.exp2(a1 - mx), jnp.exp2(a2 - mx), jnp.exp2(a3 - mx)
        y_attn = (e1 * _natural_order(o1, None, u) + e2 * _natural_order(o2, s_o2, u)
                  + e3 * _natural_order(o3, s_o3, u)) / (e1 + e2 + e3)
        return _rms(y_attn, ga_ref[...]).astype(BF16)

    def out_proj(u, mix_attn):
        rows = slice(u * SUB_ROWS, (u + 1) * SUB_ROWS)
        proj = jnp.dot(mr_ref[0, rows, :], wo_ref[0:D_RNN, :], preferred_element_type=F32)
        proj = proj + jnp.dot(mix_attn, wo_ref[D_RNN:D_RNN + D_ATTN, :], preferred_element_type=F32)
        x1 = proj + x_ref[0, rows, :]
        return x1, _rms(x1, gm_ref[...]).astype(BF16)

    def ff_chunk(h, c):
        z = jnp.dot(h, wu_ref[:, c * FF_CHUNK:(c + 1) * FF_CHUNK], preferred_element_type=F32)
        z = jnp.square(jnp.maximum(z, 0.0)).astype(BF16)
        return jnp.dot(z, wd_ref[c * FF_CHUNK:(c + 1) * FF_CHUNK, :], preferred_element_type=F32)

    def finish(u, ff, x1):
        rows = slice(u * SUB_ROWS, (u + 1) * SUB_ROWS)
        out_ref[0, rows, :] = _rms(ff + x1, gf_ref[...])

    n_chunk = D_FF // FF_CHUNK
    x1, h = out_proj(0, merge(0))
    pending = None
    for u in range(n_sub):
        ff = None
        nxt = None
        for c in range(n_chunk):
            d = ff_chunk(h, c)
            ff = d if ff is None else ff + d
            if c == 0 and pending is not None:
                finish(*pending)
            if c == 1 and u + 1 < n_sub:
                mix_next = merge(u + 1)
            if c == 2 and u + 1 < n_sub:
                nxt = out_proj(u + 1, mix_next)
        pending = (u, ff, x1)
        if nxt is not None:
            x1, h = nxt
    finish(*pending)


def _out_mlp(x3, mix_rnn, os_, lses, ga, wo, gm, wu, wd, gf, tm):
    bsz, s, _ = x3.shape
    row = lambda b, i: (b, i, 0)
    cls = lambda b, i: (b, 0, i, 0)
    const = lambda b, i: (0, 0)
    once = pl.Buffered(1)
    cls_specs = [pl.BlockSpec((1, dil, tm // dil, D_ATTN), cls) for _, dil in PATTERNS]
    in_specs = [pl.BlockSpec((1, tm, D_MODEL), row), pl.BlockSpec((1, tm, D_RNN), row)]
    in_specs += cls_specs + cls_specs
    in_specs += [
        pl.BlockSpec((1, D_ATTN), const),
        pl.BlockSpec((D_RNN + D_ATTN, D_MODEL), const, pipeline_mode=once),
        pl.BlockSpec((1, D_MODEL), const),
        pl.BlockSpec((D_MODEL, D_FF), const, pipeline_mode=once),
        pl.BlockSpec((D_FF, D_MODEL), const, pipeline_mode=once),
        pl.BlockSpec((1, D_MODEL), const),
    ]
    return pl.pallas_call(
        functools.partial(_out_mlp_kernel, tm=tm),
        grid=(bsz, s // tm),
        in_specs=in_specs,
        out_specs=pl.BlockSpec((1, tm, D_MODEL), row),
        out_shape=jax.ShapeDtypeStruct((bsz, s, D_MODEL), F32),
        scratch_shapes=[pltpu.VMEM((tm // SUB_ROWS, D_ATTN // LANES, SUB_ROWS, LANES), F32)] * 4,
        compiler_params=pltpu.CompilerParams(
            dimension_semantics=("arbitrary", "arbitrary"), vmem_limit_bytes=ATTN_VMEM_LIMIT),
        name="out_mlp",
    )(x3, mix_rnn, *os_, *lses, ga, wo, gm, wu, wd, gf)


def kernel(x, attn_norm_g, w_in, conv_w, conv_b, lru_wa_fwd, lru_ba_fwd, lru_wx_fwd, lru_bx_fwd, lru_lam_fwd, lru_wa_bwd, lru_ba_bwd, lru_wx_bwd, lru_bx_bwd, lru_lam_bwd, rel_bias, norm_rnn_g, norm_attn_g, w_out, mlp_norm_g, w_up, w_down, final_norm_g):
    depth = w_in.shape[0]
    assert depth == 1, "the final RMSNorm is fused into the single layer's last call"
    l = 0
    n_pat = len(PATTERNS)
    row = lambda v: v.reshape(1, -1)
    bias = _bias_tables(rel_bias)
    proj = _in_proj(x, row(attn_norm_g[l]), w_in[l].astype(BF16), tm=RNN_CHUNK)
    xr, gate = proj[0], proj[1]
    qs, ks, vs = (proj[2 + a * n_pat:2 + (a + 1) * n_pat] for a in range(3))
    fwd = (conv_w[l], row(conv_b[l]), _gate_weights(lru_wa_fwd[l]), _gate_weights(lru_wx_fwd[l]),
           row(lru_ba_fwd[l]), row(lru_bx_fwd[l]), row(lru_lam_fwd[l]), None)
    bwd = (conv_w[l], row(conv_b[l]), _gate_weights(lru_wa_bwd[l]), _gate_weights(lru_wx_bwd[l]),
           row(lru_ba_bwd[l]), row(lru_bx_bwd[l]), row(lru_lam_bwd[l]), row(norm_rnn_g[l]))
    h_f = _rnn_pass(xr, (), fwd, reverse=False, final=False)
    mix_rnn = _rnn_pass(xr, (h_f, gate), bwd, reverse=True, final=True)
    os_, lses = [], []
    for pat in range(n_pat):
        o, lse = _attn_pattern(qs[pat], ks[pat], vs[pat], bias, pat)
        os_.append(o)
        lses.append(lse)
    return _out_mlp(x, mix_rnn, os_, lses, row(norm_attn_g[l]), w_out[l].astype(BF16),
                    row(mlp_norm_g[l]), w_up[l].astype(BF16), w_down[l].astype(BF16),
                    row(final_norm_g), tm=512)
```

```python
import functools
import math

import numpy as np
import jax
import jax.numpy as jnp
from jax import lax
from jax.experimental import pallas as pl
from jax.experimental.pallas import tpu as pltpu

F32 = jnp.float32
BF16 = jnp.bfloat16

D_MODEL = 1024
D_RNN = 512
N_RNN_BLOCKS = 8
RNN_BLOCK = D_RNN // N_RNN_BLOCKS
CONV_WIDTH = 4
CONV_LEFT = 2
LRU_C = 8.0
N_HEADS = 8
HEAD_DIM = 64
D_ATTN = N_HEADS * HEAD_DIM
PATTERNS = ((128, 1), (512, 4), (2048, 16))
HALF_STEPS = 64
N_BUCKETS = 32
MAX_DISTANCE = 1024
D_IN = 2 * D_RNN + 3 * D_ATTN
D_FF = 4 * D_MODEL
EPS = 1e-6
NEG_INF = -1e30

LANES = 128
SUBLANES = 8
MXU_DIM = 256

Q_TILE = 128
K_TILE = 64
K_WIN = Q_TILE + 2 * HALF_STEPS
MAX_TILES_PER_STEP = 8
LOG2E = math.log2(math.e)
INTERIOR, FIRST, LAST = 0, 1, 2
GATE_BLK = MXU_DIM

VMEM_LIMIT = 48 * 1024 * 1024
ATTN_VMEM_LIMIT = 56 * 1024 * 1024


def _rms(x, g):
    ms = jnp.mean(x * x, axis=-1, keepdims=True)
    return x * lax.rsqrt(ms + EPS) * g


def _t5_bucket_np(rel):
    nb = N_BUCKETS // 2
    max_exact = nb // 2
    ret = np.where(rel > 0, nb, 0)
    n = np.abs(rel)
    nf = np.maximum(n, 1).astype(np.float32)
    large = max_exact + (np.log(nf / np.float32(max_exact)) / np.float32(math.log(MAX_DISTANCE / max_exact))
                         * np.float32(nb - max_exact)).astype(np.int32)
    large = np.minimum(large, nb - 1)
    return ret + np.where(n < max_exact, n, large)


def _bucket_index_tables():
    qi = np.arange(Q_TILE)[:, None]
    ci = np.arange(K_WIN)[None, :]
    step = ci - qi - HALF_STEPS
    tabs = []
    for _, dil in PATTERNS:
        b = _t5_bucket_np((step * dil).astype(np.int32))
        tabs.append(np.where(np.abs(step) <= HALF_STEPS, b, -1))
    return np.stack(tabs, axis=0).astype(np.int32)


def _bias_table_kernel(idx_ref, rb_ref, out_ref):
    idx = idx_ref[0]
    col = lax.broadcasted_iota(jnp.int32, idx.shape, 1)
    hit = [idx == b for b in range(N_BUCKETS)]
    for h in range(N_HEADS):
        acc = jnp.full(idx.shape, NEG_INF, F32)
        for b in range(N_BUCKETS):
            acc = jnp.where(hit[b], rb_ref[b, h] * LOG2E, acc)
        out_ref[0, INTERIOR, h] = acc
        out_ref[0, FIRST, h] = jnp.where(col < K_WIN - HALF_STEPS, pltpu.roll(acc, K_WIN - HALF_STEPS, 1), NEG_INF)
        out_ref[0, LAST, h] = jnp.where(col >= HALF_STEPS, pltpu.roll(acc, HALF_STEPS, 1), NEG_INF)


def _bias_tables(rel_bias):
    idx = jnp.asarray(_bucket_index_tables())
    n_pat = idx.shape[0]
    return pl.pallas_call(
        _bias_table_kernel,
        grid=(n_pat,),
        in_specs=[
            pl.BlockSpec((1, Q_TILE, K_WIN), lambda g: (g, 0, 0)),
            pl.BlockSpec(memory_space=pltpu.SMEM),
        ],
        out_specs=pl.BlockSpec((1, 3, N_HEADS, Q_TILE, K_WIN), lambda g: (g, 0, 0, 0, 0)),
        out_shape=jax.ShapeDtypeStruct((n_pat, 3, N_HEADS, Q_TILE, K_WIN), F32),
        name="bias_table",
    )(idx, rel_bias)


def _stripe_store(ref, p, part):
    stripes = IN_TILE // STRIPE_ROWS
    for st in range(stripes):
        for j in range(D_RNN // LANES):
            ref[0, j, pl.ds(part * stripes + st, STRIPE_ROWS, stride=SUBLANES), :] = (
                p[st * STRIPE_ROWS:(st + 1) * STRIPE_ROWS, j * LANES:(j + 1) * LANES])


PERM_ROWS = MXU_DIM


def _class_permutations():
    mats = []
    for _, dil in PATTERNS:
        if dil == 1:
            continue
        per = PERM_ROWS // dil
        p = np.zeros((PERM_ROWS, PERM_ROWS), np.float32)
        for r in range(dil):
            for m in range(per):
                p[r * per + m, m * dil + r] = 1.0
        mats.append(p)
    return np.stack(mats)


def _in_proj_kernel(x_ref, g_ref, w_ref, perm_ref, xr_ref, gate_ref, *rest, tm):
    n_pat = len(PATTERNS)
    outs = [rest[a * n_pat:(a + 1) * n_pat] for a in range(3)]
    h_scr = rest[3 * n_pat]
    h_scr[...] = _rms(x_ref[0], g_ref[...]).astype(BF16)

    def seg(lo, width):
        return jnp.dot(h_scr[...], w_ref[:, lo:lo + width], preferred_element_type=F32)

    for a, scale in enumerate((HEAD_DIM ** -0.5 * LOG2E, 1.0, 1.0)):
        p = seg(2 * D_RNN + a * D_ATTN, D_ATTN)
        if scale != 1.0:
            p = p * scale
        pb = p.astype(BF16)
        dilated = 0
        for (_, dil), o_ref in zip(PATTERNS, outs[a]):
            if dil == 1:
                o_ref[0, 0] = pb
                continue
            per = PERM_ROWS // dil
            for blk in range(tm // PERM_ROWS):
                y = jnp.dot(perm_ref[dilated], pb[blk * PERM_ROWS:(blk + 1) * PERM_ROWS, :],
                            preferred_element_type=F32).astype(BF16)
                for r in range(dil):
                    o_ref[0, r, blk * per:(blk + 1) * per, :] = y[r * per:(r + 1) * per, :]
            dilated += 1
    g = seg(D_RNN, D_RNN)
    gelu = g * (0.5 * (1.0 + jnp.tanh(math.sqrt(2.0 / math.pi) * (g + 0.044715 * (g * g * g)))))
    part = pl.program_id(1) % (RNN_CHUNK // IN_TILE)
    _stripe_store(gate_ref, gelu, part)
    _stripe_store(xr_ref, seg(0, D_RNN), part)


def _in_proj(x3, g, w_in_bf, tm):
    assert tm == IN_TILE
    bsz, s, _ = x3.shape
    row = lambda b, i: (b, i, 0)
    cls = lambda b, i: (b, 0, i, 0)
    chunk = lambda b, i: (b, 0, i // (RNN_CHUNK // IN_TILE), 0)
    const = lambda b, i: (0, 0)
    perms = jnp.asarray(_class_permutations(), BF16)
    qkv_specs, qkv_shapes = [], []
    for _ in range(3):
        for _, dil in PATTERNS:
            qkv_specs.append(pl.BlockSpec((1, dil, tm // dil, D_ATTN), cls))
            qkv_shapes.append(jax.ShapeDtypeStruct((bsz, dil, s // dil, D_ATTN), BF16))
    return pl.pallas_call(
        functools.partial(_in_proj_kernel, tm=tm),
        grid=(bsz, s // tm),
        in_specs=[
            pl.BlockSpec((1, tm, D_MODEL), row),
            pl.BlockSpec((1, D_MODEL), const),
            pl.BlockSpec((D_MODEL, D_IN), const),
            pl.BlockSpec(perms.shape, lambda b, i: (0, 0, 0)),
        ],
        out_specs=[pl.BlockSpec((1, D_RNN // LANES, RNN_CHUNK, LANES), chunk)] * 2 + qkv_specs,
        out_shape=[jax.ShapeDtypeStruct((bsz, D_RNN // LANES, s, LANES), F32)] * 2 + qkv_shapes,
        scratch_shapes=[pltpu.VMEM((tm, D_MODEL), BF16)],
        compiler_params=pltpu.CompilerParams(
            dimension_semantics=("arbitrary", "arbitrary"), vmem_limit_bytes=VMEM_LIMIT),
        name="in_proj",
    )(x3, g, w_in_bf, perms)


IN_TILE = 512
RNN_CHUNK = 1024
STRIPE_ROWS = RNN_CHUNK // SUBLANES


def _scan_block(a, b, reverse):
    n = a.shape[0]
    row = lax.broadcasted_iota(jnp.int32, a.shape, 0)
    s = 1
    while s < n:
        if reverse:
            ra = pltpu.roll(a, n - s, 0)
            rb = pltpu.roll(b, n - s, 0)
            m = row < n - s
        else:
            ra = pltpu.roll(a, s, 0)
            rb = pltpu.roll(b, s, 0)
            m = row >= s
        b = jnp.where(m, a * rb + b, b)
        a = jnp.where(m, a * ra, a)
        s *= 2
    return a, b


def _rnn_kernel(*refs, reverse, final, nchunks):
    if final:
        (xc_ref, hf_ref, gate_ref, wa_ref, wx_ref, ba_ref, bx_ref, lam_ref, gn_ref,
         out_ref, a_scr, b_scr, h_scr, carry) = refs
    else:
        (pa_ref, pb_ref, cur_ref, nx_ref, cw_ref, cb_ref, wa_ref, wx_ref, ba_ref, bx_ref, lam_ref,
         out_ref, xc_out_ref, a_scr, b_scr, carry) = refs
        h_scr = None

    step = pl.program_id(1)
    chunk = (nchunks - 1 - step) if reverse else step
    n_slab = D_RNN // LANES
    nv = STRIPE_ROWS
    sub = lax.broadcasted_iota(jnp.int32, (SUBLANES, LANES), 0)
    top, bot = SUBLANES - 1, 0

    @pl.when(step == 0)
    def _():
        carry[...] = jnp.zeros_like(carry)

    if final:
        xc = [xc_ref[0, j] for j in range(n_slab)]
    else:
        xc = []
        for j in range(n_slab):
            lanes = slice(j * LANES, (j + 1) * LANES)
            x = cur_ref[0, j].reshape(nv, SUBLANES, LANES)
            pa = jnp.where(chunk > 0, pa_ref[0, j], 0.0)
            pb = jnp.where(chunk > 0, pb_ref[0, j], 0.0)
            nx = jnp.where(chunk < nchunks - 1, nx_ref[0, j], 0.0)
            xm2 = pltpu.roll(jnp.where(sub == top, pa, x[nv - 2]), 1, 0)
            xm1 = pltpu.roll(jnp.where(sub == top, pb, x[nv - 1]), 1, 0)
            xp1 = pltpu.roll(jnp.where(sub == bot, nx, x[0]), SUBLANES - 1, 0)
            xext = jnp.concatenate([xm2[None], xm1[None], x, xp1[None]], axis=0)
            acc = cb_ref[:, lanes] + xext[0:nv] * cw_ref[0:1, lanes]
            for k in range(1, CONV_WIDTH):
                acc = acc + xext[k:k + nv] * cw_ref[k:k + 1, lanes]
            xc.append(acc.reshape(RNN_CHUNK, LANES))
            xc_out_ref[0, j] = xc[j]

    nlam = -lam_ref[...]
    softplus = jnp.maximum(nlam, 0.0) + jnp.log1p(jnp.exp(-jnp.abs(nlam)))
    coef = -LRU_C * softplus
    slabs_per_blk = GATE_BLK // LANES
    for jj in range(D_RNN // GATE_BLK):
        blk = slice(jj * GATE_BLK, (jj + 1) * GATE_BLK)
        xj = jnp.concatenate(xc[jj * slabs_per_blk:(jj + 1) * slabs_per_blk], axis=-1)
        xjb = xj.astype(BF16)
        za = jnp.dot(xjb, wa_ref[jj], preferred_element_type=F32) + ba_ref[:, blk]
        zx = jnp.dot(xjb, wx_ref[jj], preferred_element_type=F32) + bx_ref[:, blk]
        r = 1.0 / (1.0 + jnp.exp(-za))
        i = 1.0 / (1.0 + jnp.exp(-zx))
        log_a = coef[:, blk] * r
        a = jnp.exp(log_a)
        u = -jnp.tanh(log_a) * (1.0 + a * a)
        root = jnp.where(u > 0.0, u * lax.rsqrt(u), 0.0)
        b = root * (i * xj)
        for t in range(slabs_per_blk):
            a_scr[jj * slabs_per_blk + t] = a[:, t * LANES:(t + 1) * LANES]
            b_scr[jj * slabs_per_blk + t] = b[:, t * LANES:(t + 1) * LANES]

    order = range(nv - 1, -1, -1) if reverse else range(nv)
    group = lambda v: slice(v * SUBLANES, (v + 1) * SUBLANES)

    h = [jnp.zeros((SUBLANES, LANES), F32)] * n_slab
    p = [jnp.ones((SUBLANES, LANES), F32)] * n_slab
    for v in order:
        for j in range(n_slab):
            av = a_scr[j, group(v), :]
            h[j] = av * h[j] + b_scr[j, group(v), :]
            p[j] = av * p[j]

    start = []
    for j in range(n_slab):
        pc, hc = _scan_block(p[j], h[j], reverse)
        prev = carry[j]
        if reverse:
            seed = jnp.broadcast_to(prev[bot:bot + 1, :], (SUBLANES, LANES))
            ends = hc + pc * seed
            start.append(jnp.where(sub == top, pltpu.roll(prev, SUBLANES - 1, 0),
                                   pltpu.roll(ends, SUBLANES - 1, 0)))
        else:
            seed = jnp.broadcast_to(prev[top:top + 1, :], (SUBLANES, LANES))
            ends = hc + pc * seed
            start.append(jnp.where(sub == bot, pltpu.roll(prev, 1, 0), pltpu.roll(ends, 1, 0)))
        carry[j] = ends

    dst = h_scr if final else None
    h = start
    for v in order:
        for j in range(n_slab):
            h[j] = a_scr[j, group(v), :] * h[j] + b_scr[j, group(v), :]
            if final:
                dst[j, group(v), :] = h[j]
            else:
                out_ref[0, j, group(v), :] = h[j]

    if final:
        ys = []
        for j in range(n_slab):
            ys.append((hf_ref[0, j] + h_scr[j]) * gate_ref[0, j])
        ss = ys[0] * ys[0]
        for j in range(1, n_slab):
            ss = ss + ys[j] * ys[j]
        inv = lax.rsqrt(jnp.sum(ss, axis=-1, keepdims=True) * (1.0 / D_RNN) + EPS)
        for j in range(n_slab):
            lanes = slice(j * LANES, (j + 1) * LANES)
            h_scr[j] = ys[j] * inv * gn_ref[:, lanes]
        for st in range(SUBLANES):
            for j in range(n_slab):
                out_ref[0, st * nv:(st + 1) * nv, j * LANES:(j + 1) * LANES] = (
                    h_scr[j, pl.ds(st, nv, stride=SUBLANES), :].astype(BF16))


def _rnn_pass(x4, extra, params, reverse, final):
    bsz, n_slab, s, _ = x4.shape
    nchunks = s // RNN_CHUNK
    hb = RNN_CHUNK // SUBLANES
    nhalo = s // SUBLANES

    def cidx(c):
        return (nchunks - 1 - c) if reverse else c

    cur_map = lambda b, c: (b, 0, cidx(c), 0)
    const2 = lambda b, c: (0, 0)
    const3 = lambda b, c: (0, 0, 0)
    cw, cb, wa, wx, ba, bx, lam, gn = params
    halo = (1, n_slab, SUBLANES, LANES)
    full = (1, n_slab, RNN_CHUNK, LANES)
    gate_specs = [
        pl.BlockSpec((D_RNN // GATE_BLK, GATE_BLK, GATE_BLK), const3),
        pl.BlockSpec((D_RNN // GATE_BLK, GATE_BLK, GATE_BLK), const3),
        pl.BlockSpec((1, D_RNN), const2),
        pl.BlockSpec((1, D_RNN), const2),
        pl.BlockSpec((1, D_RNN), const2),
    ]
    slab_scr = pltpu.VMEM((n_slab, RNN_CHUNK, LANES), F32)
    carry_scr = pltpu.VMEM((n_slab, SUBLANES, LANES), F32)
    if final:
        in_specs = [pl.BlockSpec(full, cur_map)] * 3 + gate_specs + [pl.BlockSpec((1, D_RNN), const2)]
        args = [x4, *extra, wa, wx, ba, bx, lam, gn]
        scratch = [slab_scr, slab_scr, slab_scr, carry_scr]
        out_specs = pl.BlockSpec((1, RNN_CHUNK, D_RNN), lambda b, c: (b, cidx(c), 0))
        out_shape = jax.ShapeDtypeStruct((bsz, s, D_RNN), BF16)
    else:
        pa_map = lambda b, c: (b, 0, jnp.maximum(cidx(c) * hb - 2, 0), 0)
        pb_map = lambda b, c: (b, 0, jnp.maximum(cidx(c) * hb - 1, 0), 0)
        nx_map = lambda b, c: (b, 0, jnp.minimum((cidx(c) + 1) * hb, nhalo - 1), 0)
        in_specs = [pl.BlockSpec(halo, pa_map), pl.BlockSpec(halo, pb_map), pl.BlockSpec(full, cur_map),
                    pl.BlockSpec(halo, nx_map),
                    pl.BlockSpec((CONV_WIDTH, D_RNN), const2), pl.BlockSpec((1, D_RNN), const2)] + gate_specs
        args = [x4, x4, x4, x4, cw, cb, wa, wx, ba, bx, lam]
        scratch = [slab_scr, slab_scr, carry_scr]
        out_specs = [pl.BlockSpec(full, cur_map)] * 2
        out_shape = [jax.ShapeDtypeStruct(x4.shape, F32)] * 2
    return pl.pallas_call(
        functools.partial(_rnn_kernel, reverse=reverse, final=final, nchunks=nchunks),
        grid=(bsz, nchunks),
        in_specs=in_specs,
        out_specs=out_specs,
        out_shape=out_shape,
        scratch_shapes=scratch,
        compiler_params=pltpu.CompilerParams(
            dimension_semantics=("arbitrary", "arbitrary"), vmem_limit_bytes=VMEM_LIMIT),
        name="rnn_bwd" if reverse else "rnn_fwd",
    )(*args)


def _gate_weights(w):
    per = GATE_BLK // RNN_BLOCK
    w5 = w.reshape(D_RNN // GATE_BLK, per, RNN_BLOCK, 1, RNN_BLOCK)
    on_diag = jnp.asarray(np.eye(per, dtype=bool)).reshape(1, per, 1, per, 1)
    dense = jnp.where(on_diag, w5, 0.0)
    return dense.reshape(D_RNN // GATE_BLK, GATE_BLK, GATE_BLK).astype(BF16)


def _attn_kernel(q_ref, k_ref, v_ref, bias_ref, o_ref, lse_ref, *, sub_len, tiles_per_step):
    i = pl.program_id(2)
    n_tiles = sub_len // Q_TILE
    lane = lax.broadcasted_iota(jnp.int32, (Q_TILE, LANES), 1)
    heads_per_slab = LANES // HEAD_DIM
    first_half = lane < HEAD_DIM
    for cl in range(q_ref.shape[1]):
        for t in range(tiles_per_step):
            rows = slice(t * Q_TILE, (t + 1) * Q_TILE)
            q = q_ref[0, cl, rows, :]
            tile = i * tiles_per_step + t
            variant = jnp.where(tile == 0, FIRST, jnp.where(tile == n_tiles - 1, LAST, INTERIOR))
            start = pl.multiple_of(jnp.clip(tile * Q_TILE - HALF_STEPS, 0, sub_len - K_WIN), K_TILE)
            kk = k_ref[0, cl, pl.ds(start, K_WIN), :]
            vv = v_ref[0, cl, pl.ds(start, K_WIN), :]
            for p in range(D_ATTN // LANES):
                sl = slice(p * LANES, (p + 1) * LANES)
                qp, kp, vp = q[:, sl], kk[:, sl], vv[:, sl]
                zero = jnp.zeros_like(qp)
                q2 = jnp.concatenate([jnp.where(first_half, qp, zero), jnp.where(first_half, zero, qp)], axis=0)
                s = lax.dot_general(q2, kp, (((1,), (1,)), ((), ())), preferred_element_type=F32)
                h0 = p * heads_per_slab
                s = s + bias_ref[variant, h0:h0 + heads_per_slab].reshape(heads_per_slab * Q_TILE, K_WIN)
                m = jnp.max(s, axis=-1, keepdims=True)
                e = jnp.exp2(s - m)
                l = jnp.sum(e, axis=-1, keepdims=True)
                o2 = jnp.dot(e.astype(BF16), vp, preferred_element_type=F32) * (1.0 / l)
                lse2 = jnp.broadcast_to(m + jnp.log(l) * LOG2E, (heads_per_slab * Q_TILE, LANES))
                o_ref[0, cl, rows, sl] = jnp.where(first_half, o2[:Q_TILE], o2[Q_TILE:])
                lse_ref[0, cl, rows, sl] = jnp.where(first_half, lse2[:Q_TILE], lse2[Q_TILE:])


def _attn_pattern(qc, kc, vc, bias, pat):
    bsz, dil, sub_len, _ = qc.shape
    tiles_per_step = min(MAX_TILES_PER_STEP, sub_len // Q_TILE)
    classes_per_step = MAX_TILES_PER_STEP // tiles_per_step
    q_rows = tiles_per_step * Q_TILE
    qmap = lambda b, r, i: (b, r, i, 0)
    seq_map = lambda b, r, i: (b, r, 0, 0)
    in_specs = [
        pl.BlockSpec((1, classes_per_step, q_rows, D_ATTN), qmap),
        pl.BlockSpec((1, classes_per_step, sub_len, D_ATTN), seq_map),
        pl.BlockSpec((1, classes_per_step, sub_len, D_ATTN), seq_map),
        pl.BlockSpec((None, 3, N_HEADS, Q_TILE, K_WIN), lambda b, r, i: (pat, 0, 0, 0, 0)),
    ]
    return pl.pallas_call(
        functools.partial(_attn_kernel, sub_len=sub_len, tiles_per_step=tiles_per_step),
        grid=(bsz, dil // classes_per_step, sub_len // q_rows),
        in_specs=in_specs,
        out_specs=[pl.BlockSpec((1, classes_per_step, q_rows, D_ATTN), qmap)] * 2,
        out_shape=[jax.ShapeDtypeStruct((bsz, dil, sub_len, D_ATTN), F32)] * 2,
        compiler_params=pltpu.CompilerParams(
            dimension_semantics=("arbitrary",) * 3, vmem_limit_bytes=ATTN_VMEM_LIMIT),
        name=f"attn_d{dil}",
    )(qc, kc, vc, bias)


FF_CHUNK = 1024


SUB_ROWS = 512


def _natural_order(ref, scr, u):
    dil = ref.shape[1]
    per = SUB_ROWS // dil
    if dil == 1:
        return ref[0, 0, u * per:(u + 1) * per, :]
    n_slab = D_ATTN // LANES
    for r in range(dil):
        for j in range(n_slab):
            scr[u, j, pl.ds(r, per, stride=dil), :] = ref[0, r, u * per:(u + 1) * per, j * LANES:(j + 1) * LANES]
    return jnp.concatenate([scr[u, j] for j in range(n_slab)], axis=-1)


def _out_mlp_kernel(x_ref, mr_ref, o1, o2, o3, l1, l2, l3, ga_ref, wo_ref, gm_ref, wu_ref, wd_ref,
                    gf_ref, out_ref, s_o2, s_o3, s_l2, s_l3, *, tm):
    n_sub = tm // SUB_ROWS

    def merge(u):
        a1 = _natural_order(l1, None, u)
        a2 = _natural_order(l2, s_l2, u)
        a3 = _natural_order(l3, s_l3, u)
        mx = jnp.maximum(jnp.maximum(a1, a2), a3)
        e1, e2, e3 = jnp.exp2(a1 - mx), jnp.exp2(a2 - mx), jnp.exp2(a3 - mx)
        y_attn = (e1 * _natural_order(o1, None, u) + e2 * _natural_order(o2, s_o2, u)
                  + e3 * _natural_order(o3, s_o3, u)) / (e1 + e2 + e3)
        return _rms(y_attn, ga_ref[...]).astype(BF16)

    def out_proj(u, mix_attn):
        rows = slice(u * SUB_ROWS, (u + 1) * SUB_ROWS)
        proj = jnp.dot(mr_ref[0, rows, :], wo_ref[0:D_RNN, :], preferred_element_type=F32)
        proj = proj + jnp.dot(mix_attn, wo_ref[D_RNN:D_RNN + D_ATTN, :], preferred_element_type=F32)
        x1 = proj + x_ref[0, rows, :]
        return x1, _rms(x1, gm_ref[...]).astype(BF16)

    def ff_chunk(h, c):
        z = jnp.dot(h, wu_ref[:, c * FF_CHUNK:(c + 1) * FF_CHUNK], preferred_element_type=F32)
        z = jnp.square(jnp.maximum(z, 0.0)).astype(BF16)
        return jnp.dot(z, wd_ref[c * FF_CHUNK:(c + 1) * FF_CHUNK, :], preferred_element_type=F32)

    def finish(u, ff, x1):
        rows = slice(u * SUB_ROWS, (u + 1) * SUB_ROWS)
        out_ref[0, rows, :] = _rms(ff + x1, gf_ref[...])

    n_chunk = D_FF // FF_CHUNK
    x1, h = out_proj(0, merge(0))
    pending = None
    for u in range(n_sub):
        ff = None
        nxt = None
        for c in range(n_chunk):
            d = ff_chunk(h, c)
            ff = d if ff is None else ff + d
            if c == 0 and pending is not None:
                finish(*pending)
            if c == 1 and u + 1 < n_sub:
                mix_next = merge(u + 1)
            if c == 2 and u + 1 < n_sub:
                nxt = out_proj(u + 1, mix_next)
        pending = (u, ff, x1)
        if nxt is not None:
            x1, h = nxt
    finish(*pending)


def _out_mlp(x3, mix_rnn, os_, lses, ga, wo, gm, wu, wd, gf, tm):
    bsz, s, _ = x3.shape
    row = lambda b, i: (b, i, 0)
    cls = lambda b, i: (b, 0, i, 0)
    const = lambda b, i: (0, 0)
    once = pl.Buffered(1)
    cls_specs = [pl.BlockSpec((1, dil, tm // dil, D_ATTN), cls) for _, dil in PATTERNS]
    in_specs = [pl.BlockSpec((1, tm, D_MODEL), row), pl.BlockSpec((1, tm, D_RNN), row)]
    in_specs += cls_specs + cls_specs
    in_specs += [
        pl.BlockSpec((1, D_ATTN), const),
        pl.BlockSpec((D_RNN + D_ATTN, D_MODEL), const, pipeline_mode=once),
        pl.BlockSpec((1, D_MODEL), const),
        pl.BlockSpec((D_MODEL, D_FF), const, pipeline_mode=once),
        pl.BlockSpec((D_FF, D_MODEL), const, pipeline_mode=once),
        pl.BlockSpec((1, D_MODEL), const),
    ]
    return pl.pallas_call(
        functools.partial(_out_mlp_kernel, tm=tm),
        grid=(bsz, s // tm),
        in_specs=in_specs,
        out_specs=pl.BlockSpec((1, tm, D_MODEL), row),
        out_shape=jax.ShapeDtypeStruct((bsz, s, D_MODEL), F32),
        scratch_shapes=[pltpu.VMEM((tm // SUB_ROWS, D_ATTN // LANES, SUB_ROWS, LANES), F32)] * 4,
        compiler_params=pltpu.CompilerParams(
            dimension_semantics=("arbitrary", "arbitrary"), vmem_limit_bytes=ATTN_VMEM_LIMIT),
        name="out_mlp",
    )(x3, mix_rnn, *os_, *lses, ga, wo, gm, wu, wd, gf)


def kernel(x, attn_norm_g, w_in, conv_w, conv_b, lru_wa_fwd, lru_ba_fwd, lru_wx_fwd, lru_bx_fwd, lru_lam_fwd, lru_wa_bwd, lru_ba_bwd, lru_wx_bwd, lru_bx_bwd, lru_lam_bwd, rel_bias, norm_rnn_g, norm_attn_g, w_out, mlp_norm_g, w_up, w_down, final_norm_g):
    depth = w_in.shape[0]
    assert depth == 1, "the final RMSNorm is fused into the single layer's last call"
    l = 0
    n_pat = len(PATTERNS)
    row = lambda v: v.reshape(1, -1)
    bias = _bias_tables(rel_bias)
    proj = _in_proj(x, row(attn_norm_g[l]), w_in[l].astype(BF16), tm=IN_TILE)
    xr, gate = proj[0], proj[1]
    qs, ks, vs = (proj[2 + a * n_pat:2 + (a + 1) * n_pat] for a in range(3))
    fwd = (conv_w[l], row(conv_b[l]), _gate_weights(lru_wa_fwd[l]), _gate_weights(lru_wx_fwd[l]),
           row(lru_ba_fwd[l]), row(lru_bx_fwd[l]), row(lru_lam_fwd[l]), None)
    bwd = (conv_w[l], row(conv_b[l]), _gate_weights(lru_wa_bwd[l]), _gate_weights(lru_wx_bwd[l]),
           row(lru_ba_bwd[l]), row(lru_bx_bwd[l]), row(lru_lam_bwd[l]), row(norm_rnn_g[l]))
    h_f, xc = _rnn_pass(xr, (), fwd, reverse=False, final=False)
    mix_rnn = _rnn_pass(xc, (h_f, gate), bwd, reverse=True, final=True)
    os_, lses = [], []
    for pat in range(n_pat):
        o, lse = _attn_pattern(qs[pat], ks[pat], vs[pat], bias, pat)
        os_.append(o)
        lses.append(lse)
    return _out_mlp(x, mix_rnn, os_, lses, row(norm_attn_g[l]), w_out[l].astype(BF16),
                    row(mlp_norm_g[l]), w_up[l].astype(BF16), w_down[l].astype(BF16),
                    row(final_norm_g), tm=512)
```

```python
import functools
import math

import numpy as np
import jax
import jax.numpy as jnp
from jax import lax
from jax.experimental import pallas as pl
from jax.experimental.pallas import tpu as pltpu

F32 = jnp.float32
BF16 = jnp.bfloat16

D_MODEL = 1024
D_RNN = 512
N_RNN_BLOCKS = 8
RNN_BLOCK = D_RNN // N_RNN_BLOCKS
CONV_WIDTH = 4
CONV_LEFT = 2
LRU_C = 8.0
N_HEADS = 8
HEAD_DIM = 64
D_ATTN = N_HEADS * HEAD_DIM
PATTERNS = ((128, 1), (512, 4), (2048, 16))
HALF_STEPS = 64
N_BUCKETS = 32
MAX_DISTANCE = 1024
D_IN = 2 * D_RNN + 3 * D_ATTN
D_FF = 4 * D_MODEL
EPS = 1e-6
NEG_INF = -1e30

LANES = 128
SUBLANES = 8
MXU_DIM = 256

Q_TILE = 128
K_TILE = 64
K_WIN = Q_TILE + 2 * HALF_STEPS
MAX_TILES_PER_STEP = 8
LOG2E = math.log2(math.e)
INTERIOR, FIRST, LAST = 0, 1, 2
GATE_BLK = MXU_DIM

VMEM_LIMIT = 48 * 1024 * 1024
ATTN_VMEM_LIMIT = 56 * 1024 * 1024


def _rms(x, g):
    ms = jnp.mean(x * x, axis=-1, keepdims=True)
    return x * lax.rsqrt(ms + EPS) * g


def _t5_bucket_np(rel):
    nb = N_BUCKETS // 2
    max_exact = nb // 2
    ret = np.where(rel > 0, nb, 0)
    n = np.abs(rel)
    nf = np.maximum(n, 1).astype(np.float32)
    large = max_exact + (np.log(nf / np.float32(max_exact)) / np.float32(math.log(MAX_DISTANCE / max_exact))
                         * np.float32(nb - max_exact)).astype(np.int32)
    large = np.minimum(large, nb - 1)
    return ret + np.where(n < max_exact, n, large)


def _bucket_index_tables():
    qi = np.arange(Q_TILE)[:, None]
    ci = np.arange(K_WIN)[None, :]
    step = ci - qi - HALF_STEPS
    tabs = []
    for _, dil in PATTERNS:
        b = _t5_bucket_np((step * dil).astype(np.int32))
        tabs.append(np.where(np.abs(step) <= HALF_STEPS, b, -1))
    return np.stack(tabs, axis=0).astype(np.int32)


def _bias_table_kernel(idx_ref, rb_ref, out_ref):
    idx = idx_ref[0]
    col = lax.broadcasted_iota(jnp.int32, idx.shape, 1)
    hit = [idx == b for b in range(N_BUCKETS)]
    for h in range(N_HEADS):
        acc = jnp.full(idx.shape, NEG_INF, F32)
        for b in range(N_BUCKETS):
            acc = jnp.where(hit[b], rb_ref[b, h] * LOG2E, acc)
        out_ref[0, INTERIOR, h] = acc
        out_ref[0, FIRST, h] = jnp.where(col < K_WIN - HALF_STEPS, pltpu.roll(acc, K_WIN - HALF_STEPS, 1), NEG_INF)
        out_ref[0, LAST, h] = jnp.where(col >= HALF_STEPS, pltpu.roll(acc, HALF_STEPS, 1), NEG_INF)


def _bias_tables(rel_bias):
    idx = jnp.asarray(_bucket_index_tables())
    n_pat = idx.shape[0]
    return pl.pallas_call(
        _bias_table_kernel,
        grid=(n_pat,),
        in_specs=[
            pl.BlockSpec((1, Q_TILE, K_WIN), lambda g: (g, 0, 0)),
            pl.BlockSpec(memory_space=pltpu.SMEM),
        ],
        out_specs=pl.BlockSpec((1, 3, N_HEADS, Q_TILE, K_WIN), lambda g: (g, 0, 0, 0, 0)),
        out_shape=jax.ShapeDtypeStruct((n_pat, 3, N_HEADS, Q_TILE, K_WIN), F32),
        name="bias_table",
    )(idx, rel_bias)


def _stripe_store(ref, p, part):
    stripes = IN_TILE // STRIPE_ROWS
    for st in range(stripes):
        for j in range(D_RNN // LANES):
            ref[0, j, pl.ds(part * stripes + st, STRIPE_ROWS, stride=SUBLANES), :] = (
                p[st * STRIPE_ROWS:(st + 1) * STRIPE_ROWS, j * LANES:(j + 1) * LANES])


PERM_ROWS = MXU_DIM


def _class_permutations():
    mats = []
    for _, dil in PATTERNS:
        if dil == 1:
            continue
        per = PERM_ROWS // dil
        p = np.zeros((PERM_ROWS, PERM_ROWS), np.float32)
        for r in range(dil):
            for m in range(per):
                p[r * per + m, m * dil + r] = 1.0
        mats.append(p)
    return np.stack(mats)


def _in_proj_kernel(x_ref, g_ref, w_ref, perm_ref, xr_ref, gate_ref, *rest, tm):
    n_pat = len(PATTERNS)
    outs = [rest[a * n_pat:(a + 1) * n_pat] for a in range(3)]
    h_scr = rest[3 * n_pat]
    h_scr[...] = _rms(x_ref[0], g_ref[...]).astype(BF16)

    def seg(lo, width):
        return jnp.dot(h_scr[...], w_ref[:, lo:lo + width], preferred_element_type=F32)

    for a, scale in enumerate((HEAD_DIM ** -0.5 * LOG2E, 1.0, 1.0)):
        p = seg(2 * D_RNN + a * D_ATTN, D_ATTN)
        if scale != 1.0:
            p = p * scale
        pb = p.astype(BF16)
        dilated = 0
        for (_, dil), o_ref in zip(PATTERNS, outs[a]):
            if dil == 1:
                o_ref[0, 0] = pb
                continue
            per = PERM_ROWS // dil
            for blk in range(tm // PERM_ROWS):
                y = jnp.dot(perm_ref[dilated], pb[blk * PERM_ROWS:(blk + 1) * PERM_ROWS, :],
                            preferred_element_type=F32).astype(BF16)
                for r in range(dil):
                    o_ref[0, r, blk * per:(blk + 1) * per, :] = y[r * per:(r + 1) * per, :]
            dilated += 1
    g = seg(D_RNN, D_RNN)
    gelu = g * (0.5 * (1.0 + jnp.tanh(math.sqrt(2.0 / math.pi) * (g + 0.044715 * (g * g * g)))))
    part = pl.program_id(1) % (RNN_CHUNK // IN_TILE)
    _stripe_store(gate_ref, gelu, part)
    _stripe_store(xr_ref, seg(0, D_RNN), part)


def _in_proj(x3, g, w_in_bf, tm):
    assert tm == IN_TILE
    bsz, s, _ = x3.shape
    row = lambda b, i: (b, i, 0)
    cls = lambda b, i: (b, 0, i, 0)
    chunk = lambda b, i: (b, 0, i // (RNN_CHUNK // IN_TILE), 0)
    const = lambda b, i: (0, 0)
    perms = jnp.asarray(_class_permutations(), BF16)
    qkv_specs, qkv_shapes = [], []
    for _ in range(3):
        for _, dil in PATTERNS:
            qkv_specs.append(pl.BlockSpec((1, dil, tm // dil, D_ATTN), cls))
            qkv_shapes.append(jax.ShapeDtypeStruct((bsz, dil, s // dil, D_ATTN), BF16))
    return pl.pallas_call(
        functools.partial(_in_proj_kernel, tm=tm),
        grid=(bsz, s // tm),
        in_specs=[
            pl.BlockSpec((1, tm, D_MODEL), row),
            pl.BlockSpec((1, D_MODEL), const),
            pl.BlockSpec((D_MODEL, D_IN), const, pipeline_mode=pl.Buffered(1)),
            pl.BlockSpec(perms.shape, lambda b, i: (0, 0, 0)),
        ],
        out_specs=[pl.BlockSpec((1, D_RNN // LANES, RNN_CHUNK, LANES), chunk)] * 2 + qkv_specs,
        out_shape=[jax.ShapeDtypeStruct((bsz, D_RNN // LANES, s, LANES), F32)] * 2 + qkv_shapes,
        scratch_shapes=[pltpu.VMEM((tm, D_MODEL), BF16)],
        compiler_params=pltpu.CompilerParams(
            dimension_semantics=("arbitrary", "arbitrary"), vmem_limit_bytes=VMEM_LIMIT),
        name="in_proj",
    )(x3, g, w_in_bf, perms)


IN_TILE = 1024
RNN_CHUNK = 1024
STRIPE_ROWS = RNN_CHUNK // SUBLANES


def _scan_block(a, b, reverse):
    n = a.shape[0]
    row = lax.broadcasted_iota(jnp.int32, a.shape, 0)
    s = 1
    while s < n:
        if reverse:
            ra = pltpu.roll(a, n - s, 0)
            rb = pltpu.roll(b, n - s, 0)
            m = row < n - s
        else:
            ra = pltpu.roll(a, s, 0)
            rb = pltpu.roll(b, s, 0)
            m = row >= s
        b = jnp.where(m, a * rb + b, b)
        a = jnp.where(m, a * ra, a)
        s *= 2
    return a, b


def _rnn_kernel(*refs, reverse, final, nchunks):
    if final:
        (xc_ref, hf_ref, gate_ref, wa_ref, wx_ref, ba_ref, bx_ref, lam_ref, gn_ref,
         out_ref, a_scr, b_scr, h_scr, carry) = refs
    else:
        (pa_ref, pb_ref, cur_ref, nx_ref, cw_ref, cb_ref, wa_ref, wx_ref, ba_ref, bx_ref, lam_ref,
         out_ref, xc_out_ref, a_scr, b_scr, carry) = refs
        h_scr = None

    step = pl.program_id(1)
    chunk = (nchunks - 1 - step) if reverse else step
    n_slab = D_RNN // LANES
    nv = STRIPE_ROWS
    sub = lax.broadcasted_iota(jnp.int32, (SUBLANES, LANES), 0)
    top, bot = SUBLANES - 1, 0

    @pl.when(step == 0)
    def _():
        carry[...] = jnp.zeros_like(carry)

    if final:
        xc = [xc_ref[0, j] for j in range(n_slab)]
    else:
        xc = []
        for j in range(n_slab):
            lanes = slice(j * LANES, (j + 1) * LANES)
            x = cur_ref[0, j].reshape(nv, SUBLANES, LANES)
            pa = jnp.where(chunk > 0, pa_ref[0, j], 0.0)
            pb = jnp.where(chunk > 0, pb_ref[0, j], 0.0)
            nx = jnp.where(chunk < nchunks - 1, nx_ref[0, j], 0.0)
            xm2 = pltpu.roll(jnp.where(sub == top, pa, x[nv - 2]), 1, 0)
            xm1 = pltpu.roll(jnp.where(sub == top, pb, x[nv - 1]), 1, 0)
            xp1 = pltpu.roll(jnp.where(sub == bot, nx, x[0]), SUBLANES - 1, 0)
            xext = jnp.concatenate([xm2[None], xm1[None], x, xp1[None]], axis=0)
            acc = cb_ref[:, lanes] + xext[0:nv] * cw_ref[0:1, lanes]
            for k in range(1, CONV_WIDTH):
                acc = acc + xext[k:k + nv] * cw_ref[k:k + 1, lanes]
            xc.append(acc.reshape(RNN_CHUNK, LANES))
            xc_out_ref[0, j] = xc[j]

    nlam = -lam_ref[...]
    softplus = jnp.maximum(nlam, 0.0) + jnp.log1p(jnp.exp(-jnp.abs(nlam)))
    coef = -LRU_C * softplus
    slabs_per_blk = GATE_BLK // LANES
    for jj in range(D_RNN // GATE_BLK):
        blk = slice(jj * GATE_BLK, (jj + 1) * GATE_BLK)
        xj = jnp.concatenate(xc[jj * slabs_per_blk:(jj + 1) * slabs_per_blk], axis=-1)
        xjb = xj.astype(BF16)
        za = jnp.dot(xjb, wa_ref[jj], preferred_element_type=F32) + ba_ref[:, blk]
        zx = jnp.dot(xjb, wx_ref[jj], preferred_element_type=F32) + bx_ref[:, blk]
        r = 1.0 / (1.0 + jnp.exp(-za))
        i = 1.0 / (1.0 + jnp.exp(-zx))
        log_a = coef[:, blk] * r
        a = jnp.exp(log_a)
        u = -jnp.tanh(log_a) * (1.0 + a * a)
        root = jnp.where(u > 0.0, u * lax.rsqrt(u), 0.0)
        b = root * (i * xj)
        for t in range(slabs_per_blk):
            a_scr[jj * slabs_per_blk + t] = a[:, t * LANES:(t + 1) * LANES]
            b_scr[jj * slabs_per_blk + t] = b[:, t * LANES:(t + 1) * LANES]

    order = range(nv - 1, -1, -1) if reverse else range(nv)
    group = lambda v: slice(v * SUBLANES, (v + 1) * SUBLANES)

    h = [jnp.zeros((SUBLANES, LANES), F32)] * n_slab
    p = [jnp.ones((SUBLANES, LANES), F32)] * n_slab
    for v in order:
        for j in range(n_slab):
            av = a_scr[j, group(v), :]
            h[j] = av * h[j] + b_scr[j, group(v), :]
            p[j] = av * p[j]

    start = []
    for j in range(n_slab):
        pc, hc = _scan_block(p[j], h[j], reverse)
        prev = carry[j]
        if reverse:
            seed = jnp.broadcast_to(prev[bot:bot + 1, :], (SUBLANES, LANES))
            ends = hc + pc * seed
            start.append(jnp.where(sub == top, pltpu.roll(prev, SUBLANES - 1, 0),
                                   pltpu.roll(ends, SUBLANES - 1, 0)))
        else:
            seed = jnp.broadcast_to(prev[top:top + 1, :], (SUBLANES, LANES))
            ends = hc + pc * seed
            start.append(jnp.where(sub == bot, pltpu.roll(prev, 1, 0), pltpu.roll(ends, 1, 0)))
        carry[j] = ends

    dst = h_scr if final else None
    h = start
    for v in order:
        for j in range(n_slab):
            h[j] = a_scr[j, group(v), :] * h[j] + b_scr[j, group(v), :]
            if final:
                dst[j, group(v), :] = h[j]
            else:
                out_ref[0, j, group(v), :] = h[j]

    if final:
        ys = []
        for j in range(n_slab):
            ys.append((hf_ref[0, j] + h_scr[j]) * gate_ref[0, j])
        ss = ys[0] * ys[0]
        for j in range(1, n_slab):
            ss = ss + ys[j] * ys[j]
        inv = lax.rsqrt(jnp.sum(ss, axis=-1, keepdims=True) * (1.0 / D_RNN) + EPS)
        for j in range(n_slab):
            lanes = slice(j * LANES, (j + 1) * LANES)
            h_scr[j] = ys[j] * inv * gn_ref[:, lanes]
        for st in range(SUBLANES):
            for j in range(n_slab):
                out_ref[0, st * nv:(st + 1) * nv, j * LANES:(j + 1) * LANES] = (
                    h_scr[j, pl.ds(st, nv, stride=SUBLANES), :].astype(BF16))


def _rnn_pass(x4, extra, params, reverse, final):
    bsz, n_slab, s, _ = x4.shape
    nchunks = s // RNN_CHUNK
    hb = RNN_CHUNK // SUBLANES
    nhalo = s // SUBLANES

    def cidx(c):
        return (nchunks - 1 - c) if reverse else c

    cur_map = lambda b, c: (b, 0, cidx(c), 0)
    const2 = lambda b, c: (0, 0)
    const3 = lambda b, c: (0, 0, 0)
    cw, cb, wa, wx, ba, bx, lam, gn = params
    halo = (1, n_slab, SUBLANES, LANES)
    full = (1, n_slab, RNN_CHUNK, LANES)
    gate_specs = [
        pl.BlockSpec((D_RNN // GATE_BLK, GATE_BLK, GATE_BLK), const3),
        pl.BlockSpec((D_RNN // GATE_BLK, GATE_BLK, GATE_BLK), const3),
        pl.BlockSpec((1, D_RNN), const2),
        pl.BlockSpec((1, D_RNN), const2),
        pl.BlockSpec((1, D_RNN), const2),
    ]
    slab_scr = pltpu.VMEM((n_slab, RNN_CHUNK, LANES), F32)
    carry_scr = pltpu.VMEM((n_slab, SUBLANES, LANES), F32)
    if final:
        in_specs = [pl.BlockSpec(full, cur_map)] * 3 + gate_specs + [pl.BlockSpec((1, D_RNN), const2)]
        args = [x4, *extra, wa, wx, ba, bx, lam, gn]
        scratch = [slab_scr, slab_scr, slab_scr, carry_scr]
        out_specs = pl.BlockSpec((1, RNN_CHUNK, D_RNN), lambda b, c: (b, cidx(c), 0))
        out_shape = jax.ShapeDtypeStruct((bsz, s, D_RNN), BF16)
    else:
        pa_map = lambda b, c: (b, 0, jnp.maximum(cidx(c) * hb - 2, 0), 0)
        pb_map = lambda b, c: (b, 0, jnp.maximum(cidx(c) * hb - 1, 0), 0)
        nx_map = lambda b, c: (b, 0, jnp.minimum((cidx(c) + 1) * hb, nhalo - 1), 0)
        in_specs = [pl.BlockSpec(halo, pa_map), pl.BlockSpec(halo, pb_map), pl.BlockSpec(full, cur_map),
                    pl.BlockSpec(halo, nx_map),
                    pl.BlockSpec((CONV_WIDTH, D_RNN), const2), pl.BlockSpec((1, D_RNN), const2)] + gate_specs
        args = [x4, x4, x4, x4, cw, cb, wa, wx, ba, bx, lam]
        scratch = [slab_scr, slab_scr, carry_scr]
        out_specs = [pl.BlockSpec(full, cur_map)] * 2
        out_shape = [jax.ShapeDtypeStruct(x4.shape, F32)] * 2
    return pl.pallas_call(
        functools.partial(_rnn_kernel, reverse=reverse, final=final, nchunks=nchunks),
        grid=(bsz, nchunks),
        in_specs=in_specs,
        out_specs=out_specs,
        out_shape=out_shape,
        scratch_shapes=scratch,
        compiler_params=pltpu.CompilerParams(
            dimension_semantics=("arbitrary", "arbitrary"), vmem_limit_bytes=VMEM_LIMIT),
        name="rnn_bwd" if reverse else "rnn_fwd",
    )(*args)


def _gate_weights(w):
    per = GATE_BLK // RNN_BLOCK
    w5 = w.reshape(D_RNN // GATE_BLK, per, RNN_BLOCK, 1, RNN_BLOCK)
    on_diag = jnp.asarray(np.eye(per, dtype=bool)).reshape(1, per, 1, per, 1)
    dense = jnp.where(on_diag, w5, 0.0)
    return dense.reshape(D_RNN // GATE_BLK, GATE_BLK, GATE_BLK).astype(BF16)


def _attn_kernel(q_ref, k_ref, v_ref, bias_ref, o_ref, lse_ref, *, sub_len, tiles_per_step):
    i = pl.program_id(2)
    n_tiles = sub_len // Q_TILE
    lane = lax.broadcasted_iota(jnp.int32, (Q_TILE, LANES), 1)
    heads_per_slab = LANES // HEAD_DIM
    first_half = lane < HEAD_DIM
    for cl in range(q_ref.shape[1]):
        for t in range(tiles_per_step):
            rows = slice(t * Q_TILE, (t + 1) * Q_TILE)
            q = q_ref[0, cl, rows, :]
            tile = i * tiles_per_step + t
            variant = jnp.where(tile == 0, FIRST, jnp.where(tile == n_tiles - 1, LAST, INTERIOR))
            start = pl.multiple_of(jnp.clip(tile * Q_TILE - HALF_STEPS, 0, sub_len - K_WIN), K_TILE)
            kk = k_ref[0, cl, pl.ds(start, K_WIN), :]
            vv = v_ref[0, cl, pl.ds(start, K_WIN), :]
            for p in range(D_ATTN // LANES):
                sl = slice(p * LANES, (p + 1) * LANES)
                qp, kp, vp = q[:, sl], kk[:, sl], vv[:, sl]
                zero = jnp.zeros_like(qp)
                q2 = jnp.concatenate([jnp.where(first_half, qp, zero), jnp.where(first_half, zero, qp)], axis=0)
                s = lax.dot_general(q2, kp, (((1,), (1,)), ((), ())), preferred_element_type=F32)
                h0 = p * heads_per_slab
                s = s + bias_ref[variant, h0:h0 + heads_per_slab].reshape(heads_per_slab * Q_TILE, K_WIN)
                m = jnp.max(s, axis=-1, keepdims=True)
                e = jnp.exp2(s - m)
                l = jnp.sum(e, axis=-1, keepdims=True)
                o2 = jnp.dot(e.astype(BF16), vp, preferred_element_type=F32) * (1.0 / l)
                lse2 = jnp.broadcast_to(m + jnp.log(l) * LOG2E, (heads_per_slab * Q_TILE, LANES))
                o_ref[0, cl, rows, sl] = jnp.where(first_half, o2[:Q_TILE], o2[Q_TILE:])
                lse_ref[0, cl, rows, sl] = jnp.where(first_half, lse2[:Q_TILE], lse2[Q_TILE:])


def _attn_pattern(qc, kc, vc, bias, pat):
    bsz, dil, sub_len, _ = qc.shape
    tiles_per_step = min(MAX_TILES_PER_STEP, sub_len // Q_TILE)
    classes_per_step = MAX_TILES_PER_STEP // tiles_per_step
    q_rows = tiles_per_step * Q_TILE
    qmap = lambda b, r, i: (b, r, i, 0)
    seq_map = lambda b, r, i: (b, r, 0, 0)
    in_specs = [
        pl.BlockSpec((1, classes_per_step, q_rows, D_ATTN), qmap),
        pl.BlockSpec((1, classes_per_step, sub_len, D_ATTN), seq_map),
        pl.BlockSpec((1, classes_per_step, sub_len, D_ATTN), seq_map),
        pl.BlockSpec((None, 3, N_HEADS, Q_TILE, K_WIN), lambda b, r, i: (pat, 0, 0, 0, 0)),
    ]
    return pl.pallas_call(
        functools.partial(_attn_kernel, sub_len=sub_len, tiles_per_step=tiles_per_step),
        grid=(bsz, dil // classes_per_step, sub_len // q_rows),
        in_specs=in_specs,
        out_specs=[pl.BlockSpec((1, classes_per_step, q_rows, D_ATTN), qmap)] * 2,
        out_shape=[jax.ShapeDtypeStruct((bsz, dil, sub_len, D_ATTN), F32)] * 2,
        compiler_params=pltpu.CompilerParams(
            dimension_semantics=("arbitrary",) * 3, vmem_limit_bytes=ATTN_VMEM_LIMIT),
        name=f"attn_d{dil}",
    )(qc, kc, vc, bias)


FF_CHUNK = 1024


SUB_ROWS = 512


def _natural_order(ref, scr, u):
    dil = ref.shape[1]
    per = SUB_ROWS // dil
    if dil == 1:
        return ref[0, 0, u * per:(u + 1) * per, :]
    n_slab = D_ATTN // LANES
    for r in range(dil):
        for j in range(n_slab):
            scr[u, j, pl.ds(r, per, stride=dil), :] = ref[0, r, u * per:(u + 1) * per, j * LANES:(j + 1) * LANES]
    return jnp.concatenate([scr[u, j] for j in range(n_slab)], axis=-1)


def _out_mlp_kernel(x_ref, mr_ref, o1, o2, o3, l1, l2, l3, ga_ref, wo_ref, gm_ref, wu_ref, wd_ref,
                    gf_ref, out_ref, s_o2, s_o3, s_l2, s_l3, *, tm):
    n_sub = tm // SUB_ROWS

    def merge(u):
        a1 = _natural_order(l1, None, u)
        a2 = _natural_order(l2, s_l2, u)
        a3 = _natural_order(l3, s_l3, u)
        mx = jnp.maximum(jnp.maximum(a1, a2), a3)
        e1, e2, e3 = jnp.exp2(a1 - mx), jnp.exp2(a2 - mx), jnp.exp2(a3 - mx)
        y_attn = (e1 * _natural_order(o1, None, u) + e2 * _natural_order(o2, s_o2, u)
                  + e3 * _natural_order(o3, s_o3, u)) / (e1 + e2 + e3)
        return _rms(y_attn, ga_ref[...]).astype(BF16)

    def out_proj(u, mix_attn):
        rows = slice(u * SUB_ROWS, (u + 1) * SUB_ROWS)
        proj = jnp.dot(mr_ref[0, rows, :], wo_ref[0:D_RNN, :], preferred_element_type=F32)
        proj = proj + jnp.dot(mix_attn, wo_ref[D_RNN:D_RNN + D_ATTN, :], preferred_element_type=F32)
        x1 = proj + x_ref[0, rows, :]
        return x1, _rms(x1, gm_ref[...]).astype(BF16)

    def ff_chunk(h, c):
        z = jnp.dot(h, wu_ref[:, c * FF_CHUNK:(c + 1) * FF_CHUNK], preferred_element_type=F32)
        z = jnp.square(jnp.maximum(z, 0.0)).astype(BF16)
        return jnp.dot(z, wd_ref[c * FF_CHUNK:(c + 1) * FF_CHUNK, :], preferred_element_type=F32)

    def finish(u, ff, x1):
        rows = slice(u * SUB_ROWS, (u + 1) * SUB_ROWS)
        out_ref[0, rows, :] = _rms(ff + x1, gf_ref[...])

    n_chunk = D_FF // FF_CHUNK
    x1, h = out_proj(0, merge(0))
    pending = None
    for u in range(n_sub):
        ff = None
        nxt = None
        for c in range(n_chunk):
            d = ff_chunk(h, c)
            ff = d if ff is None else ff + d
            if c == 0 and pending is not None:
                finish(*pending)
            if c == 1 and u + 1 < n_sub:
                mix_next = merge(u + 1)
            if c == 2 and u + 1 < n_sub:
                nxt = out_proj(u + 1, mix_next)
        pending = (u, ff, x1)
        if nxt is not None:
            x1, h = nxt
    finish(*pending)


def _out_mlp(x3, mix_rnn, os_, lses, ga, wo, gm, wu, wd, gf, tm):
    bsz, s, _ = x3.shape
    row = lambda b, i: (b, i, 0)
    cls = lambda b, i: (b, 0, i, 0)
    const = lambda b, i: (0, 0)
    once = pl.Buffered(1)
    cls_specs = [pl.BlockSpec((1, dil, tm // dil, D_ATTN), cls) for _, dil in PATTERNS]
    in_specs = [pl.BlockSpec((1, tm, D_MODEL), row), pl.BlockSpec((1, tm, D_RNN), row)]
    in_specs += cls_specs + cls_specs
    in_specs += [
        pl.BlockSpec((1, D_ATTN), const),
        pl.BlockSpec((D_RNN + D_ATTN, D_MODEL), const, pipeline_mode=once),
        pl.BlockSpec((1, D_MODEL), const),
        pl.BlockSpec((D_MODEL, D_FF), const, pipeline_mode=once),
        pl.BlockSpec((D_FF, D_MODEL), const, pipeline_mode=once),
        pl.BlockSpec((1, D_MODEL), const),
    ]
    return pl.pallas_call(
        functools.partial(_out_mlp_kernel, tm=tm),
        grid=(bsz, s // tm),
        in_specs=in_specs,
        out_specs=pl.BlockSpec((1, tm, D_MODEL), row),
        out_shape=jax.ShapeDtypeStruct((bsz, s, D_MODEL), F32),
        scratch_shapes=[pltpu.VMEM((tm // SUB_ROWS, D_ATTN // LANES, SUB_ROWS, LANES), F32)] * 4,
        compiler_params=pltpu.CompilerParams(
            dimension_semantics=("arbitrary", "arbitrary"), vmem_limit_bytes=ATTN_VMEM_LIMIT),
        name="out_mlp",
    )(x3, mix_rnn, *os_, *lses, ga, wo, gm, wu, wd, gf)


def kernel(x, attn_norm_g, w_in, conv_w, conv_b, lru_wa_fwd, lru_ba_fwd, lru_wx_fwd, lru_bx_fwd, lru_lam_fwd, lru_wa_bwd, lru_ba_bwd, lru_wx_bwd, lru_bx_bwd, lru_lam_bwd, rel_bias, norm_rnn_g, norm_attn_g, w_out, mlp_norm_g, w_up, w_down, final_norm_g):
    depth = w_in.shape[0]
    assert depth == 1, "the final RMSNorm is fused into the single layer's last call"
    l = 0
    n_pat = len(PATTERNS)
    row = lambda v: v.reshape(1, -1)
    bias = _bias_tables(rel_bias)
    proj = _in_proj(x, row(attn_norm_g[l]), w_in[l].astype(BF16), tm=IN_TILE)
    xr, gate = proj[0], proj[1]
    qs, ks, vs = (proj[2 + a * n_pat:2 + (a + 1) * n_pat] for a in range(3))
    fwd = (conv_w[l], row(conv_b[l]), _gate_weights(lru_wa_fwd[l]), _gate_weights(lru_wx_fwd[l]),
           row(lru_ba_fwd[l]), row(lru_bx_fwd[l]), row(lru_lam_fwd[l]), None)
    bwd = (conv_w[l], row(conv_b[l]), _gate_weights(lru_wa_bwd[l]), _gate_weights(lru_wx_bwd[l]),
           row(lru_ba_bwd[l]), row(lru_bx_bwd[l]), row(lru_lam_bwd[l]), row(norm_rnn_g[l]))
    h_f, xc = _rnn_pass(xr, (), fwd, reverse=False, final=False)
    mix_rnn = _rnn_pass(xc, (h_f, gate), bwd, reverse=True, final=True)
    os_, lses = [], []
    for pat in range(n_pat):
        o, lse = _attn_pattern(qs[pat], ks[pat], vs[pat], bias, pat)
        os_.append(o)
        lses.append(lse)
    return _out_mlp(x, mix_rnn, os_, lses, row(norm_attn_g[l]), w_out[l].astype(BF16),
                    row(mlp_norm_g[l]), w_up[l].astype(BF16), w_down[l].astype(BF16),
                    row(final_norm_g), tm=512)
```

```python
import functools
import math

import numpy as np
import jax
import jax.numpy as jnp
from jax import lax
from jax.experimental import pallas as pl
from jax.experimental.pallas import tpu as pltpu

F32 = jnp.float32
BF16 = jnp.bfloat16

D_MODEL = 1024
D_RNN = 512
N_RNN_BLOCKS = 8
RNN_BLOCK = D_RNN // N_RNN_BLOCKS
CONV_WIDTH = 4
CONV_LEFT = 2
LRU_C = 8.0
N_HEADS = 8
HEAD_DIM = 64
D_ATTN = N_HEADS * HEAD_DIM
PATTERNS = ((128, 1), (512, 4), (2048, 16))
HALF_STEPS = 64
N_BUCKETS = 32
MAX_DISTANCE = 1024
D_IN = 2 * D_RNN + 3 * D_ATTN
D_FF = 4 * D_MODEL
EPS = 1e-6
NEG_INF = -1e30

LANES = 128
SUBLANES = 8
MXU_DIM = 256

Q_TILE = 128
K_TILE = 64
K_WIN = Q_TILE + 2 * HALF_STEPS
MAX_TILES_PER_STEP = 8
LOG2E = math.log2(math.e)
INTERIOR, FIRST, LAST = 0, 1, 2
GATE_BLK = MXU_DIM

VMEM_LIMIT = 48 * 1024 * 1024
BIG_VMEM_LIMIT = 56 * 1024 * 1024


def _rms(x, g):
    ms = jnp.mean(x * x, axis=-1, keepdims=True)
    return x * lax.rsqrt(ms + EPS) * g


def _t5_bucket_np(rel):
    nb = N_BUCKETS // 2
    max_exact = nb // 2
    ret = np.where(rel > 0, nb, 0)
    n = np.abs(rel)
    nf = np.maximum(n, 1).astype(np.float32)
    large = max_exact + (np.log(nf / np.float32(max_exact)) / np.float32(math.log(MAX_DISTANCE / max_exact))
                         * np.float32(nb - max_exact)).astype(np.int32)
    large = np.minimum(large, nb - 1)
    return ret + np.where(n < max_exact, n, large)


def _bucket_index_tables():
    qi = np.arange(Q_TILE)[:, None]
    ci = np.arange(K_WIN)[None, :]
    step = ci - qi - HALF_STEPS
    tabs = []
    for _, dil in PATTERNS:
        b = _t5_bucket_np((step * dil).astype(np.int32))
        tabs.append(np.where(np.abs(step) <= HALF_STEPS, b, -1))
    return np.stack(tabs, axis=0).astype(np.int32)


def _bias_table_kernel(idx_ref, rb_ref, out_ref):
    idx = idx_ref[0]
    col = lax.broadcasted_iota(jnp.int32, idx.shape, 1)
    hit = [idx == b for b in range(N_BUCKETS)]
    for h in range(N_HEADS):
        acc = jnp.full(idx.shape, NEG_INF, F32)
        for b in range(N_BUCKETS):
            acc = jnp.where(hit[b], rb_ref[b, h] * LOG2E, acc)
        out_ref[0, INTERIOR, h] = acc
        out_ref[0, FIRST, h] = jnp.where(col < K_WIN - HALF_STEPS, pltpu.roll(acc, K_WIN - HALF_STEPS, 1), NEG_INF)
        out_ref[0, LAST, h] = jnp.where(col >= HALF_STEPS, pltpu.roll(acc, HALF_STEPS, 1), NEG_INF)


def _bias_tables(rel_bias):
    idx = jnp.asarray(_bucket_index_tables())
    n_pat = idx.shape[0]
    return pl.pallas_call(
        _bias_table_kernel,
        grid=(n_pat,),
        in_specs=[
            pl.BlockSpec((1, Q_TILE, K_WIN), lambda g: (g, 0, 0)),
            pl.BlockSpec(memory_space=pltpu.SMEM),
        ],
        out_specs=pl.BlockSpec((1, 3, N_HEADS, Q_TILE, K_WIN), lambda g: (g, 0, 0, 0, 0)),
        out_shape=jax.ShapeDtypeStruct((n_pat, 3, N_HEADS, Q_TILE, K_WIN), F32),
        name="bias_table",
    )(idx, rel_bias)


def _stripe_store(ref, p, part):
    stripes = IN_TILE // STRIPE_ROWS
    for st in range(stripes):
        for j in range(D_RNN // LANES):
            ref[0, j, pl.ds(part * stripes + st, STRIPE_ROWS, stride=SUBLANES), :] = (
                p[st * STRIPE_ROWS:(st + 1) * STRIPE_ROWS, j * LANES:(j + 1) * LANES])


PERM_ROWS = MXU_DIM


def _class_permutations():
    mats = []
    for _, dil in PATTERNS:
        if dil == 1:
            continue
        per = PERM_ROWS // dil
        p = np.zeros((PERM_ROWS, PERM_ROWS), np.float32)
        for r in range(dil):
            for m in range(per):
                p[r * per + m, m * dil + r] = 1.0
        mats.append(p)
    return np.stack(mats)


def _in_proj_kernel(x_ref, g_ref, w_ref, perm_ref, *rest, tm, n_later):
    n_pat = len(PATTERNS)
    later_in, rest = rest[:n_later], rest[n_later:]
    xr_ref, gate_ref, rest = rest[0], rest[1], rest[2:]
    outs = [rest[a * n_pat:(a + 1) * n_pat] for a in range(3)]
    later_out = rest[3 * n_pat:3 * n_pat + n_later]
    h_scr = rest[3 * n_pat + n_later]
    for src, dst in zip(later_in, later_out):
        dst[...] = src[...].astype(BF16)
    h_scr[...] = _rms(x_ref[0], g_ref[...]).astype(BF16)

    def seg(lo, width):
        return jnp.dot(h_scr[...], w_ref[:, lo:lo + width], preferred_element_type=F32)

    for a, scale in enumerate((HEAD_DIM ** -0.5 * LOG2E, 1.0, 1.0)):
        p = seg(2 * D_RNN + a * D_ATTN, D_ATTN)
        if scale != 1.0:
            p = p * scale
        pb = p.astype(BF16)
        dilated = 0
        for (_, dil), o_ref in zip(PATTERNS, outs[a]):
            if dil == 1:
                o_ref[0, 0] = pb
                continue
            per = PERM_ROWS // dil
            for blk in range(tm // PERM_ROWS):
                y = jnp.dot(perm_ref[dilated], pb[blk * PERM_ROWS:(blk + 1) * PERM_ROWS, :],
                            preferred_element_type=F32).astype(BF16)
                for r in range(dil):
                    o_ref[0, r, blk * per:(blk + 1) * per, :] = y[r * per:(r + 1) * per, :]
            dilated += 1
    g = seg(D_RNN, D_RNN)
    gelu = g * (0.5 * (1.0 + jnp.tanh(math.sqrt(2.0 / math.pi) * (g + 0.044715 * (g * g * g)))))
    part = pl.program_id(1) % (RNN_CHUNK // IN_TILE)
    _stripe_store(gate_ref, gelu, part)
    _stripe_store(xr_ref, seg(0, D_RNN), part)


def _in_proj(x3, g, w_in_bf, later_weights, tm):
    assert tm == IN_TILE
    bsz, s, _ = x3.shape
    per_seq = s // tm
    n_steps = bsz * per_seq
    row_slice = lambda b, i: (b * per_seq + i, 0)
    later_specs = [pl.BlockSpec((w.shape[0] // n_steps, w.shape[1]), row_slice) for w in later_weights]
    later_shapes = [jax.ShapeDtypeStruct(w.shape, BF16) for w in later_weights]
    row = lambda b, i: (b, i, 0)
    cls = lambda b, i: (b, 0, i, 0)
    chunk = lambda b, i: (b, 0, i // (RNN_CHUNK // IN_TILE), 0)
    const = lambda b, i: (0, 0)
    perms = jnp.asarray(_class_permutations(), BF16)
    qkv_specs, qkv_shapes = [], []
    for _ in range(3):
        for _, dil in PATTERNS:
            qkv_specs.append(pl.BlockSpec((1, dil, tm // dil, D_ATTN), cls))
            qkv_shapes.append(jax.ShapeDtypeStruct((bsz, dil, s // dil, D_ATTN), BF16))
    return pl.pallas_call(
        functools.partial(_in_proj_kernel, tm=tm, n_later=len(later_weights)),
        grid=(bsz, per_seq),
        in_specs=[
            pl.BlockSpec((1, tm, D_MODEL), row),
            pl.BlockSpec((1, D_MODEL), const),
            pl.BlockSpec((D_MODEL, D_IN), const, pipeline_mode=pl.Buffered(1)),
            pl.BlockSpec(perms.shape, lambda b, i: (0, 0, 0)),
        ] + later_specs,
        out_specs=[pl.BlockSpec((1, D_RNN // LANES, RNN_CHUNK, LANES), chunk)] * 2 + qkv_specs + later_specs,
        out_shape=[jax.ShapeDtypeStruct((bsz, D_RNN // LANES, s, LANES), F32)] * 2 + qkv_shapes + later_shapes,
        scratch_shapes=[pltpu.VMEM((tm, D_MODEL), BF16)],
        compiler_params=pltpu.CompilerParams(
            dimension_semantics=("arbitrary", "arbitrary"), vmem_limit_bytes=BIG_VMEM_LIMIT),
        name="in_proj",
    )(x3, g, w_in_bf, perms, *later_weights)


IN_TILE = 1024
RNN_CHUNK = 1024
STRIPE_ROWS = RNN_CHUNK // SUBLANES


def _scan_block(a, b, reverse):
    n = a.shape[0]
    row = lax.broadcasted_iota(jnp.int32, a.shape, 0)
    s = 1
    while s < n:
        if reverse:
            ra = pltpu.roll(a, n - s, 0)
            rb = pltpu.roll(b, n - s, 0)
            m = row < n - s
        else:
            ra = pltpu.roll(a, s, 0)
            rb = pltpu.roll(b, s, 0)
            m = row >= s
        b = jnp.where(m, a * rb + b, b)
        a = jnp.where(m, a * ra, a)
        s *= 2
    return a, b


def _rnn_kernel(*refs, reverse, final, nchunks):
    if final:
        (xc_ref, hf_ref, gate_ref, wa_ref, wx_ref, ba_ref, bx_ref, lam_ref, gn_ref,
         out_ref, a_scr, b_scr, h_scr, carry) = refs
    else:
        (pa_ref, pb_ref, cur_ref, nx_ref, cw_ref, cb_ref, wa_ref, wx_ref, ba_ref, bx_ref, lam_ref,
         out_ref, xc_out_ref, a_scr, b_scr, carry) = refs
        h_scr = None

    step = pl.program_id(1)
    chunk = (nchunks - 1 - step) if reverse else step
    n_slab = D_RNN // LANES
    nv = STRIPE_ROWS
    sub = lax.broadcasted_iota(jnp.int32, (SUBLANES, LANES), 0)
    top, bot = SUBLANES - 1, 0

    @pl.when(step == 0)
    def _():
        carry[...] = jnp.zeros_like(carry)

    if final:
        xc = [xc_ref[0, j] for j in range(n_slab)]
    else:
        xc = []
        for j in range(n_slab):
            lanes = slice(j * LANES, (j + 1) * LANES)
            x = cur_ref[0, j].reshape(nv, SUBLANES, LANES)
            pa = jnp.where(chunk > 0, pa_ref[0, j], 0.0)
            pb = jnp.where(chunk > 0, pb_ref[0, j], 0.0)
            nx = jnp.where(chunk < nchunks - 1, nx_ref[0, j], 0.0)
            xm2 = pltpu.roll(jnp.where(sub == top, pa, x[nv - 2]), 1, 0)
            xm1 = pltpu.roll(jnp.where(sub == top, pb, x[nv - 1]), 1, 0)
            xp1 = pltpu.roll(jnp.where(sub == bot, nx, x[0]), SUBLANES - 1, 0)
            xext = jnp.concatenate([xm2[None], xm1[None], x, xp1[None]], axis=0)
            acc = cb_ref[:, lanes] + xext[0:nv] * cw_ref[0:1, lanes]
            for k in range(1, CONV_WIDTH):
                acc = acc + xext[k:k + nv] * cw_ref[k:k + 1, lanes]
            xc.append(acc.reshape(RNN_CHUNK, LANES))
            xc_out_ref[0, j] = xc[j]

    nlam = -lam_ref[...]
    softplus = jnp.maximum(nlam, 0.0) + jnp.log1p(jnp.exp(-jnp.abs(nlam)))
    coef = -LRU_C * softplus
    slabs_per_blk = GATE_BLK // LANES
    for jj in range(D_RNN // GATE_BLK):
        blk = slice(jj * GATE_BLK, (jj + 1) * GATE_BLK)
        xj = jnp.concatenate(xc[jj * slabs_per_blk:(jj + 1) * slabs_per_blk], axis=-1)
        xjb = xj.astype(BF16)
        za = jnp.dot(xjb, wa_ref[jj], preferred_element_type=F32) + ba_ref[:, blk]
        zx = jnp.dot(xjb, wx_ref[jj], preferred_element_type=F32) + bx_ref[:, blk]
        r = 1.0 / (1.0 + jnp.exp(-za))
        i = 1.0 / (1.0 + jnp.exp(-zx))
        log_a = coef[:, blk] * r
        a = jnp.exp(log_a)
        u = -jnp.tanh(log_a) * (1.0 + a * a)
        root = jnp.where(u > 0.0, u * lax.rsqrt(u), 0.0)
        b = root * (i * xj)
        for t in range(slabs_per_blk):
            a_scr[jj * slabs_per_blk + t] = a[:, t * LANES:(t + 1) * LANES]
            b_scr[jj * slabs_per_blk + t] = b[:, t * LANES:(t + 1) * LANES]

    order = range(nv - 1, -1, -1) if reverse else range(nv)
    group = lambda v: slice(v * SUBLANES, (v + 1) * SUBLANES)

    h = [jnp.zeros((SUBLANES, LANES), F32)] * n_slab
    p = [jnp.ones((SUBLANES, LANES), F32)] * n_slab
    for v in order:
        for j in range(n_slab):
            av = a_scr[j, group(v), :]
            h[j] = av * h[j] + b_scr[j, group(v), :]
            p[j] = av * p[j]

    start = []
    for j in range(n_slab):
        pc, hc = _scan_block(p[j], h[j], reverse)
        prev = carry[j]
        if reverse:
            seed = jnp.broadcast_to(prev[bot:bot + 1, :], (SUBLANES, LANES))
            ends = hc + pc * seed
            start.append(jnp.where(sub == top, pltpu.roll(prev, SUBLANES - 1, 0),
                                   pltpu.roll(ends, SUBLANES - 1, 0)))
        else:
            seed = jnp.broadcast_to(prev[top:top + 1, :], (SUBLANES, LANES))
            ends = hc + pc * seed
            start.append(jnp.where(sub == bot, pltpu.roll(prev, 1, 0), pltpu.roll(ends, 1, 0)))
        carry[j] = ends

    dst = h_scr if final else None
    h = start
    for v in order:
        for j in range(n_slab):
            h[j] = a_scr[j, group(v), :] * h[j] + b_scr[j, group(v), :]
            if final:
                dst[j, group(v), :] = h[j]
            else:
                out_ref[0, j, group(v), :] = h[j]

    if final:
        ys = []
        for j in range(n_slab):
            ys.append((hf_ref[0, j] + h_scr[j]) * gate_ref[0, j])
        ss = ys[0] * ys[0]
        for j in range(1, n_slab):
            ss = ss + ys[j] * ys[j]
        inv = lax.rsqrt(jnp.sum(ss, axis=-1, keepdims=True) * (1.0 / D_RNN) + EPS)
        for j in range(n_slab):
            lanes = slice(j * LANES, (j + 1) * LANES)
            h_scr[j] = ys[j] * inv * gn_ref[:, lanes]
        for st in range(SUBLANES):
            for j in range(n_slab):
                out_ref[0, st * nv:(st + 1) * nv, j * LANES:(j + 1) * LANES] = (
                    h_scr[j, pl.ds(st, nv, stride=SUBLANES), :].astype(BF16))


def _rnn_pass(x4, extra, params, reverse, final):
    bsz, n_slab, s, _ = x4.shape
    nchunks = s // RNN_CHUNK
    hb = RNN_CHUNK // SUBLANES
    nhalo = s // SUBLANES

    def cidx(c):
        return (nchunks - 1 - c) if reverse else c

    cur_map = lambda b, c: (b, 0, cidx(c), 0)
    const2 = lambda b, c: (0, 0)
    const3 = lambda b, c: (0, 0, 0)
    cw, cb, wa, wx, ba, bx, lam, gn = params
    halo = (1, n_slab, SUBLANES, LANES)
    full = (1, n_slab, RNN_CHUNK, LANES)
    gate_specs = [
        pl.BlockSpec((D_RNN // GATE_BLK, GATE_BLK, GATE_BLK), const3),
        pl.BlockSpec((D_RNN // GATE_BLK, GATE_BLK, GATE_BLK), const3),
        pl.BlockSpec((1, D_RNN), const2),
        pl.BlockSpec((1, D_RNN), const2),
        pl.BlockSpec((1, D_RNN), const2),
    ]
    slab_scr = pltpu.VMEM((n_slab, RNN_CHUNK, LANES), F32)
    carry_scr = pltpu.VMEM((n_slab, SUBLANES, LANES), F32)
    if final:
        in_specs = [pl.BlockSpec(full, cur_map)] * 3 + gate_specs + [pl.BlockSpec((1, D_RNN), const2)]
        args = [x4, *extra, wa, wx, ba, bx, lam, gn]
        scratch = [slab_scr, slab_scr, slab_scr, carry_scr]
        out_specs = pl.BlockSpec((1, RNN_CHUNK, D_RNN), lambda b, c: (b, cidx(c), 0))
        out_shape = jax.ShapeDtypeStruct((bsz, s, D_RNN), BF16)
    else:
        pa_map = lambda b, c: (b, 0, jnp.maximum(cidx(c) * hb - 2, 0), 0)
        pb_map = lambda b, c: (b, 0, jnp.maximum(cidx(c) * hb - 1, 0), 0)
        nx_map = lambda b, c: (b, 0, jnp.minimum((cidx(c) + 1) * hb, nhalo - 1), 0)
        in_specs = [pl.BlockSpec(halo, pa_map), pl.BlockSpec(halo, pb_map), pl.BlockSpec(full, cur_map),
                    pl.BlockSpec(halo, nx_map),
                    pl.BlockSpec((CONV_WIDTH, D_RNN), const2), pl.BlockSpec((1, D_RNN), const2)] + gate_specs
        args = [x4, x4, x4, x4, cw, cb, wa, wx, ba, bx, lam]
        scratch = [slab_scr, slab_scr, carry_scr]
        out_specs = [pl.BlockSpec(full, cur_map)] * 2
        out_shape = [jax.ShapeDtypeStruct(x4.shape, F32)] * 2
    return pl.pallas_call(
        functools.partial(_rnn_kernel, reverse=reverse, final=final, nchunks=nchunks),
        grid=(bsz, nchunks),
        in_specs=in_specs,
        out_specs=out_specs,
        out_shape=out_shape,
        scratch_shapes=scratch,
        compiler_params=pltpu.CompilerParams(
            dimension_semantics=("arbitrary", "arbitrary"), vmem_limit_bytes=VMEM_LIMIT),
        name="rnn_bwd" if reverse else "rnn_fwd",
    )(*args)


def _gate_weights(w):
    per = GATE_BLK // RNN_BLOCK
    w5 = w.reshape(D_RNN // GATE_BLK, per, RNN_BLOCK, 1, RNN_BLOCK)
    on_diag = jnp.asarray(np.eye(per, dtype=bool)).reshape(1, per, 1, per, 1)
    dense = jnp.where(on_diag, w5, 0.0)
    return dense.reshape(D_RNN // GATE_BLK, GATE_BLK, GATE_BLK).astype(BF16)


def _attn_kernel(q_ref, k_ref, v_ref, bias_ref, o_ref, lse_ref, *, sub_len, tiles_per_step):
    i = pl.program_id(2)
    n_tiles = sub_len // Q_TILE
    lane = lax.broadcasted_iota(jnp.int32, (Q_TILE, LANES), 1)
    heads_per_slab = LANES // HEAD_DIM
    first_half = lane < HEAD_DIM
    for cl in range(q_ref.shape[1]):
        for t in range(tiles_per_step):
            rows = slice(t * Q_TILE, (t + 1) * Q_TILE)
            q = q_ref[0, cl, rows, :]
            tile = i * tiles_per_step + t
            variant = jnp.where(tile == 0, FIRST, jnp.where(tile == n_tiles - 1, LAST, INTERIOR))
            start = pl.multiple_of(jnp.clip(tile * Q_TILE - HALF_STEPS, 0, sub_len - K_WIN), K_TILE)
            kk = k_ref[0, cl, pl.ds(start, K_WIN), :]
            vv = v_ref[0, cl, pl.ds(start, K_WIN), :]
            for p in range(D_ATTN // LANES):
                sl = slice(p * LANES, (p + 1) * LANES)
                qp, kp, vp = q[:, sl], kk[:, sl], vv[:, sl]
                zero = jnp.zeros_like(qp)
                q2 = jnp.concatenate([jnp.where(first_half, qp, zero), jnp.where(first_half, zero, qp)], axis=0)
                s = lax.dot_general(q2, kp, (((1,), (1,)), ((), ())), preferred_element_type=F32)
                h0 = p * heads_per_slab
                s = s + bias_ref[variant, h0:h0 + heads_per_slab].reshape(heads_per_slab * Q_TILE, K_WIN)
                m = jnp.max(s, axis=-1, keepdims=True)
                e = jnp.exp2(s - m)
                l = jnp.sum(e, axis=-1, keepdims=True)
                o2 = jnp.dot(e.astype(BF16), vp, preferred_element_type=F32) * (1.0 / l)
                lse2 = jnp.broadcast_to(m + jnp.log(l) * LOG2E, (heads_per_slab * Q_TILE, LANES))
                o_ref[0, cl, rows, sl] = jnp.where(first_half, o2[:Q_TILE], o2[Q_TILE:])
                lse_ref[0, cl, rows, sl] = jnp.where(first_half, lse2[:Q_TILE], lse2[Q_TILE:])


def _attn_pattern(qc, kc, vc, bias, pat):
    bsz, dil, sub_len, _ = qc.shape
    tiles_per_step = min(MAX_TILES_PER_STEP, sub_len // Q_TILE)
    classes_per_step = MAX_TILES_PER_STEP // tiles_per_step
    q_rows = tiles_per_step * Q_TILE
    qmap = lambda b, r, i: (b, r, i, 0)
    seq_map = lambda b, r, i: (b, r, 0, 0)
    in_specs = [
        pl.BlockSpec((1, classes_per_step, q_rows, D_ATTN), qmap),
        pl.BlockSpec((1, classes_per_step, sub_len, D_ATTN), seq_map),
        pl.BlockSpec((1, classes_per_step, sub_len, D_ATTN), seq_map),
        pl.BlockSpec((None, 3, N_HEADS, Q_TILE, K_WIN), lambda b, r, i: (pat, 0, 0, 0, 0)),
    ]
    return pl.pallas_call(
        functools.partial(_attn_kernel, sub_len=sub_len, tiles_per_step=tiles_per_step),
        grid=(bsz, dil // classes_per_step, sub_len // q_rows),
        in_specs=in_specs,
        out_specs=[pl.BlockSpec((1, classes_per_step, q_rows, D_ATTN), qmap)] * 2,
        out_shape=[jax.ShapeDtypeStruct((bsz, dil, sub_len, D_ATTN), F32)] * 2,
        compiler_params=pltpu.CompilerParams(
            dimension_semantics=("arbitrary",) * 3, vmem_limit_bytes=BIG_VMEM_LIMIT),
        name=f"attn_d{dil}",
    )(qc, kc, vc, bias)


FF_CHUNK = 1024


SUB_ROWS = 512


def _natural_order(ref, scr, u):
    dil = ref.shape[1]
    per = SUB_ROWS // dil
    if dil == 1:
        return ref[0, 0, u * per:(u + 1) * per, :]
    n_slab = D_ATTN // LANES
    for r in range(dil):
        for j in range(n_slab):
            scr[u, j, pl.ds(r, per, stride=dil), :] = ref[0, r, u * per:(u + 1) * per, j * LANES:(j + 1) * LANES]
    return jnp.concatenate([scr[u, j] for j in range(n_slab)], axis=-1)


def _out_mlp_kernel(x_ref, mr_ref, o1, o2, o3, l1, l2, l3, ga_ref, wo_ref, gm_ref, wu_ref, wd_ref,
                    gf_ref, out_ref, s_o2, s_o3, s_l2, s_l3, *, tm):
    n_sub = tm // SUB_ROWS

    def merge(u):
        a1 = _natural_order(l1, None, u)
        a2 = _natural_order(l2, s_l2, u)
        a3 = _natural_order(l3, s_l3, u)
        mx = jnp.maximum(jnp.maximum(a1, a2), a3)
        e1, e2, e3 = jnp.exp2(a1 - mx), jnp.exp2(a2 - mx), jnp.exp2(a3 - mx)
        y_attn = (e1 * _natural_order(o1, None, u) + e2 * _natural_order(o2, s_o2, u)
                  + e3 * _natural_order(o3, s_o3, u)) / (e1 + e2 + e3)
        return _rms(y_attn, ga_ref[...]).astype(BF16)

    def out_proj(u, mix_attn):
        rows = slice(u * SUB_ROWS, (u + 1) * SUB_ROWS)
        proj = jnp.dot(mr_ref[0, rows, :], wo_ref[0:D_RNN, :], preferred_element_type=F32)
        proj = proj + jnp.dot(mix_attn, wo_ref[D_RNN:D_RNN + D_ATTN, :], preferred_element_type=F32)
        x1 = proj + x_ref[0, rows, :]
        return x1, _rms(x1, gm_ref[...]).astype(BF16)

    def ff_chunk(h, c):
        z = jnp.dot(h, wu_ref[:, c * FF_CHUNK:(c + 1) * FF_CHUNK], preferred_element_type=F32)
        z = jnp.square(jnp.maximum(z, 0.0)).astype(BF16)
        return jnp.dot(z, wd_ref[c * FF_CHUNK:(c + 1) * FF_CHUNK, :], preferred_element_type=F32)

    def finish(u, ff, x1):
        rows = slice(u * SUB_ROWS, (u + 1) * SUB_ROWS)
        out_ref[0, rows, :] = _rms(ff + x1, gf_ref[...])

    n_chunk = D_FF // FF_CHUNK
    x1, h = out_proj(0, merge(0))
    pending = None
    for u in range(n_sub):
        ff = None
        nxt = None
        for c in range(n_chunk):
            d = ff_chunk(h, c)
            ff = d if ff is None else ff + d
            if c == 0 and pending is not None:
                finish(*pending)
            if c == 1 and u + 1 < n_sub:
                mix_next = merge(u + 1)
            if c == 2 and u + 1 < n_sub:
                nxt = out_proj(u + 1, mix_next)
        pending = (u, ff, x1)
        if nxt is not None:
            x1, h = nxt
    finish(*pending)


def _out_mlp(x3, mix_rnn, os_, lses, ga, wo, gm, wu, wd, gf, tm):
    bsz, s, _ = x3.shape
    row = lambda b, i: (b, i, 0)
    cls = lambda b, i: (b, 0, i, 0)
    const = lambda b, i: (0, 0)
    once = pl.Buffered(1)
    cls_specs = [pl.BlockSpec((1, dil, tm // dil, D_ATTN), cls) for _, dil in PATTERNS]
    in_specs = [pl.BlockSpec((1, tm, D_MODEL), row), pl.BlockSpec((1, tm, D_RNN), row)]
    in_specs += cls_specs + cls_specs
    in_specs += [
        pl.BlockSpec((1, D_ATTN), const),
        pl.BlockSpec((D_RNN + D_ATTN, D_MODEL), const, pipeline_mode=once),
        pl.BlockSpec((1, D_MODEL), const),
        pl.BlockSpec((D_MODEL, D_FF), const, pipeline_mode=once),
        pl.BlockSpec((D_FF, D_MODEL), const, pipeline_mode=once),
        pl.BlockSpec((1, D_MODEL), const),
    ]
    return pl.pallas_call(
        functools.partial(_out_mlp_kernel, tm=tm),
        grid=(bsz, s // tm),
        in_specs=in_specs,
        out_specs=pl.BlockSpec((1, tm, D_MODEL), row),
        out_shape=jax.ShapeDtypeStruct((bsz, s, D_MODEL), F32),
        scratch_shapes=[pltpu.VMEM((tm // SUB_ROWS, D_ATTN // LANES, SUB_ROWS, LANES), F32)] * 4,
        compiler_params=pltpu.CompilerParams(
            dimension_semantics=("arbitrary", "arbitrary"), vmem_limit_bytes=BIG_VMEM_LIMIT),
        name="out_mlp",
    )(x3, mix_rnn, *os_, *lses, ga, wo, gm, wu, wd, gf)


def kernel(x, attn_norm_g, w_in, conv_w, conv_b, lru_wa_fwd, lru_ba_fwd, lru_wx_fwd, lru_bx_fwd, lru_lam_fwd, lru_wa_bwd, lru_ba_bwd, lru_wx_bwd, lru_bx_bwd, lru_lam_bwd, rel_bias, norm_rnn_g, norm_attn_g, w_out, mlp_norm_g, w_up, w_down, final_norm_g):
    depth = w_in.shape[0]
    assert depth == 1, "the final RMSNorm is fused into the single layer's last call"
    l = 0
    n_pat = len(PATTERNS)
    row = lambda v: v.reshape(1, -1)
    bias = _bias_tables(rel_bias)
    proj = _in_proj(x, row(attn_norm_g[l]), w_in[l].astype(BF16), (w_out[l], w_up[l], w_down[l]), tm=IN_TILE)
    xr, gate = proj[0], proj[1]
    qs, ks, vs = (proj[2 + a * n_pat:2 + (a + 1) * n_pat] for a in range(3))
    w_out_bf, w_up_bf, w_down_bf = proj[2 + 3 * n_pat:]
    fwd = (conv_w[l], row(conv_b[l]), _gate_weights(lru_wa_fwd[l]), _gate_weights(lru_wx_fwd[l]),
           row(lru_ba_fwd[l]), row(lru_bx_fwd[l]), row(lru_lam_fwd[l]), None)
    bwd = (conv_w[l], row(conv_b[l]), _gate_weights(lru_wa_bwd[l]), _gate_weights(lru_wx_bwd[l]),
           row(lru_ba_bwd[l]), row(lru_bx_bwd[l]), row(lru_lam_bwd[l]), row(norm_rnn_g[l]))
    h_f, xc = _rnn_pass(xr, (), fwd, reverse=False, final=False)
    mix_rnn = _rnn_pass(xc, (h_f, gate), bwd, reverse=True, final=True)
    os_, lses = [], []
    for pat in range(n_pat):
        o, lse = _attn_pattern(qs[pat], ks[pat], vs[pat], bias, pat)
        os_.append(o)
        lses.append(lse)
    return _out_mlp(x, mix_rnn, os_, lses, row(norm_attn_g[l]), w_out_bf,
                    row(mlp_norm_g[l]), w_up_bf, w_down_bf, row(final_norm_g), tm=512)
```

```python
import functools
import math

import numpy as np
import jax
import jax.numpy as jnp
from jax import lax
from jax.experimental import pallas as pl
from jax.experimental.pallas import tpu as pltpu

F32 = jnp.float32
BF16 = jnp.bfloat16

D_MODEL = 1024
D_RNN = 512
N_RNN_BLOCKS = 8
RNN_BLOCK = D_RNN // N_RNN_BLOCKS
CONV_WIDTH = 4
CONV_LEFT = 2
LRU_C = 8.0
N_HEADS = 8
HEAD_DIM = 64
D_ATTN = N_HEADS * HEAD_DIM
PATTERNS = ((128, 1), (512, 4), (2048, 16))
HALF_STEPS = 64
N_BUCKETS = 32
MAX_DISTANCE = 1024
D_IN = 2 * D_RNN + 3 * D_ATTN
D_FF = 4 * D_MODEL
EPS = 1e-6
NEG_INF = -1e30

LANES = 128
SUBLANES = 8
MXU_DIM = 256

Q_TILE = 128
K_TILE = 64
K_WIN = Q_TILE + 2 * HALF_STEPS
TILES_PER_STEP = 8
CLASS_GROUP = 4
LOG2E = math.log2(math.e)
INTERIOR, FIRST, LAST = 0, 1, 2
GATE_BLK = MXU_DIM

VMEM_LIMIT = 48 * 1024 * 1024
BIG_VMEM_LIMIT = 56 * 1024 * 1024


def _rms(x, g):
    ms = jnp.mean(x * x, axis=-1, keepdims=True)
    return x * lax.rsqrt(ms + EPS) * g


def _t5_bucket_np(rel):
    nb = N_BUCKETS // 2
    max_exact = nb // 2
    ret = np.where(rel > 0, nb, 0)
    n = np.abs(rel)
    nf = np.maximum(n, 1).astype(np.float32)
    large = max_exact + (np.log(nf / np.float32(max_exact)) / np.float32(math.log(MAX_DISTANCE / max_exact))
                         * np.float32(nb - max_exact)).astype(np.int32)
    large = np.minimum(large, nb - 1)
    return ret + np.where(n < max_exact, n, large)


def _bucket_index_rows():
    step = np.arange(K_WIN) - HALF_STEPS
    rows = []
    for _, dil in PATTERNS:
        b = _t5_bucket_np((step * dil).astype(np.int32))
        rows.append(np.where(np.abs(step) <= HALF_STEPS, b, -1))
    return np.stack(rows, axis=0).astype(np.int32)[:, None, :]


def _bias_table_kernel(idx_ref, rb_ref, out_ref):
    idx = idx_ref[0]
    col = lax.broadcasted_iota(jnp.int32, (Q_TILE, K_WIN), 1)
    hit = [idx == b for b in range(N_BUCKETS)]
    for h in range(N_HEADS):
        first_row = jnp.full(idx.shape, NEG_INF, F32)
        for b in range(N_BUCKETS):
            first_row = jnp.where(hit[b], rb_ref[b, h] * LOG2E, first_row)
        acc = pltpu.roll(jnp.broadcast_to(first_row, (Q_TILE, K_WIN)), 0, 1, stride=1, stride_axis=0)
        out_ref[0, INTERIOR, h] = acc
        out_ref[0, FIRST, h] = jnp.where(col < K_WIN - HALF_STEPS, pltpu.roll(acc, K_WIN - HALF_STEPS, 1), NEG_INF)
        out_ref[0, LAST, h] = jnp.where(col >= HALF_STEPS, pltpu.roll(acc, HALF_STEPS, 1), NEG_INF)


def _bias_tables(rel_bias):
    idx = jnp.asarray(_bucket_index_rows())
    n_pat = idx.shape[0]
    return pl.pallas_call(
        _bias_table_kernel,
        grid=(n_pat,),
        in_specs=[
            pl.BlockSpec((1, 1, K_WIN), lambda g: (g, 0, 0)),
            pl.BlockSpec(memory_space=pltpu.SMEM),
        ],
        out_specs=pl.BlockSpec((1, 3, N_HEADS, Q_TILE, K_WIN), lambda g: (g, 0, 0, 0, 0)),
        out_shape=jax.ShapeDtypeStruct((n_pat, 3, N_HEADS, Q_TILE, K_WIN), F32),
        name="bias_table",
    )(idx, rel_bias)


def _stripe_store(ref, p, part):
    stripes = IN_TILE // STRIPE_ROWS
    for st in range(stripes):
        for j in range(D_RNN // LANES):
            ref[0, j, pl.ds(part * stripes + st, STRIPE_ROWS, stride=SUBLANES), :] = (
                p[st * STRIPE_ROWS:(st + 1) * STRIPE_ROWS, j * LANES:(j + 1) * LANES])


PERM_ROWS = MXU_DIM


def _class_permutations():
    mats = []
    for _, dil in PATTERNS:
        if dil == 1:
            continue
        per = PERM_ROWS // dil
        p = np.zeros((PERM_ROWS, PERM_ROWS), np.float32)
        for r in range(dil):
            for m in range(per):
                p[r * per + m, m * dil + r] = 1.0
        mats.append(p)
    return np.stack(mats)


def _in_proj_kernel(x_ref, g_ref, w_ref, perm_ref, *rest, tm, n_later):
    n_pat = len(PATTERNS)
    later_in, rest = rest[:n_later], rest[n_later:]
    xr_ref, gate_ref, rest = rest[0], rest[1], rest[2:]
    outs = [rest[a * n_pat:(a + 1) * n_pat] for a in range(3)]
    later_out = rest[3 * n_pat:3 * n_pat + n_later]
    h_scr = rest[3 * n_pat + n_later]
    for src, dst in zip(later_in, later_out):
        dst[...] = src[...].astype(BF16)
    h_scr[...] = _rms(x_ref[0], g_ref[...]).astype(BF16)

    def seg(lo, width):
        return jnp.dot(h_scr[...], w_ref[:, lo:lo + width], preferred_element_type=F32)

    for a, scale in enumerate((HEAD_DIM ** -0.5 * LOG2E, 1.0, 1.0)):
        p = seg(2 * D_RNN + a * D_ATTN, D_ATTN)
        if scale != 1.0:
            p = p * scale
        pb = p.astype(BF16)
        dilated = 0
        for (_, dil), o_ref in zip(PATTERNS, outs[a]):
            if dil == 1:
                o_ref[0, 0, 0] = pb
                continue
            per = PERM_ROWS // dil
            for blk in range(tm // PERM_ROWS):
                y = jnp.dot(perm_ref[dilated], pb[blk * PERM_ROWS:(blk + 1) * PERM_ROWS, :],
                            preferred_element_type=F32).astype(BF16)
                groups = o_ref.shape[1]
                for r in range(dil):
                    o_ref[0, r % groups, r // groups, blk * per:(blk + 1) * per, :] = y[r * per:(r + 1) * per, :]
            dilated += 1
    g = seg(D_RNN, D_RNN)
    gelu = g * (0.5 * (1.0 + jnp.tanh(math.sqrt(2.0 / math.pi) * (g + 0.044715 * (g * g * g)))))
    part = pl.program_id(1) % (RNN_CHUNK // IN_TILE)
    _stripe_store(gate_ref, gelu, part)
    _stripe_store(xr_ref, seg(0, D_RNN), part)


def _in_proj(x3, g, w_in_bf, later_weights, tm):
    assert tm == IN_TILE
    bsz, s, _ = x3.shape
    per_seq = s // tm
    n_steps = bsz * per_seq
    row_slice = lambda b, i: (b * per_seq + i, 0)
    later_specs = [pl.BlockSpec((w.shape[0] // n_steps, w.shape[1]), row_slice) for w in later_weights]
    later_shapes = [jax.ShapeDtypeStruct(w.shape, BF16) for w in later_weights]
    row = lambda b, i: (b, i, 0)
    cls = lambda b, i: (b, 0, 0, i, 0)
    chunk = lambda b, i: (b, 0, i // (RNN_CHUNK // IN_TILE), 0)
    const = lambda b, i: (0, 0)
    perms = jnp.asarray(_class_permutations(), BF16)
    qkv_specs, qkv_shapes = [], []
    for _ in range(3):
        for _, dil in PATTERNS:
            groups, per_group = dil // min(dil, CLASS_GROUP), min(dil, CLASS_GROUP)
            qkv_specs.append(pl.BlockSpec((1, groups, per_group, tm // dil, D_ATTN), cls))
            qkv_shapes.append(jax.ShapeDtypeStruct((bsz, groups, per_group, s // dil, D_ATTN), BF16))
    return pl.pallas_call(
        functools.partial(_in_proj_kernel, tm=tm, n_later=len(later_weights)),
        grid=(bsz, per_seq),
        in_specs=[
            pl.BlockSpec((1, tm, D_MODEL), row),
            pl.BlockSpec((1, D_MODEL), const),
            pl.BlockSpec((D_MODEL, D_IN), const, pipeline_mode=pl.Buffered(1)),
            pl.BlockSpec(perms.shape, lambda b, i: (0, 0, 0)),
        ] + later_specs,
        out_specs=[pl.BlockSpec((1, D_RNN // LANES, RNN_CHUNK, LANES), chunk)] * 2 + qkv_specs + later_specs,
        out_shape=[jax.ShapeDtypeStruct((bsz, D_RNN // LANES, s, LANES), F32)] * 2 + qkv_shapes + later_shapes,
        scratch_shapes=[pltpu.VMEM((tm, D_MODEL), BF16)],
        compiler_params=pltpu.CompilerParams(
            dimension_semantics=("arbitrary", "arbitrary"), vmem_limit_bytes=BIG_VMEM_LIMIT),
        name="in_proj",
    )(x3, g, w_in_bf, perms, *later_weights)


IN_TILE = 1024
RNN_CHUNK = 1024
STRIPE_ROWS = RNN_CHUNK // SUBLANES


def _scan_block(a, b, reverse):
    n = a.shape[0]
    row = lax.broadcasted_iota(jnp.int32, a.shape, 0)
    s = 1
    while s < n:
        if reverse:
            ra = pltpu.roll(a, n - s, 0)
            rb = pltpu.roll(b, n - s, 0)
            m = row < n - s
        else:
            ra = pltpu.roll(a, s, 0)
            rb = pltpu.roll(b, s, 0)
            m = row >= s
        b = jnp.where(m, a * rb + b, b)
        a = jnp.where(m, a * ra, a)
        s *= 2
    return a, b


def _rnn_kernel(*refs, reverse, final, nchunks):
    if final:
        (xc_ref, hf_ref, gate_ref, wa_ref, wx_ref, ba_ref, bx_ref, lam_ref, gn_ref,
         out_ref, a_scr, b_scr, h_scr, carry) = refs
    else:
        (pa_ref, pb_ref, cur_ref, nx_ref, cw_ref, cb_ref, wa_ref, wx_ref, ba_ref, bx_ref, lam_ref,
         out_ref, xc_out_ref, a_scr, b_scr, carry) = refs
        h_scr = None

    step = pl.program_id(1)
    chunk = (nchunks - 1 - step) if reverse else step
    n_slab = D_RNN // LANES
    nv = STRIPE_ROWS
    sub = lax.broadcasted_iota(jnp.int32, (SUBLANES, LANES), 0)
    top, bot = SUBLANES - 1, 0

    @pl.when(step == 0)
    def _():
        carry[...] = jnp.zeros_like(carry)

    if final:
        xc = [xc_ref[0, j] for j in range(n_slab)]
    else:
        xc = []
        for j in range(n_slab):
            lanes = slice(j * LANES, (j + 1) * LANES)
            x = cur_ref[0, j].reshape(nv, SUBLANES, LANES)
            pa = jnp.where(chunk > 0, pa_ref[0, j], 0.0)
            pb = jnp.where(chunk > 0, pb_ref[0, j], 0.0)
            nx = jnp.where(chunk < nchunks - 1, nx_ref[0, j], 0.0)
            xm2 = pltpu.roll(jnp.where(sub == top, pa, x[nv - 2]), 1, 0)
            xm1 = pltpu.roll(jnp.where(sub == top, pb, x[nv - 1]), 1, 0)
            xp1 = pltpu.roll(jnp.where(sub == bot, nx, x[0]), SUBLANES - 1, 0)
            xext = jnp.concatenate([xm2[None], xm1[None], x, xp1[None]], axis=0)
            acc = cb_ref[:, lanes] + xext[0:nv] * cw_ref[0:1, lanes]
            for k in range(1, CONV_WIDTH):
                acc = acc + xext[k:k + nv] * cw_ref[k:k + 1, lanes]
            xc.append(acc.reshape(RNN_CHUNK, LANES))
            xc_out_ref[0, j] = xc[j]

    nlam = -lam_ref[...]
    softplus = jnp.maximum(nlam, 0.0) + jnp.log1p(jnp.exp(-jnp.abs(nlam)))
    coef = -LRU_C * softplus
    slabs_per_blk = GATE_BLK // LANES
    for jj in range(D_RNN // GATE_BLK):
        blk = slice(jj * GATE_BLK, (jj + 1) * GATE_BLK)
        xj = jnp.concatenate(xc[jj * slabs_per_blk:(jj + 1) * slabs_per_blk], axis=-1)
        xjb = xj.astype(BF16)
        za = jnp.dot(xjb, wa_ref[jj], preferred_element_type=F32) + ba_ref[:, blk]
        zx = jnp.dot(xjb, wx_ref[jj], preferred_element_type=F32) + bx_ref[:, blk]
        r = 1.0 / (1.0 + jnp.exp(-za))
        i = 1.0 / (1.0 + jnp.exp(-zx))
        log_a = coef[:, blk] * r
        a = jnp.exp(log_a)
        u = -jnp.tanh(log_a) * (1.0 + a * a)
        root = jnp.where(u > 0.0, u * lax.rsqrt(u), 0.0)
        b = root * (i * xj)
        for t in range(slabs_per_blk):
            a_scr[jj * slabs_per_blk + t] = a[:, t * LANES:(t + 1) * LANES]
            b_scr[jj * slabs_per_blk + t] = b[:, t * LANES:(t + 1) * LANES]

    order = range(nv - 1, -1, -1) if reverse else range(nv)
    group = lambda v: slice(v * SUBLANES, (v + 1) * SUBLANES)

    h = [jnp.zeros((SUBLANES, LANES), F32)] * n_slab
    p = [jnp.ones((SUBLANES, LANES), F32)] * n_slab
    for v in order:
        for j in range(n_slab):
            av = a_scr[j, group(v), :]
            h[j] = av * h[j] + b_scr[j, group(v), :]
            p[j] = av * p[j]

    start = []
    for j in range(n_slab):
        pc, hc = _scan_block(p[j], h[j], reverse)
        prev = carry[j]
        if reverse:
            seed = jnp.broadcast_to(prev[bot:bot + 1, :], (SUBLANES, LANES))
            ends = hc + pc * seed
            start.append(jnp.where(sub == top, pltpu.roll(prev, SUBLANES - 1, 0),
                                   pltpu.roll(ends, SUBLANES - 1, 0)))
        else:
            seed = jnp.broadcast_to(prev[top:top + 1, :], (SUBLANES, LANES))
            ends = hc + pc * seed
            start.append(jnp.where(sub == bot, pltpu.roll(prev, 1, 0), pltpu.roll(ends, 1, 0)))
        carry[j] = ends

    dst = h_scr if final else None
    h = start
    for v in order:
        for j in range(n_slab):
            h[j] = a_scr[j, group(v), :] * h[j] + b_scr[j, group(v), :]
            if final:
                dst[j, group(v), :] = h[j]
            else:
                out_ref[0, j, group(v), :] = h[j]

    if final:
        ys = []
        for j in range(n_slab):
            ys.append((hf_ref[0, j] + h_scr[j]) * gate_ref[0, j])
        ss = ys[0] * ys[0]
        for j in range(1, n_slab):
            ss = ss + ys[j] * ys[j]
        inv = lax.rsqrt(jnp.sum(ss, axis=-1, keepdims=True) * (1.0 / D_RNN) + EPS)
        for j in range(n_slab):
            lanes = slice(j * LANES, (j + 1) * LANES)
            h_scr[j] = ys[j] * inv * gn_ref[:, lanes]
        for st in range(SUBLANES):
            for j in range(n_slab):
                out_ref[0, st * nv:(st + 1) * nv, j * LANES:(j + 1) * LANES] = (
                    h_scr[j, pl.ds(st, nv, stride=SUBLANES), :].astype(BF16))


def _rnn_pass(x4, extra, params, reverse, final):
    bsz, n_slab, s, _ = x4.shape
    nchunks = s // RNN_CHUNK
    hb = RNN_CHUNK // SUBLANES
    nhalo = s // SUBLANES

    def cidx(c):
        return (nchunks - 1 - c) if reverse else c

    cur_map = lambda b, c: (b, 0, cidx(c), 0)
    const2 = lambda b, c: (0, 0)
    const3 = lambda b, c: (0, 0, 0)
    cw, cb, wa, wx, ba, bx, lam, gn = params
    halo = (1, n_slab, SUBLANES, LANES)
    full = (1, n_slab, RNN_CHUNK, LANES)
    gate_specs = [
        pl.BlockSpec((D_RNN // GATE_BLK, GATE_BLK, GATE_BLK), const3),
        pl.BlockSpec((D_RNN // GATE_BLK, GATE_BLK, GATE_BLK), const3),
        pl.BlockSpec((1, D_RNN), const2),
        pl.BlockSpec((1, D_RNN), const2),
        pl.BlockSpec((1, D_RNN), const2),
    ]
    slab_scr = pltpu.VMEM((n_slab, RNN_CHUNK, LANES), F32)
    carry_scr = pltpu.VMEM((n_slab, SUBLANES, LANES), F32)
    if final:
        in_specs = [pl.BlockSpec(full, cur_map)] * 3 + gate_specs + [pl.BlockSpec((1, D_RNN), const2)]
        args = [x4, *extra, wa, wx, ba, bx, lam, gn]
        scratch = [slab_scr, slab_scr, slab_scr, carry_scr]
        out_specs = pl.BlockSpec((1, RNN_CHUNK, D_RNN), lambda b, c: (b, cidx(c), 0))
        out_shape = jax.ShapeDtypeStruct((bsz, s, D_RNN), BF16)
    else:
        pa_map = lambda b, c: (b, 0, jnp.maximum(cidx(c) * hb - 2, 0), 0)
        pb_map = lambda b, c: (b, 0, jnp.maximum(cidx(c) * hb - 1, 0), 0)
        nx_map = lambda b, c: (b, 0, jnp.minimum((cidx(c) + 1) * hb, nhalo - 1), 0)
        in_specs = [pl.BlockSpec(halo, pa_map), pl.BlockSpec(halo, pb_map), pl.BlockSpec(full, cur_map),
                    pl.BlockSpec(halo, nx_map),
                    pl.BlockSpec((CONV_WIDTH, D_RNN), const2), pl.BlockSpec((1, D_RNN), const2)] + gate_specs
        args = [x4, x4, x4, x4, cw, cb, wa, wx, ba, bx, lam]
        scratch = [slab_scr, slab_scr, carry_scr]
        out_specs = [pl.BlockSpec(full, cur_map)] * 2
        out_shape = [jax.ShapeDtypeStruct(x4.shape, F32)] * 2
    return pl.pallas_call(
        functools.partial(_rnn_kernel, reverse=reverse, final=final, nchunks=nchunks),
        grid=(bsz, nchunks),
        in_specs=in_specs,
        out_specs=out_specs,
        out_shape=out_shape,
        scratch_shapes=scratch,
        compiler_params=pltpu.CompilerParams(
            dimension_semantics=("arbitrary", "arbitrary"), vmem_limit_bytes=VMEM_LIMIT),
        name="rnn_bwd" if reverse else "rnn_fwd",
    )(*args)


def _gate_weights(w):
    per = GATE_BLK // RNN_BLOCK
    w5 = w.reshape(D_RNN // GATE_BLK, per, RNN_BLOCK, 1, RNN_BLOCK)
    on_diag = jnp.asarray(np.eye(per, dtype=bool)).reshape(1, per, 1, per, 1)
    dense = jnp.where(on_diag, w5, 0.0)
    return dense.reshape(D_RNN // GATE_BLK, GATE_BLK, GATE_BLK).astype(BF16)


def _attn_kernel(q_ref, k_ref, v_ref, bias_ref, o_ref, lse_ref, *, sub_len, tiles_per_class):
    i = pl.program_id(2)
    n_tiles = sub_len // Q_TILE
    n_classes = q_ref.shape[1]
    lane = lax.broadcasted_iota(jnp.int32, (Q_TILE, LANES), 1)
    heads_per_slab = LANES // HEAD_DIM
    first_half = lane < HEAD_DIM
    for cl in range(n_classes):
        for t in range(tiles_per_class):
            rows = slice(t * Q_TILE, (t + 1) * Q_TILE)
            if n_classes == 1:
                out_rows = rows
            else:
                out_rows = pl.ds(t * Q_TILE * n_classes + cl, Q_TILE, stride=n_classes)
            q = q_ref[0, cl, rows, :]
            tile = i * tiles_per_class + t
            variant = jnp.where(tile == 0, FIRST, jnp.where(tile == n_tiles - 1, LAST, INTERIOR))
            start = pl.multiple_of(jnp.clip(tile * Q_TILE - HALF_STEPS, 0, sub_len - K_WIN), K_TILE)
            kk = k_ref[0, cl, pl.ds(start, K_WIN), :]
            vv = v_ref[0, cl, pl.ds(start, K_WIN), :]
            for p in range(D_ATTN // LANES):
                sl = slice(p * LANES, (p + 1) * LANES)
                qp, kp, vp = q[:, sl], kk[:, sl], vv[:, sl]
                zero = jnp.zeros_like(qp)
                q2 = jnp.concatenate([jnp.where(first_half, qp, zero), jnp.where(first_half, zero, qp)], axis=0)
                s = lax.dot_general(q2, kp, (((1,), (1,)), ((), ())), preferred_element_type=F32)
                h0 = p * heads_per_slab
                s = s + bias_ref[variant, h0:h0 + heads_per_slab].reshape(heads_per_slab * Q_TILE, K_WIN)
                m = jnp.max(s, axis=-1, keepdims=True)
                e = jnp.exp2(s - m)
                l = jnp.sum(e, axis=-1, keepdims=True)
                o2 = jnp.dot(e.astype(BF16), vp, preferred_element_type=F32) * (1.0 / l)
                lse2 = jnp.broadcast_to(m + jnp.log(l) * LOG2E, (heads_per_slab * Q_TILE, LANES))
                o_ref[0, p, out_rows, :] = jnp.where(first_half, o2[:Q_TILE], o2[Q_TILE:])
                lse_ref[0, p, out_rows, :] = jnp.where(first_half, lse2[:Q_TILE], lse2[Q_TILE:])


def _attn_pattern(qc, kc, vc, bias, pat):
    bsz, groups, n_classes, sub_len, _ = qc.shape
    tiles_per_class = TILES_PER_STEP // n_classes
    q_rows = tiles_per_class * Q_TILE
    n_slab = D_ATTN // LANES
    qmap = lambda b, g, i: (b, g, 0, i, 0)
    seq_map = lambda b, g, i: (b, g, 0, 0, 0)
    in_specs = [
        pl.BlockSpec((1, None, n_classes, q_rows, D_ATTN), qmap),
        pl.BlockSpec((1, None, n_classes, sub_len, D_ATTN), seq_map),
        pl.BlockSpec((1, None, n_classes, sub_len, D_ATTN), seq_map),
        pl.BlockSpec((None, 3, N_HEADS, Q_TILE, K_WIN), lambda b, g, i: (pat, 0, 0, 0, 0)),
    ]
    out_spec = pl.BlockSpec((1, None, n_slab, n_classes * q_rows, LANES), qmap)
    return pl.pallas_call(
        functools.partial(_attn_kernel, sub_len=sub_len, tiles_per_class=tiles_per_class),
        grid=(bsz, groups, sub_len // q_rows),
        in_specs=in_specs,
        out_specs=[out_spec] * 2,
        out_shape=[jax.ShapeDtypeStruct((bsz, groups, n_slab, n_classes * sub_len, LANES), F32)] * 2,
        compiler_params=pltpu.CompilerParams(
            dimension_semantics=("arbitrary",) * 3, vmem_limit_bytes=BIG_VMEM_LIMIT),
        name=f"attn_d{groups * n_classes}",
    )(qc, kc, vc, bias)


FF_CHUNK = 1024


SUB_ROWS = 512


def _token_order(ref, scr, u, slab):
    groups = ref.shape[1]
    per = SUB_ROWS // groups
    if groups == 1:
        return ref[0, 0, slab, u * per:(u + 1) * per, :]
    for g in range(groups):
        scr[u, slab, pl.ds(g, per, stride=groups), :] = ref[0, g, slab, u * per:(u + 1) * per, :]
    return scr[u, slab]


def _out_mlp_kernel(x_ref, mr_ref, o1, o2, o3, l1, l2, l3, ga_ref, wo_ref, gm_ref, wu_ref, wd_ref,
                    gf_ref, out_ref, s_o3, s_l3, *, tm):
    n_sub = tm // SUB_ROWS

    def merge(u):
        ys = []
        for slab in range(D_ATTN // LANES):
            a1 = _token_order(l1, None, u, slab)
            a2 = _token_order(l2, None, u, slab)
            a3 = _token_order(l3, s_l3, u, slab)
            mx = jnp.maximum(jnp.maximum(a1, a2), a3)
            e1, e2, e3 = jnp.exp2(a1 - mx), jnp.exp2(a2 - mx), jnp.exp2(a3 - mx)
            ys.append((e1 * _token_order(o1, None, u, slab) + e2 * _token_order(o2, None, u, slab)
                       + e3 * _token_order(o3, s_o3, u, slab)) / (e1 + e2 + e3))
        return _rms(jnp.concatenate(ys, axis=-1), ga_ref[...]).astype(BF16)

    def out_proj(u, mix_attn):
        rows = slice(u * SUB_ROWS, (u + 1) * SUB_ROWS)
        proj = jnp.dot(mr_ref[0, rows, :], wo_ref[0:D_RNN, :], preferred_element_type=F32)
        proj = proj + jnp.dot(mix_attn, wo_ref[D_RNN:D_RNN + D_ATTN, :], preferred_element_type=F32)
        x1 = proj + x_ref[0, rows, :]
        return x1, _rms(x1, gm_ref[...]).astype(BF16)

    def ff_chunk(h, c):
        z = jnp.dot(h, wu_ref[:, c * FF_CHUNK:(c + 1) * FF_CHUNK], preferred_element_type=F32)
        z = jnp.square(jnp.maximum(z, 0.0)).astype(BF16)
        return jnp.dot(z, wd_ref[c * FF_CHUNK:(c + 1) * FF_CHUNK, :], preferred_element_type=F32)

    def finish(u, ff, x1):
        rows = slice(u * SUB_ROWS, (u + 1) * SUB_ROWS)
        out_ref[0, rows, :] = _rms(ff + x1, gf_ref[...])

    n_chunk = D_FF // FF_CHUNK
    x1, h = out_proj(0, merge(0))
    pending = None
    for u in range(n_sub):
        ff = None
        nxt = None
        for c in range(n_chunk):
            d = ff_chunk(h, c)
            ff = d if ff is None else ff + d
            if c == 0 and pending is not None:
                finish(*pending)
            if c == 1 and u + 1 < n_sub:
                mix_next = merge(u + 1)
            if c == 2 and u + 1 < n_sub:
                nxt = out_proj(u + 1, mix_next)
        pending = (u, ff, x1)
        if nxt is not None:
            x1, h = nxt
    finish(*pending)


def _out_mlp(x3, mix_rnn, os_, lses, ga, wo, gm, wu, wd, gf, tm):
    bsz, s, _ = x3.shape
    row = lambda b, i: (b, i, 0)
    const = lambda b, i: (0, 0)
    once = pl.Buffered(1)
    n_slab = D_ATTN // LANES
    attn_specs = [pl.BlockSpec((1, o.shape[1], n_slab, tm // o.shape[1], LANES), lambda b, i: (b, 0, 0, i, 0))
                  for o in os_]
    in_specs = [pl.BlockSpec((1, tm, D_MODEL), row), pl.BlockSpec((1, tm, D_RNN), row)]
    in_specs += attn_specs + attn_specs
    in_specs += [
        pl.BlockSpec((1, D_ATTN), const),
        pl.BlockSpec((D_RNN + D_ATTN, D_MODEL), const, pipeline_mode=once),
        pl.BlockSpec((1, D_MODEL), const),
        pl.BlockSpec((D_MODEL, D_FF), const, pipeline_mode=once),
        pl.BlockSpec((D_FF, D_MODEL), const, pipeline_mode=once),
        pl.BlockSpec((1, D_MODEL), const),
    ]
    return pl.pallas_call(
        functools.partial(_out_mlp_kernel, tm=tm),
        grid=(bsz, s // tm),
        in_specs=in_specs,
        out_specs=pl.BlockSpec((1, tm, D_MODEL), row),
        out_shape=jax.ShapeDtypeStruct((bsz, s, D_MODEL), F32),
        scratch_shapes=[pltpu.VMEM((tm // SUB_ROWS, n_slab, SUB_ROWS, LANES), F32)] * 2,
        compiler_params=pltpu.CompilerParams(
            dimension_semantics=("arbitrary", "arbitrary"), vmem_limit_bytes=BIG_VMEM_LIMIT),
        name="out_mlp",
    )(x3, mix_rnn, *os_, *lses, ga, wo, gm, wu, wd, gf)


def kernel(x, attn_norm_g, w_in, conv_w, conv_b, lru_wa_fwd, lru_ba_fwd, lru_wx_fwd, lru_bx_fwd, lru_lam_fwd, lru_wa_bwd, lru_ba_bwd, lru_wx_bwd, lru_bx_bwd, lru_lam_bwd, rel_bias, norm_rnn_g, norm_attn_g, w_out, mlp_norm_g, w_up, w_down, final_norm_g):
    depth = w_in.shape[0]
    assert depth == 1, "the final RMSNorm is fused into the single layer's last call"
    l = 0
    n_pat = len(PATTERNS)
    row = lambda v: v.reshape(1, -1)
    bias = _bias_tables(rel_bias)
    proj = _in_proj(x, row(attn_norm_g[l]), w_in[l].astype(BF16), (w_out[l], w_up[l], w_down[l]), tm=IN_TILE)
    xr, gate = proj[0], proj[1]
    qs, ks, vs = (proj[2 + a * n_pat:2 + (a + 1) * n_pat] for a in range(3))
    w_out_bf, w_up_bf, w_down_bf = proj[2 + 3 * n_pat:]
    fwd = (conv_w[l], row(conv_b[l]), _gate_weights(lru_wa_fwd[l]), _gate_weights(lru_wx_fwd[l]),
           row(lru_ba_fwd[l]), row(lru_bx_fwd[l]), row(lru_lam_fwd[l]), None)
    bwd = (conv_w[l], row(conv_b[l]), _gate_weights(lru_wa_bwd[l]), _gate_weights(lru_wx_bwd[l]),
           row(lru_ba_bwd[l]), row(lru_bx_bwd[l]), row(lru_lam_bwd[l]), row(norm_rnn_g[l]))
    h_f, xc = _rnn_pass(xr, (), fwd, reverse=False, final=False)
    mix_rnn = _rnn_pass(xc, (h_f, gate), bwd, reverse=True, final=True)
    os_, lses = [], []
    for pat in range(n_pat):
        o, lse = _attn_pattern(qs[pat], ks[pat], vs[pat], bias, pat)
        os_.append(o)
        lses.append(lse)
    return _out_mlp(x, mix_rnn, os_, lses, row(norm_attn_g[l]), w_out_bf,
                    row(mlp_norm_g[l]), w_up_bf, w_down_bf, row(final_norm_g), tm=512)
```

```python
import functools
import math

import numpy as np
import jax
import jax.numpy as jnp
from jax import lax
from jax.experimental import pallas as pl
from jax.experimental.pallas import tpu as pltpu

F32 = jnp.float32
BF16 = jnp.bfloat16

D_MODEL = 1024
D_RNN = 512
N_RNN_BLOCKS = 8
RNN_BLOCK = D_RNN // N_RNN_BLOCKS
CONV_WIDTH = 4
CONV_LEFT = 2
LRU_C = 8.0
N_HEADS = 8
HEAD_DIM = 64
D_ATTN = N_HEADS * HEAD_DIM
PATTERNS = ((128, 1), (512, 4), (2048, 16))
HALF_STEPS = 64
N_BUCKETS = 32
MAX_DISTANCE = 1024
D_IN = 2 * D_RNN + 3 * D_ATTN
D_FF = 4 * D_MODEL
EPS = 1e-6
NEG_INF = -1e30

LANES = 128
SUBLANES = 8
MXU_DIM = 256

Q_TILE = 128
K_TILE = 64
K_WIN = Q_TILE + 2 * HALF_STEPS
TILES_PER_STEP = 8
CLASS_GROUP = 4
LOG2E = math.log2(math.e)
INTERIOR, FIRST, LAST = 0, 1, 2
GATE_BLK = MXU_DIM

VMEM_LIMIT = 48 * 1024 * 1024
BIG_VMEM_LIMIT = 56 * 1024 * 1024


def _rms(x, g):
    ms = jnp.mean(x * x, axis=-1, keepdims=True)
    return x * lax.rsqrt(ms + EPS) * g


def _t5_bucket_np(rel):
    nb = N_BUCKETS // 2
    max_exact = nb // 2
    ret = np.where(rel > 0, nb, 0)
    n = np.abs(rel)
    nf = np.maximum(n, 1).astype(np.float32)
    large = max_exact + (np.log(nf / np.float32(max_exact)) / np.float32(math.log(MAX_DISTANCE / max_exact))
                         * np.float32(nb - max_exact)).astype(np.int32)
    large = np.minimum(large, nb - 1)
    return ret + np.where(n < max_exact, n, large)


def _bucket_index_rows():
    step = np.arange(K_WIN) - HALF_STEPS
    rows = []
    for _, dil in PATTERNS:
        b = _t5_bucket_np((step * dil).astype(np.int32))
        rows.append(np.where(np.abs(step) <= HALF_STEPS, b, -1))
    return np.stack(rows, axis=0).astype(np.int32)[:, None, :]


def _bias_table_kernel(idx_ref, rb_ref, out_ref):
    idx = idx_ref[0]
    col = lax.broadcasted_iota(jnp.int32, (Q_TILE, K_WIN), 1)
    hit = [idx == b for b in range(N_BUCKETS)]
    for h in range(N_HEADS):
        first_row = jnp.full(idx.shape, NEG_INF, F32)
        for b in range(N_BUCKETS):
            first_row = jnp.where(hit[b], rb_ref[b, h] * LOG2E, first_row)
        acc = pltpu.roll(jnp.broadcast_to(first_row, (Q_TILE, K_WIN)), 0, 1, stride=1, stride_axis=0)
        out_ref[0, INTERIOR, h] = acc
        out_ref[0, FIRST, h] = jnp.where(col < K_WIN - HALF_STEPS, pltpu.roll(acc, K_WIN - HALF_STEPS, 1), NEG_INF)
        out_ref[0, LAST, h] = jnp.where(col >= HALF_STEPS, pltpu.roll(acc, HALF_STEPS, 1), NEG_INF)


def _bias_tables(rel_bias):
    idx = jnp.asarray(_bucket_index_rows())
    n_pat = idx.shape[0]
    return pl.pallas_call(
        _bias_table_kernel,
        grid=(n_pat,),
        in_specs=[
            pl.BlockSpec((1, 1, K_WIN), lambda g: (g, 0, 0)),
            pl.BlockSpec(memory_space=pltpu.SMEM),
        ],
        out_specs=pl.BlockSpec((1, 3, N_HEADS, Q_TILE, K_WIN), lambda g: (g, 0, 0, 0, 0)),
        out_shape=jax.ShapeDtypeStruct((n_pat, 3, N_HEADS, Q_TILE, K_WIN), F32),
        name="bias_table",
    )(idx, rel_bias)


def _stripe_store(ref, p, part):
    stripes = IN_TILE // STRIPE_ROWS
    for st in range(stripes):
        for j in range(D_RNN // LANES):
            ref[0, j, pl.ds(part * stripes + st, STRIPE_ROWS, stride=SUBLANES), :] = (
                p[st * STRIPE_ROWS:(st + 1) * STRIPE_ROWS, j * LANES:(j + 1) * LANES])


PERM_ROWS = MXU_DIM


def _class_permutations():
    mats = []
    for _, dil in PATTERNS:
        if dil == 1:
            continue
        per = PERM_ROWS // dil
        p = np.zeros((PERM_ROWS, PERM_ROWS), np.float32)
        for r in range(dil):
            for m in range(per):
                p[r * per + m, m * dil + r] = 1.0
        mats.append(p)
    return np.stack(mats)


def _in_proj_kernel(x_ref, g_ref, w_ref, perm_ref, *rest, tm, n_later):
    n_pat = len(PATTERNS)
    later_in, rest = rest[:n_later], rest[n_later:]
    xr_ref, gate_ref, rest = rest[0], rest[1], rest[2:]
    outs = [rest[a * n_pat:(a + 1) * n_pat] for a in range(3)]
    later_out = rest[3 * n_pat:3 * n_pat + n_later]
    h_scr = rest[3 * n_pat + n_later]
    for src, dst in zip(later_in, later_out):
        dst[...] = src[...].astype(BF16)
    h_scr[...] = _rms(x_ref[0], g_ref[...]).astype(BF16)

    def seg(lo, width):
        return jnp.dot(h_scr[...], w_ref[:, lo:lo + width], preferred_element_type=F32)

    for a, scale in enumerate((HEAD_DIM ** -0.5 * LOG2E, 1.0, 1.0)):
        p = seg(2 * D_RNN + a * D_ATTN, D_ATTN)
        if scale != 1.0:
            p = p * scale
        pb = p.astype(BF16)
        dilated = 0
        for (_, dil), o_ref in zip(PATTERNS, outs[a]):
            if dil == 1:
                o_ref[0, 0, 0] = pb
                continue
            per = PERM_ROWS // dil
            for blk in range(tm // PERM_ROWS):
                y = jnp.dot(perm_ref[dilated], pb[blk * PERM_ROWS:(blk + 1) * PERM_ROWS, :],
                            preferred_element_type=F32).astype(BF16)
                groups = o_ref.shape[1]
                for r in range(dil):
                    o_ref[0, r % groups, r // groups, blk * per:(blk + 1) * per, :] = y[r * per:(r + 1) * per, :]
            dilated += 1
    g = seg(D_RNN, D_RNN)
    gelu = g * (0.5 * (1.0 + jnp.tanh(math.sqrt(2.0 / math.pi) * (g + 0.044715 * (g * g * g)))))
    part = pl.program_id(1) % (RNN_CHUNK // IN_TILE)
    _stripe_store(gate_ref, gelu, part)
    _stripe_store(xr_ref, seg(0, D_RNN), part)


def _in_proj(x3, g, w_in_bf, later_weights, tm):
    assert tm == IN_TILE
    bsz, s, _ = x3.shape
    per_seq = s // tm
    n_steps = bsz * per_seq
    row_slice = lambda b, i: (b * per_seq + i, 0)
    later_specs = [pl.BlockSpec((w.shape[0] // n_steps, w.shape[1]), row_slice) for w in later_weights]
    later_shapes = [jax.ShapeDtypeStruct(w.shape, BF16) for w in later_weights]
    row = lambda b, i: (b, i, 0)
    cls = lambda b, i: (b, 0, 0, i, 0)
    chunk = lambda b, i: (b, 0, i // (RNN_CHUNK // IN_TILE), 0)
    const = lambda b, i: (0, 0)
    perms = jnp.asarray(_class_permutations(), BF16)
    qkv_specs, qkv_shapes = [], []
    for _ in range(3):
        for _, dil in PATTERNS:
            groups, per_group = dil // min(dil, CLASS_GROUP), min(dil, CLASS_GROUP)
            qkv_specs.append(pl.BlockSpec((1, groups, per_group, tm // dil, D_ATTN), cls))
            qkv_shapes.append(jax.ShapeDtypeStruct((bsz, groups, per_group, s // dil, D_ATTN), BF16))
    return pl.pallas_call(
        functools.partial(_in_proj_kernel, tm=tm, n_later=len(later_weights)),
        grid=(bsz, per_seq),
        in_specs=[
            pl.BlockSpec((1, tm, D_MODEL), row),
            pl.BlockSpec((1, D_MODEL), const),
            pl.BlockSpec((D_MODEL, D_IN), const, pipeline_mode=pl.Buffered(1)),
            pl.BlockSpec(perms.shape, lambda b, i: (0, 0, 0)),
        ] + later_specs,
        out_specs=[pl.BlockSpec((1, D_RNN // LANES, RNN_CHUNK, LANES), chunk)] * 2 + qkv_specs + later_specs,
        out_shape=[jax.ShapeDtypeStruct((bsz, D_RNN // LANES, s, LANES), F32)] * 2 + qkv_shapes + later_shapes,
        scratch_shapes=[pltpu.VMEM((tm, D_MODEL), BF16)],
        compiler_params=pltpu.CompilerParams(
            dimension_semantics=("arbitrary", "arbitrary"), vmem_limit_bytes=BIG_VMEM_LIMIT),
        name="in_proj",
    )(x3, g, w_in_bf, perms, *later_weights)


IN_TILE = 1024
RNN_CHUNK = 1024
STRIPE_ROWS = RNN_CHUNK // SUBLANES


def _scan_block(a, b, reverse):
    n = a.shape[0]
    row = lax.broadcasted_iota(jnp.int32, a.shape, 0)
    s = 1
    while s < n:
        if reverse:
            ra = pltpu.roll(a, n - s, 0)
            rb = pltpu.roll(b, n - s, 0)
            m = row < n - s
        else:
            ra = pltpu.roll(a, s, 0)
            rb = pltpu.roll(b, s, 0)
            m = row >= s
        b = jnp.where(m, a * rb + b, b)
        a = jnp.where(m, a * ra, a)
        s *= 2
    return a, b


def _rnn_kernel(*refs, reverse, final, nchunks):
    if final:
        (xc_ref, hf_ref, gate_ref, wa_ref, wx_ref, ba_ref, bx_ref, lam_ref, gn_ref,
         out_ref, a_scr, b_scr, h_scr, carry) = refs
    else:
        (pa_ref, pb_ref, cur_ref, nx_ref, cw_ref, cb_ref, wa_ref, wx_ref, ba_ref, bx_ref, lam_ref,
         out_ref, xc_out_ref, a_scr, b_scr, carry) = refs
        h_scr = None

    step = pl.program_id(1)
    chunk = (nchunks - 1 - step) if reverse else step
    n_slab = D_RNN // LANES
    nv = STRIPE_ROWS
    sub = lax.broadcasted_iota(jnp.int32, (SUBLANES, LANES), 0)
    top, bot = SUBLANES - 1, 0

    @pl.when(step == 0)
    def _():
        carry[...] = jnp.zeros_like(carry)

    if final:
        xc = [xc_ref[0, j] for j in range(n_slab)]
    else:
        xc = []
        for j in range(n_slab):
            lanes = slice(j * LANES, (j + 1) * LANES)
            x = cur_ref[0, j].reshape(nv, SUBLANES, LANES)
            pa = jnp.where(chunk > 0, pa_ref[0, j], 0.0)
            pb = jnp.where(chunk > 0, pb_ref[0, j], 0.0)
            nx = jnp.where(chunk < nchunks - 1, nx_ref[0, j], 0.0)
            xm2 = pltpu.roll(jnp.where(sub == top, pa, x[nv - 2]), 1, 0)
            xm1 = pltpu.roll(jnp.where(sub == top, pb, x[nv - 1]), 1, 0)
            xp1 = pltpu.roll(jnp.where(sub == bot, nx, x[0]), SUBLANES - 1, 0)
            xext = jnp.concatenate([xm2[None], xm1[None], x, xp1[None]], axis=0)
            acc = cb_ref[:, lanes] + xext[0:nv] * cw_ref[0:1, lanes]
            for k in range(1, CONV_WIDTH):
                acc = acc + xext[k:k + nv] * cw_ref[k:k + 1, lanes]
            xc.append(acc.reshape(RNN_CHUNK, LANES))
            xc_out_ref[0, j] = xc[j]

    nlam = -lam_ref[...]
    softplus = jnp.maximum(nlam, 0.0) + jnp.log1p(jnp.exp(-jnp.abs(nlam)))
    coef = -LRU_C * softplus
    slabs_per_blk = GATE_BLK // LANES
    for jj in range(D_RNN // GATE_BLK):
        blk = slice(jj * GATE_BLK, (jj + 1) * GATE_BLK)
        xj = jnp.concatenate(xc[jj * slabs_per_blk:(jj + 1) * slabs_per_blk], axis=-1)
        xjb = xj.astype(BF16)
        za = jnp.dot(xjb, wa_ref[jj], preferred_element_type=F32) + ba_ref[:, blk]
        zx = jnp.dot(xjb, wx_ref[jj], preferred_element_type=F32) + bx_ref[:, blk]
        r = 1.0 / (1.0 + jnp.exp2(za))
        i = 1.0 / (1.0 + jnp.exp2(zx))
        log_a = coef[:, blk] * r
        a = jnp.exp(log_a)
        u = -jnp.tanh(log_a) * (1.0 + a * a)
        root = jnp.where(u > 0.0, u * lax.rsqrt(u), 0.0)
        b = root * (i * xj)
        for t in range(slabs_per_blk):
            a_scr[jj * slabs_per_blk + t] = a[:, t * LANES:(t + 1) * LANES]
            b_scr[jj * slabs_per_blk + t] = b[:, t * LANES:(t + 1) * LANES]

    order = range(nv - 1, -1, -1) if reverse else range(nv)
    group = lambda v: slice(v * SUBLANES, (v + 1) * SUBLANES)

    h = [jnp.zeros((SUBLANES, LANES), F32)] * n_slab
    p = [jnp.ones((SUBLANES, LANES), F32)] * n_slab
    for v in order:
        for j in range(n_slab):
            av = a_scr[j, group(v), :]
            h[j] = av * h[j] + b_scr[j, group(v), :]
            p[j] = av * p[j]

    start = []
    for j in range(n_slab):
        pc, hc = _scan_block(p[j], h[j], reverse)
        prev = carry[j]
        if reverse:
            seed = jnp.broadcast_to(prev[bot:bot + 1, :], (SUBLANES, LANES))
            ends = hc + pc * seed
            start.append(jnp.where(sub == top, pltpu.roll(prev, SUBLANES - 1, 0),
                                   pltpu.roll(ends, SUBLANES - 1, 0)))
        else:
            seed = jnp.broadcast_to(prev[top:top + 1, :], (SUBLANES, LANES))
            ends = hc + pc * seed
            start.append(jnp.where(sub == bot, pltpu.roll(prev, 1, 0), pltpu.roll(ends, 1, 0)))
        carry[j] = ends

    dst = h_scr if final else None
    h = start
    for v in order:
        for j in range(n_slab):
            h[j] = a_scr[j, group(v), :] * h[j] + b_scr[j, group(v), :]
            if final:
                dst[j, group(v), :] = h[j]
            else:
                out_ref[0, j, group(v), :] = h[j]

    if final:
        ys = []
        for j in range(n_slab):
            ys.append((hf_ref[0, j] + h_scr[j]) * gate_ref[0, j])
        ss = ys[0] * ys[0]
        for j in range(1, n_slab):
            ss = ss + ys[j] * ys[j]
        inv = lax.rsqrt(jnp.sum(ss, axis=-1, keepdims=True) * (1.0 / D_RNN) + EPS)
        for j in range(n_slab):
            lanes = slice(j * LANES, (j + 1) * LANES)
            h_scr[j] = ys[j] * inv * gn_ref[:, lanes]
        for st in range(SUBLANES):
            for j in range(n_slab):
                out_ref[0, st * nv:(st + 1) * nv, j * LANES:(j + 1) * LANES] = (
                    h_scr[j, pl.ds(st, nv, stride=SUBLANES), :].astype(BF16))


def _rnn_pass(x4, extra, params, reverse, final):
    bsz, n_slab, s, _ = x4.shape
    nchunks = s // RNN_CHUNK
    hb = RNN_CHUNK // SUBLANES
    nhalo = s // SUBLANES

    def cidx(c):
        return (nchunks - 1 - c) if reverse else c

    cur_map = lambda b, c: (b, 0, cidx(c), 0)
    const2 = lambda b, c: (0, 0)
    const3 = lambda b, c: (0, 0, 0)
    cw, cb, wa, wx, ba, bx, lam, gn = params
    halo = (1, n_slab, SUBLANES, LANES)
    full = (1, n_slab, RNN_CHUNK, LANES)
    gate_specs = [
        pl.BlockSpec((D_RNN // GATE_BLK, GATE_BLK, GATE_BLK), const3),
        pl.BlockSpec((D_RNN // GATE_BLK, GATE_BLK, GATE_BLK), const3),
        pl.BlockSpec((1, D_RNN), const2),
        pl.BlockSpec((1, D_RNN), const2),
        pl.BlockSpec((1, D_RNN), const2),
    ]
    slab_scr = pltpu.VMEM((n_slab, RNN_CHUNK, LANES), F32)
    carry_scr = pltpu.VMEM((n_slab, SUBLANES, LANES), F32)
    if final:
        in_specs = [pl.BlockSpec(full, cur_map)] * 3 + gate_specs + [pl.BlockSpec((1, D_RNN), const2)]
        args = [x4, *extra, wa, wx, ba, bx, lam, gn]
        scratch = [slab_scr, slab_scr, slab_scr, carry_scr]
        out_specs = pl.BlockSpec((1, RNN_CHUNK, D_RNN), lambda b, c: (b, cidx(c), 0))
        out_shape = jax.ShapeDtypeStruct((bsz, s, D_RNN), BF16)
    else:
        pa_map = lambda b, c: (b, 0, jnp.maximum(cidx(c) * hb - 2, 0), 0)
        pb_map = lambda b, c: (b, 0, jnp.maximum(cidx(c) * hb - 1, 0), 0)
        nx_map = lambda b, c: (b, 0, jnp.minimum((cidx(c) + 1) * hb, nhalo - 1), 0)
        in_specs = [pl.BlockSpec(halo, pa_map), pl.BlockSpec(halo, pb_map), pl.BlockSpec(full, cur_map),
                    pl.BlockSpec(halo, nx_map),
                    pl.BlockSpec((CONV_WIDTH, D_RNN), const2), pl.BlockSpec((1, D_RNN), const2)] + gate_specs
        args = [x4, x4, x4, x4, cw, cb, wa, wx, ba, bx, lam]
        scratch = [slab_scr, slab_scr, carry_scr]
        out_specs = [pl.BlockSpec(full, cur_map)] * 2
        out_shape = [jax.ShapeDtypeStruct(x4.shape, F32)] * 2
    return pl.pallas_call(
        functools.partial(_rnn_kernel, reverse=reverse, final=final, nchunks=nchunks),
        grid=(bsz, nchunks),
        in_specs=in_specs,
        out_specs=out_specs,
        out_shape=out_shape,
        scratch_shapes=scratch,
        compiler_params=pltpu.CompilerParams(
            dimension_semantics=("arbitrary", "arbitrary"), vmem_limit_bytes=VMEM_LIMIT),
        name="rnn_bwd" if reverse else "rnn_fwd",
    )(*args)


def _gate_weights(w):
    per = GATE_BLK // RNN_BLOCK
    w5 = (w * -LOG2E).reshape(D_RNN // GATE_BLK, per, RNN_BLOCK, 1, RNN_BLOCK)
    on_diag = jnp.asarray(np.eye(per, dtype=bool)).reshape(1, per, 1, per, 1)
    dense = jnp.where(on_diag, w5, 0.0)
    return dense.reshape(D_RNN // GATE_BLK, GATE_BLK, GATE_BLK).astype(BF16)


def _attn_kernel(q_ref, k_ref, v_ref, bias_ref, o_ref, lse_ref, *, sub_len, tiles_per_class):
    i = pl.program_id(2)
    n_tiles = sub_len // Q_TILE
    n_classes = q_ref.shape[1]
    lane = lax.broadcasted_iota(jnp.int32, (Q_TILE, LANES), 1)
    heads_per_slab = LANES // HEAD_DIM
    first_half = lane < HEAD_DIM
    for cl in range(n_classes):
        for t in range(tiles_per_class):
            rows = slice(t * Q_TILE, (t + 1) * Q_TILE)
            if n_classes == 1:
                out_rows = rows
            else:
                out_rows = pl.ds(t * Q_TILE * n_classes + cl, Q_TILE, stride=n_classes)
            q = q_ref[0, cl, rows, :]
            tile = i * tiles_per_class + t
            variant = jnp.where(tile == 0, FIRST, jnp.where(tile == n_tiles - 1, LAST, INTERIOR))
            start = pl.multiple_of(jnp.clip(tile * Q_TILE - HALF_STEPS, 0, sub_len - K_WIN), K_TILE)
            kk = k_ref[0, cl, pl.ds(start, K_WIN), :]
            vv = v_ref[0, cl, pl.ds(start, K_WIN), :]
            for p in range(D_ATTN // LANES):
                sl = slice(p * LANES, (p + 1) * LANES)
                qp, kp, vp = q[:, sl], kk[:, sl], vv[:, sl]
                zero = jnp.zeros_like(qp)
                q2 = jnp.concatenate([jnp.where(first_half, qp, zero), jnp.where(first_half, zero, qp)], axis=0)
                s = lax.dot_general(q2, kp, (((1,), (1,)), ((), ())), preferred_element_type=F32)
                h0 = p * heads_per_slab
                s = s + bias_ref[variant, h0:h0 + heads_per_slab].reshape(heads_per_slab * Q_TILE, K_WIN)
                m = jnp.max(s, axis=-1, keepdims=True)
                e = jnp.exp2(s - m)
                l = jnp.sum(e, axis=-1, keepdims=True)
                o2 = jnp.dot(e.astype(BF16), vp, preferred_element_type=F32) * (1.0 / l)
                lse2 = jnp.broadcast_to(m + jnp.log(l) * LOG2E, (heads_per_slab * Q_TILE, LANES))
                o_ref[0, p, out_rows, :] = jnp.where(first_half, o2[:Q_TILE], o2[Q_TILE:])
                lse_ref[0, p, out_rows, :] = jnp.where(first_half, lse2[:Q_TILE], lse2[Q_TILE:])


def _attn_pattern(qc, kc, vc, bias, pat):
    bsz, groups, n_classes, sub_len, _ = qc.shape
    tiles_per_class = TILES_PER_STEP // n_classes
    q_rows = tiles_per_class * Q_TILE
    n_slab = D_ATTN // LANES
    qmap = lambda b, g, i: (b, g, 0, i, 0)
    seq_map = lambda b, g, i: (b, g, 0, 0, 0)
    in_specs = [
        pl.BlockSpec((1, None, n_classes, q_rows, D_ATTN), qmap),
        pl.BlockSpec((1, None, n_classes, sub_len, D_ATTN), seq_map),
        pl.BlockSpec((1, None, n_classes, sub_len, D_ATTN), seq_map),
        pl.BlockSpec((None, 3, N_HEADS, Q_TILE, K_WIN), lambda b, g, i: (pat, 0, 0, 0, 0)),
    ]
    out_spec = pl.BlockSpec((1, None, n_slab, n_classes * q_rows, LANES), qmap)
    return pl.pallas_call(
        functools.partial(_attn_kernel, sub_len=sub_len, tiles_per_class=tiles_per_class),
        grid=(bsz, groups, sub_len // q_rows),
        in_specs=in_specs,
        out_specs=[out_spec] * 2,
        out_shape=[jax.ShapeDtypeStruct((bsz, groups, n_slab, n_classes * sub_len, LANES), F32)] * 2,
        compiler_params=pltpu.CompilerParams(
            dimension_semantics=("arbitrary",) * 3, vmem_limit_bytes=BIG_VMEM_LIMIT),
        name=f"attn_d{groups * n_classes}",
    )(qc, kc, vc, bias)


FF_CHUNK = 1024


SUB_ROWS = 512


def _token_order(ref, scr, u, slab):
    groups = ref.shape[1]
    per = SUB_ROWS // groups
    if groups == 1:
        return ref[0, 0, slab, u * per:(u + 1) * per, :]
    for g in range(groups):
        scr[u, slab, pl.ds(g, per, stride=groups), :] = ref[0, g, slab, u * per:(u + 1) * per, :]
    return scr[u, slab]


def _out_mlp_kernel(x_ref, mr_ref, o1, o2, o3, l1, l2, l3, ga_ref, wo_ref, gm_ref, wu_ref, wd_ref,
                    gf_ref, out_ref, s_o3, s_l3, *, tm):
    n_sub = tm // SUB_ROWS

    def merge(u):
        ys = []
        for slab in range(D_ATTN // LANES):
            a1 = _token_order(l1, None, u, slab)
            a2 = _token_order(l2, None, u, slab)
            a3 = _token_order(l3, s_l3, u, slab)
            mx = jnp.maximum(jnp.maximum(a1, a2), a3)
            e1, e2, e3 = jnp.exp2(a1 - mx), jnp.exp2(a2 - mx), jnp.exp2(a3 - mx)
            ys.append((e1 * _token_order(o1, None, u, slab) + e2 * _token_order(o2, None, u, slab)
                       + e3 * _token_order(o3, s_o3, u, slab)) / (e1 + e2 + e3))
        return _rms(jnp.concatenate(ys, axis=-1), ga_ref[...]).astype(BF16)

    def ff_chunk(h, c):
        z = jnp.dot(h, wu_ref[:, c * FF_CHUNK:(c + 1) * FF_CHUNK], preferred_element_type=F32)
        return jnp.square(jnp.maximum(z, 0.0)).astype(BF16)

    def down(z, c):
        return jnp.dot(z, wd_ref[c * FF_CHUNK:(c + 1) * FF_CHUNK, :], preferred_element_type=F32)

    assert n_sub == 1
    halves = [slice(0, tm // 2), slice(tm // 2, tm)]
    rnn_part = [jnp.dot(mr_ref[0, rows, :], wo_ref[0:D_RNN, :], preferred_element_type=F32) for rows in halves]
    mix_attn = merge(0)
    x1s, hs = [], []
    for rows, part in zip(halves, rnn_part):
        proj = part + jnp.dot(mix_attn[rows, :], wo_ref[D_RNN:D_RNN + D_ATTN, :], preferred_element_type=F32)
        x1 = proj + x_ref[0, rows, :]
        x1s.append(x1)
        hs.append(_rms(x1, gm_ref[...]).astype(BF16))
    h = jnp.concatenate(hs, axis=0)
    n_chunk = D_FF // FF_CHUNK
    ff = None
    for c in range(n_chunk - 1):
        d = down(ff_chunk(h, c), c)
        ff = d if ff is None else ff + d
    z = ff_chunk(h, n_chunk - 1)
    for rows, x1 in zip(halves, x1s):
        out_ref[0, rows, :] = _rms(ff[rows, :] + down(z[rows, :], n_chunk - 1) + x1, gf_ref[...])


def _out_mlp(x3, mix_rnn, os_, lses, ga, wo, gm, wu, wd, gf, tm):
    bsz, s, _ = x3.shape
    row = lambda b, i: (b, i, 0)
    const = lambda b, i: (0, 0)
    once = pl.Buffered(1)
    n_slab = D_ATTN // LANES
    attn_specs = [pl.BlockSpec((1, o.shape[1], n_slab, tm // o.shape[1], LANES), lambda b, i: (b, 0, 0, i, 0))
                  for o in os_]
    in_specs = [pl.BlockSpec((1, tm, D_MODEL), row), pl.BlockSpec((1, tm, D_RNN), row)]
    in_specs += attn_specs + attn_specs
    in_specs += [
        pl.BlockSpec((1, D_ATTN), const),
        pl.BlockSpec((D_RNN + D_ATTN, D_MODEL), const, pipeline_mode=once),
        pl.BlockSpec((1, D_MODEL), const),
        pl.BlockSpec((D_MODEL, D_FF), const, pipeline_mode=once),
        pl.BlockSpec((D_FF, D_MODEL), const, pipeline_mode=once),
        pl.BlockSpec((1, D_MODEL), const),
    ]
    return pl.pallas_call(
        functools.partial(_out_mlp_kernel, tm=tm),
        grid=(bsz, s // tm),
        in_specs=in_specs,
        out_specs=pl.BlockSpec((1, tm, D_MODEL), row),
        out_shape=jax.ShapeDtypeStruct((bsz, s, D_MODEL), F32),
        scratch_shapes=[pltpu.VMEM((tm // SUB_ROWS, n_slab, SUB_ROWS, LANES), F32)] * 2,
        compiler_params=pltpu.CompilerParams(
            dimension_semantics=("arbitrary", "arbitrary"), vmem_limit_bytes=BIG_VMEM_LIMIT),
        name="out_mlp",
    )(x3, mix_rnn, *os_, *lses, ga, wo, gm, wu, wd, gf)


def kernel(x, attn_norm_g, w_in, conv_w, conv_b, lru_wa_fwd, lru_ba_fwd, lru_wx_fwd, lru_bx_fwd, lru_lam_fwd, lru_wa_bwd, lru_ba_bwd, lru_wx_bwd, lru_bx_bwd, lru_lam_bwd, rel_bias, norm_rnn_g, norm_attn_g, w_out, mlp_norm_g, w_up, w_down, final_norm_g):
    depth = w_in.shape[0]
    assert depth == 1, "the final RMSNorm is fused into the single layer's last call"
    l = 0
    n_pat = len(PATTERNS)
    row = lambda v: v.reshape(1, -1)
    bias = _bias_tables(rel_bias)
    proj = _in_proj(x, row(attn_norm_g[l]), w_in[l].astype(BF16), (w_out[l], w_up[l], w_down[l]), tm=IN_TILE)
    xr, gate = proj[0], proj[1]
    qs, ks, vs = (proj[2 + a * n_pat:2 + (a + 1) * n_pat] for a in range(3))
    w_out_bf, w_up_bf, w_down_bf = proj[2 + 3 * n_pat:]
    gate_bias = lambda v: row(v * -LOG2E)
    fwd = (conv_w[l], row(conv_b[l]), _gate_weights(lru_wa_fwd[l]), _gate_weights(lru_wx_fwd[l]),
           gate_bias(lru_ba_fwd[l]), gate_bias(lru_bx_fwd[l]), row(lru_lam_fwd[l]), None)
    bwd = (conv_w[l], row(conv_b[l]), _gate_weights(lru_wa_bwd[l]), _gate_weights(lru_wx_bwd[l]),
           gate_bias(lru_ba_bwd[l]), gate_bias(lru_bx_bwd[l]), row(lru_lam_bwd[l]), row(norm_rnn_g[l]))
    h_f, xc = _rnn_pass(xr, (), fwd, reverse=False, final=False)
    mix_rnn = _rnn_pass(xc, (h_f, gate), bwd, reverse=True, final=True)
    os_, lses = [], []
    for pat in range(n_pat):
        o, lse = _attn_pattern(qs[pat], ks[pat], vs[pat], bias, pat)
        os_.append(o)
        lses.append(lse)
    return _out_mlp(x, mix_rnn, os_, lses, row(norm_attn_g[l]), w_out_bf,
                    row(mlp_norm_g[l]), w_up_bf, w_down_bf, row(final_norm_g), tm=512)
```

```python
import functools
import math

import numpy as np
import jax
import jax.numpy as jnp
from jax import lax
from jax.experimental import pallas as pl
from jax.experimental.pallas import tpu as pltpu

F32 = jnp.float32
BF16 = jnp.bfloat16

D_MODEL = 1024
D_RNN = 512
N_RNN_BLOCKS = 8
RNN_BLOCK = D_RNN // N_RNN_BLOCKS
CONV_WIDTH = 4
CONV_LEFT = 2
LRU_C = 8.0
N_HEADS = 8
HEAD_DIM = 64
D_ATTN = N_HEADS * HEAD_DIM
PATTERNS = ((128, 1), (512, 4), (2048, 16))
HALF_STEPS = 64
N_BUCKETS = 32
MAX_DISTANCE = 1024
D_IN = 2 * D_RNN + 3 * D_ATTN
D_FF = 4 * D_MODEL
EPS = 1e-6
NEG_INF = -1e30

LANES = 128
SUBLANES = 8
MXU_DIM = 256

Q_TILE = 128
K_TILE = 64
K_WIN = Q_TILE + 2 * HALF_STEPS
TILES_PER_STEP = 8
CLASS_GROUP = 4
LOG2E = math.log2(math.e)
INTERIOR, FIRST, LAST = 0, 1, 2
GATE_BLK = MXU_DIM

VMEM_LIMIT = 48 * 1024 * 1024
BIG_VMEM_LIMIT = 56 * 1024 * 1024


def _rms(x, g):
    ms = jnp.mean(x * x, axis=-1, keepdims=True)
    return x * lax.rsqrt(ms + EPS) * g


def _t5_bucket_np(rel):
    nb = N_BUCKETS // 2
    max_exact = nb // 2
    ret = np.where(rel > 0, nb, 0)
    n = np.abs(rel)
    nf = np.maximum(n, 1).astype(np.float32)
    large = max_exact + (np.log(nf / np.float32(max_exact)) / np.float32(math.log(MAX_DISTANCE / max_exact))
                         * np.float32(nb - max_exact)).astype(np.int32)
    large = np.minimum(large, nb - 1)
    return ret + np.where(n < max_exact, n, large)


def _bucket_index_rows():
    step = np.arange(K_WIN) - HALF_STEPS
    rows = []
    for _, dil in PATTERNS:
        b = _t5_bucket_np((step * dil).astype(np.int32))
        rows.append(np.where(np.abs(step) <= HALF_STEPS, b, -1))
    return np.stack(rows, axis=0).astype(np.int32)[:, None, :]


def _bias_table_kernel(idx_ref, rb_ref, out_ref):
    idx = idx_ref[0]
    col = lax.broadcasted_iota(jnp.int32, (Q_TILE, K_WIN), 1)
    hit = [idx == b for b in range(N_BUCKETS)]
    for h in range(N_HEADS):
        first_row = jnp.full(idx.shape, NEG_INF, F32)
        for b in range(N_BUCKETS):
            first_row = jnp.where(hit[b], rb_ref[b, h] * LOG2E, first_row)
        acc = pltpu.roll(jnp.broadcast_to(first_row, (Q_TILE, K_WIN)), 0, 1, stride=1, stride_axis=0)
        out_ref[0, INTERIOR, h] = acc
        out_ref[0, FIRST, h] = jnp.where(col < K_WIN - HALF_STEPS, pltpu.roll(acc, K_WIN - HALF_STEPS, 1), NEG_INF)
        out_ref[0, LAST, h] = jnp.where(col >= HALF_STEPS, pltpu.roll(acc, HALF_STEPS, 1), NEG_INF)


def _bias_tables(rel_bias):
    idx = jnp.asarray(_bucket_index_rows())
    n_pat = idx.shape[0]
    return pl.pallas_call(
        _bias_table_kernel,
        grid=(n_pat,),
        in_specs=[
            pl.BlockSpec((1, 1, K_WIN), lambda g: (g, 0, 0)),
            pl.BlockSpec(memory_space=pltpu.SMEM),
        ],
        out_specs=pl.BlockSpec((1, 3, N_HEADS, Q_TILE, K_WIN), lambda g: (g, 0, 0, 0, 0)),
        out_shape=jax.ShapeDtypeStruct((n_pat, 3, N_HEADS, Q_TILE, K_WIN), F32),
        name="bias_table",
    )(idx, rel_bias)


def _stripe_store(ref, p, part):
    stripes = IN_TILE // STRIPE_ROWS
    for st in range(stripes):
        for j in range(D_RNN // LANES):
            ref[0, j, pl.ds(part * stripes + st, STRIPE_ROWS, stride=SUBLANES), :] = (
                p[st * STRIPE_ROWS:(st + 1) * STRIPE_ROWS, j * LANES:(j + 1) * LANES])


PERM_ROWS = MXU_DIM


def _class_permutations():
    mats = []
    for _, dil in PATTERNS:
        if dil == 1:
            continue
        per = PERM_ROWS // dil
        p = np.zeros((PERM_ROWS, PERM_ROWS), np.float32)
        for r in range(dil):
            for m in range(per):
                p[r * per + m, m * dil + r] = 1.0
        mats.append(p)
    return np.stack(mats)


def _in_proj_kernel(x_ref, g_ref, w_ref, perm_ref, *rest, tm, n_later):
    n_pat = len(PATTERNS)
    later_in, rest = rest[:n_later], rest[n_later:]
    xr_ref, gate_ref, rest = rest[0], rest[1], rest[2:]
    outs = [rest[a * n_pat:(a + 1) * n_pat] for a in range(3)]
    later_out = rest[3 * n_pat:3 * n_pat + n_later]
    h_scr = rest[3 * n_pat + n_later]
    for src, dst in zip(later_in, later_out):
        dst[...] = src[...].astype(BF16)
    h_scr[...] = _rms(x_ref[0], g_ref[...]).astype(BF16)

    def seg(lo, width):
        return jnp.dot(h_scr[...], w_ref[:, lo:lo + width], preferred_element_type=F32)

    for a, scale in enumerate((HEAD_DIM ** -0.5 * LOG2E, 1.0, 1.0)):
        p = seg(2 * D_RNN + a * D_ATTN, D_ATTN)
        if scale != 1.0:
            p = p * scale
        pb = p.astype(BF16)
        dilated = 0
        for (_, dil), o_ref in zip(PATTERNS, outs[a]):
            if dil == 1:
                o_ref[0, 0, 0] = pb
                continue
            per = PERM_ROWS // dil
            for blk in range(tm // PERM_ROWS):
                y = jnp.dot(perm_ref[dilated], pb[blk * PERM_ROWS:(blk + 1) * PERM_ROWS, :],
                            preferred_element_type=F32).astype(BF16)
                groups = o_ref.shape[1]
                for r in range(dil):
                    o_ref[0, r % groups, r // groups, blk * per:(blk + 1) * per, :] = y[r * per:(r + 1) * per, :]
            dilated += 1
    g = seg(D_RNN, D_RNN)
    gelu = g * (0.5 * (1.0 + jnp.tanh(math.sqrt(2.0 / math.pi) * (g + 0.044715 * (g * g * g)))))
    part = pl.program_id(1) % (RNN_CHUNK // IN_TILE)
    _stripe_store(gate_ref, gelu, part)
    _stripe_store(xr_ref, seg(0, D_RNN), part)


def _in_proj(x3, g, w_in_bf, later_weights, tm):
    assert tm == IN_TILE
    bsz, s, _ = x3.shape
    per_seq = s // tm
    n_steps = bsz * per_seq
    row_slice = lambda b, i: (b * per_seq + i, 0)
    later_specs = [pl.BlockSpec((w.shape[0] // n_steps, w.shape[1]), row_slice) for w in later_weights]
    later_shapes = [jax.ShapeDtypeStruct(w.shape, BF16) for w in later_weights]
    row = lambda b, i: (b, i, 0)
    cls = lambda b, i: (b, 0, 0, i, 0)
    chunk = lambda b, i: (b, 0, i // (RNN_CHUNK // IN_TILE), 0)
    const = lambda b, i: (0, 0)
    perms = jnp.asarray(_class_permutations(), BF16)
    qkv_specs, qkv_shapes = [], []
    for _ in range(3):
        for _, dil in PATTERNS:
            groups, per_group = dil // min(dil, CLASS_GROUP), min(dil, CLASS_GROUP)
            qkv_specs.append(pl.BlockSpec((1, groups, per_group, tm // dil, D_ATTN), cls))
            qkv_shapes.append(jax.ShapeDtypeStruct((bsz, groups, per_group, s // dil, D_ATTN), BF16))
    return pl.pallas_call(
        functools.partial(_in_proj_kernel, tm=tm, n_later=len(later_weights)),
        grid=(bsz, per_seq),
        in_specs=[
            pl.BlockSpec((1, tm, D_MODEL), row),
            pl.BlockSpec((1, D_MODEL), const),
            pl.BlockSpec((D_MODEL, D_IN), const, pipeline_mode=pl.Buffered(1)),
            pl.BlockSpec(perms.shape, lambda b, i: (0, 0, 0)),
        ] + later_specs,
        out_specs=[pl.BlockSpec((1, D_RNN // LANES, RNN_CHUNK, LANES), chunk)] * 2 + qkv_specs + later_specs,
        out_shape=[jax.ShapeDtypeStruct((bsz, D_RNN // LANES, s, LANES), F32)] * 2 + qkv_shapes + later_shapes,
        scratch_shapes=[pltpu.VMEM((tm, D_MODEL), BF16)],
        compiler_params=pltpu.CompilerParams(
            dimension_semantics=("arbitrary", "arbitrary"), vmem_limit_bytes=BIG_VMEM_LIMIT),
        name="in_proj",
    )(x3, g, w_in_bf, perms, *later_weights)


IN_TILE = 1024
RNN_CHUNK = 1024
STRIPE_ROWS = RNN_CHUNK // SUBLANES


def _scan_block(a, b, reverse):
    n = a.shape[0]
    row = lax.broadcasted_iota(jnp.int32, a.shape, 0)
    s = 1
    while s < n:
        if reverse:
            ra = pltpu.roll(a, n - s, 0)
            rb = pltpu.roll(b, n - s, 0)
            m = row < n - s
        else:
            ra = pltpu.roll(a, s, 0)
            rb = pltpu.roll(b, s, 0)
            m = row >= s
        b = jnp.where(m, a * rb + b, b)
        a = jnp.where(m, a * ra, a)
        s *= 2
    return a, b


def _rnn_kernel(*refs, reverse, final, nchunks):
    if final:
        (xc_ref, hf_ref, gate_ref, wa_ref, wx_ref, ba_ref, bx_ref, lam_ref, gn_ref,
         out_ref, a_scr, b_scr, h_scr, carry) = refs
    else:
        (pa_ref, pb_ref, cur_ref, nx_ref, cw_ref, cb_ref, wa_ref, wx_ref, ba_ref, bx_ref, lam_ref,
         out_ref, xc_out_ref, a_scr, b_scr, carry) = refs
        h_scr = None

    step = pl.program_id(1)
    chunk = (nchunks - 1 - step) if reverse else step
    n_slab = D_RNN // LANES
    nv = STRIPE_ROWS
    sub = lax.broadcasted_iota(jnp.int32, (SUBLANES, LANES), 0)
    top, bot = SUBLANES - 1, 0

    @pl.when(step == 0)
    def _():
        carry[...] = jnp.zeros_like(carry)

    if final:
        xc = [xc_ref[0, j] for j in range(n_slab)]
    else:
        xc = []
        for j in range(n_slab):
            lanes = slice(j * LANES, (j + 1) * LANES)
            x = cur_ref[0, j].reshape(nv, SUBLANES, LANES)
            pa = jnp.where(chunk > 0, pa_ref[0, j], 0.0)
            pb = jnp.where(chunk > 0, pb_ref[0, j], 0.0)
            nx = jnp.where(chunk < nchunks - 1, nx_ref[0, j], 0.0)
            xm2 = pltpu.roll(jnp.where(sub == top, pa, x[nv - 2]), 1, 0)
            xm1 = pltpu.roll(jnp.where(sub == top, pb, x[nv - 1]), 1, 0)
            xp1 = pltpu.roll(jnp.where(sub == bot, nx, x[0]), SUBLANES - 1, 0)
            xext = jnp.concatenate([xm2[None], xm1[None], x, xp1[None]], axis=0)
            acc = cb_ref[:, lanes] + xext[0:nv] * cw_ref[0:1, lanes]
            for k in range(1, CONV_WIDTH):
                acc = acc + xext[k:k + nv] * cw_ref[k:k + 1, lanes]
            xc.append(acc.reshape(RNN_CHUNK, LANES))
            xc_out_ref[0, j] = xc[j]

    nlam = -lam_ref[...]
    softplus = jnp.maximum(nlam, 0.0) + jnp.log1p(jnp.exp(-jnp.abs(nlam)))
    coef = -LRU_C * softplus
    slabs_per_blk = GATE_BLK // LANES
    for jj in range(D_RNN // GATE_BLK):
        blk = slice(jj * GATE_BLK, (jj + 1) * GATE_BLK)
        xj = jnp.concatenate(xc[jj * slabs_per_blk:(jj + 1) * slabs_per_blk], axis=-1)
        xjb = xj.astype(BF16)
        za = jnp.dot(xjb, wa_ref[jj], preferred_element_type=F32) + ba_ref[:, blk]
        zx = jnp.dot(xjb, wx_ref[jj], preferred_element_type=F32) + bx_ref[:, blk]
        r = 1.0 / (1.0 + jnp.exp(-za))
        i = 1.0 / (1.0 + jnp.exp(-zx))
        log_a = coef[:, blk] * r
        a = jnp.exp(log_a)
        u = -jnp.tanh(log_a) * (1.0 + a * a)
        root = jnp.where(u > 0.0, u * lax.rsqrt(u), 0.0)
        b = root * (i * xj)
        for t in range(slabs_per_blk):
            a_scr[jj * slabs_per_blk + t] = a[:, t * LANES:(t + 1) * LANES]
            b_scr[jj * slabs_per_blk + t] = b[:, t * LANES:(t + 1) * LANES]

    order = range(nv - 1, -1, -1) if reverse else range(nv)
    group = lambda v: slice(v * SUBLANES, (v + 1) * SUBLANES)

    h = [jnp.zeros((SUBLANES, LANES), F32)] * n_slab
    p = [jnp.ones((SUBLANES, LANES), F32)] * n_slab
    for v in order:
        for j in range(n_slab):
            av = a_scr[j, group(v), :]
            h[j] = av * h[j] + b_scr[j, group(v), :]
            p[j] = av * p[j]

    start = []
    for j in range(n_slab):
        pc, hc = _scan_block(p[j], h[j], reverse)
        prev = carry[j]
        if reverse:
            seed = jnp.broadcast_to(prev[bot:bot + 1, :], (SUBLANES, LANES))
            ends = hc + pc * seed
            start.append(jnp.where(sub == top, pltpu.roll(prev, SUBLANES - 1, 0),
                                   pltpu.roll(ends, SUBLANES - 1, 0)))
        else:
            seed = jnp.broadcast_to(prev[top:top + 1, :], (SUBLANES, LANES))
            ends = hc + pc * seed
            start.append(jnp.where(sub == bot, pltpu.roll(prev, 1, 0), pltpu.roll(ends, 1, 0)))
        carry[j] = ends

    dst = h_scr if final else None
    h = start
    for v in order:
        for j in range(n_slab):
            h[j] = a_scr[j, group(v), :] * h[j] + b_scr[j, group(v), :]
            if final:
                dst[j, group(v), :] = h[j]
            else:
                out_ref[0, j, group(v), :] = h[j]

    if final:
        ys = []
        for j in range(n_slab):
            ys.append((hf_ref[0, j] + h_scr[j]) * gate_ref[0, j])
        ss = ys[0] * ys[0]
        for j in range(1, n_slab):
            ss = ss + ys[j] * ys[j]
        inv = lax.rsqrt(jnp.sum(ss, axis=-1, keepdims=True) * (1.0 / D_RNN) + EPS)
        for j in range(n_slab):
            lanes = slice(j * LANES, (j + 1) * LANES)
            h_scr[j] = ys[j] * inv * gn_ref[:, lanes]
        for st in range(SUBLANES):
            for j in range(n_slab):
                out_ref[0, st * nv:(st + 1) * nv, j * LANES:(j + 1) * LANES] = (
                    h_scr[j, pl.ds(st, nv, stride=SUBLANES), :].astype(BF16))


def _rnn_pass(x4, extra, params, reverse, final):
    bsz, n_slab, s, _ = x4.shape
    nchunks = s // RNN_CHUNK
    hb = RNN_CHUNK // SUBLANES
    nhalo = s // SUBLANES

    def cidx(c):
        return (nchunks - 1 - c) if reverse else c

    cur_map = lambda b, c: (b, 0, cidx(c), 0)
    const2 = lambda b, c: (0, 0)
    const3 = lambda b, c: (0, 0, 0)
    cw, cb, wa, wx, ba, bx, lam, gn = params
    halo = (1, n_slab, SUBLANES, LANES)
    full = (1, n_slab, RNN_CHUNK, LANES)
    gate_specs = [
        pl.BlockSpec((D_RNN // GATE_BLK, GATE_BLK, GATE_BLK), const3),
        pl.BlockSpec((D_RNN // GATE_BLK, GATE_BLK, GATE_BLK), const3),
        pl.BlockSpec((1, D_RNN), const2),
        pl.BlockSpec((1, D_RNN), const2),
        pl.BlockSpec((1, D_RNN), const2),
    ]
    slab_scr = pltpu.VMEM((n_slab, RNN_CHUNK, LANES), F32)
    carry_scr = pltpu.VMEM((n_slab, SUBLANES, LANES), F32)
    if final:
        in_specs = [pl.BlockSpec(full, cur_map)] * 3 + gate_specs + [pl.BlockSpec((1, D_RNN), const2)]
        args = [x4, *extra, wa, wx, ba, bx, lam, gn]
        scratch = [slab_scr, slab_scr, slab_scr, carry_scr]
        out_specs = pl.BlockSpec((1, RNN_CHUNK, D_RNN), lambda b, c: (b, cidx(c), 0))
        out_shape = jax.ShapeDtypeStruct((bsz, s, D_RNN), BF16)
    else:
        pa_map = lambda b, c: (b, 0, jnp.maximum(cidx(c) * hb - 2, 0), 0)
        pb_map = lambda b, c: (b, 0, jnp.maximum(cidx(c) * hb - 1, 0), 0)
        nx_map = lambda b, c: (b, 0, jnp.minimum((cidx(c) + 1) * hb, nhalo - 1), 0)
        in_specs = [pl.BlockSpec(halo, pa_map), pl.BlockSpec(halo, pb_map), pl.BlockSpec(full, cur_map),
                    pl.BlockSpec(halo, nx_map),
                    pl.BlockSpec((CONV_WIDTH, D_RNN), const2), pl.BlockSpec((1, D_RNN), const2)] + gate_specs
        args = [x4, x4, x4, x4, cw, cb, wa, wx, ba, bx, lam]
        scratch = [slab_scr, slab_scr, carry_scr]
        out_specs = [pl.BlockSpec(full, cur_map)] * 2
        out_shape = [jax.ShapeDtypeStruct(x4.shape, F32)] * 2
    return pl.pallas_call(
        functools.partial(_rnn_kernel, reverse=reverse, final=final, nchunks=nchunks),
        grid=(bsz, nchunks),
        in_specs=in_specs,
        out_specs=out_specs,
        out_shape=out_shape,
        scratch_shapes=scratch,
        compiler_params=pltpu.CompilerParams(
            dimension_semantics=("arbitrary", "arbitrary"), vmem_limit_bytes=VMEM_LIMIT),
        name="rnn_bwd" if reverse else "rnn_fwd",
    )(*args)


def _gate_weights(w):
    per = GATE_BLK // RNN_BLOCK
    w5 = w.reshape(D_RNN // GATE_BLK, per, RNN_BLOCK, 1, RNN_BLOCK)
    on_diag = jnp.asarray(np.eye(per, dtype=bool)).reshape(1, per, 1, per, 1)
    dense = jnp.where(on_diag, w5, 0.0)
    return dense.reshape(D_RNN // GATE_BLK, GATE_BLK, GATE_BLK).astype(BF16)


def _attn_kernel(q_ref, kp_ref, k_ref, kn_ref, vp_ref, v_ref, vn_ref, bias_ref, o_ref, lse_ref, kbuf, vbuf,
                 *, sub_len, tiles_per_class):
    i = pl.program_id(2)
    n_tiles = sub_len // Q_TILE
    n_classes = q_ref.shape[1]
    q_rows = tiles_per_class * Q_TILE
    for buf, before, own, after in ((kbuf, kp_ref, k_ref, kn_ref), (vbuf, vp_ref, v_ref, vn_ref)):
        for cl in range(n_classes):
            buf[cl, 0:HALF_STEPS, :] = before[0, cl]
            buf[cl, HALF_STEPS:HALF_STEPS + q_rows, :] = own[0, cl]
            buf[cl, HALF_STEPS + q_rows:, :] = after[0, cl]
    buf_origin = i * q_rows - HALF_STEPS
    lane = lax.broadcasted_iota(jnp.int32, (Q_TILE, LANES), 1)
    heads_per_slab = LANES // HEAD_DIM
    first_half = lane < HEAD_DIM
    for cl in range(n_classes):
        for t in range(tiles_per_class):
            rows = slice(t * Q_TILE, (t + 1) * Q_TILE)
            if n_classes == 1:
                out_rows = rows
            else:
                out_rows = pl.ds(t * Q_TILE * n_classes + cl, Q_TILE, stride=n_classes)
            q = q_ref[0, cl, rows, :]
            tile = i * tiles_per_class + t
            variant = jnp.where(tile == 0, FIRST, jnp.where(tile == n_tiles - 1, LAST, INTERIOR))
            start = jnp.clip(tile * Q_TILE - HALF_STEPS, 0, sub_len - K_WIN) - buf_origin
            start = pl.multiple_of(start, K_TILE)
            kk = kbuf[cl, pl.ds(start, K_WIN), :]
            vv = vbuf[cl, pl.ds(start, K_WIN), :]
            for p in range(D_ATTN // LANES):
                sl = slice(p * LANES, (p + 1) * LANES)
                qp, kp, vp = q[:, sl], kk[:, sl], vv[:, sl]
                zero = jnp.zeros_like(qp)
                q2 = jnp.concatenate([jnp.where(first_half, qp, zero), jnp.where(first_half, zero, qp)], axis=0)
                s = lax.dot_general(q2, kp, (((1,), (1,)), ((), ())), preferred_element_type=F32)
                h0 = p * heads_per_slab
                s = s + bias_ref[variant, h0:h0 + heads_per_slab].reshape(heads_per_slab * Q_TILE, K_WIN)
                m = jnp.max(s, axis=-1, keepdims=True)
                e = jnp.exp2(s - m)
                l = jnp.sum(e, axis=-1, keepdims=True)
                o2 = jnp.dot(e.astype(BF16), vp, preferred_element_type=F32) * (1.0 / l)
                lse2 = jnp.broadcast_to(m + jnp.log(l) * LOG2E, (heads_per_slab * Q_TILE, LANES))
                o_ref[0, p, out_rows, :] = jnp.where(first_half, o2[:Q_TILE], o2[Q_TILE:])
                lse_ref[0, p, out_rows, :] = jnp.where(first_half, lse2[:Q_TILE], lse2[Q_TILE:])


def _attn_pattern(qc, kc, vc, bias, pat):
    bsz, groups, n_classes, sub_len, _ = qc.shape
    tiles_per_class = TILES_PER_STEP // n_classes
    q_rows = tiles_per_class * Q_TILE
    n_slab = D_ATTN // LANES
    qmap = lambda b, g, i: (b, g, 0, i, 0)
    halo_per_step = q_rows // HALF_STEPS
    n_halo = sub_len // HALF_STEPS
    before = lambda b, g, i: (b, g, 0, jnp.maximum(i * halo_per_step - 1, 0), 0)
    after = lambda b, g, i: (b, g, 0, jnp.minimum((i + 1) * halo_per_step, n_halo - 1), 0)
    own_spec = pl.BlockSpec((1, None, n_classes, q_rows, D_ATTN), qmap)
    kv_specs = [pl.BlockSpec((1, None, n_classes, HALF_STEPS, D_ATTN), before), own_spec,
                pl.BlockSpec((1, None, n_classes, HALF_STEPS, D_ATTN), after)]
    in_specs = [own_spec] + kv_specs + kv_specs
    in_specs.append(pl.BlockSpec((None, 3, N_HEADS, Q_TILE, K_WIN), lambda b, g, i: (pat, 0, 0, 0, 0)))
    window_buf = pltpu.VMEM((n_classes, q_rows + 2 * HALF_STEPS, D_ATTN), BF16)
    out_spec = pl.BlockSpec((1, None, n_slab, n_classes * q_rows, LANES), qmap)
    return pl.pallas_call(
        functools.partial(_attn_kernel, sub_len=sub_len, tiles_per_class=tiles_per_class),
        grid=(bsz, groups, sub_len // q_rows),
        in_specs=in_specs,
        out_specs=[out_spec] * 2,
        out_shape=[jax.ShapeDtypeStruct((bsz, groups, n_slab, n_classes * sub_len, LANES), F32)] * 2,
        scratch_shapes=[window_buf, window_buf],
        compiler_params=pltpu.CompilerParams(
            dimension_semantics=("arbitrary",) * 3, vmem_limit_bytes=VMEM_LIMIT),
        name=f"attn_d{groups * n_classes}",
    )(qc, kc, kc, kc, vc, vc, vc, bias)


FF_CHUNK = 1024


SUB_ROWS = 512


def _token_order(ref, scr, u, slab):
    groups = ref.shape[1]
    per = SUB_ROWS // groups
    if groups == 1:
        return ref[0, 0, slab, u * per:(u + 1) * per, :]
    for g in range(groups):
        scr[u, slab, pl.ds(g, per, stride=groups), :] = ref[0, g, slab, u * per:(u + 1) * per, :]
    return scr[u, slab]


def _out_mlp_kernel(x_ref, mr_ref, o1, o2, o3, l1, l2, l3, ga_ref, wo_ref, gm_ref, wu_ref, wd_ref,
                    gf_ref, out_ref, s_o3, s_l3, *, tm):
    n_sub = tm // SUB_ROWS

    def merge(u):
        ys = []
        for slab in range(D_ATTN // LANES):
            a1 = _token_order(l1, None, u, slab)
            a2 = _token_order(l2, None, u, slab)
            a3 = _token_order(l3, s_l3, u, slab)
            mx = jnp.maximum(jnp.maximum(a1, a2), a3)
            e1, e2, e3 = jnp.exp2(a1 - mx), jnp.exp2(a2 - mx), jnp.exp2(a3 - mx)
            ys.append((e1 * _token_order(o1, None, u, slab) + e2 * _token_order(o2, None, u, slab)
                       + e3 * _token_order(o3, s_o3, u, slab)) / (e1 + e2 + e3))
        return _rms(jnp.concatenate(ys, axis=-1), ga_ref[...]).astype(BF16)

    def out_proj(u, mix_attn):
        rows = slice(u * SUB_ROWS, (u + 1) * SUB_ROWS)
        proj = jnp.dot(mr_ref[0, rows, :], wo_ref[0:D_RNN, :], preferred_element_type=F32)
        proj = proj + jnp.dot(mix_attn, wo_ref[D_RNN:D_RNN + D_ATTN, :], preferred_element_type=F32)
        x1 = proj + x_ref[0, rows, :]
        return x1, _rms(x1, gm_ref[...]).astype(BF16)

    def ff_chunk(h, c):
        z = jnp.dot(h, wu_ref[:, c * FF_CHUNK:(c + 1) * FF_CHUNK], preferred_element_type=F32)
        z = jnp.square(jnp.maximum(z, 0.0)).astype(BF16)
        return jnp.dot(z, wd_ref[c * FF_CHUNK:(c + 1) * FF_CHUNK, :], preferred_element_type=F32)

    def finish(u, ff, x1):
        rows = slice(u * SUB_ROWS, (u + 1) * SUB_ROWS)
        out_ref[0, rows, :] = _rms(ff + x1, gf_ref[...])

    n_chunk = D_FF // FF_CHUNK
    x1, h = out_proj(0, merge(0))
    pending = None
    for u in range(n_sub):
        ff = None
        nxt = None
        for c in range(n_chunk):
            d = ff_chunk(h, c)
            ff = d if ff is None else ff + d
            if c == 0 and pending is not None:
                finish(*pending)
            if c == 1 and u + 1 < n_sub:
                mix_next = merge(u + 1)
            if c == 2 and u + 1 < n_sub:
                nxt = out_proj(u + 1, mix_next)
        pending = (u, ff, x1)
        if nxt is not None:
            x1, h = nxt
    finish(*pending)


def _out_mlp(x3, mix_rnn, os_, lses, ga, wo, gm, wu, wd, gf, tm):
    bsz, s, _ = x3.shape
    row = lambda b, i: (b, i, 0)
    const = lambda b, i: (0, 0)
    once = pl.Buffered(1)
    n_slab = D_ATTN // LANES
    attn_specs = [pl.BlockSpec((1, o.shape[1], n_slab, tm // o.shape[1], LANES), lambda b, i: (b, 0, 0, i, 0))
                  for o in os_]
    in_specs = [pl.BlockSpec((1, tm, D_MODEL), row), pl.BlockSpec((1, tm, D_RNN), row)]
    in_specs += attn_specs + attn_specs
    in_specs += [
        pl.BlockSpec((1, D_ATTN), const),
        pl.BlockSpec((D_RNN + D_ATTN, D_MODEL), const, pipeline_mode=once),
        pl.BlockSpec((1, D_MODEL), const),
        pl.BlockSpec((D_MODEL, D_FF), const, pipeline_mode=once),
        pl.BlockSpec((D_FF, D_MODEL), const, pipeline_mode=once),
        pl.BlockSpec((1, D_MODEL), const),
    ]
    return pl.pallas_call(
        functools.partial(_out_mlp_kernel, tm=tm),
        grid=(bsz, s // tm),
        in_specs=in_specs,
        out_specs=pl.BlockSpec((1, tm, D_MODEL), row),
        out_shape=jax.ShapeDtypeStruct((bsz, s, D_MODEL), F32),
        scratch_shapes=[pltpu.VMEM((tm // SUB_ROWS, n_slab, SUB_ROWS, LANES), F32)] * 2,
        compiler_params=pltpu.CompilerParams(
            dimension_semantics=("arbitrary", "arbitrary"), vmem_limit_bytes=BIG_VMEM_LIMIT),
        name="out_mlp",
    )(x3, mix_rnn, *os_, *lses, ga, wo, gm, wu, wd, gf)


def kernel(x, attn_norm_g, w_in, conv_w, conv_b, lru_wa_fwd, lru_ba_fwd, lru_wx_fwd, lru_bx_fwd, lru_lam_fwd, lru_wa_bwd, lru_ba_bwd, lru_wx_bwd, lru_bx_bwd, lru_lam_bwd, rel_bias, norm_rnn_g, norm_attn_g, w_out, mlp_norm_g, w_up, w_down, final_norm_g):
    depth = w_in.shape[0]
    assert depth == 1, "the final RMSNorm is fused into the single layer's last call"
    l = 0
    n_pat = len(PATTERNS)
    row = lambda v: v.reshape(1, -1)
    bias = _bias_tables(rel_bias)
    proj = _in_proj(x, row(attn_norm_g[l]), w_in[l].astype(BF16), (w_out[l], w_up[l], w_down[l]), tm=IN_TILE)
    xr, gate = proj[0], proj[1]
    qs, ks, vs = (proj[2 + a * n_pat:2 + (a + 1) * n_pat] for a in range(3))
    w_out_bf, w_up_bf, w_down_bf = proj[2 + 3 * n_pat:]
    fwd = (conv_w[l], row(conv_b[l]), _gate_weights(lru_wa_fwd[l]), _gate_weights(lru_wx_fwd[l]),
           row(lru_ba_fwd[l]), row(lru_bx_fwd[l]), row(lru_lam_fwd[l]), None)
    bwd = (conv_w[l], row(conv_b[l]), _gate_weights(lru_wa_bwd[l]), _gate_weights(lru_wx_bwd[l]),
           row(lru_ba_bwd[l]), row(lru_bx_bwd[l]), row(lru_lam_bwd[l]), row(norm_rnn_g[l]))
    h_f, xc = _rnn_pass(xr, (), fwd, reverse=False, final=False)
    mix_rnn = _rnn_pass(xc, (h_f, gate), bwd, reverse=True, final=True)
    os_, lses = [], []
    for pat in range(n_pat):
        o, lse = _attn_pattern(qs[pat], ks[pat], vs[pat], bias, pat)
        os_.append(o)
        lses.append(lse)
    return _out_mlp(x, mix_rnn, os_, lses, row(norm_attn_g[l]), w_out_bf,
                    row(mlp_norm_g[l]), w_up_bf, w_down_bf, row(final_norm_g), tm=512)
```

```python
import functools
import math

import numpy as np
import jax
import jax.numpy as jnp
from jax import lax
from jax.experimental import pallas as pl
from jax.experimental.pallas import tpu as pltpu

F32 = jnp.float32
BF16 = jnp.bfloat16

D_MODEL = 1024
D_RNN = 512
N_RNN_BLOCKS = 8
RNN_BLOCK = D_RNN // N_RNN_BLOCKS
CONV_WIDTH = 4
CONV_LEFT = 2
LRU_C = 8.0
N_HEADS = 8
HEAD_DIM = 64
D_ATTN = N_HEADS * HEAD_DIM
PATTERNS = ((128, 1), (512, 4), (2048, 16))
HALF_STEPS = 64
N_BUCKETS = 32
MAX_DISTANCE = 1024
D_IN = 2 * D_RNN + 3 * D_ATTN
D_FF = 4 * D_MODEL
EPS = 1e-6
NEG_INF = -1e30

LANES = 128
SUBLANES = 8
MXU_DIM = 256

Q_TILE = 128
K_TILE = 64
K_WIN = Q_TILE + 2 * HALF_STEPS
TILES_PER_STEP = 16
CLASS_GROUP = 4
LOG2E = math.log2(math.e)
INTERIOR, FIRST, LAST = 0, 1, 2
GATE_BLK = MXU_DIM

VMEM_LIMIT = 48 * 1024 * 1024
BIG_VMEM_LIMIT = 56 * 1024 * 1024


def _rms(x, g):
    ms = jnp.mean(x * x, axis=-1, keepdims=True)
    return x * lax.rsqrt(ms + EPS) * g


def _t5_bucket_np(rel):
    nb = N_BUCKETS // 2
    max_exact = nb // 2
    ret = np.where(rel > 0, nb, 0)
    n = np.abs(rel)
    nf = np.maximum(n, 1).astype(np.float32)
    large = max_exact + (np.log(nf / np.float32(max_exact)) / np.float32(math.log(MAX_DISTANCE / max_exact))
                         * np.float32(nb - max_exact)).astype(np.int32)
    large = np.minimum(large, nb - 1)
    return ret + np.where(n < max_exact, n, large)


def _bucket_index_rows():
    step = np.arange(K_WIN) - HALF_STEPS
    rows = []
    for _, dil in PATTERNS:
        b = _t5_bucket_np((step * dil).astype(np.int32))
        rows.append(np.where(np.abs(step) <= HALF_STEPS, b, -1))
    return np.stack(rows, axis=0).astype(np.int32)[:, None, :]


def _bias_table_kernel(idx_ref, rb_ref, out_ref):
    idx = idx_ref[0]
    col = lax.broadcasted_iota(jnp.int32, (Q_TILE, K_WIN), 1)
    hit = [idx == b for b in range(N_BUCKETS)]
    for h in range(N_HEADS):
        first_row = jnp.full(idx.shape, NEG_INF, F32)
        for b in range(N_BUCKETS):
            first_row = jnp.where(hit[b], rb_ref[b, h] * LOG2E, first_row)
        acc = pltpu.roll(jnp.broadcast_to(first_row, (Q_TILE, K_WIN)), 0, 1, stride=1, stride_axis=0)
        out_ref[0, INTERIOR, h] = acc
        out_ref[0, FIRST, h] = jnp.where(col < K_WIN - HALF_STEPS, pltpu.roll(acc, K_WIN - HALF_STEPS, 1), NEG_INF)
        out_ref[0, LAST, h] = jnp.where(col >= HALF_STEPS, pltpu.roll(acc, HALF_STEPS, 1), NEG_INF)


def _bias_tables(rel_bias):
    idx = jnp.asarray(_bucket_index_rows())
    n_pat = idx.shape[0]
    return pl.pallas_call(
        _bias_table_kernel,
        grid=(n_pat,),
        in_specs=[
            pl.BlockSpec((1, 1, K_WIN), lambda g: (g, 0, 0)),
            pl.BlockSpec(memory_space=pltpu.SMEM),
        ],
        out_specs=pl.BlockSpec((1, 3, N_HEADS, Q_TILE, K_WIN), lambda g: (g, 0, 0, 0, 0)),
        out_shape=jax.ShapeDtypeStruct((n_pat, 3, N_HEADS, Q_TILE, K_WIN), F32),
        name="bias_table",
    )(idx, rel_bias)


def _stripe_store(ref, p, part):
    stripes = IN_TILE // STRIPE_ROWS
    for st in range(stripes):
        for j in range(D_RNN // LANES):
            ref[0, j, pl.ds(part * stripes + st, STRIPE_ROWS, stride=SUBLANES), :] = (
                p[st * STRIPE_ROWS:(st + 1) * STRIPE_ROWS, j * LANES:(j + 1) * LANES])


PERM_ROWS = MXU_DIM


def _class_permutations():
    mats = []
    for _, dil in PATTERNS:
        if dil == 1:
            continue
        per = PERM_ROWS // dil
        p = np.zeros((PERM_ROWS, PERM_ROWS), np.float32)
        for r in range(dil):
            for m in range(per):
                p[r * per + m, m * dil + r] = 1.0
        mats.append(p)
    return np.stack(mats)


def _in_proj_kernel(x_ref, g_ref, w_ref, perm_ref, *rest, tm, n_later):
    n_pat = len(PATTERNS)
    later_in, rest = rest[:n_later], rest[n_later:]
    xr_ref, gate_ref, rest = rest[0], rest[1], rest[2:]
    outs = [rest[a * n_pat:(a + 1) * n_pat] for a in range(3)]
    later_out = rest[3 * n_pat:3 * n_pat + n_later]
    h_scr = rest[3 * n_pat + n_later]
    for src, dst in zip(later_in, later_out):
        dst[...] = src[...].astype(BF16)
    h_scr[...] = _rms(x_ref[0], g_ref[...]).astype(BF16)

    def seg(lo, width):
        return jnp.dot(h_scr[...], w_ref[:, lo:lo + width], preferred_element_type=F32)

    for a, scale in enumerate((HEAD_DIM ** -0.5 * LOG2E, 1.0, 1.0)):
        p = seg(2 * D_RNN + a * D_ATTN, D_ATTN)
        if scale != 1.0:
            p = p * scale
        pb = p.astype(BF16)
        dilated = 0
        for (_, dil), o_ref in zip(PATTERNS, outs[a]):
            if dil == 1:
                o_ref[0, 0, 0] = pb
                continue
            per = PERM_ROWS // dil
            for blk in range(tm // PERM_ROWS):
                y = jnp.dot(perm_ref[dilated], pb[blk * PERM_ROWS:(blk + 1) * PERM_ROWS, :],
                            preferred_element_type=F32).astype(BF16)
                groups = o_ref.shape[1]
                for r in range(dil):
                    o_ref[0, r % groups, r // groups, blk * per:(blk + 1) * per, :] = y[r * per:(r + 1) * per, :]
            dilated += 1
    g = seg(D_RNN, D_RNN)
    gelu = g * (0.5 * (1.0 + jnp.tanh(math.sqrt(2.0 / math.pi) * (g + 0.044715 * (g * g * g)))))
    part = pl.program_id(1) % (RNN_CHUNK // IN_TILE)
    _stripe_store(gate_ref, gelu, part)
    _stripe_store(xr_ref, seg(0, D_RNN), part)


def _in_proj(x3, g, w_in_bf, later_weights, tm):
    assert tm == IN_TILE
    bsz, s, _ = x3.shape
    per_seq = s // tm
    n_steps = bsz * per_seq
    row_slice = lambda b, i: (b * per_seq + i, 0)
    later_specs = [pl.BlockSpec((w.shape[0] // n_steps, w.shape[1]), row_slice) for w in later_weights]
    later_shapes = [jax.ShapeDtypeStruct(w.shape, BF16) for w in later_weights]
    row = lambda b, i: (b, i, 0)
    cls = lambda b, i: (b, 0, 0, i, 0)
    chunk = lambda b, i: (b, 0, i // (RNN_CHUNK // IN_TILE), 0)
    const = lambda b, i: (0, 0)
    perms = jnp.asarray(_class_permutations(), BF16)
    qkv_specs, qkv_shapes = [], []
    for _ in range(3):
        for _, dil in PATTERNS:
            groups, per_group = dil // min(dil, CLASS_GROUP), min(dil, CLASS_GROUP)
            qkv_specs.append(pl.BlockSpec((1, groups, per_group, tm // dil, D_ATTN), cls))
            qkv_shapes.append(jax.ShapeDtypeStruct((bsz, groups, per_group, s // dil, D_ATTN), BF16))
    return pl.pallas_call(
        functools.partial(_in_proj_kernel, tm=tm, n_later=len(later_weights)),
        grid=(bsz, per_seq),
        in_specs=[
            pl.BlockSpec((1, tm, D_MODEL), row),
            pl.BlockSpec((1, D_MODEL), const),
            pl.BlockSpec((D_MODEL, D_IN), const, pipeline_mode=pl.Buffered(1)),
            pl.BlockSpec(perms.shape, lambda b, i: (0, 0, 0)),
        ] + later_specs,
        out_specs=[pl.BlockSpec((1, D_RNN // LANES, RNN_CHUNK, LANES), chunk)] * 2 + qkv_specs + later_specs,
        out_shape=[jax.ShapeDtypeStruct((bsz, D_RNN // LANES, s, LANES), F32)] * 2 + qkv_shapes + later_shapes,
        scratch_shapes=[pltpu.VMEM((tm, D_MODEL), BF16)],
        compiler_params=pltpu.CompilerParams(
            dimension_semantics=("arbitrary", "arbitrary"), vmem_limit_bytes=BIG_VMEM_LIMIT),
        name="in_proj",
    )(x3, g, w_in_bf, perms, *later_weights)


IN_TILE = 1024
RNN_CHUNK = 1024
STRIPE_ROWS = RNN_CHUNK // SUBLANES


def _scan_block(a, b, reverse):
    n = a.shape[0]
    row = lax.broadcasted_iota(jnp.int32, a.shape, 0)
    s = 1
    while s < n:
        if reverse:
            ra = pltpu.roll(a, n - s, 0)
            rb = pltpu.roll(b, n - s, 0)
            m = row < n - s
        else:
            ra = pltpu.roll(a, s, 0)
            rb = pltpu.roll(b, s, 0)
            m = row >= s
        b = jnp.where(m, a * rb + b, b)
        a = jnp.where(m, a * ra, a)
        s *= 2
    return a, b


def _rnn_kernel(*refs, reverse, final, nchunks):
    if final:
        (xc_ref, hf_ref, gate_ref, wa_ref, wx_ref, ba_ref, bx_ref, lam_ref, gn_ref,
         out_ref, a_scr, b_scr, h_scr, carry) = refs
    else:
        (pa_ref, pb_ref, cur_ref, nx_ref, cw_ref, cb_ref, wa_ref, wx_ref, ba_ref, bx_ref, lam_ref,
         out_ref, xc_out_ref, a_scr, b_scr, carry) = refs
        h_scr = None

    step = pl.program_id(1)
    chunk = (nchunks - 1 - step) if reverse else step
    n_slab = D_RNN // LANES
    nv = STRIPE_ROWS
    sub = lax.broadcasted_iota(jnp.int32, (SUBLANES, LANES), 0)
    top, bot = SUBLANES - 1, 0

    @pl.when(step == 0)
    def _():
        carry[...] = jnp.zeros_like(carry)

    if final:
        xc = [xc_ref[0, j] for j in range(n_slab)]
    else:
        xc = []
        for j in range(n_slab):
            lanes = slice(j * LANES, (j + 1) * LANES)
            x = cur_ref[0, j].reshape(nv, SUBLANES, LANES)
            pa = jnp.where(chunk > 0, pa_ref[0, j], 0.0)
            pb = jnp.where(chunk > 0, pb_ref[0, j], 0.0)
            nx = jnp.where(chunk < nchunks - 1, nx_ref[0, j], 0.0)
            xm2 = pltpu.roll(jnp.where(sub == top, pa, x[nv - 2]), 1, 0)
            xm1 = pltpu.roll(jnp.where(sub == top, pb, x[nv - 1]), 1, 0)
            xp1 = pltpu.roll(jnp.where(sub == bot, nx, x[0]), SUBLANES - 1, 0)
            xext = jnp.concatenate([xm2[None], xm1[None], x, xp1[None]], axis=0)
            acc = cb_ref[:, lanes] + xext[0:nv] * cw_ref[0:1, lanes]
            for k in range(1, CONV_WIDTH):
                acc = acc + xext[k:k + nv] * cw_ref[k:k + 1, lanes]
            xc.append(acc.reshape(RNN_CHUNK, LANES))
            xc_out_ref[0, j] = xc[j]

    nlam = -lam_ref[...]
    softplus = jnp.maximum(nlam, 0.0) + jnp.log1p(jnp.exp(-jnp.abs(nlam)))
    coef = -LRU_C * softplus
    slabs_per_blk = GATE_BLK // LANES
    for jj in range(D_RNN // GATE_BLK):
        blk = slice(jj * GATE_BLK, (jj + 1) * GATE_BLK)
        xj = jnp.concatenate(xc[jj * slabs_per_blk:(jj + 1) * slabs_per_blk], axis=-1)
        xjb = xj.astype(BF16)
        za = jnp.dot(xjb, wa_ref[jj], preferred_element_type=F32) + ba_ref[:, blk]
        zx = jnp.dot(xjb, wx_ref[jj], preferred_element_type=F32) + bx_ref[:, blk]
        r = 1.0 / (1.0 + jnp.exp(-za))
        i = 1.0 / (1.0 + jnp.exp(-zx))
        log_a = coef[:, blk] * r
        a = jnp.exp(log_a)
        u = -jnp.tanh(log_a) * (1.0 + a * a)
        root = jnp.where(u > 0.0, u * lax.rsqrt(u), 0.0)
        b = root * (i * xj)
        for t in range(slabs_per_blk):
            a_scr[jj * slabs_per_blk + t] = a[:, t * LANES:(t + 1) * LANES]
            b_scr[jj * slabs_per_blk + t] = b[:, t * LANES:(t + 1) * LANES]

    order = range(nv - 1, -1, -1) if reverse else range(nv)
    group = lambda v: slice(v * SUBLANES, (v + 1) * SUBLANES)

    h = [jnp.zeros((SUBLANES, LANES), F32)] * n_slab
    p = [jnp.ones((SUBLANES, LANES), F32)] * n_slab
    for v in order:
        for j in range(n_slab):
            av = a_scr[j, group(v), :]
            h[j] = av * h[j] + b_scr[j, group(v), :]
            p[j] = av * p[j]

    start = []
    for j in range(n_slab):
        pc, hc = _scan_block(p[j], h[j], reverse)
        prev = carry[j]
        if reverse:
            seed = jnp.broadcast_to(prev[bot:bot + 1, :], (SUBLANES, LANES))
            ends = hc + pc * seed
            start.append(jnp.where(sub == top, pltpu.roll(prev, SUBLANES - 1, 0),
                                   pltpu.roll(ends, SUBLANES - 1, 0)))
        else:
            seed = jnp.broadcast_to(prev[top:top + 1, :], (SUBLANES, LANES))
            ends = hc + pc * seed
            start.append(jnp.where(sub == bot, pltpu.roll(prev, 1, 0), pltpu.roll(ends, 1, 0)))
        carry[j] = ends

    dst = h_scr if final else None
    h = start
    for v in order:
        for j in range(n_slab):
            h[j] = a_scr[j, group(v), :] * h[j] + b_scr[j, group(v), :]
            if final:
                dst[j, group(v), :] = h[j]
            else:
                out_ref[0, j, group(v), :] = h[j]

    if final:
        ys = []
        for j in range(n_slab):
            ys.append((hf_ref[0, j] + h_scr[j]) * gate_ref[0, j])
        ss = ys[0] * ys[0]
        for j in range(1, n_slab):
            ss = ss + ys[j] * ys[j]
        inv = lax.rsqrt(jnp.sum(ss, axis=-1, keepdims=True) * (1.0 / D_RNN) + EPS)
        for j in range(n_slab):
            lanes = slice(j * LANES, (j + 1) * LANES)
            h_scr[j] = ys[j] * inv * gn_ref[:, lanes]
        for st in range(SUBLANES):
            for j in range(n_slab):
                out_ref[0, st * nv:(st + 1) * nv, j * LANES:(j + 1) * LANES] = (
                    h_scr[j, pl.ds(st, nv, stride=SUBLANES), :].astype(BF16))


def _rnn_pass(x4, extra, params, reverse, final):
    bsz, n_slab, s, _ = x4.shape
    nchunks = s // RNN_CHUNK
    hb = RNN_CHUNK // SUBLANES
    nhalo = s // SUBLANES

    def cidx(c):
        return (nchunks - 1 - c) if reverse else c

    cur_map = lambda b, c: (b, 0, cidx(c), 0)
    const2 = lambda b, c: (0, 0)
    const3 = lambda b, c: (0, 0, 0)
    cw, cb, wa, wx, ba, bx, lam, gn = params
    halo = (1, n_slab, SUBLANES, LANES)
    full = (1, n_slab, RNN_CHUNK, LANES)
    gate_specs = [
        pl.BlockSpec((D_RNN // GATE_BLK, GATE_BLK, GATE_BLK), const3),
        pl.BlockSpec((D_RNN // GATE_BLK, GATE_BLK, GATE_BLK), const3),
        pl.BlockSpec((1, D_RNN), const2),
        pl.BlockSpec((1, D_RNN), const2),
        pl.BlockSpec((1, D_RNN), const2),
    ]
    slab_scr = pltpu.VMEM((n_slab, RNN_CHUNK, LANES), F32)
    carry_scr = pltpu.VMEM((n_slab, SUBLANES, LANES), F32)
    if final:
        in_specs = [pl.BlockSpec(full, cur_map)] * 3 + gate_specs + [pl.BlockSpec((1, D_RNN), const2)]
        args = [x4, *extra, wa, wx, ba, bx, lam, gn]
        scratch = [slab_scr, slab_scr, slab_scr, carry_scr]
        out_specs = pl.BlockSpec((1, RNN_CHUNK, D_RNN), lambda b, c: (b, cidx(c), 0))
        out_shape = jax.ShapeDtypeStruct((bsz, s, D_RNN), BF16)
    else:
        pa_map = lambda b, c: (b, 0, jnp.maximum(cidx(c) * hb - 2, 0), 0)
        pb_map = lambda b, c: (b, 0, jnp.maximum(cidx(c) * hb - 1, 0), 0)
        nx_map = lambda b, c: (b, 0, jnp.minimum((cidx(c) + 1) * hb, nhalo - 1), 0)
        in_specs = [pl.BlockSpec(halo, pa_map), pl.BlockSpec(halo, pb_map), pl.BlockSpec(full, cur_map),
                    pl.BlockSpec(halo, nx_map),
                    pl.BlockSpec((CONV_WIDTH, D_RNN), const2), pl.BlockSpec((1, D_RNN), const2)] + gate_specs
        args = [x4, x4, x4, x4, cw, cb, wa, wx, ba, bx, lam]
        scratch = [slab_scr, slab_scr, carry_scr]
        out_specs = [pl.BlockSpec(full, cur_map)] * 2
        out_shape = [jax.ShapeDtypeStruct(x4.shape, F32)] * 2
    return pl.pallas_call(
        functools.partial(_rnn_kernel, reverse=reverse, final=final, nchunks=nchunks),
        grid=(bsz, nchunks),
        in_specs=in_specs,
        out_specs=out_specs,
        out_shape=out_shape,
        scratch_shapes=scratch,
        compiler_params=pltpu.CompilerParams(
            dimension_semantics=("arbitrary", "arbitrary"), vmem_limit_bytes=VMEM_LIMIT),
        name="rnn_bwd" if reverse else "rnn_fwd",
    )(*args)


def _gate_weights(w):
    per = GATE_BLK // RNN_BLOCK
    w5 = w.reshape(D_RNN // GATE_BLK, per, RNN_BLOCK, 1, RNN_BLOCK)
    on_diag = jnp.asarray(np.eye(per, dtype=bool)).reshape(1, per, 1, per, 1)
    dense = jnp.where(on_diag, w5, 0.0)
    return dense.reshape(D_RNN // GATE_BLK, GATE_BLK, GATE_BLK).astype(BF16)


def _attn_kernel(q_ref, kp_ref, k_ref, kn_ref, vp_ref, v_ref, vn_ref, bias_ref, o_ref, lse_ref, kbuf, vbuf,
                 *, sub_len, tiles_per_class):
    i = pl.program_id(2)
    n_tiles = sub_len // Q_TILE
    n_classes = q_ref.shape[1]
    q_rows = tiles_per_class * Q_TILE
    for buf, before, own, after in ((kbuf, kp_ref, k_ref, kn_ref), (vbuf, vp_ref, v_ref, vn_ref)):
        for cl in range(n_classes):
            buf[cl, 0:HALF_STEPS, :] = before[0, cl]
            buf[cl, HALF_STEPS:HALF_STEPS + q_rows, :] = own[0, cl]
            buf[cl, HALF_STEPS + q_rows:, :] = after[0, cl]
    buf_origin = i * q_rows - HALF_STEPS
    lane = lax.broadcasted_iota(jnp.int32, (Q_TILE, LANES), 1)
    heads_per_slab = LANES // HEAD_DIM
    first_half = lane < HEAD_DIM
    for cl in range(n_classes):
        for t in range(tiles_per_class):
            rows = slice(t * Q_TILE, (t + 1) * Q_TILE)
            if n_classes == 1:
                out_rows = rows
            else:
                out_rows = pl.ds(t * Q_TILE * n_classes + cl, Q_TILE, stride=n_classes)
            q = q_ref[0, cl, rows, :]
            tile = i * tiles_per_class + t
            variant = jnp.where(tile == 0, FIRST, jnp.where(tile == n_tiles - 1, LAST, INTERIOR))
            start = jnp.clip(tile * Q_TILE - HALF_STEPS, 0, sub_len - K_WIN) - buf_origin
            start = pl.multiple_of(start, K_TILE)
            kk = kbuf[cl, pl.ds(start, K_WIN), :]
            vv = vbuf[cl, pl.ds(start, K_WIN), :]
            for p in range(D_ATTN // LANES):
                sl = slice(p * LANES, (p + 1) * LANES)
                qp, kp, vp = q[:, sl], kk[:, sl], vv[:, sl]
                zero = jnp.zeros_like(qp)
                q2 = jnp.concatenate([jnp.where(first_half, qp, zero), jnp.where(first_half, zero, qp)], axis=0)
                s = lax.dot_general(q2, kp, (((1,), (1,)), ((), ())), preferred_element_type=F32)
                h0 = p * heads_per_slab
                s = s + bias_ref[variant, h0:h0 + heads_per_slab].reshape(heads_per_slab * Q_TILE, K_WIN)
                m = jnp.max(s, axis=-1, keepdims=True)
                e = jnp.exp2(s - m)
                l = jnp.sum(e, axis=-1, keepdims=True)
                o2 = jnp.dot(e.astype(BF16), vp, preferred_element_type=F32) * (1.0 / l)
                lse2 = jnp.broadcast_to(m + jnp.log(l) * LOG2E, (heads_per_slab * Q_TILE, LANES))
                o_ref[0, p, out_rows, :] = jnp.where(first_half, o2[:Q_TILE], o2[Q_TILE:])
                lse_ref[0, p, out_rows, :] = jnp.where(first_half, lse2[:Q_TILE], lse2[Q_TILE:])


def _attn_pattern(qc, kc, vc, bias, pat):
    bsz, groups, n_classes, sub_len, _ = qc.shape
    tiles_per_class = TILES_PER_STEP // n_classes
    q_rows = tiles_per_class * Q_TILE
    n_slab = D_ATTN // LANES
    qmap = lambda b, g, i: (b, g, 0, i, 0)
    halo_per_step = q_rows // HALF_STEPS
    n_halo = sub_len // HALF_STEPS
    before = lambda b, g, i: (b, g, 0, jnp.maximum(i * halo_per_step - 1, 0), 0)
    after = lambda b, g, i: (b, g, 0, jnp.minimum((i + 1) * halo_per_step, n_halo - 1), 0)
    own_spec = pl.BlockSpec((1, None, n_classes, q_rows, D_ATTN), qmap)
    kv_specs = [pl.BlockSpec((1, None, n_classes, HALF_STEPS, D_ATTN), before), own_spec,
                pl.BlockSpec((1, None, n_classes, HALF_STEPS, D_ATTN), after)]
    in_specs = [own_spec] + kv_specs + kv_specs
    in_specs.append(pl.BlockSpec((None, 3, N_HEADS, Q_TILE, K_WIN), lambda b, g, i: (pat, 0, 0, 0, 0)))
    window_buf = pltpu.VMEM((n_classes, q_rows + 2 * HALF_STEPS, D_ATTN), BF16)
    out_spec = pl.BlockSpec((1, None, n_slab, n_classes * q_rows, LANES), qmap)
    return pl.pallas_call(
        functools.partial(_attn_kernel, sub_len=sub_len, tiles_per_class=tiles_per_class),
        grid=(bsz, groups, sub_len // q_rows),
        in_specs=in_specs,
        out_specs=[out_spec] * 2,
        out_shape=[jax.ShapeDtypeStruct((bsz, groups, n_slab, n_classes * sub_len, LANES), F32)] * 2,
        scratch_shapes=[window_buf, window_buf],
        compiler_params=pltpu.CompilerParams(
            dimension_semantics=("arbitrary",) * 3, vmem_limit_bytes=VMEM_LIMIT),
        name=f"attn_d{groups * n_classes}",
    )(qc, kc, kc, kc, vc, vc, vc, bias)


FF_CHUNK = 1024


SUB_ROWS = 512


def _token_order(ref, scr, u, slab):
    groups = ref.shape[1]
    per = SUB_ROWS // groups
    if groups == 1:
        return ref[0, 0, slab, u * per:(u + 1) * per, :]
    for g in range(groups):
        scr[u, slab, pl.ds(g, per, stride=groups), :] = ref[0, g, slab, u * per:(u + 1) * per, :]
    return scr[u, slab]


def _out_mlp_kernel(x_ref, mr_ref, o1, o2, o3, l1, l2, l3, ga_ref, wo_ref, gm_ref, wu_ref, wd_ref,
                    gf_ref, out_ref, s_o3, s_l3, *, tm):
    n_sub = tm // SUB_ROWS

    def merge(u):
        ys = []
        for slab in range(D_ATTN // LANES):
            a1 = _token_order(l1, None, u, slab)
            a2 = _token_order(l2, None, u, slab)
            a3 = _token_order(l3, s_l3, u, slab)
            mx = jnp.maximum(jnp.maximum(a1, a2), a3)
            e1, e2, e3 = jnp.exp2(a1 - mx), jnp.exp2(a2 - mx), jnp.exp2(a3 - mx)
            ys.append((e1 * _token_order(o1, None, u, slab) + e2 * _token_order(o2, None, u, slab)
                       + e3 * _token_order(o3, s_o3, u, slab)) / (e1 + e2 + e3))
        return _rms(jnp.concatenate(ys, axis=-1), ga_ref[...]).astype(BF16)

    def out_proj(u, mix_attn):
        rows = slice(u * SUB_ROWS, (u + 1) * SUB_ROWS)
        proj = jnp.dot(mr_ref[0, rows, :], wo_ref[0:D_RNN, :], preferred_element_type=F32)
        proj = proj + jnp.dot(mix_attn, wo_ref[D_RNN:D_RNN + D_ATTN, :], preferred_element_type=F32)
        x1 = proj + x_ref[0, rows, :]
        return x1, _rms(x1, gm_ref[...]).astype(BF16)

    def ff_chunk(h, c):
        z = jnp.dot(h, wu_ref[:, c * FF_CHUNK:(c + 1) * FF_CHUNK], preferred_element_type=F32)
        z = jnp.square(jnp.maximum(z, 0.0)).astype(BF16)
        return jnp.dot(z, wd_ref[c * FF_CHUNK:(c + 1) * FF_CHUNK, :], preferred_element_type=F32)

    def finish(u, ff, x1):
        rows = slice(u * SUB_ROWS, (u + 1) * SUB_ROWS)
        out_ref[0, rows, :] = _rms(ff + x1, gf_ref[...])

    n_chunk = D_FF // FF_CHUNK
    x1, h = out_proj(0, merge(0))
    pending = None
    for u in range(n_sub):
        ff = None
        nxt = None
        for c in range(n_chunk):
            d = ff_chunk(h, c)
            ff = d if ff is None else ff + d
            if c == 0 and pending is not None:
                finish(*pending)
            if c == 1 and u + 1 < n_sub:
                mix_next = merge(u + 1)
            if c == 2 and u + 1 < n_sub:
                nxt = out_proj(u + 1, mix_next)
        pending = (u, ff, x1)
        if nxt is not None:
            x1, h = nxt
    finish(*pending)


def _out_mlp(x3, mix_rnn, os_, lses, ga, wo, gm, wu, wd, gf, tm):
    bsz, s, _ = x3.shape
    row = lambda b, i: (b, i, 0)
    const = lambda b, i: (0, 0)
    once = pl.Buffered(1)
    n_slab = D_ATTN // LANES
    attn_specs = [pl.BlockSpec((1, o.shape[1], n_slab, tm // o.shape[1], LANES), lambda b, i: (b, 0, 0, i, 0))
                  for o in os_]
    in_specs = [pl.BlockSpec((1, tm, D_MODEL), row), pl.BlockSpec((1, tm, D_RNN), row)]
    in_specs += attn_specs + attn_specs
    in_specs += [
        pl.BlockSpec((1, D_ATTN), const),
        pl.BlockSpec((D_RNN + D_ATTN, D_MODEL), const, pipeline_mode=once),
        pl.BlockSpec((1, D_MODEL), const),
        pl.BlockSpec((D_MODEL, D_FF), const, pipeline_mode=once),
        pl.BlockSpec((D_FF, D_MODEL), const, pipeline_mode=once),
        pl.BlockSpec((1, D_MODEL), const),
    ]
    return pl.pallas_call(
        functools.partial(_out_mlp_kernel, tm=tm),
        grid=(bsz, s // tm),
        in_specs=in_specs,
        out_specs=pl.BlockSpec((1, tm, D_MODEL), row),
        out_shape=jax.ShapeDtypeStruct((bsz, s, D_MODEL), F32),
        scratch_shapes=[pltpu.VMEM((tm // SUB_ROWS, n_slab, SUB_ROWS, LANES), F32)] * 2,
        compiler_params=pltpu.CompilerParams(
            dimension_semantics=("arbitrary", "arbitrary"), vmem_limit_bytes=BIG_VMEM_LIMIT),
        name="out_mlp",
    )(x3, mix_rnn, *os_, *lses, ga, wo, gm, wu, wd, gf)


def kernel(x, attn_norm_g, w_in, conv_w, conv_b, lru_wa_fwd, lru_ba_fwd, lru_wx_fwd, lru_bx_fwd, lru_lam_fwd, lru_wa_bwd, lru_ba_bwd, lru_wx_bwd, lru_bx_bwd, lru_lam_bwd, rel_bias, norm_rnn_g, norm_attn_g, w_out, mlp_norm_g, w_up, w_down, final_norm_g):
    depth = w_in.shape[0]
    assert depth == 1, "the final RMSNorm is fused into the single layer's last call"
    l = 0
    n_pat = len(PATTERNS)
    row = lambda v: v.reshape(1, -1)
    bias = _bias_tables(rel_bias)
    proj = _in_proj(x, row(attn_norm_g[l]), w_in[l].astype(BF16), (w_out[l], w_up[l], w_down[l]), tm=IN_TILE)
    xr, gate = proj[0], proj[1]
    qs, ks, vs = (proj[2 + a * n_pat:2 + (a + 1) * n_pat] for a in range(3))
    w_out_bf, w_up_bf, w_down_bf = proj[2 + 3 * n_pat:]
    fwd = (conv_w[l], row(conv_b[l]), _gate_weights(lru_wa_fwd[l]), _gate_weights(lru_wx_fwd[l]),
           row(lru_ba_fwd[l]), row(lru_bx_fwd[l]), row(lru_lam_fwd[l]), None)
    bwd = (conv_w[l], row(conv_b[l]), _gate_weights(lru_wa_bwd[l]), _gate_weights(lru_wx_bwd[l]),
           row(lru_ba_bwd[l]), row(lru_bx_bwd[l]), row(lru_lam_bwd[l]), row(norm_rnn_g[l]))
    h_f, xc = _rnn_pass(xr, (), fwd, reverse=False, final=False)
    mix_rnn = _rnn_pass(xc, (h_f, gate), bwd, reverse=True, final=True)
    os_, lses = [], []
    for pat in range(n_pat):
        o, lse = _attn_pattern(qs[pat], ks[pat], vs[pat], bias, pat)
        os_.append(o)
        lses.append(lse)
    return _out_mlp(x, mix_rnn, os_, lses, row(norm_attn_g[l]), w_out_bf,
                    row(mlp_norm_g[l]), w_up_bf, w_down_bf, row(final_norm_g), tm=512)
```

```python
import functools
import math

import numpy as np
import jax
import jax.numpy as jnp
from jax import lax
from jax.experimental import pallas as pl
from jax.experimental.pallas import tpu as pltpu

F32 = jnp.float32
BF16 = jnp.bfloat16

D_MODEL = 1024
D_RNN = 512
N_RNN_BLOCKS = 8
RNN_BLOCK = D_RNN // N_RNN_BLOCKS
CONV_WIDTH = 4
CONV_LEFT = 2
LRU_C = 8.0
N_HEADS = 8
HEAD_DIM = 64
D_ATTN = N_HEADS * HEAD_DIM
PATTERNS = ((128, 1), (512, 4), (2048, 16))
HALF_STEPS = 64
N_BUCKETS = 32
MAX_DISTANCE = 1024
D_IN = 2 * D_RNN + 3 * D_ATTN
D_FF = 4 * D_MODEL
EPS = 1e-6
NEG_INF = -1e30

LANES = 128
SUBLANES = 8
MXU_DIM = 256

Q_TILE = 128
K_TILE = 64
K_WIN = Q_TILE + 2 * HALF_STEPS
TILES_PER_STEP = 16
CLASS_GROUP = 4
LOG2E = math.log2(math.e)
INTERIOR, FIRST, LAST = 0, 1, 2
GATE_BLK = MXU_DIM

VMEM_LIMIT = 48 * 1024 * 1024
BIG_VMEM_LIMIT = 56 * 1024 * 1024


def _rms(x, g):
    ms = jnp.mean(x * x, axis=-1, keepdims=True)
    return x * lax.rsqrt(ms + EPS) * g


def _t5_bucket_np(rel):
    nb = N_BUCKETS // 2
    max_exact = nb // 2
    ret = np.where(rel > 0, nb, 0)
    n = np.abs(rel)
    nf = np.maximum(n, 1).astype(np.float32)
    large = max_exact + (np.log(nf / np.float32(max_exact)) / np.float32(math.log(MAX_DISTANCE / max_exact))
                         * np.float32(nb - max_exact)).astype(np.int32)
    large = np.minimum(large, nb - 1)
    return ret + np.where(n < max_exact, n, large)


def _bucket_index_rows():
    step = np.arange(K_WIN) - HALF_STEPS
    rows = []
    for _, dil in PATTERNS:
        b = _t5_bucket_np((step * dil).astype(np.int32))
        rows.append(np.where(np.abs(step) <= HALF_STEPS, b, -1))
    return np.stack(rows, axis=0).astype(np.int32)[:, None, :]


def _bias_table_kernel(idx_ref, rb_ref, out_ref):
    idx = idx_ref[0]
    col = lax.broadcasted_iota(jnp.int32, (Q_TILE, K_WIN), 1)
    hit = [idx == b for b in range(N_BUCKETS)]
    for h in range(N_HEADS):
        first_row = jnp.full(idx.shape, NEG_INF, F32)
        for b in range(N_BUCKETS):
            first_row = jnp.where(hit[b], rb_ref[b, h] * LOG2E, first_row)
        acc = pltpu.roll(jnp.broadcast_to(first_row, (Q_TILE, K_WIN)), 0, 1, stride=1, stride_axis=0)
        out_ref[0, INTERIOR, h] = acc
        out_ref[0, FIRST, h] = jnp.where(col < K_WIN - HALF_STEPS, pltpu.roll(acc, K_WIN - HALF_STEPS, 1), NEG_INF)
        out_ref[0, LAST, h] = jnp.where(col >= HALF_STEPS, pltpu.roll(acc, HALF_STEPS, 1), NEG_INF)


def _bias_tables(rel_bias):
    idx = jnp.asarray(_bucket_index_rows())
    n_pat = idx.shape[0]
    return pl.pallas_call(
        _bias_table_kernel,
        grid=(n_pat,),
        in_specs=[
            pl.BlockSpec((1, 1, K_WIN), lambda g: (g, 0, 0)),
            pl.BlockSpec(memory_space=pltpu.SMEM),
        ],
        out_specs=pl.BlockSpec((1, 3, N_HEADS, Q_TILE, K_WIN), lambda g: (g, 0, 0, 0, 0)),
        out_shape=jax.ShapeDtypeStruct((n_pat, 3, N_HEADS, Q_TILE, K_WIN), F32),
        name="bias_table",
    )(idx, rel_bias)


def _stripe_store(ref, p, part):
    stripes = IN_TILE // STRIPE_ROWS
    for st in range(stripes):
        for j in range(D_RNN // LANES):
            ref[0, j, pl.ds(part * stripes + st, STRIPE_ROWS, stride=SUBLANES), :] = (
                p[st * STRIPE_ROWS:(st + 1) * STRIPE_ROWS, j * LANES:(j + 1) * LANES])


PERM_ROWS = MXU_DIM


def _class_permutations():
    mats = []
    for _, dil in PATTERNS:
        if dil == 1:
            continue
        per = PERM_ROWS // dil
        p = np.zeros((PERM_ROWS, PERM_ROWS), np.float32)
        for r in range(dil):
            for m in range(per):
                p[r * per + m, m * dil + r] = 1.0
        mats.append(p)
    return np.stack(mats)


def _in_proj_kernel(x_ref, g_ref, w_ref, perm_ref, *rest, tm, n_later):
    n_pat = len(PATTERNS)
    later_in, rest = rest[:n_later], rest[n_later:]
    xr_ref, gate_ref, rest = rest[0], rest[1], rest[2:]
    outs = [rest[a * n_pat:(a + 1) * n_pat] for a in range(3)]
    later_out = rest[3 * n_pat:3 * n_pat + n_later]
    h_scr = rest[3 * n_pat + n_later]
    for src, dst in zip(later_in, later_out):
        dst[...] = src[...].astype(BF16)
    h_scr[...] = _rms(x_ref[0], g_ref[...]).astype(BF16)

    def seg(lo, width):
        return jnp.dot(h_scr[...], w_ref[:, lo:lo + width], preferred_element_type=F32)

    for a, scale in enumerate((HEAD_DIM ** -0.5 * LOG2E, 1.0, 1.0)):
        p = seg(2 * D_RNN + a * D_ATTN, D_ATTN)
        if scale != 1.0:
            p = p * scale
        pb = p.astype(BF16)
        dilated = 0
        for (_, dil), o_ref in zip(PATTERNS, outs[a]):
            if dil == 1:
                o_ref[0, 0, 0] = pb
                continue
            per = PERM_ROWS // dil
            for blk in range(tm // PERM_ROWS):
                y = jnp.dot(perm_ref[dilated], pb[blk * PERM_ROWS:(blk + 1) * PERM_ROWS, :],
                            preferred_element_type=F32).astype(BF16)
                groups = o_ref.shape[1]
                for r in range(dil):
                    o_ref[0, r % groups, r // groups, blk * per:(blk + 1) * per, :] = y[r * per:(r + 1) * per, :]
            dilated += 1
    g = seg(D_RNN, D_RNN)
    gelu = g * (0.5 * (1.0 + jnp.tanh(math.sqrt(2.0 / math.pi) * (g + 0.044715 * (g * g * g)))))
    part = pl.program_id(1) % (RNN_CHUNK // IN_TILE)
    _stripe_store(gate_ref, gelu, part)
    _stripe_store(xr_ref, seg(0, D_RNN), part)


def _in_proj(x3, g, w_in_bf, later_weights, tm):
    assert tm == IN_TILE
    bsz, s, _ = x3.shape
    per_seq = s // tm
    n_steps = bsz * per_seq
    row_slice = lambda b, i: (b * per_seq + i, 0)
    later_specs = [pl.BlockSpec((w.shape[0] // n_steps, w.shape[1]), row_slice) for w in later_weights]
    later_shapes = [jax.ShapeDtypeStruct(w.shape, BF16) for w in later_weights]
    row = lambda b, i: (b, i, 0)
    cls = lambda b, i: (b, 0, 0, i, 0)
    chunk = lambda b, i: (b, 0, i // (RNN_CHUNK // IN_TILE), 0)
    const = lambda b, i: (0, 0)
    perms = jnp.asarray(_class_permutations(), BF16)
    qkv_specs, qkv_shapes = [], []
    for _ in range(3):
        for _, dil in PATTERNS:
            groups, per_group = dil // min(dil, CLASS_GROUP), min(dil, CLASS_GROUP)
            qkv_specs.append(pl.BlockSpec((1, groups, per_group, tm // dil, D_ATTN), cls))
            qkv_shapes.append(jax.ShapeDtypeStruct((bsz, groups, per_group, s // dil, D_ATTN), BF16))
    return pl.pallas_call(
        functools.partial(_in_proj_kernel, tm=tm, n_later=len(later_weights)),
        grid=(bsz, per_seq),
        in_specs=[
            pl.BlockSpec((1, tm, D_MODEL), row),
            pl.BlockSpec((1, D_MODEL), const),
            pl.BlockSpec((D_MODEL, D_IN), const, pipeline_mode=pl.Buffered(1)),
            pl.BlockSpec(perms.shape, lambda b, i: (0, 0, 0)),
        ] + later_specs,
        out_specs=[pl.BlockSpec((1, D_RNN // LANES, RNN_CHUNK, LANES), chunk)] * 2 + qkv_specs + later_specs,
        out_shape=[jax.ShapeDtypeStruct((bsz, D_RNN // LANES, s, LANES), F32)] * 2 + qkv_shapes + later_shapes,
        scratch_shapes=[pltpu.VMEM((tm, D_MODEL), BF16)],
        compiler_params=pltpu.CompilerParams(
            dimension_semantics=("arbitrary", "arbitrary"), vmem_limit_bytes=BIG_VMEM_LIMIT),
        name="in_proj",
    )(x3, g, w_in_bf, perms, *later_weights)


IN_TILE = 1024
RNN_CHUNK = 1024
STRIPE_ROWS = RNN_CHUNK // SUBLANES


def _scan_block(a, b, reverse):
    n = a.shape[0]
    row = lax.broadcasted_iota(jnp.int32, a.shape, 0)
    s = 1
    while s < n:
        if reverse:
            ra = pltpu.roll(a, n - s, 0)
            rb = pltpu.roll(b, n - s, 0)
            m = row < n - s
        else:
            ra = pltpu.roll(a, s, 0)
            rb = pltpu.roll(b, s, 0)
            m = row >= s
        b = jnp.where(m, a * rb + b, b)
        a = jnp.where(m, a * ra, a)
        s *= 2
    return a, b


def _rnn_kernel(*refs, reverse, final, nchunks):
    if final:
        (xc_ref, hf_ref, gate_ref, wa_ref, wx_ref, ba_ref, bx_ref, lam_ref, gn_ref,
         out_ref, a_scr, b_scr, h_scr, carry) = refs
    else:
        (pa_ref, pb_ref, cur_ref, nx_ref, cw_ref, cb_ref, wa_ref, wx_ref, ba_ref, bx_ref, lam_ref,
         out_ref, xc_out_ref, a_scr, b_scr, carry) = refs
        h_scr = None

    step = pl.program_id(1)
    chunk = (nchunks - 1 - step) if reverse else step
    n_slab = D_RNN // LANES
    nv = STRIPE_ROWS
    sub = lax.broadcasted_iota(jnp.int32, (SUBLANES, LANES), 0)
    top, bot = SUBLANES - 1, 0

    @pl.when(step == 0)
    def _():
        carry[...] = jnp.zeros_like(carry)

    if final:
        xc = [xc_ref[0, j] for j in range(n_slab)]
    else:
        xc = []
        for j in range(n_slab):
            lanes = slice(j * LANES, (j + 1) * LANES)
            x = cur_ref[0, j].reshape(nv, SUBLANES, LANES)
            pa = jnp.where(chunk > 0, pa_ref[0, j], 0.0)
            pb = jnp.where(chunk > 0, pb_ref[0, j], 0.0)
            nx = jnp.where(chunk < nchunks - 1, nx_ref[0, j], 0.0)
            xm2 = pltpu.roll(jnp.where(sub == top, pa, x[nv - 2]), 1, 0)
            xm1 = pltpu.roll(jnp.where(sub == top, pb, x[nv - 1]), 1, 0)
            xp1 = pltpu.roll(jnp.where(sub == bot, nx, x[0]), SUBLANES - 1, 0)
            xext = jnp.concatenate([xm2[None], xm1[None], x, xp1[None]], axis=0)
            acc = cb_ref[:, lanes] + xext[0:nv] * cw_ref[0:1, lanes]
            for k in range(1, CONV_WIDTH):
                acc = acc + xext[k:k + nv] * cw_ref[k:k + 1, lanes]
            xc.append(acc.reshape(RNN_CHUNK, LANES))
            xc_out_ref[0, j] = xc[j]

    nlam = -lam_ref[...]
    softplus = jnp.maximum(nlam, 0.0) + jnp.log1p(jnp.exp(-jnp.abs(nlam)))
    coef = -LRU_C * softplus
    slabs_per_blk = GATE_BLK // LANES
    for jj in range(D_RNN // GATE_BLK):
        blk = slice(jj * GATE_BLK, (jj + 1) * GATE_BLK)
        xj = jnp.concatenate(xc[jj * slabs_per_blk:(jj + 1) * slabs_per_blk], axis=-1)
        xjb = xj.astype(BF16)
        za = jnp.dot(xjb, wa_ref[jj], preferred_element_type=F32) + ba_ref[:, blk]
        zx = jnp.dot(xjb, wx_ref[jj], preferred_element_type=F32) + bx_ref[:, blk]
        r = 1.0 / (1.0 + jnp.exp(-za))
        i = 1.0 / (1.0 + jnp.exp(-zx))
        log_a = coef[:, blk] * r
        a = jnp.exp(log_a)
        u = -jnp.tanh(log_a) * (1.0 + a * a)
        root = jnp.where(u > 0.0, u * lax.rsqrt(u), 0.0)
        b = root * (i * xj)
        for t in range(slabs_per_blk):
            a_scr[jj * slabs_per_blk + t] = a[:, t * LANES:(t + 1) * LANES]
            b_scr[jj * slabs_per_blk + t] = b[:, t * LANES:(t + 1) * LANES]

    order = range(nv - 1, -1, -1) if reverse else range(nv)
    group = lambda v: slice(v * SUBLANES, (v + 1) * SUBLANES)

    h = [jnp.zeros((SUBLANES, LANES), F32)] * n_slab
    p = [jnp.ones((SUBLANES, LANES), F32)] * n_slab
    for v in order:
        for j in range(n_slab):
            av = a_scr[j, group(v), :]
            h[j] = av * h[j] + b_scr[j, group(v), :]
            p[j] = av * p[j]

    start = []
    for j in range(n_slab):
        pc, hc = _scan_block(p[j], h[j], reverse)
        prev = carry[j]
        if reverse:
            seed = jnp.broadcast_to(prev[bot:bot + 1, :], (SUBLANES, LANES))
            ends = hc + pc * seed
            start.append(jnp.where(sub == top, pltpu.roll(prev, SUBLANES - 1, 0),
                                   pltpu.roll(ends, SUBLANES - 1, 0)))
        else:
            seed = jnp.broadcast_to(prev[top:top + 1, :], (SUBLANES, LANES))
            ends = hc + pc * seed
            start.append(jnp.where(sub == bot, pltpu.roll(prev, 1, 0), pltpu.roll(ends, 1, 0)))
        carry[j] = ends

    dst = h_scr if final else None
    h = start
    for v in order:
        for j in range(n_slab):
            h[j] = a_scr[j, group(v), :] * h[j] + b_scr[j, group(v), :]
            if final:
                dst[j, group(v), :] = h[j]
            else:
                out_ref[0, j, group(v), :] = h[j]

    if final:
        ys = []
        for j in range(n_slab):
            ys.append((hf_ref[0, j] + h_scr[j]) * gate_ref[0, j])
        ss = ys[0] * ys[0]
        for j in range(1, n_slab):
            ss = ss + ys[j] * ys[j]
        inv = lax.rsqrt(jnp.sum(ss, axis=-1, keepdims=True) * (1.0 / D_RNN) + EPS)
        for j in range(n_slab):
            lanes = slice(j * LANES, (j + 1) * LANES)
            h_scr[j] = ys[j] * inv * gn_ref[:, lanes]
        for st in range(SUBLANES):
            for j in range(n_slab):
                out_ref[0, st * nv:(st + 1) * nv, j * LANES:(j + 1) * LANES] = (
                    h_scr[j, pl.ds(st, nv, stride=SUBLANES), :].astype(BF16))


def _rnn_pass(x4, extra, params, reverse, final):
    bsz, n_slab, s, _ = x4.shape
    nchunks = s // RNN_CHUNK
    hb = RNN_CHUNK // SUBLANES
    nhalo = s // SUBLANES

    def cidx(c):
        return (nchunks - 1 - c) if reverse else c

    cur_map = lambda b, c: (b, 0, cidx(c), 0)
    const2 = lambda b, c: (0, 0)
    const3 = lambda b, c: (0, 0, 0)
    cw, cb, wa, wx, ba, bx, lam, gn = params
    halo = (1, n_slab, SUBLANES, LANES)
    full = (1, n_slab, RNN_CHUNK, LANES)
    gate_specs = [
        pl.BlockSpec((D_RNN // GATE_BLK, GATE_BLK, GATE_BLK), const3),
        pl.BlockSpec((D_RNN // GATE_BLK, GATE_BLK, GATE_BLK), const3),
        pl.BlockSpec((1, D_RNN), const2),
        pl.BlockSpec((1, D_RNN), const2),
        pl.BlockSpec((1, D_RNN), const2),
    ]
    slab_scr = pltpu.VMEM((n_slab, RNN_CHUNK, LANES), F32)
    carry_scr = pltpu.VMEM((n_slab, SUBLANES, LANES), F32)
    if final:
        in_specs = [pl.BlockSpec(full, cur_map)] * 3 + gate_specs + [pl.BlockSpec((1, D_RNN), const2)]
        args = [x4, *extra, wa, wx, ba, bx, lam, gn]
        scratch = [slab_scr, slab_scr, slab_scr, carry_scr]
        out_specs = pl.BlockSpec((1, RNN_CHUNK, D_RNN), lambda b, c: (b, cidx(c), 0))
        out_shape = jax.ShapeDtypeStruct((bsz, s, D_RNN), BF16)
    else:
        pa_map = lambda b, c: (b, 0, jnp.maximum(cidx(c) * hb - 2, 0), 0)
        pb_map = lambda b, c: (b, 0, jnp.maximum(cidx(c) * hb - 1, 0), 0)
        nx_map = lambda b, c: (b, 0, jnp.minimum((cidx(c) + 1) * hb, nhalo - 1), 0)
        in_specs = [pl.BlockSpec(halo, pa_map), pl.BlockSpec(halo, pb_map), pl.BlockSpec(full, cur_map),
                    pl.BlockSpec(halo, nx_map),
                    pl.BlockSpec((CONV_WIDTH, D_RNN), const2), pl.BlockSpec((1, D_RNN), const2)] + gate_specs
        args = [x4, x4, x4, x4, cw, cb, wa, wx, ba, bx, lam]
        scratch = [slab_scr, slab_scr, carry_scr]
        out_specs = [pl.BlockSpec(full, cur_map)] * 2
        out_shape = [jax.ShapeDtypeStruct(x4.shape, F32)] * 2
    return pl.pallas_call(
        functools.partial(_rnn_kernel, reverse=reverse, final=final, nchunks=nchunks),
        grid=(bsz, nchunks),
        in_specs=in_specs,
        out_specs=out_specs,
        out_shape=out_shape,
        scratch_shapes=scratch,
        compiler_params=pltpu.CompilerParams(
            dimension_semantics=("arbitrary", "arbitrary"), vmem_limit_bytes=VMEM_LIMIT),
        name="rnn_bwd" if reverse else "rnn_fwd",
    )(*args)


def _gate_weights(w):
    per = GATE_BLK // RNN_BLOCK
    w5 = w.reshape(D_RNN // GATE_BLK, per, RNN_BLOCK, 1, RNN_BLOCK)
    on_diag = jnp.asarray(np.eye(per, dtype=bool)).reshape(1, per, 1, per, 1)
    dense = jnp.where(on_diag, w5, 0.0)
    return dense.reshape(D_RNN // GATE_BLK, GATE_BLK, GATE_BLK).astype(BF16)


def _attn_kernel(q_ref, kp_ref, k_ref, kn_ref, vp_ref, v_ref, vn_ref, bias_ref, o_ref, lse_ref, kbuf, vbuf,
                 *, sub_len, tiles_per_class):
    i = pl.program_id(2)
    n_tiles = sub_len // Q_TILE
    n_classes = q_ref.shape[1]
    q_rows = tiles_per_class * Q_TILE
    for buf, before, own, after in ((kbuf, kp_ref, k_ref, kn_ref), (vbuf, vp_ref, v_ref, vn_ref)):
        for cl in range(n_classes):
            buf[cl, 0:HALF_STEPS, :] = before[0, cl]
            buf[cl, HALF_STEPS:HALF_STEPS + q_rows, :] = own[0, cl]
            buf[cl, HALF_STEPS + q_rows:, :] = after[0, cl]
    buf_origin = i * q_rows - HALF_STEPS
    lane = lax.broadcasted_iota(jnp.int32, (Q_TILE, LANES), 1)
    heads_per_slab = LANES // HEAD_DIM
    first_half = lane < HEAD_DIM
    for cl in range(n_classes):
        for t in range(tiles_per_class):
            rows = slice(t * Q_TILE, (t + 1) * Q_TILE)
            if n_classes == 1:
                out_rows = rows
            else:
                out_rows = pl.ds(t * Q_TILE * n_classes + cl, Q_TILE, stride=n_classes)
            q = q_ref[0, cl, rows, :]
            tile = i * tiles_per_class + t
            variant = jnp.where(tile == 0, FIRST, jnp.where(tile == n_tiles - 1, LAST, INTERIOR))
            start = jnp.clip(tile * Q_TILE - HALF_STEPS, 0, sub_len - K_WIN) - buf_origin
            start = pl.multiple_of(start, K_TILE)
            kk = kbuf[cl, pl.ds(start, K_WIN), :]
            vv = vbuf[cl, pl.ds(start, K_WIN), :]
            for p in range(D_ATTN // LANES):
                sl = slice(p * LANES, (p + 1) * LANES)
                qp, kp, vp = q[:, sl], kk[:, sl], vv[:, sl]
                zero = jnp.zeros_like(qp)
                q2 = jnp.concatenate([jnp.where(first_half, qp, zero), jnp.where(first_half, zero, qp)], axis=0)
                s = lax.dot_general(q2, kp, (((1,), (1,)), ((), ())), preferred_element_type=F32)
                h0 = p * heads_per_slab
                s = s + bias_ref[variant, h0:h0 + heads_per_slab].reshape(heads_per_slab * Q_TILE, K_WIN)
                m = jnp.max(s, axis=-1, keepdims=True)
                e = jnp.exp2(s - m)
                l = jnp.sum(e, axis=-1, keepdims=True)
                o2 = jnp.dot(e.astype(BF16), vp, preferred_element_type=F32)
                shape = (Q_TILE, LANES)
                pick = lambda a: jnp.where(first_half, jnp.broadcast_to(a[:Q_TILE], shape),
                                           jnp.broadcast_to(a[Q_TILE:], shape))
                l_slab = pick(l)
                o_ref[0, p, out_rows, :] = pick(o2) * (1.0 / l_slab)
                lse_ref[0, p, out_rows, :] = pick(m) + jnp.log(l_slab) * LOG2E


def _attn_pattern(qc, kc, vc, bias, pat):
    bsz, groups, n_classes, sub_len, _ = qc.shape
    tiles_per_class = TILES_PER_STEP // n_classes
    q_rows = tiles_per_class * Q_TILE
    n_slab = D_ATTN // LANES
    qmap = lambda b, g, i: (b, g, 0, i, 0)
    halo_per_step = q_rows // HALF_STEPS
    n_halo = sub_len // HALF_STEPS
    before = lambda b, g, i: (b, g, 0, jnp.maximum(i * halo_per_step - 1, 0), 0)
    after = lambda b, g, i: (b, g, 0, jnp.minimum((i + 1) * halo_per_step, n_halo - 1), 0)
    own_spec = pl.BlockSpec((1, None, n_classes, q_rows, D_ATTN), qmap)
    kv_specs = [pl.BlockSpec((1, None, n_classes, HALF_STEPS, D_ATTN), before), own_spec,
                pl.BlockSpec((1, None, n_classes, HALF_STEPS, D_ATTN), after)]
    in_specs = [own_spec] + kv_specs + kv_specs
    in_specs.append(pl.BlockSpec((None, 3, N_HEADS, Q_TILE, K_WIN), lambda b, g, i: (pat, 0, 0, 0, 0)))
    window_buf = pltpu.VMEM((n_classes, q_rows + 2 * HALF_STEPS, D_ATTN), BF16)
    out_spec = pl.BlockSpec((1, None, n_slab, n_classes * q_rows, LANES), qmap)
    return pl.pallas_call(
        functools.partial(_attn_kernel, sub_len=sub_len, tiles_per_class=tiles_per_class),
        grid=(bsz, groups, sub_len // q_rows),
        in_specs=in_specs,
        out_specs=[out_spec] * 2,
        out_shape=[jax.ShapeDtypeStruct((bsz, groups, n_slab, n_classes * sub_len, LANES), F32)] * 2,
        scratch_shapes=[window_buf, window_buf],
        compiler_params=pltpu.CompilerParams(
            dimension_semantics=("arbitrary",) * 3, vmem_limit_bytes=VMEM_LIMIT),
        name=f"attn_d{groups * n_classes}",
    )(qc, kc, kc, kc, vc, vc, vc, bias)


FF_CHUNK = 1024
OUT_TILE = 512


def _token_order(ref, scr, slab):
    groups = ref.shape[1]
    if groups == 1:
        return ref[0, 0, slab]
    for g in range(groups):
        scr[slab, pl.ds(g, OUT_TILE // groups, stride=groups), :] = ref[0, g, slab]
    return scr[slab]


def _out_mlp_kernel(x_ref, mr_ref, o1, o2, o3, l1, l2, l3, ga_ref, wo_ref, gm_ref, wu_ref, wd_ref,
                    gf_ref, out_ref, s_o3, s_l3):
    ys = []
    for slab in range(D_ATTN // LANES):
        a1 = _token_order(l1, None, slab)
        a2 = _token_order(l2, None, slab)
        a3 = _token_order(l3, s_l3, slab)
        mx = jnp.maximum(jnp.maximum(a1, a2), a3)
        e1, e2, e3 = jnp.exp2(a1 - mx), jnp.exp2(a2 - mx), jnp.exp2(a3 - mx)
        ys.append((e1 * _token_order(o1, None, slab) + e2 * _token_order(o2, None, slab)
                   + e3 * _token_order(o3, s_o3, slab)) / (e1 + e2 + e3))
    mix_attn = _rms(jnp.concatenate(ys, axis=-1), ga_ref[...]).astype(BF16)

    proj = jnp.dot(mr_ref[0], wo_ref[0:D_RNN, :], preferred_element_type=F32)
    proj = proj + jnp.dot(mix_attn, wo_ref[D_RNN:D_RNN + D_ATTN, :], preferred_element_type=F32)
    x1 = proj + x_ref[0]
    h = _rms(x1, gm_ref[...]).astype(BF16)
    ff = None
    for c in range(D_FF // FF_CHUNK):
        cols = slice(c * FF_CHUNK, (c + 1) * FF_CHUNK)
        z = jnp.dot(h, wu_ref[:, cols], preferred_element_type=F32)
        z = jnp.square(jnp.maximum(z, 0.0)).astype(BF16)
        d = jnp.dot(z, wd_ref[cols, :], preferred_element_type=F32)
        ff = d if ff is None else ff + d
    out_ref[0] = _rms(ff + x1, gf_ref[...])


def _out_mlp(x3, mix_rnn, os_, lses, ga, wo, gm, wu, wd, gf):
    bsz, s, _ = x3.shape
    tm = OUT_TILE
    row = lambda b, i: (b, i, 0)
    const = lambda b, i: (0, 0)
    once = pl.Buffered(1)
    n_slab = D_ATTN // LANES
    attn_specs = [pl.BlockSpec((1, o.shape[1], n_slab, tm // o.shape[1], LANES), lambda b, i: (b, 0, 0, i, 0))
                  for o in os_]
    in_specs = [pl.BlockSpec((1, tm, D_MODEL), row), pl.BlockSpec((1, tm, D_RNN), row)]
    in_specs += attn_specs + attn_specs
    in_specs += [
        pl.BlockSpec((1, D_ATTN), const),
        pl.BlockSpec((D_RNN + D_ATTN, D_MODEL), const, pipeline_mode=once),
        pl.BlockSpec((1, D_MODEL), const),
        pl.BlockSpec((D_MODEL, D_FF), const, pipeline_mode=once),
        pl.BlockSpec((D_FF, D_MODEL), const, pipeline_mode=once),
        pl.BlockSpec((1, D_MODEL), const),
    ]
    return pl.pallas_call(
        _out_mlp_kernel,
        grid=(bsz, s // tm),
        in_specs=in_specs,
        out_specs=pl.BlockSpec((1, tm, D_MODEL), row),
        out_shape=jax.ShapeDtypeStruct((bsz, s, D_MODEL), F32),
        scratch_shapes=[pltpu.VMEM((n_slab, tm, LANES), F32)] * 2,
        compiler_params=pltpu.CompilerParams(
            dimension_semantics=("arbitrary", "arbitrary"), vmem_limit_bytes=BIG_VMEM_LIMIT),
        name="out_mlp",
    )(x3, mix_rnn, *os_, *lses, ga, wo, gm, wu, wd, gf)


def kernel(x, attn_norm_g, w_in, conv_w, conv_b, lru_wa_fwd, lru_ba_fwd, lru_wx_fwd, lru_bx_fwd, lru_lam_fwd, lru_wa_bwd, lru_ba_bwd, lru_wx_bwd, lru_bx_bwd, lru_lam_bwd, rel_bias, norm_rnn_g, norm_attn_g, w_out, mlp_norm_g, w_up, w_down, final_norm_g):
    depth = w_in.shape[0]
    assert depth == 1, "the final RMSNorm is fused into the single layer's last call"
    l = 0
    n_pat = len(PATTERNS)
    row = lambda v: v.reshape(1, -1)
    bias = _bias_tables(rel_bias)
    proj = _in_proj(x, row(attn_norm_g[l]), w_in[l].astype(BF16), (w_out[l], w_up[l], w_down[l]), tm=IN_TILE)
    xr, gate = proj[0], proj[1]
    qs, ks, vs = (proj[2 + a * n_pat:2 + (a + 1) * n_pat] for a in range(3))
    w_out_bf, w_up_bf, w_down_bf = proj[2 + 3 * n_pat:]
    fwd = (conv_w[l], row(conv_b[l]), _gate_weights(lru_wa_fwd[l]), _gate_weights(lru_wx_fwd[l]),
           row(lru_ba_fwd[l]), row(lru_bx_fwd[l]), row(lru_lam_fwd[l]), None)
    bwd = (conv_w[l], row(conv_b[l]), _gate_weights(lru_wa_bwd[l]), _gate_weights(lru_wx_bwd[l]),
           row(lru_ba_bwd[l]), row(lru_bx_bwd[l]), row(lru_lam_bwd[l]), row(norm_rnn_g[l]))
    h_f, xc = _rnn_pass(xr, (), fwd, reverse=False, final=False)
    mix_rnn = _rnn_pass(xc, (h_f, gate), bwd, reverse=True, final=True)
    os_, lses = [], []
    for pat in range(n_pat):
        o, lse = _attn_pattern(qs[pat], ks[pat], vs[pat], bias, pat)
        os_.append(o)
        lses.append(lse)
    return _out_mlp(x, mix_rnn, os_, lses, row(norm_attn_g[l]), w_out_bf,
                    row(mlp_norm_g[l]), w_up_bf, w_down_bf, row(final_norm_g))
```

```python
import functools
import math

import numpy as np
import jax
import jax.numpy as jnp
from jax import lax
from jax.experimental import pallas as pl
from jax.experimental.pallas import tpu as pltpu

F32 = jnp.float32
BF16 = jnp.bfloat16

D_MODEL = 1024
D_RNN = 512
N_RNN_BLOCKS = 8
RNN_BLOCK = D_RNN // N_RNN_BLOCKS
CONV_WIDTH = 4
CONV_LEFT = 2
LRU_C = 8.0
N_HEADS = 8
HEAD_DIM = 64
D_ATTN = N_HEADS * HEAD_DIM
PATTERNS = ((128, 1), (512, 4), (2048, 16))
HALF_STEPS = 64
N_BUCKETS = 32
MAX_DISTANCE = 1024
D_IN = 2 * D_RNN + 3 * D_ATTN
D_FF = 4 * D_MODEL
EPS = 1e-6
NEG_INF = -1e30

LANES = 128
SUBLANES = 8
MXU_DIM = 256

Q_TILE = 128
K_TILE = 64
K_WIN = Q_TILE + 2 * HALF_STEPS
TILES_PER_STEP = 16
CLASS_GROUP = 4
LOG2E = math.log2(math.e)
INTERIOR, FIRST, LAST = 0, 1, 2
GATE_BLK = MXU_DIM

VMEM_LIMIT = 48 * 1024 * 1024
BIG_VMEM_LIMIT = 56 * 1024 * 1024


def _rms(x, g):
    ms = jnp.mean(x * x, axis=-1, keepdims=True)
    return x * lax.rsqrt(ms + EPS) * g


def _t5_bucket_np(rel):
    nb = N_BUCKETS // 2
    max_exact = nb // 2
    ret = np.where(rel > 0, nb, 0)
    n = np.abs(rel)
    nf = np.maximum(n, 1).astype(np.float32)
    large = max_exact + (np.log(nf / np.float32(max_exact)) / np.float32(math.log(MAX_DISTANCE / max_exact))
                         * np.float32(nb - max_exact)).astype(np.int32)
    large = np.minimum(large, nb - 1)
    return ret + np.where(n < max_exact, n, large)


def _bucket_index_rows():
    step = np.arange(K_WIN) - HALF_STEPS
    rows = []
    for _, dil in PATTERNS:
        b = _t5_bucket_np((step * dil).astype(np.int32))
        rows.append(np.where(np.abs(step) <= HALF_STEPS, b, -1))
    return np.stack(rows, axis=0).astype(np.int32)[:, None, :]


def _bias_table_kernel(idx_ref, rb_ref, out_ref):
    idx = idx_ref[0]
    col = lax.broadcasted_iota(jnp.int32, (Q_TILE, K_WIN), 1)
    hit = [idx == b for b in range(N_BUCKETS)]
    for h in range(N_HEADS):
        first_row = jnp.full(idx.shape, NEG_INF, F32)
        for b in range(N_BUCKETS):
            first_row = jnp.where(hit[b], rb_ref[b, h] * LOG2E, first_row)
        acc = pltpu.roll(jnp.broadcast_to(first_row, (Q_TILE, K_WIN)), 0, 1, stride=1, stride_axis=0)
        out_ref[0, INTERIOR, h] = acc
        out_ref[0, FIRST, h] = jnp.where(col < K_WIN - HALF_STEPS, pltpu.roll(acc, K_WIN - HALF_STEPS, 1), NEG_INF)
        out_ref[0, LAST, h] = jnp.where(col >= HALF_STEPS, pltpu.roll(acc, HALF_STEPS, 1), NEG_INF)


def _bias_tables(rel_bias):
    idx = jnp.asarray(_bucket_index_rows())
    n_pat = idx.shape[0]
    return pl.pallas_call(
        _bias_table_kernel,
        grid=(n_pat,),
        in_specs=[
            pl.BlockSpec((1, 1, K_WIN), lambda g: (g, 0, 0)),
            pl.BlockSpec(memory_space=pltpu.SMEM),
        ],
        out_specs=pl.BlockSpec((1, 3, N_HEADS, Q_TILE, K_WIN), lambda g: (g, 0, 0, 0, 0)),
        out_shape=jax.ShapeDtypeStruct((n_pat, 3, N_HEADS, Q_TILE, K_WIN), F32),
        name="bias_table",
    )(idx, rel_bias)


def _stripe_store(ref, p, part):
    stripes = IN_TILE // STRIPE_ROWS
    for st in range(stripes):
        for j in range(D_RNN // LANES):
            ref[0, j, pl.ds(part * stripes + st, STRIPE_ROWS, stride=SUBLANES), :] = (
                p[st * STRIPE_ROWS:(st + 1) * STRIPE_ROWS, j * LANES:(j + 1) * LANES])


PERM_ROWS = MXU_DIM


def _class_permutations():
    mats = []
    for _, dil in PATTERNS:
        if dil == 1:
            continue
        per = PERM_ROWS // dil
        p = np.zeros((PERM_ROWS, PERM_ROWS), np.float32)
        for r in range(dil):
            for m in range(per):
                p[r * per + m, m * dil + r] = 1.0
        mats.append(p)
    return np.stack(mats)


def _in_proj_kernel(x_ref, g_ref, w_ref, perm_ref, *rest, tm, n_later):
    n_pat = len(PATTERNS)
    later_in, rest = rest[:n_later], rest[n_later:]
    xr_ref, gate_ref, rest = rest[0], rest[1], rest[2:]
    outs = [rest[a * n_pat:(a + 1) * n_pat] for a in range(3)]
    later_out = rest[3 * n_pat:3 * n_pat + n_later]
    h_scr = rest[3 * n_pat + n_later]
    for src, dst in zip(later_in, later_out):
        dst[...] = src[...].astype(BF16)
    h_scr[...] = _rms(x_ref[0], g_ref[...]).astype(BF16)

    def seg(lo, width):
        return jnp.dot(h_scr[...], w_ref[:, lo:lo + width], preferred_element_type=F32)

    for a, scale in enumerate((HEAD_DIM ** -0.5 * LOG2E, 1.0, 1.0)):
        p = seg(2 * D_RNN + a * D_ATTN, D_ATTN)
        if scale != 1.0:
            p = p * scale
        pb = p.astype(BF16)
        dilated = 0
        for (_, dil), o_ref in zip(PATTERNS, outs[a]):
            if dil == 1:
                o_ref[0, 0, 0] = pb
                continue
            per = PERM_ROWS // dil
            for blk in range(tm // PERM_ROWS):
                y = jnp.dot(perm_ref[dilated], pb[blk * PERM_ROWS:(blk + 1) * PERM_ROWS, :],
                            preferred_element_type=F32).astype(BF16)
                groups = o_ref.shape[1]
                for r in range(dil):
                    o_ref[0, r % groups, r // groups, blk * per:(blk + 1) * per, :] = y[r * per:(r + 1) * per, :]
            dilated += 1
    g = seg(D_RNN, D_RNN)
    gelu = g * (0.5 * (1.0 + jnp.tanh(math.sqrt(2.0 / math.pi) * (g + 0.044715 * (g * g * g)))))
    part = pl.program_id(1) % (RNN_CHUNK // IN_TILE)
    _stripe_store(gate_ref, gelu, part)
    _stripe_store(xr_ref, seg(0, D_RNN), part)


def _in_proj(x3, g, w_in_bf, later_weights, tm):
    assert tm == IN_TILE
    bsz, s, _ = x3.shape
    per_seq = s // tm
    n_steps = bsz * per_seq
    row_slice = lambda b, i: (b * per_seq + i, 0)
    later_specs = [pl.BlockSpec((w.shape[0] // n_steps, w.shape[1]), row_slice) for w in later_weights]
    later_shapes = [jax.ShapeDtypeStruct(w.shape, BF16) for w in later_weights]
    row = lambda b, i: (b, i, 0)
    cls = lambda b, i: (b, 0, 0, i, 0)
    chunk = lambda b, i: (b, 0, i // (RNN_CHUNK // IN_TILE), 0)
    const = lambda b, i: (0, 0)
    perms = jnp.asarray(_class_permutations(), BF16)
    qkv_specs, qkv_shapes = [], []
    for _ in range(3):
        for _, dil in PATTERNS:
            groups, per_group = dil // min(dil, CLASS_GROUP), min(dil, CLASS_GROUP)
            qkv_specs.append(pl.BlockSpec((1, groups, per_group, tm // dil, D_ATTN), cls))
            qkv_shapes.append(jax.ShapeDtypeStruct((bsz, groups, per_group, s // dil, D_ATTN), BF16))
    return pl.pallas_call(
        functools.partial(_in_proj_kernel, tm=tm, n_later=len(later_weights)),
        grid=(bsz, per_seq),
        in_specs=[
            pl.BlockSpec((1, tm, D_MODEL), row),
            pl.BlockSpec((1, D_MODEL), const),
            pl.BlockSpec((D_MODEL, D_IN), const, pipeline_mode=pl.Buffered(1)),
            pl.BlockSpec(perms.shape, lambda b, i: (0, 0, 0)),
        ] + later_specs,
        out_specs=[pl.BlockSpec((1, D_RNN // LANES, RNN_CHUNK, LANES), chunk)] * 2 + qkv_specs + later_specs,
        out_shape=[jax.ShapeDtypeStruct((bsz, D_RNN // LANES, s, LANES), F32)] * 2 + qkv_shapes + later_shapes,
        scratch_shapes=[pltpu.VMEM((tm, D_MODEL), BF16)],
        compiler_params=pltpu.CompilerParams(
            dimension_semantics=("arbitrary", "arbitrary"), vmem_limit_bytes=BIG_VMEM_LIMIT),
        name="in_proj",
    )(x3, g, w_in_bf, perms, *later_weights)


IN_TILE = 1024
RNN_CHUNK = 1024
STRIPE_ROWS = RNN_CHUNK // SUBLANES


def _scan_block(a, b, reverse):
    n = a.shape[0]
    row = lax.broadcasted_iota(jnp.int32, a.shape, 0)
    s = 1
    while s < n:
        if reverse:
            ra = pltpu.roll(a, n - s, 0)
            rb = pltpu.roll(b, n - s, 0)
            m = row < n - s
        else:
            ra = pltpu.roll(a, s, 0)
            rb = pltpu.roll(b, s, 0)
            m = row >= s
        b = jnp.where(m, a * rb + b, b)
        a = jnp.where(m, a * ra, a)
        s *= 2
    return a, b


def _rnn_kernel(*refs, reverse, final, nchunks):
    if final:
        (xc_ref, hf_ref, gate_ref, wa_ref, wx_ref, ba_ref, bx_ref, lam_ref, gn_ref,
         out_ref, a_scr, b_scr, h_scr, carry) = refs
    else:
        (pa_ref, pb_ref, cur_ref, nx_ref, cw_ref, cb_ref, wa_ref, wx_ref, ba_ref, bx_ref, lam_ref,
         out_ref, xc_out_ref, a_scr, b_scr, carry) = refs
        h_scr = None

    step = pl.program_id(1)
    chunk = (nchunks - 1 - step) if reverse else step
    n_slab = D_RNN // LANES
    nv = STRIPE_ROWS
    sub = lax.broadcasted_iota(jnp.int32, (SUBLANES, LANES), 0)
    top, bot = SUBLANES - 1, 0

    @pl.when(step == 0)
    def _():
        carry[...] = jnp.zeros_like(carry)

    if final:
        xc = [xc_ref[0, j] for j in range(n_slab)]
    else:
        xc = []
        for j in range(n_slab):
            lanes = slice(j * LANES, (j + 1) * LANES)
            x = cur_ref[0, j].reshape(nv, SUBLANES, LANES)
            pa = jnp.where(chunk > 0, pa_ref[0, j], 0.0)
            pb = jnp.where(chunk > 0, pb_ref[0, j], 0.0)
            nx = jnp.where(chunk < nchunks - 1, nx_ref[0, j], 0.0)
            xm2 = pltpu.roll(jnp.where(sub == top, pa, x[nv - 2]), 1, 0)
            xm1 = pltpu.roll(jnp.where(sub == top, pb, x[nv - 1]), 1, 0)
            xp1 = pltpu.roll(jnp.where(sub == bot, nx, x[0]), SUBLANES - 1, 0)
            xext = jnp.concatenate([xm2[None], xm1[None], x, xp1[None]], axis=0)
            acc = cb_ref[:, lanes] + xext[0:nv] * cw_ref[0:1, lanes]
            for k in range(1, CONV_WIDTH):
                acc = acc + xext[k:k + nv] * cw_ref[k:k + 1, lanes]
            xc.append(acc.reshape(RNN_CHUNK, LANES))
            xc_out_ref[0, j] = xc[j]

    nlam = -lam_ref[...]
    softplus = jnp.maximum(nlam, 0.0) + jnp.log1p(jnp.exp(-jnp.abs(nlam)))
    half_coef = (-0.5 * LRU_C) * softplus
    slabs_per_blk = GATE_BLK // LANES
    for jj in range(D_RNN // GATE_BLK):
        blk = slice(jj * GATE_BLK, (jj + 1) * GATE_BLK)
        xj = jnp.concatenate(xc[jj * slabs_per_blk:(jj + 1) * slabs_per_blk], axis=-1)
        xjb = xj.astype(BF16)
        za = jnp.dot(xjb, wa_ref[jj], preferred_element_type=F32) + ba_ref[:, blk]
        zx = jnp.dot(xjb, wx_ref[jj], preferred_element_type=F32) + bx_ref[:, blk]
        i = 0.5 * jnp.tanh(zx) + 0.5
        log_a = half_coef[:, blk] * jnp.tanh(za) + half_coef[:, blk]
        a = jnp.exp(log_a)
        u = -jnp.tanh(log_a) * (1.0 + a * a)
        root = jnp.where(u > 0.0, u * lax.rsqrt(u), 0.0)
        b = root * (i * xj)
        for t in range(slabs_per_blk):
            a_scr[jj * slabs_per_blk + t] = a[:, t * LANES:(t + 1) * LANES]
            b_scr[jj * slabs_per_blk + t] = b[:, t * LANES:(t + 1) * LANES]

    order = range(nv - 1, -1, -1) if reverse else range(nv)
    group = lambda v: slice(v * SUBLANES, (v + 1) * SUBLANES)

    h = [jnp.zeros((SUBLANES, LANES), F32)] * n_slab
    p = [jnp.ones((SUBLANES, LANES), F32)] * n_slab
    for v in order:
        for j in range(n_slab):
            av = a_scr[j, group(v), :]
            h[j] = av * h[j] + b_scr[j, group(v), :]
            p[j] = av * p[j]

    start = []
    for j in range(n_slab):
        pc, hc = _scan_block(p[j], h[j], reverse)
        prev = carry[j]
        if reverse:
            seed = jnp.broadcast_to(prev[bot:bot + 1, :], (SUBLANES, LANES))
            ends = hc + pc * seed
            start.append(jnp.where(sub == top, pltpu.roll(prev, SUBLANES - 1, 0),
                                   pltpu.roll(ends, SUBLANES - 1, 0)))
        else:
            seed = jnp.broadcast_to(prev[top:top + 1, :], (SUBLANES, LANES))
            ends = hc + pc * seed
            start.append(jnp.where(sub == bot, pltpu.roll(prev, 1, 0), pltpu.roll(ends, 1, 0)))
        carry[j] = ends

    dst = h_scr if final else None
    h = start
    for v in order:
        for j in range(n_slab):
            h[j] = a_scr[j, group(v), :] * h[j] + b_scr[j, group(v), :]
            if final:
                dst[j, group(v), :] = h[j]
            else:
                out_ref[0, j, group(v), :] = h[j]

    if final:
        ys = []
        for j in range(n_slab):
            ys.append((hf_ref[0, j] + h_scr[j]) * gate_ref[0, j])
        ss = ys[0] * ys[0]
        for j in range(1, n_slab):
            ss = ss + ys[j] * ys[j]
        inv = lax.rsqrt(jnp.sum(ss, axis=-1, keepdims=True) * (1.0 / D_RNN) + EPS)
        for j in range(n_slab):
            lanes = slice(j * LANES, (j + 1) * LANES)
            h_scr[j] = ys[j] * inv * gn_ref[:, lanes]
        for st in range(SUBLANES):
            for j in range(n_slab):
                out_ref[0, st * nv:(st + 1) * nv, j * LANES:(j + 1) * LANES] = (
                    h_scr[j, pl.ds(st, nv, stride=SUBLANES), :].astype(BF16))


def _rnn_pass(x4, extra, params, reverse, final):
    bsz, n_slab, s, _ = x4.shape
    nchunks = s // RNN_CHUNK
    hb = RNN_CHUNK // SUBLANES
    nhalo = s // SUBLANES

    def cidx(c):
        return (nchunks - 1 - c) if reverse else c

    cur_map = lambda b, c: (b, 0, cidx(c), 0)
    const2 = lambda b, c: (0, 0)
    const3 = lambda b, c: (0, 0, 0)
    cw, cb, wa, wx, ba, bx, lam, gn = params
    halo = (1, n_slab, SUBLANES, LANES)
    full = (1, n_slab, RNN_CHUNK, LANES)
    gate_specs = [
        pl.BlockSpec((D_RNN // GATE_BLK, GATE_BLK, GATE_BLK), const3),
        pl.BlockSpec((D_RNN // GATE_BLK, GATE_BLK, GATE_BLK), const3),
        pl.BlockSpec((1, D_RNN), const2),
        pl.BlockSpec((1, D_RNN), const2),
        pl.BlockSpec((1, D_RNN), const2),
    ]
    slab_scr = pltpu.VMEM((n_slab, RNN_CHUNK, LANES), F32)
    carry_scr = pltpu.VMEM((n_slab, SUBLANES, LANES), F32)
    if final:
        in_specs = [pl.BlockSpec(full, cur_map)] * 3 + gate_specs + [pl.BlockSpec((1, D_RNN), const2)]
        args = [x4, *extra, wa, wx, ba, bx, lam, gn]
        scratch = [slab_scr, slab_scr, slab_scr, carry_scr]
        out_specs = pl.BlockSpec((1, RNN_CHUNK, D_RNN), lambda b, c: (b, cidx(c), 0))
        out_shape = jax.ShapeDtypeStruct((bsz, s, D_RNN), BF16)
    else:
        pa_map = lambda b, c: (b, 0, jnp.maximum(cidx(c) * hb - 2, 0), 0)
        pb_map = lambda b, c: (b, 0, jnp.maximum(cidx(c) * hb - 1, 0), 0)
        nx_map = lambda b, c: (b, 0, jnp.minimum((cidx(c) + 1) * hb, nhalo - 1), 0)
        in_specs = [pl.BlockSpec(halo, pa_map), pl.BlockSpec(halo, pb_map), pl.BlockSpec(full, cur_map),
                    pl.BlockSpec(halo, nx_map),
                    pl.BlockSpec((CONV_WIDTH, D_RNN), const2), pl.BlockSpec((1, D_RNN), const2)] + gate_specs
        args = [x4, x4, x4, x4, cw, cb, wa, wx, ba, bx, lam]
        scratch = [slab_scr, slab_scr, carry_scr]
        out_specs = [pl.BlockSpec(full, cur_map)] * 2
        out_shape = [jax.ShapeDtypeStruct(x4.shape, F32)] * 2
    return pl.pallas_call(
        functools.partial(_rnn_kernel, reverse=reverse, final=final, nchunks=nchunks),
        grid=(bsz, nchunks),
        in_specs=in_specs,
        out_specs=out_specs,
        out_shape=out_shape,
        scratch_shapes=scratch,
        compiler_params=pltpu.CompilerParams(
            dimension_semantics=("arbitrary", "arbitrary"), vmem_limit_bytes=VMEM_LIMIT),
        name="rnn_bwd" if reverse else "rnn_fwd",
    )(*args)


def _gate_weights(w):
    per = GATE_BLK // RNN_BLOCK
    w5 = (0.5 * w).reshape(D_RNN // GATE_BLK, per, RNN_BLOCK, 1, RNN_BLOCK)
    on_diag = jnp.asarray(np.eye(per, dtype=bool)).reshape(1, per, 1, per, 1)
    dense = jnp.where(on_diag, w5, 0.0)
    return dense.reshape(D_RNN // GATE_BLK, GATE_BLK, GATE_BLK).astype(BF16)


def _attn_kernel(q_ref, kp_ref, k_ref, kn_ref, vp_ref, v_ref, vn_ref, bias_ref, o_ref, lse_ref, kbuf, vbuf,
                 *, sub_len, tiles_per_class):
    i = pl.program_id(2)
    n_tiles = sub_len // Q_TILE
    n_classes = q_ref.shape[1]
    q_rows = tiles_per_class * Q_TILE
    for buf, before, own, after in ((kbuf, kp_ref, k_ref, kn_ref), (vbuf, vp_ref, v_ref, vn_ref)):
        for cl in range(n_classes):
            buf[cl, 0:HALF_STEPS, :] = before[0, cl]
            buf[cl, HALF_STEPS:HALF_STEPS + q_rows, :] = own[0, cl]
            buf[cl, HALF_STEPS + q_rows:, :] = after[0, cl]
    buf_origin = i * q_rows - HALF_STEPS
    lane = lax.broadcasted_iota(jnp.int32, (Q_TILE, LANES), 1)
    heads_per_slab = LANES // HEAD_DIM
    first_half = lane < HEAD_DIM
    for cl in range(n_classes):
        for t in range(tiles_per_class):
            rows = slice(t * Q_TILE, (t + 1) * Q_TILE)
            if n_classes == 1:
                out_rows = rows
            else:
                out_rows = pl.ds(t * Q_TILE * n_classes + cl, Q_TILE, stride=n_classes)
            q = q_ref[0, cl, rows, :]
            tile = i * tiles_per_class + t
            variant = jnp.where(tile == 0, FIRST, jnp.where(tile == n_tiles - 1, LAST, INTERIOR))
            start = jnp.clip(tile * Q_TILE - HALF_STEPS, 0, sub_len - K_WIN) - buf_origin
            start = pl.multiple_of(start, K_TILE)
            kk = kbuf[cl, pl.ds(start, K_WIN), :]
            vv = vbuf[cl, pl.ds(start, K_WIN), :]
            for p in range(D_ATTN // LANES):
                sl = slice(p * LANES, (p + 1) * LANES)
                qp, kp, vp = q[:, sl], kk[:, sl], vv[:, sl]
                zero = jnp.zeros_like(qp)
                q2 = jnp.concatenate([jnp.where(first_half, qp, zero), jnp.where(first_half, zero, qp)], axis=0)
                s = lax.dot_general(q2, kp, (((1,), (1,)), ((), ())), preferred_element_type=F32)
                h0 = p * heads_per_slab
                s = s + bias_ref[variant, h0:h0 + heads_per_slab].reshape(heads_per_slab * Q_TILE, K_WIN)
                m = jnp.max(s, axis=-1, keepdims=True)
                e = jnp.exp2(s - m)
                l = jnp.sum(e, axis=-1, keepdims=True)
                o2 = jnp.dot(e.astype(BF16), vp, preferred_element_type=F32)
                shape = (Q_TILE, LANES)
                pick = lambda a: jnp.where(first_half, jnp.broadcast_to(a[:Q_TILE], shape),
                                           jnp.broadcast_to(a[Q_TILE:], shape))
                l_slab = pick(l)
                o_ref[0, p, out_rows, :] = pick(o2) * (1.0 / l_slab)
                lse_ref[0, p, out_rows, :] = pick(m) + jnp.log(l_slab) * LOG2E


def _attn_pattern(qc, kc, vc, bias, pat):
    bsz, groups, n_classes, sub_len, _ = qc.shape
    tiles_per_class = TILES_PER_STEP // n_classes
    q_rows = tiles_per_class * Q_TILE
    n_slab = D_ATTN // LANES
    qmap = lambda b, g, i: (b, g, 0, i, 0)
    halo_per_step = q_rows // HALF_STEPS
    n_halo = sub_len // HALF_STEPS
    before = lambda b, g, i: (b, g, 0, jnp.maximum(i * halo_per_step - 1, 0), 0)
    after = lambda b, g, i: (b, g, 0, jnp.minimum((i + 1) * halo_per_step, n_halo - 1), 0)
    own_spec = pl.BlockSpec((1, None, n_classes, q_rows, D_ATTN), qmap)
    kv_specs = [pl.BlockSpec((1, None, n_classes, HALF_STEPS, D_ATTN), before), own_spec,
                pl.BlockSpec((1, None, n_classes, HALF_STEPS, D_ATTN), after)]
    in_specs = [own_spec] + kv_specs + kv_specs
    in_specs.append(pl.BlockSpec((None, 3, N_HEADS, Q_TILE, K_WIN), lambda b, g, i: (pat, 0, 0, 0, 0)))
    window_buf = pltpu.VMEM((n_classes, q_rows + 2 * HALF_STEPS, D_ATTN), BF16)
    out_spec = pl.BlockSpec((1, None, n_slab, n_classes * q_rows, LANES), qmap)
    return pl.pallas_call(
        functools.partial(_attn_kernel, sub_len=sub_len, tiles_per_class=tiles_per_class),
        grid=(bsz, groups, sub_len // q_rows),
        in_specs=in_specs,
        out_specs=[out_spec] * 2,
        out_shape=[jax.ShapeDtypeStruct((bsz, groups, n_slab, n_classes * sub_len, LANES), F32)] * 2,
        scratch_shapes=[window_buf, window_buf],
        compiler_params=pltpu.CompilerParams(
            dimension_semantics=("arbitrary",) * 3, vmem_limit_bytes=VMEM_LIMIT),
        name=f"attn_d{groups * n_classes}",
    )(qc, kc, kc, kc, vc, vc, vc, bias)


FF_CHUNK = 1024
OUT_TILE = 512


def _token_order(ref, scr, slab):
    groups = ref.shape[1]
    if groups == 1:
        return ref[0, 0, slab]
    for g in range(groups):
        scr[slab, pl.ds(g, OUT_TILE // groups, stride=groups), :] = ref[0, g, slab]
    return scr[slab]


def _out_mlp_kernel(x_ref, mr_ref, o1, o2, o3, l1, l2, l3, ga_ref, wo_ref, gm_ref, wu_ref, wd_ref,
                    gf_ref, out_ref, s_o3, s_l3):
    ys = []
    for slab in range(D_ATTN // LANES):
        a1 = _token_order(l1, None, slab)
        a2 = _token_order(l2, None, slab)
        a3 = _token_order(l3, s_l3, slab)
        mx = jnp.maximum(jnp.maximum(a1, a2), a3)
        e1, e2, e3 = jnp.exp2(a1 - mx), jnp.exp2(a2 - mx), jnp.exp2(a3 - mx)
        ys.append((e1 * _token_order(o1, None, slab) + e2 * _token_order(o2, None, slab)
                   + e3 * _token_order(o3, s_o3, slab)) / (e1 + e2 + e3))
    mix_attn = _rms(jnp.concatenate(ys, axis=-1), ga_ref[...]).astype(BF16)

    proj = jnp.dot(mr_ref[0], wo_ref[0:D_RNN, :], preferred_element_type=F32)
    proj = proj + jnp.dot(mix_attn, wo_ref[D_RNN:D_RNN + D_ATTN, :], preferred_element_type=F32)
    x1 = proj + x_ref[0]
    h = _rms(x1, gm_ref[...]).astype(BF16)
    ff = None
    for c in range(D_FF // FF_CHUNK):
        cols = slice(c * FF_CHUNK, (c + 1) * FF_CHUNK)
        z = jnp.dot(h, wu_ref[:, cols], preferred_element_type=F32)
        z = jnp.square(jnp.maximum(z, 0.0)).astype(BF16)
        d = jnp.dot(z, wd_ref[cols, :], preferred_element_type=F32)
        ff = d if ff is None else ff + d
    out_ref[0] = _rms(ff + x1, gf_ref[...])


def _out_mlp(x3, mix_rnn, os_, lses, ga, wo, gm, wu, wd, gf):
    bsz, s, _ = x3.shape
    tm = OUT_TILE
    row = lambda b, i: (b, i, 0)
    const = lambda b, i: (0, 0)
    once = pl.Buffered(1)
    n_slab = D_ATTN // LANES
    attn_specs = [pl.BlockSpec((1, o.shape[1], n_slab, tm // o.shape[1], LANES), lambda b, i: (b, 0, 0, i, 0))
                  for o in os_]
    in_specs = [pl.BlockSpec((1, tm, D_MODEL), row), pl.BlockSpec((1, tm, D_RNN), row)]
    in_specs += attn_specs + attn_specs
    in_specs += [
        pl.BlockSpec((1, D_ATTN), const),
        pl.BlockSpec((D_RNN + D_ATTN, D_MODEL), const, pipeline_mode=once),
        pl.BlockSpec((1, D_MODEL), const),
        pl.BlockSpec((D_MODEL, D_FF), const, pipeline_mode=once),
        pl.BlockSpec((D_FF, D_MODEL), const, pipeline_mode=once),
        pl.BlockSpec((1, D_MODEL), const),
    ]
    return pl.pallas_call(
        _out_mlp_kernel,
        grid=(bsz, s // tm),
        in_specs=in_specs,
        out_specs=pl.BlockSpec((1, tm, D_MODEL), row),
        out_shape=jax.ShapeDtypeStruct((bsz, s, D_MODEL), F32),
        scratch_shapes=[pltpu.VMEM((n_slab, tm, LANES), F32)] * 2,
        compiler_params=pltpu.CompilerParams(
            dimension_semantics=("arbitrary", "arbitrary"), vmem_limit_bytes=BIG_VMEM_LIMIT),
        name="out_mlp",
    )(x3, mix_rnn, *os_, *lses, ga, wo, gm, wu, wd, gf)


def kernel(x, attn_norm_g, w_in, conv_w, conv_b, lru_wa_fwd, lru_ba_fwd, lru_wx_fwd, lru_bx_fwd, lru_lam_fwd, lru_wa_bwd, lru_ba_bwd, lru_wx_bwd, lru_bx_bwd, lru_lam_bwd, rel_bias, norm_rnn_g, norm_attn_g, w_out, mlp_norm_g, w_up, w_down, final_norm_g):
    depth = w_in.shape[0]
    assert depth == 1, "the final RMSNorm is fused into the single layer's last call"
    l = 0
    n_pat = len(PATTERNS)
    row = lambda v: v.reshape(1, -1)
    bias = _bias_tables(rel_bias)
    proj = _in_proj(x, row(attn_norm_g[l]), w_in[l].astype(BF16), (w_out[l], w_up[l], w_down[l]), tm=IN_TILE)
    xr, gate = proj[0], proj[1]
    qs, ks, vs = (proj[2 + a * n_pat:2 + (a + 1) * n_pat] for a in range(3))
    w_out_bf, w_up_bf, w_down_bf = proj[2 + 3 * n_pat:]
    half = lambda v: row(0.5 * v)
    fwd = (conv_w[l], row(conv_b[l]), _gate_weights(lru_wa_fwd[l]), _gate_weights(lru_wx_fwd[l]),
           half(lru_ba_fwd[l]), half(lru_bx_fwd[l]), row(lru_lam_fwd[l]), None)
    bwd = (conv_w[l], row(conv_b[l]), _gate_weights(lru_wa_bwd[l]), _gate_weights(lru_wx_bwd[l]),
           half(lru_ba_bwd[l]), half(lru_bx_bwd[l]), row(lru_lam_bwd[l]), row(norm_rnn_g[l]))
    h_f, xc = _rnn_pass(xr, (), fwd, reverse=False, final=False)
    mix_rnn = _rnn_pass(xc, (h_f, gate), bwd, reverse=True, final=True)
    os_, lses = [], []
    for pat in range(n_pat):
        o, lse = _attn_pattern(qs[pat], ks[pat], vs[pat], bias, pat)
        os_.append(o)
        lses.append(lse)
    return _out_mlp(x, mix_rnn, os_, lses, row(norm_attn_g[l]), w_out_bf,
                    row(mlp_norm_g[l]), w_up_bf, w_down_bf, row(final_norm_g))
```

```python
import functools
import math

import numpy as np
import jax
import jax.numpy as jnp
from jax import lax
from jax.experimental import pallas as pl
from jax.experimental.pallas import tpu as pltpu

F32 = jnp.float32
BF16 = jnp.bfloat16

D_MODEL = 1024
D_RNN = 512
N_RNN_BLOCKS = 8
RNN_BLOCK = D_RNN // N_RNN_BLOCKS
CONV_WIDTH = 4
CONV_LEFT = 2
LRU_C = 8.0
N_HEADS = 8
HEAD_DIM = 64
D_ATTN = N_HEADS * HEAD_DIM
PATTERNS = ((128, 1), (512, 4), (2048, 16))
HALF_STEPS = 64
N_BUCKETS = 32
MAX_DISTANCE = 1024
D_IN = 2 * D_RNN + 3 * D_ATTN
D_FF = 4 * D_MODEL
EPS = 1e-6
NEG_INF = -1e30

LANES = 128
SUBLANES = 8
MXU_DIM = 256

Q_TILE = 128
K_TILE = 64
K_WIN = Q_TILE + 2 * HALF_STEPS
TILES_PER_STEP = 16
CLASS_GROUP = 4
LOG2E = math.log2(math.e)
INTERIOR, FIRST, LAST = 0, 1, 2
GATE_BLK = MXU_DIM

IN_TILE = 1024
RNN_CHUNK = 1024
STRIPE_ROWS = RNN_CHUNK // SUBLANES
OUT_TILE = 512
FF_CHUNK = 1024
assert RNN_CHUNK % IN_TILE == 0 and (CONV_WIDTH, CONV_LEFT) == (4, 2)

VMEM_LIMIT = 48 * 1024 * 1024
BIG_VMEM_LIMIT = 56 * 1024 * 1024


def _rms(x, g):
    ms = jnp.mean(x * x, axis=-1, keepdims=True)
    return x * lax.rsqrt(ms + EPS) * g


def _t5_bucket_np(rel):
    nb = N_BUCKETS // 2
    max_exact = nb // 2
    ret = np.where(rel > 0, nb, 0)
    n = np.abs(rel)
    nf = np.maximum(n, 1).astype(np.float32)
    large = max_exact + (np.log(nf / np.float32(max_exact)) / np.float32(math.log(MAX_DISTANCE / max_exact))
                         * np.float32(nb - max_exact)).astype(np.int32)
    large = np.minimum(large, nb - 1)
    return ret + np.where(n < max_exact, n, large)


def _bucket_index_rows():
    step = np.arange(K_WIN) - HALF_STEPS
    rows = []
    for _, dil in PATTERNS:
        b = _t5_bucket_np((step * dil).astype(np.int32))
        rows.append(np.where(np.abs(step) <= HALF_STEPS, b, -1))
    return np.stack(rows, axis=0).astype(np.int32)[:, None, :]


def _bias_table_kernel(idx_ref, rb_ref, out_ref):
    idx = idx_ref[0]
    col = lax.broadcasted_iota(jnp.int32, (Q_TILE, K_WIN), 1)
    hit = [idx == b for b in range(N_BUCKETS)]
    for h in range(N_HEADS):
        first_row = jnp.full(idx.shape, NEG_INF, F32)
        for b in range(N_BUCKETS):
            first_row = jnp.where(hit[b], rb_ref[b, h] * LOG2E, first_row)
        acc = pltpu.roll(jnp.broadcast_to(first_row, (Q_TILE, K_WIN)), 0, 1, stride=1, stride_axis=0)
        out_ref[0, INTERIOR, h] = acc
        out_ref[0, FIRST, h] = jnp.where(col < K_WIN - HALF_STEPS, pltpu.roll(acc, K_WIN - HALF_STEPS, 1), NEG_INF)
        out_ref[0, LAST, h] = jnp.where(col >= HALF_STEPS, pltpu.roll(acc, HALF_STEPS, 1), NEG_INF)


def _bias_tables(rel_bias):
    idx = jnp.asarray(_bucket_index_rows())
    n_pat = idx.shape[0]
    return pl.pallas_call(
        _bias_table_kernel,
        grid=(n_pat,),
        in_specs=[
            pl.BlockSpec((1, 1, K_WIN), lambda g: (g, 0, 0)),
            pl.BlockSpec(memory_space=pltpu.SMEM),
        ],
        out_specs=pl.BlockSpec((1, 3, N_HEADS, Q_TILE, K_WIN), lambda g: (g, 0, 0, 0, 0)),
        out_shape=jax.ShapeDtypeStruct((n_pat, 3, N_HEADS, Q_TILE, K_WIN), F32),
        name="bias_table",
    )(idx, rel_bias)


def _stripe_store(ref, p, part):
    stripes = IN_TILE // STRIPE_ROWS
    for st in range(stripes):
        for j in range(D_RNN // LANES):
            ref[0, j, pl.ds(part * stripes + st, STRIPE_ROWS, stride=SUBLANES), :] = (
                p[st * STRIPE_ROWS:(st + 1) * STRIPE_ROWS, j * LANES:(j + 1) * LANES])


PERM_ROWS = MXU_DIM


def _class_permutations():
    mats = []
    for _, dil in PATTERNS:
        if dil == 1:
            continue
        per = PERM_ROWS // dil
        p = np.zeros((PERM_ROWS, PERM_ROWS), np.float32)
        for r in range(dil):
            for m in range(per):
                p[r * per + m, m * dil + r] = 1.0
        mats.append(p)
    return np.stack(mats)


def _in_proj_kernel(x_ref, g_ref, w_ref, perm_ref, *rest, tm, n_later):
    n_pat = len(PATTERNS)
    later_in, rest = rest[:n_later], rest[n_later:]
    xr_ref, gate_ref, rest = rest[0], rest[1], rest[2:]
    outs = [rest[a * n_pat:(a + 1) * n_pat] for a in range(3)]
    later_out = rest[3 * n_pat:3 * n_pat + n_later]
    h_scr = rest[3 * n_pat + n_later]
    for src, dst in zip(later_in, later_out):
        dst[...] = src[...].astype(BF16)
    h_scr[...] = _rms(x_ref[0], g_ref[...]).astype(BF16)

    def seg(lo, width):
        return jnp.dot(h_scr[...], w_ref[:, lo:lo + width], preferred_element_type=F32)

    for a, scale in enumerate((HEAD_DIM ** -0.5 * LOG2E, 1.0, 1.0)):
        p = seg(2 * D_RNN + a * D_ATTN, D_ATTN)
        if scale != 1.0:
            p = p * scale
        pb = p.astype(BF16)
        dilated = 0
        for (_, dil), o_ref in zip(PATTERNS, outs[a]):
            if dil == 1:
                o_ref[0, 0, 0] = pb
                continue
            per = PERM_ROWS // dil
            for blk in range(tm // PERM_ROWS):
                y = jnp.dot(perm_ref[dilated], pb[blk * PERM_ROWS:(blk + 1) * PERM_ROWS, :],
                            preferred_element_type=F32).astype(BF16)
                groups = o_ref.shape[1]
                for r in range(dil):
                    o_ref[0, r % groups, r // groups, blk * per:(blk + 1) * per, :] = y[r * per:(r + 1) * per, :]
            dilated += 1
    g = seg(D_RNN, D_RNN)
    c = math.sqrt(2.0 / math.pi)
    half_g = 0.5 * g
    gelu = half_g + half_g * jnp.tanh(g * (c + (c * 0.044715) * (g * g)))
    part = pl.program_id(1) % (RNN_CHUNK // IN_TILE)
    _stripe_store(gate_ref, gelu, part)
    _stripe_store(xr_ref, seg(0, D_RNN), part)


def _in_proj(x3, g, w_in_bf, later_weights, tm):
    assert tm == IN_TILE
    bsz, s, _ = x3.shape
    per_seq = s // tm
    n_steps = bsz * per_seq
    row_slice = lambda b, i: (b * per_seq + i, 0)
    later_specs = [pl.BlockSpec((w.shape[0] // n_steps, w.shape[1]), row_slice) for w in later_weights]
    later_shapes = [jax.ShapeDtypeStruct(w.shape, BF16) for w in later_weights]
    row = lambda b, i: (b, i, 0)
    cls = lambda b, i: (b, 0, 0, i, 0)
    chunk = lambda b, i: (b, 0, i // (RNN_CHUNK // IN_TILE), 0)
    const = lambda b, i: (0, 0)
    perms = jnp.asarray(_class_permutations(), BF16)
    qkv_specs, qkv_shapes = [], []
    for _ in range(3):
        for _, dil in PATTERNS:
            groups, per_group = dil // min(dil, CLASS_GROUP), min(dil, CLASS_GROUP)
            qkv_specs.append(pl.BlockSpec((1, groups, per_group, tm // dil, D_ATTN), cls))
            qkv_shapes.append(jax.ShapeDtypeStruct((bsz, groups, per_group, s // dil, D_ATTN), BF16))
    return pl.pallas_call(
        functools.partial(_in_proj_kernel, tm=tm, n_later=len(later_weights)),
        grid=(bsz, per_seq),
        in_specs=[
            pl.BlockSpec((1, tm, D_MODEL), row),
            pl.BlockSpec((1, D_MODEL), const),
            pl.BlockSpec((D_MODEL, D_IN), const, pipeline_mode=pl.Buffered(1)),
            pl.BlockSpec(perms.shape, lambda b, i: (0, 0, 0)),
        ] + later_specs,
        out_specs=[pl.BlockSpec((1, D_RNN // LANES, RNN_CHUNK, LANES), chunk)] * 2 + qkv_specs + later_specs,
        out_shape=[jax.ShapeDtypeStruct((bsz, D_RNN // LANES, s, LANES), F32)] * 2 + qkv_shapes + later_shapes,
        scratch_shapes=[pltpu.VMEM((tm, D_MODEL), BF16)],
        compiler_params=pltpu.CompilerParams(
            dimension_semantics=("arbitrary", "arbitrary"), vmem_limit_bytes=BIG_VMEM_LIMIT),
        name="in_proj",
    )(x3, g, w_in_bf, perms, *later_weights)


def _scan_block(a, b, reverse):
    n = a.shape[0]
    row = lax.broadcasted_iota(jnp.int32, a.shape, 0)
    s = 1
    while s < n:
        if reverse:
            ra = pltpu.roll(a, n - s, 0)
            rb = pltpu.roll(b, n - s, 0)
            m = row < n - s
        else:
            ra = pltpu.roll(a, s, 0)
            rb = pltpu.roll(b, s, 0)
            m = row >= s
        b = jnp.where(m, a * rb + b, b)
        a = jnp.where(m, a * ra, a)
        s *= 2
    return a, b


def _rnn_kernel(*refs, reverse, final, nchunks):
    if final:
        (xc_ref, hf_ref, gate_ref, wa_ref, wx_ref, ba_ref, bx_ref, lam_ref, gn_ref,
         out_ref, a_scr, b_scr, h_scr, carry) = refs
    else:
        (pa_ref, pb_ref, cur_ref, nx_ref, cw_ref, cb_ref, wa_ref, wx_ref, ba_ref, bx_ref, lam_ref,
         out_ref, xc_out_ref, a_scr, b_scr, carry) = refs
        h_scr = None

    step = pl.program_id(1)
    chunk = (nchunks - 1 - step) if reverse else step
    n_slab = D_RNN // LANES
    nv = STRIPE_ROWS
    sub = lax.broadcasted_iota(jnp.int32, (SUBLANES, LANES), 0)
    top, bot = SUBLANES - 1, 0

    @pl.when(step == 0)
    def _():
        carry[...] = jnp.zeros_like(carry)

    if final:
        xc = [xc_ref[0, j] for j in range(n_slab)]
    else:
        xc = []
        for j in range(n_slab):
            lanes = slice(j * LANES, (j + 1) * LANES)
            x = cur_ref[0, j].reshape(nv, SUBLANES, LANES)
            pa = jnp.where(chunk > 0, pa_ref[0, j], 0.0)
            pb = jnp.where(chunk > 0, pb_ref[0, j], 0.0)
            nx = jnp.where(chunk < nchunks - 1, nx_ref[0, j], 0.0)
            xm2 = pltpu.roll(jnp.where(sub == top, pa, x[nv - 2]), 1, 0)
            xm1 = pltpu.roll(jnp.where(sub == top, pb, x[nv - 1]), 1, 0)
            xp1 = pltpu.roll(jnp.where(sub == bot, nx, x[0]), SUBLANES - 1, 0)
            xext = jnp.concatenate([xm2[None], xm1[None], x, xp1[None]], axis=0)
            acc = cb_ref[:, lanes] + xext[0:nv] * cw_ref[0:1, lanes]
            for k in range(1, CONV_WIDTH):
                acc = acc + xext[k:k + nv] * cw_ref[k:k + 1, lanes]
            xc.append(acc.reshape(RNN_CHUNK, LANES))
            xc_out_ref[0, j] = xc[j]

    nlam = -lam_ref[...]
    softplus = jnp.maximum(nlam, 0.0) + jnp.log1p(jnp.exp(-jnp.abs(nlam)))
    half_coef = (-0.5 * LRU_C) * softplus
    slabs_per_blk = GATE_BLK // LANES
    for jj in range(D_RNN // GATE_BLK):
        blk = slice(jj * GATE_BLK, (jj + 1) * GATE_BLK)
        xj = jnp.concatenate(xc[jj * slabs_per_blk:(jj + 1) * slabs_per_blk], axis=-1)
        xjb = xj.astype(BF16)
        za = jnp.dot(xjb, wa_ref[jj], preferred_element_type=F32) + ba_ref[:, blk]
        zx = jnp.dot(xjb, wx_ref[jj], preferred_element_type=F32) + bx_ref[:, blk]
        i = 0.5 * jnp.tanh(zx) + 0.5
        log_a = half_coef[:, blk] * jnp.tanh(za) + half_coef[:, blk]
        a = jnp.exp(log_a)
        u = -jnp.tanh(log_a) * (1.0 + a * a)
        root = jnp.where(u > 0.0, u * lax.rsqrt(u), 0.0)
        b = root * (i * xj)
        for t in range(slabs_per_blk):
            a_scr[jj * slabs_per_blk + t] = a[:, t * LANES:(t + 1) * LANES]
            b_scr[jj * slabs_per_blk + t] = b[:, t * LANES:(t + 1) * LANES]

    order = range(nv - 1, -1, -1) if reverse else range(nv)
    group = lambda v: slice(v * SUBLANES, (v + 1) * SUBLANES)

    h = [jnp.zeros((SUBLANES, LANES), F32)] * n_slab
    p = [jnp.ones((SUBLANES, LANES), F32)] * n_slab
    for v in order:
        for j in range(n_slab):
            av = a_scr[j, group(v), :]
            h[j] = av * h[j] + b_scr[j, group(v), :]
            p[j] = av * p[j]

    start = []
    for j in range(n_slab):
        pc, hc = _scan_block(p[j], h[j], reverse)
        prev = carry[j]
        if reverse:
            seed = jnp.broadcast_to(prev[bot:bot + 1, :], (SUBLANES, LANES))
            ends = hc + pc * seed
            start.append(jnp.where(sub == top, pltpu.roll(prev, SUBLANES - 1, 0),
                                   pltpu.roll(ends, SUBLANES - 1, 0)))
        else:
            seed = jnp.broadcast_to(prev[top:top + 1, :], (SUBLANES, LANES))
            ends = hc + pc * seed
            start.append(jnp.where(sub == bot, pltpu.roll(prev, 1, 0), pltpu.roll(ends, 1, 0)))
        carry[j] = ends

    dst = h_scr if final else None
    h = start
    for v in order:
        for j in range(n_slab):
            h[j] = a_scr[j, group(v), :] * h[j] + b_scr[j, group(v), :]
            if final:
                dst[j, group(v), :] = h[j]
            else:
                out_ref[0, j, group(v), :] = h[j]

    if final:
        ys = []
        for j in range(n_slab):
            ys.append((hf_ref[0, j] + h_scr[j]) * gate_ref[0, j])
        ss = ys[0] * ys[0]
        for j in range(1, n_slab):
            ss = ss + ys[j] * ys[j]
        inv = lax.rsqrt(jnp.sum(ss, axis=-1, keepdims=True) * (1.0 / D_RNN) + EPS)
        for j in range(n_slab):
            lanes = slice(j * LANES, (j + 1) * LANES)
            h_scr[j] = ys[j] * inv * gn_ref[:, lanes]
        for st in range(SUBLANES):
            for j in range(n_slab):
                out_ref[0, st * nv:(st + 1) * nv, j * LANES:(j + 1) * LANES] = (
                    h_scr[j, pl.ds(st, nv, stride=SUBLANES), :].astype(BF16))


def _rnn_pass(x4, extra, params, reverse, final):
    bsz, n_slab, s, _ = x4.shape
    nchunks = s // RNN_CHUNK
    hb = RNN_CHUNK // SUBLANES
    nhalo = s // SUBLANES

    def cidx(c):
        return (nchunks - 1 - c) if reverse else c

    cur_map = lambda b, c: (b, 0, cidx(c), 0)
    const2 = lambda b, c: (0, 0)
    const3 = lambda b, c: (0, 0, 0)
    cw, cb, wa, wx, ba, bx, lam, gn = params
    halo = (1, n_slab, SUBLANES, LANES)
    full = (1, n_slab, RNN_CHUNK, LANES)
    gate_specs = [
        pl.BlockSpec((D_RNN // GATE_BLK, GATE_BLK, GATE_BLK), const3),
        pl.BlockSpec((D_RNN // GATE_BLK, GATE_BLK, GATE_BLK), const3),
        pl.BlockSpec((1, D_RNN), const2),
        pl.BlockSpec((1, D_RNN), const2),
        pl.BlockSpec((1, D_RNN), const2),
    ]
    slab_scr = pltpu.VMEM((n_slab, RNN_CHUNK, LANES), F32)
    carry_scr = pltpu.VMEM((n_slab, SUBLANES, LANES), F32)
    if final:
        in_specs = [pl.BlockSpec(full, cur_map)] * 3 + gate_specs + [pl.BlockSpec((1, D_RNN), const2)]
        args = [x4, *extra, wa, wx, ba, bx, lam, gn]
        scratch = [slab_scr, slab_scr, slab_scr, carry_scr]
        out_specs = pl.BlockSpec((1, RNN_CHUNK, D_RNN), lambda b, c: (b, cidx(c), 0))
        out_shape = jax.ShapeDtypeStruct((bsz, s, D_RNN), BF16)
    else:
        pa_map = lambda b, c: (b, 0, jnp.maximum(cidx(c) * hb - 2, 0), 0)
        pb_map = lambda b, c: (b, 0, jnp.maximum(cidx(c) * hb - 1, 0), 0)
        nx_map = lambda b, c: (b, 0, jnp.minimum((cidx(c) + 1) * hb, nhalo - 1), 0)
        in_specs = [pl.BlockSpec(halo, pa_map), pl.BlockSpec(halo, pb_map), pl.BlockSpec(full, cur_map),
                    pl.BlockSpec(halo, nx_map),
                    pl.BlockSpec((CONV_WIDTH, D_RNN), const2), pl.BlockSpec((1, D_RNN), const2)] + gate_specs
        args = [x4, x4, x4, x4, cw, cb, wa, wx, ba, bx, lam]
        scratch = [slab_scr, slab_scr, carry_scr]
        out_specs = [pl.BlockSpec(full, cur_map)] * 2
        out_shape = [jax.ShapeDtypeStruct(x4.shape, F32)] * 2
    return pl.pallas_call(
        functools.partial(_rnn_kernel, reverse=reverse, final=final, nchunks=nchunks),
        grid=(bsz, nchunks),
        in_specs=in_specs,
        out_specs=out_specs,
        out_shape=out_shape,
        scratch_shapes=scratch,
        compiler_params=pltpu.CompilerParams(
            dimension_semantics=("arbitrary", "arbitrary"), vmem_limit_bytes=VMEM_LIMIT),
        name="rnn_bwd" if reverse else "rnn_fwd",
    )(*args)


def _gate_weights(w):
    per = GATE_BLK // RNN_BLOCK
    w5 = (0.5 * w).reshape(D_RNN // GATE_BLK, per, RNN_BLOCK, 1, RNN_BLOCK)
    on_diag = jnp.asarray(np.eye(per, dtype=bool)).reshape(1, per, 1, per, 1)
    dense = jnp.where(on_diag, w5, 0.0)
    return dense.reshape(D_RNN // GATE_BLK, GATE_BLK, GATE_BLK).astype(BF16)


def _attn_kernel(q_ref, kp_ref, k_ref, kn_ref, vp_ref, v_ref, vn_ref, bias_ref, o_ref, lse_ref, kbuf, vbuf,
                 *, sub_len, tiles_per_class):
    i = pl.program_id(2)
    n_tiles = sub_len // Q_TILE
    n_classes = q_ref.shape[1]
    q_rows = tiles_per_class * Q_TILE
    for buf, before, own, after in ((kbuf, kp_ref, k_ref, kn_ref), (vbuf, vp_ref, v_ref, vn_ref)):
        for cl in range(n_classes):
            buf[cl, 0:HALF_STEPS, :] = before[0, cl]
            buf[cl, HALF_STEPS:HALF_STEPS + q_rows, :] = own[0, cl]
            buf[cl, HALF_STEPS + q_rows:, :] = after[0, cl]
    buf_origin = i * q_rows - HALF_STEPS
    lane = lax.broadcasted_iota(jnp.int32, (Q_TILE, LANES), 1)
    heads_per_slab = LANES // HEAD_DIM
    first_half = lane < HEAD_DIM
    for cl in range(n_classes):
        for t in range(tiles_per_class):
            rows = slice(t * Q_TILE, (t + 1) * Q_TILE)
            if n_classes == 1:
                out_rows = rows
            else:
                out_rows = pl.ds(t * Q_TILE * n_classes + cl, Q_TILE, stride=n_classes)
            q = q_ref[0, cl, rows, :]
            tile = i * tiles_per_class + t
            variant = jnp.where(tile == 0, FIRST, jnp.where(tile == n_tiles - 1, LAST, INTERIOR))
            start = jnp.clip(tile * Q_TILE - HALF_STEPS, 0, sub_len - K_WIN) - buf_origin
            start = pl.multiple_of(start, K_TILE)
            kk = kbuf[cl, pl.ds(start, K_WIN), :]
            vv = vbuf[cl, pl.ds(start, K_WIN), :]
            for p in range(D_ATTN // LANES):
                sl = slice(p * LANES, (p + 1) * LANES)
                qp, kp, vp = q[:, sl], kk[:, sl], vv[:, sl]
                zero = jnp.zeros_like(qp)
                q2 = jnp.concatenate([jnp.where(first_half, qp, zero), jnp.where(first_half, zero, qp)], axis=0)
                s = lax.dot_general(q2, kp, (((1,), (1,)), ((), ())), preferred_element_type=F32)
                h0 = p * heads_per_slab
                s = s + bias_ref[variant, h0:h0 + heads_per_slab].reshape(heads_per_slab * Q_TILE, K_WIN)
                m = jnp.max(s, axis=-1, keepdims=True)
                e = jnp.exp2(s - m)
                l = jnp.sum(e, axis=-1, keepdims=True)
                o2 = jnp.dot(e.astype(BF16), vp, preferred_element_type=F32)
                shape = (Q_TILE, LANES)
                pick = lambda a: jnp.where(first_half, jnp.broadcast_to(a[:Q_TILE], shape),
                                           jnp.broadcast_to(a[Q_TILE:], shape))
                l_slab = pick(l)
                o_ref[0, p, out_rows, :] = pick(o2) * (1.0 / l_slab)
                lse_ref[0, p, out_rows, :] = pick(m) + jnp.log(l_slab) * LOG2E


def _attn_pattern(qc, kc, vc, bias, pat):
    bsz, groups, n_classes, sub_len, _ = qc.shape
    tiles_per_class = TILES_PER_STEP // n_classes
    q_rows = tiles_per_class * Q_TILE
    n_slab = D_ATTN // LANES
    qmap = lambda b, g, i: (b, g, 0, i, 0)
    halo_per_step = q_rows // HALF_STEPS
    n_halo = sub_len // HALF_STEPS
    before = lambda b, g, i: (b, g, 0, jnp.maximum(i * halo_per_step - 1, 0), 0)
    after = lambda b, g, i: (b, g, 0, jnp.minimum((i + 1) * halo_per_step, n_halo - 1), 0)
    own_spec = pl.BlockSpec((1, None, n_classes, q_rows, D_ATTN), qmap)
    kv_specs = [pl.BlockSpec((1, None, n_classes, HALF_STEPS, D_ATTN), before), own_spec,
                pl.BlockSpec((1, None, n_classes, HALF_STEPS, D_ATTN), after)]
    in_specs = [own_spec] + kv_specs + kv_specs
    in_specs.append(pl.BlockSpec((None, 3, N_HEADS, Q_TILE, K_WIN), lambda b, g, i: (pat, 0, 0, 0, 0)))
    window_buf = pltpu.VMEM((n_classes, q_rows + 2 * HALF_STEPS, D_ATTN), BF16)
    out_spec = pl.BlockSpec((1, None, n_slab, n_classes * q_rows, LANES), qmap)
    return pl.pallas_call(
        functools.partial(_attn_kernel, sub_len=sub_len, tiles_per_class=tiles_per_class),
        grid=(bsz, groups, sub_len // q_rows),
        in_specs=in_specs,
        out_specs=[out_spec] * 2,
        out_shape=[jax.ShapeDtypeStruct((bsz, groups, n_slab, n_classes * sub_len, LANES), F32)] * 2,
        scratch_shapes=[window_buf, window_buf],
        compiler_params=pltpu.CompilerParams(
            dimension_semantics=("arbitrary",) * 3, vmem_limit_bytes=VMEM_LIMIT),
        name=f"attn_d{groups * n_classes}",
    )(qc, kc, kc, kc, vc, vc, vc, bias)


def _token_order(ref, scr, slab):
    groups = ref.shape[1]
    if groups == 1:
        return ref[0, 0, slab]
    for g in range(groups):
        scr[slab, pl.ds(g, OUT_TILE // groups, stride=groups), :] = ref[0, g, slab]
    return scr[slab]


def _out_mlp_kernel(x_ref, mr_ref, o1, o2, o3, l1, l2, l3, ga_ref, wo_ref, gm_ref, wu_ref, wd_ref,
                    gf_ref, out_ref, s_o3, s_l3):
    ys = []
    for slab in range(D_ATTN // LANES):
        a1 = _token_order(l1, None, slab)
        a2 = _token_order(l2, None, slab)
        a3 = _token_order(l3, s_l3, slab)
        mx = jnp.maximum(jnp.maximum(a1, a2), a3)
        e1, e2, e3 = jnp.exp2(a1 - mx), jnp.exp2(a2 - mx), jnp.exp2(a3 - mx)
        ys.append((e1 * _token_order(o1, None, slab) + e2 * _token_order(o2, None, slab)
                   + e3 * _token_order(o3, s_o3, slab)) / (e1 + e2 + e3))
    mix_attn = _rms(jnp.concatenate(ys, axis=-1), ga_ref[...]).astype(BF16)

    proj = jnp.dot(mr_ref[0], wo_ref[0:D_RNN, :], preferred_element_type=F32)
    proj = proj + jnp.dot(mix_attn, wo_ref[D_RNN:D_RNN + D_ATTN, :], preferred_element_type=F32)
    x1 = proj + x_ref[0]
    h = _rms(x1, gm_ref[...]).astype(BF16)
    ff = None
    for c in range(D_FF // FF_CHUNK):
        cols = slice(c * FF_CHUNK, (c + 1) * FF_CHUNK)
        z = jnp.dot(h, wu_ref[:, cols], preferred_element_type=F32)
        z = jnp.square(jnp.maximum(z, 0.0)).astype(BF16)
        d = jnp.dot(z, wd_ref[cols, :], preferred_element_type=F32)
        ff = d if ff is None else ff + d
    out_ref[0] = _rms(ff + x1, gf_ref[...])


def _out_mlp(x3, mix_rnn, os_, lses, ga, wo, gm, wu, wd, gf):
    bsz, s, _ = x3.shape
    tm = OUT_TILE
    row = lambda b, i: (b, i, 0)
    const = lambda b, i: (0, 0)
    once = pl.Buffered(1)
    n_slab = D_ATTN // LANES
    attn_specs = [pl.BlockSpec((1, o.shape[1], n_slab, tm // o.shape[1], LANES), lambda b, i: (b, 0, 0, i, 0))
                  for o in os_]
    in_specs = [pl.BlockSpec((1, tm, D_MODEL), row), pl.BlockSpec((1, tm, D_RNN), row)]
    in_specs += attn_specs + attn_specs
    in_specs += [
        pl.BlockSpec((1, D_ATTN), const),
        pl.BlockSpec((D_RNN + D_ATTN, D_MODEL), const, pipeline_mode=once),
        pl.BlockSpec((1, D_MODEL), const),
        pl.BlockSpec((D_MODEL, D_FF), const, pipeline_mode=once),
        pl.BlockSpec((D_FF, D_MODEL), const, pipeline_mode=once),
        pl.BlockSpec((1, D_MODEL), const),
    ]
    return pl.pallas_call(
        _out_mlp_kernel,
        grid=(bsz, s // tm),
        in_specs=in_specs,
        out_specs=pl.BlockSpec((1, tm, D_MODEL), row),
        out_shape=jax.ShapeDtypeStruct((bsz, s, D_MODEL), F32),
        scratch_shapes=[pltpu.VMEM((n_slab, tm, LANES), F32)] * 2,
        compiler_params=pltpu.CompilerParams(
            dimension_semantics=("arbitrary", "arbitrary"), vmem_limit_bytes=BIG_VMEM_LIMIT),
        name="out_mlp",
    )(x3, mix_rnn, *os_, *lses, ga, wo, gm, wu, wd, gf)


def kernel(x, attn_norm_g, w_in, conv_w, conv_b, lru_wa_fwd, lru_ba_fwd, lru_wx_fwd, lru_bx_fwd, lru_lam_fwd, lru_wa_bwd, lru_ba_bwd, lru_wx_bwd, lru_bx_bwd, lru_lam_bwd, rel_bias, norm_rnn_g, norm_attn_g, w_out, mlp_norm_g, w_up, w_down, final_norm_g):
    depth = w_in.shape[0]
    assert depth == 1, "the final RMSNorm is fused into the single layer's last call"
    l = 0
    n_pat = len(PATTERNS)
    row = lambda v: v.reshape(1, -1)
    bias = _bias_tables(rel_bias)
    proj = _in_proj(x, row(attn_norm_g[l]), w_in[l].astype(BF16), (w_out[l], w_up[l], w_down[l]), tm=IN_TILE)
    xr, gate = proj[0], proj[1]
    qs, ks, vs = (proj[2 + a * n_pat:2 + (a + 1) * n_pat] for a in range(3))
    w_out_bf, w_up_bf, w_down_bf = proj[2 + 3 * n_pat:]
    half = lambda v: row(0.5 * v)
    fwd = (conv_w[l], row(conv_b[l]), _gate_weights(lru_wa_fwd[l]), _gate_weights(lru_wx_fwd[l]),
           half(lru_ba_fwd[l]), half(lru_bx_fwd[l]), row(lru_lam_fwd[l]), None)
    bwd = (conv_w[l], row(conv_b[l]), _gate_weights(lru_wa_bwd[l]), _gate_weights(lru_wx_bwd[l]),
           half(lru_ba_bwd[l]), half(lru_bx_bwd[l]), row(lru_lam_bwd[l]), row(norm_rnn_g[l]))
    h_f, xc = _rnn_pass(xr, (), fwd, reverse=False, final=False)
    mix_rnn = _rnn_pass(xc, (h_f, gate), bwd, reverse=True, final=True)
    os_, lses = [], []
    for pat in range(n_pat):
        o, lse = _attn_pattern(qs[pat], ks[pat], vs[pat], bias, pat)
        os_.append(o)
        lses.append(lse)
    return _out_mlp(x, mix_rnn, os_, lses, row(norm_attn_g[l]), w_out_bf,
                    row(mlp_norm_g[l]), w_up_bf, w_down_bf, row(final_norm_g))
```

```python
import functools
import math

import numpy as np
import jax
import jax.numpy as jnp
from jax import lax
from jax.experimental import pallas as pl
from jax.experimental.pallas import tpu as pltpu

F32 = jnp.float32
BF16 = jnp.bfloat16

D_MODEL = 1024
D_RNN = 512
N_RNN_BLOCKS = 8
RNN_BLOCK = D_RNN // N_RNN_BLOCKS
CONV_WIDTH = 4
CONV_LEFT = 2
LRU_C = 8.0
N_HEADS = 8
HEAD_DIM = 64
D_ATTN = N_HEADS * HEAD_DIM
PATTERNS = ((128, 1), (512, 4), (2048, 16))
HALF_STEPS = 64
N_BUCKETS = 32
MAX_DISTANCE = 1024
D_IN = 2 * D_RNN + 3 * D_ATTN
D_FF = 4 * D_MODEL
EPS = 1e-6
NEG_INF = -1e30

LANES = 128
SUBLANES = 8
MXU_DIM = 256

Q_TILE = 128
K_TILE = 64
K_WIN = Q_TILE + 2 * HALF_STEPS
TILES_PER_STEP = 16
CLASS_GROUP = 4
LOG2E = math.log2(math.e)
INTERIOR, FIRST, LAST = 0, 1, 2
GATE_BLK = MXU_DIM

IN_TILE = 1024
RNN_CHUNK = 1024
STRIPE_ROWS = RNN_CHUNK // SUBLANES
OUT_TILE = 512
FF_CHUNK = 1024
assert RNN_CHUNK % IN_TILE == 0 and (CONV_WIDTH, CONV_LEFT) == (4, 2)

VMEM_LIMIT = 48 * 1024 * 1024
BIG_VMEM_LIMIT = 56 * 1024 * 1024


def _rms(x, g):
    ms = jnp.mean(x * x, axis=-1, keepdims=True)
    return x * lax.rsqrt(ms + EPS) * g


def _t5_bucket_np(rel):
    nb = N_BUCKETS // 2
    max_exact = nb // 2
    ret = np.where(rel > 0, nb, 0)
    n = np.abs(rel)
    nf = np.maximum(n, 1).astype(np.float32)
    large = max_exact + (np.log(nf / np.float32(max_exact)) / np.float32(math.log(MAX_DISTANCE / max_exact))
                         * np.float32(nb - max_exact)).astype(np.int32)
    large = np.minimum(large, nb - 1)
    return ret + np.where(n < max_exact, n, large)


def _bucket_index_rows():
    step = np.arange(K_WIN) - HALF_STEPS
    rows = []
    for _, dil in PATTERNS:
        b = _t5_bucket_np((step * dil).astype(np.int32))
        rows.append(np.where(np.abs(step) <= HALF_STEPS, b, -1))
    return np.stack(rows, axis=0).astype(np.int32)[:, None, :]


def _bias_table_kernel(idx_ref, rb_ref, out_ref):
    idx = idx_ref[0]
    col = lax.broadcasted_iota(jnp.int32, (Q_TILE, K_WIN), 1)
    hit = [idx == b for b in range(N_BUCKETS)]
    for h in range(N_HEADS):
        first_row = jnp.full(idx.shape, NEG_INF, F32)
        for b in range(N_BUCKETS):
            first_row = jnp.where(hit[b], rb_ref[b, h] * LOG2E, first_row)
        acc = pltpu.roll(jnp.broadcast_to(first_row, (Q_TILE, K_WIN)), 0, 1, stride=1, stride_axis=0)
        out_ref[0, INTERIOR, h] = acc
        out_ref[0, FIRST, h] = jnp.where(col < K_WIN - HALF_STEPS, pltpu.roll(acc, K_WIN - HALF_STEPS, 1), NEG_INF)
        out_ref[0, LAST, h] = jnp.where(col >= HALF_STEPS, pltpu.roll(acc, HALF_STEPS, 1), NEG_INF)


def _bias_tables(rel_bias):
    idx = jnp.asarray(_bucket_index_rows())
    n_pat = idx.shape[0]
    return pl.pallas_call(
        _bias_table_kernel,
        grid=(n_pat,),
        in_specs=[
            pl.BlockSpec((1, 1, K_WIN), lambda g: (g, 0, 0)),
            pl.BlockSpec(memory_space=pltpu.SMEM),
        ],
        out_specs=pl.BlockSpec((1, 3, N_HEADS, Q_TILE, K_WIN), lambda g: (g, 0, 0, 0, 0)),
        out_shape=jax.ShapeDtypeStruct((n_pat, 3, N_HEADS, Q_TILE, K_WIN), F32),
        name="bias_table",
    )(idx, rel_bias)


def _stripe_store(ref, p, part):
    stripes = IN_TILE // STRIPE_ROWS
    for st in range(stripes):
        for j in range(D_RNN // LANES):
            ref[0, j, pl.ds(part * stripes + st, STRIPE_ROWS, stride=SUBLANES), :] = (
                p[st * STRIPE_ROWS:(st + 1) * STRIPE_ROWS, j * LANES:(j + 1) * LANES])


PERM_ROWS = MXU_DIM


def _class_permutations():
    mats = []
    for _, dil in PATTERNS:
        if dil == 1:
            continue
        per = PERM_ROWS // dil
        p = np.zeros((PERM_ROWS, PERM_ROWS), np.float32)
        for r in range(dil):
            for m in range(per):
                p[r * per + m, m * dil + r] = 1.0
        mats.append(p)
    return np.stack(mats)


def _in_proj_kernel(x_ref, g_ref, w_ref, perm_ref, *rest, tm, n_later):
    n_pat = len(PATTERNS)
    later_in, rest = rest[:n_later], rest[n_later:]
    xr_ref, gate_ref, rest = rest[0], rest[1], rest[2:]
    outs = [rest[a * n_pat:(a + 1) * n_pat] for a in range(3)]
    later_out = rest[3 * n_pat:3 * n_pat + n_later]
    h_scr = rest[3 * n_pat + n_later]
    for src, dst in zip(later_in, later_out):
        dst[...] = src[...].astype(BF16)
    h_scr[...] = _rms(x_ref[0], g_ref[...]).astype(BF16)

    def seg(lo, width):
        return jnp.dot(h_scr[...], w_ref[:, lo:lo + width], preferred_element_type=F32)

    for a, scale in enumerate((HEAD_DIM ** -0.5 * LOG2E, 1.0, 1.0)):
        p = seg(2 * D_RNN + a * D_ATTN, D_ATTN)
        if scale != 1.0:
            p = p * scale
        pb = p.astype(BF16)
        dilated = 0
        for (_, dil), o_ref in zip(PATTERNS, outs[a]):
            if dil == 1:
                o_ref[0, 0, 0] = pb
                continue
            per = PERM_ROWS // dil
            for blk in range(tm // PERM_ROWS):
                y = jnp.dot(perm_ref[dilated], pb[blk * PERM_ROWS:(blk + 1) * PERM_ROWS, :],
                            preferred_element_type=F32).astype(BF16)
                groups = o_ref.shape[1]
                for r in range(dil):
                    o_ref[0, r % groups, r // groups, blk * per:(blk + 1) * per, :] = y[r * per:(r + 1) * per, :]
            dilated += 1
    g = seg(D_RNN, D_RNN)
    c = math.sqrt(2.0 / math.pi)
    half_g = 0.5 * g
    gelu = half_g + half_g * jnp.tanh(g * (c + (c * 0.044715) * (g * g)))
    part = pl.program_id(1) % (RNN_CHUNK // IN_TILE)
    _stripe_store(gate_ref, gelu, part)
    _stripe_store(xr_ref, seg(0, D_RNN), part)


def _in_proj(x3, g, w_in_bf, later_weights, tm):
    assert tm == IN_TILE
    bsz, s, _ = x3.shape
    per_seq = s // tm
    n_steps = bsz * per_seq
    row_slice = lambda b, i: (b * per_seq + i, 0)
    later_specs = [pl.BlockSpec((w.shape[0] // n_steps, w.shape[1]), row_slice) for w in later_weights]
    later_shapes = [jax.ShapeDtypeStruct(w.shape, BF16) for w in later_weights]
    row = lambda b, i: (b, i, 0)
    cls = lambda b, i: (b, 0, 0, i, 0)
    chunk = lambda b, i: (b, 0, i // (RNN_CHUNK // IN_TILE), 0)
    const = lambda b, i: (0, 0)
    perms = jnp.asarray(_class_permutations(), BF16)
    qkv_specs, qkv_shapes = [], []
    for _ in range(3):
        for _, dil in PATTERNS:
            groups, per_group = dil // min(dil, CLASS_GROUP), min(dil, CLASS_GROUP)
            qkv_specs.append(pl.BlockSpec((1, groups, per_group, tm // dil, D_ATTN), cls))
            qkv_shapes.append(jax.ShapeDtypeStruct((bsz, groups, per_group, s // dil, D_ATTN), BF16))
    return pl.pallas_call(
        functools.partial(_in_proj_kernel, tm=tm, n_later=len(later_weights)),
        grid=(bsz, per_seq),
        in_specs=[
            pl.BlockSpec((1, tm, D_MODEL), row),
            pl.BlockSpec((1, D_MODEL), const),
            pl.BlockSpec((D_MODEL, D_IN), const, pipeline_mode=pl.Buffered(1)),
            pl.BlockSpec(perms.shape, lambda b, i: (0, 0, 0)),
        ] + later_specs,
        out_specs=[pl.BlockSpec((1, D_RNN // LANES, RNN_CHUNK, LANES), chunk)] * 2 + qkv_specs + later_specs,
        out_shape=[jax.ShapeDtypeStruct((bsz, D_RNN // LANES, s, LANES), F32)] * 2 + qkv_shapes + later_shapes,
        scratch_shapes=[pltpu.VMEM((tm, D_MODEL), BF16)],
        compiler_params=pltpu.CompilerParams(
            dimension_semantics=("arbitrary", "arbitrary"), vmem_limit_bytes=BIG_VMEM_LIMIT),
        name="in_proj",
    )(x3, g, w_in_bf, perms, *later_weights)


def _scan_block(a, b, reverse):
    n = a.shape[0]
    row = lax.broadcasted_iota(jnp.int32, a.shape, 0)
    s = 1
    while s < n:
        if reverse:
            ra = pltpu.roll(a, n - s, 0)
            rb = pltpu.roll(b, n - s, 0)
            m = row < n - s
        else:
            ra = pltpu.roll(a, s, 0)
            rb = pltpu.roll(b, s, 0)
            m = row >= s
        b = jnp.where(m, a * rb + b, b)
        a = jnp.where(m, a * ra, a)
        s *= 2
    return a, b


def _rnn_kernel(*refs, reverse, final, nchunks):
    if final:
        (xc_ref, hf_ref, gate_ref, wa_ref, wx_ref, ba_ref, bx_ref, lam_ref, gn_ref,
         out_ref, a_scr, b_scr, h_scr, carry) = refs
    else:
        (pa_ref, pb_ref, cur_ref, nx_ref, cw_ref, cb_ref, wa_ref, wx_ref, ba_ref, bx_ref, lam_ref,
         out_ref, xc_out_ref, a_scr, b_scr, carry) = refs
        h_scr = None

    step = pl.program_id(1)
    chunk = (nchunks - 1 - step) if reverse else step
    n_slab = D_RNN // LANES
    nv = STRIPE_ROWS
    sub = lax.broadcasted_iota(jnp.int32, (SUBLANES, LANES), 0)
    top, bot = SUBLANES - 1, 0

    @pl.when(step == 0)
    def _():
        carry[...] = jnp.zeros_like(carry)

    if final:
        xc = [xc_ref[0, j] for j in range(n_slab)]
    else:
        xc = []
        for j in range(n_slab):
            lanes = slice(j * LANES, (j + 1) * LANES)
            x = cur_ref[0, j].reshape(nv, SUBLANES, LANES)
            pa = jnp.where(chunk > 0, pa_ref[0, j], 0.0)
            pb = jnp.where(chunk > 0, pb_ref[0, j], 0.0)
            nx = jnp.where(chunk < nchunks - 1, nx_ref[0, j], 0.0)
            xm2 = pltpu.roll(jnp.where(sub == top, pa, x[nv - 2]), 1, 0)
            xm1 = pltpu.roll(jnp.where(sub == top, pb, x[nv - 1]), 1, 0)
            xp1 = pltpu.roll(jnp.where(sub == bot, nx, x[0]), SUBLANES - 1, 0)
            xext = jnp.concatenate([xm2[None], xm1[None], x, xp1[None]], axis=0)
            acc = cb_ref[:, lanes] + xext[0:nv] * cw_ref[0:1, lanes]
            for k in range(1, CONV_WIDTH):
                acc = acc + xext[k:k + nv] * cw_ref[k:k + 1, lanes]
            xc.append(acc.reshape(RNN_CHUNK, LANES))
            xc_out_ref[0, j] = xc[j]

    nlam = -lam_ref[...]
    softplus = jnp.maximum(nlam, 0.0) + jnp.log1p(jnp.exp(-jnp.abs(nlam)))
    half_coef = (0.5 * LRU_C) * softplus
    slabs_per_blk = GATE_BLK // LANES
    for jj in range(D_RNN // GATE_BLK):
        blk = slice(jj * GATE_BLK, (jj + 1) * GATE_BLK)
        xj = jnp.concatenate(xc[jj * slabs_per_blk:(jj + 1) * slabs_per_blk], axis=-1)
        xjb = xj.astype(BF16)
        za = jnp.dot(xjb, wa_ref[jj], preferred_element_type=F32) + ba_ref[:, blk]
        zx = jnp.dot(xjb, wx_ref[jj], preferred_element_type=F32) + bx_ref[:, blk]
        i = 0.5 * jnp.tanh(zx) + 0.5
        neg_log_a = half_coef[:, blk] * jnp.tanh(za) + half_coef[:, blk]
        a = jnp.exp2(neg_log_a * -LOG2E)
        u = jnp.tanh(neg_log_a) * (1.0 + a * a)
        root = jnp.where(u > 0.0, u * lax.rsqrt(u), 0.0)
        b = root * (i * xj)
        for t in range(slabs_per_blk):
            a_scr[jj * slabs_per_blk + t] = a[:, t * LANES:(t + 1) * LANES]
            b_scr[jj * slabs_per_blk + t] = b[:, t * LANES:(t + 1) * LANES]

    order = range(nv - 1, -1, -1) if reverse else range(nv)
    group = lambda v: slice(v * SUBLANES, (v + 1) * SUBLANES)

    h = [jnp.zeros((SUBLANES, LANES), F32)] * n_slab
    p = [jnp.ones((SUBLANES, LANES), F32)] * n_slab
    for v in order:
        for j in range(n_slab):
            av = a_scr[j, group(v), :]
            h[j] = av * h[j] + b_scr[j, group(v), :]
            p[j] = av * p[j]

    start = []
    for j in range(n_slab):
        pc, hc = _scan_block(p[j], h[j], reverse)
        prev = carry[j]
        if reverse:
            seed = jnp.broadcast_to(prev[bot:bot + 1, :], (SUBLANES, LANES))
            ends = hc + pc * seed
            start.append(jnp.where(sub == top, pltpu.roll(prev, SUBLANES - 1, 0),
                                   pltpu.roll(ends, SUBLANES - 1, 0)))
        else:
            seed = jnp.broadcast_to(prev[top:top + 1, :], (SUBLANES, LANES))
            ends = hc + pc * seed
            start.append(jnp.where(sub == bot, pltpu.roll(prev, 1, 0), pltpu.roll(ends, 1, 0)))
        carry[j] = ends

    dst = h_scr if final else None
    h = start
    for v in order:
        for j in range(n_slab):
            h[j] = a_scr[j, group(v), :] * h[j] + b_scr[j, group(v), :]
            if final:
                dst[j, group(v), :] = (hf_ref[0, j, group(v), :] + h[j]) * gate_ref[0, j, group(v), :]
            else:
                out_ref[0, j, group(v), :] = h[j]

    if final:
        ys = [h_scr[j] for j in range(n_slab)]
        ss = ys[0] * ys[0]
        for j in range(1, n_slab):
            ss = ss + ys[j] * ys[j]
        inv = lax.rsqrt(jnp.sum(ss, axis=-1, keepdims=True) * (1.0 / D_RNN) + EPS)
        for j in range(n_slab):
            lanes = slice(j * LANES, (j + 1) * LANES)
            h_scr[j] = ys[j] * inv * gn_ref[:, lanes]
        for st in range(SUBLANES):
            for j in range(n_slab):
                out_ref[0, st * nv:(st + 1) * nv, j * LANES:(j + 1) * LANES] = (
                    h_scr[j, pl.ds(st, nv, stride=SUBLANES), :].astype(BF16))


def _rnn_pass(x4, extra, params, reverse, final):
    bsz, n_slab, s, _ = x4.shape
    nchunks = s // RNN_CHUNK
    hb = RNN_CHUNK // SUBLANES
    nhalo = s // SUBLANES

    def cidx(c):
        return (nchunks - 1 - c) if reverse else c

    cur_map = lambda b, c: (b, 0, cidx(c), 0)
    const2 = lambda b, c: (0, 0)
    const3 = lambda b, c: (0, 0, 0)
    cw, cb, wa, wx, ba, bx, lam, gn = params
    halo = (1, n_slab, SUBLANES, LANES)
    full = (1, n_slab, RNN_CHUNK, LANES)
    gate_specs = [
        pl.BlockSpec((D_RNN // GATE_BLK, GATE_BLK, GATE_BLK), const3),
        pl.BlockSpec((D_RNN // GATE_BLK, GATE_BLK, GATE_BLK), const3),
        pl.BlockSpec((1, D_RNN), const2),
        pl.BlockSpec((1, D_RNN), const2),
        pl.BlockSpec((1, D_RNN), const2),
    ]
    slab_scr = pltpu.VMEM((n_slab, RNN_CHUNK, LANES), F32)
    carry_scr = pltpu.VMEM((n_slab, SUBLANES, LANES), F32)
    if final:
        in_specs = [pl.BlockSpec(full, cur_map)] * 3 + gate_specs + [pl.BlockSpec((1, D_RNN), const2)]
        args = [x4, *extra, wa, wx, ba, bx, lam, gn]
        scratch = [slab_scr, slab_scr, slab_scr, carry_scr]
        out_specs = pl.BlockSpec((1, RNN_CHUNK, D_RNN), lambda b, c: (b, cidx(c), 0))
        out_shape = jax.ShapeDtypeStruct((bsz, s, D_RNN), BF16)
    else:
        pa_map = lambda b, c: (b, 0, jnp.maximum(cidx(c) * hb - 2, 0), 0)
        pb_map = lambda b, c: (b, 0, jnp.maximum(cidx(c) * hb - 1, 0), 0)
        nx_map = lambda b, c: (b, 0, jnp.minimum((cidx(c) + 1) * hb, nhalo - 1), 0)
        in_specs = [pl.BlockSpec(halo, pa_map), pl.BlockSpec(halo, pb_map), pl.BlockSpec(full, cur_map),
                    pl.BlockSpec(halo, nx_map),
                    pl.BlockSpec((CONV_WIDTH, D_RNN), const2), pl.BlockSpec((1, D_RNN), const2)] + gate_specs
        args = [x4, x4, x4, x4, cw, cb, wa, wx, ba, bx, lam]
        scratch = [slab_scr, slab_scr, carry_scr]
        out_specs = [pl.BlockSpec(full, cur_map)] * 2
        out_shape = [jax.ShapeDtypeStruct(x4.shape, F32)] * 2
    return pl.pallas_call(
        functools.partial(_rnn_kernel, reverse=reverse, final=final, nchunks=nchunks),
        grid=(bsz, nchunks),
        in_specs=in_specs,
        out_specs=out_specs,
        out_shape=out_shape,
        scratch_shapes=scratch,
        compiler_params=pltpu.CompilerParams(
            dimension_semantics=("arbitrary", "arbitrary"), vmem_limit_bytes=VMEM_LIMIT),
        name="rnn_bwd" if reverse else "rnn_fwd",
    )(*args)


def _gate_weights(w):
    per = GATE_BLK // RNN_BLOCK
    w5 = (0.5 * w).reshape(D_RNN // GATE_BLK, per, RNN_BLOCK, 1, RNN_BLOCK)
    on_diag = jnp.asarray(np.eye(per, dtype=bool)).reshape(1, per, 1, per, 1)
    dense = jnp.where(on_diag, w5, 0.0)
    return dense.reshape(D_RNN // GATE_BLK, GATE_BLK, GATE_BLK).astype(BF16)


def _attn_kernel(q_ref, kp_ref, k_ref, kn_ref, vp_ref, v_ref, vn_ref, bias_ref, o_ref, lse_ref, kbuf, vbuf,
                 *, sub_len, tiles_per_class):
    i = pl.program_id(2)
    n_tiles = sub_len // Q_TILE
    n_classes = q_ref.shape[1]
    q_rows = tiles_per_class * Q_TILE
    for buf, before, own, after in ((kbuf, kp_ref, k_ref, kn_ref), (vbuf, vp_ref, v_ref, vn_ref)):
        for cl in range(n_classes):
            buf[cl, 0:HALF_STEPS, :] = before[0, cl]
            buf[cl, HALF_STEPS:HALF_STEPS + q_rows, :] = own[0, cl]
            buf[cl, HALF_STEPS + q_rows:, :] = after[0, cl]
    buf_origin = i * q_rows - HALF_STEPS
    lane = lax.broadcasted_iota(jnp.int32, (Q_TILE, LANES), 1)
    heads_per_slab = LANES // HEAD_DIM
    first_half = lane < HEAD_DIM
    for cl in range(n_classes):
        for t in range(tiles_per_class):
            rows = slice(t * Q_TILE, (t + 1) * Q_TILE)
            if n_classes == 1:
                out_rows = rows
            else:
                out_rows = pl.ds(t * Q_TILE * n_classes + cl, Q_TILE, stride=n_classes)
            q = q_ref[0, cl, rows, :]
            tile = i * tiles_per_class + t
            variant = jnp.where(tile == 0, FIRST, jnp.where(tile == n_tiles - 1, LAST, INTERIOR))
            start = jnp.clip(tile * Q_TILE - HALF_STEPS, 0, sub_len - K_WIN) - buf_origin
            start = pl.multiple_of(start, K_TILE)
            kk = kbuf[cl, pl.ds(start, K_WIN), :]
            vv = vbuf[cl, pl.ds(start, K_WIN), :]
            for p in range(D_ATTN // LANES):
                sl = slice(p * LANES, (p + 1) * LANES)
                qp, kp, vp = q[:, sl], kk[:, sl], vv[:, sl]
                zero = jnp.zeros_like(qp)
                q2 = jnp.concatenate([jnp.where(first_half, qp, zero), jnp.where(first_half, zero, qp)], axis=0)
                s = lax.dot_general(q2, kp, (((1,), (1,)), ((), ())), preferred_element_type=F32)
                h0 = p * heads_per_slab
                s = s + bias_ref[variant, h0:h0 + heads_per_slab].reshape(heads_per_slab * Q_TILE, K_WIN)
                m = jnp.max(s, axis=-1, keepdims=True)
                e = jnp.exp2(s - m)
                l = jnp.sum(e, axis=-1, keepdims=True)
                o2 = jnp.dot(e.astype(BF16), vp, preferred_element_type=F32)
                shape = (Q_TILE, LANES)
                pick = lambda a: jnp.where(first_half, jnp.broadcast_to(a[:Q_TILE], shape),
                                           jnp.broadcast_to(a[Q_TILE:], shape))
                l_slab = pick(l)
                o_ref[0, p, out_rows, :] = pick(o2) * (1.0 / l_slab)
                lse_ref[0, p, out_rows, :] = pick(m) + jnp.log(l_slab) * LOG2E


def _attn_pattern(qc, kc, vc, bias, pat):
    bsz, groups, n_classes, sub_len, _ = qc.shape
    tiles_per_class = TILES_PER_STEP // n_classes
    q_rows = tiles_per_class * Q_TILE
    n_slab = D_ATTN // LANES
    qmap = lambda b, g, i: (b, g, 0, i, 0)
    halo_per_step = q_rows // HALF_STEPS
    n_halo = sub_len // HALF_STEPS
    before = lambda b, g, i: (b, g, 0, jnp.maximum(i * halo_per_step - 1, 0), 0)
    after = lambda b, g, i: (b, g, 0, jnp.minimum((i + 1) * halo_per_step, n_halo - 1), 0)
    own_spec = pl.BlockSpec((1, None, n_classes, q_rows, D_ATTN), qmap)
    kv_specs = [pl.BlockSpec((1, None, n_classes, HALF_STEPS, D_ATTN), before), own_spec,
                pl.BlockSpec((1, None, n_classes, HALF_STEPS, D_ATTN), after)]
    in_specs = [own_spec] + kv_specs + kv_specs
    in_specs.append(pl.BlockSpec((None, 3, N_HEADS, Q_TILE, K_WIN), lambda b, g, i: (pat, 0, 0, 0, 0)))
    window_buf = pltpu.VMEM((n_classes, q_rows + 2 * HALF_STEPS, D_ATTN), BF16)
    out_spec = pl.BlockSpec((1, None, n_slab, n_classes * q_rows, LANES), qmap)
    return pl.pallas_call(
        functools.partial(_attn_kernel, sub_len=sub_len, tiles_per_class=tiles_per_class),
        grid=(bsz, groups, sub_len // q_rows),
        in_specs=in_specs,
        out_specs=[out_spec] * 2,
        out_shape=[jax.ShapeDtypeStruct((bsz, groups, n_slab, n_classes * sub_len, LANES), F32)] * 2,
        scratch_shapes=[window_buf, window_buf],
        compiler_params=pltpu.CompilerParams(
            dimension_semantics=("arbitrary",) * 3, vmem_limit_bytes=VMEM_LIMIT),
        name=f"attn_d{groups * n_classes}",
    )(qc, kc, kc, kc, vc, vc, vc, bias)


def _token_order(ref, scr, slab):
    groups = ref.shape[1]
    if groups == 1:
        return ref[0, 0, slab]
    for g in range(groups):
        scr[slab, pl.ds(g, OUT_TILE // groups, stride=groups), :] = ref[0, g, slab]
    return scr[slab]


def _out_mlp_kernel(x_ref, mr_ref, o1, o2, o3, l1, l2, l3, ga_ref, wo_ref, gm_ref, wu_ref, wd_ref,
                    gf_ref, out_ref, s_o3, s_l3):
    ys = []
    for slab in range(D_ATTN // LANES):
        a1 = _token_order(l1, None, slab)
        a2 = _token_order(l2, None, slab)
        a3 = _token_order(l3, s_l3, slab)
        mx = jnp.maximum(jnp.maximum(a1, a2), a3)
        e1, e2, e3 = jnp.exp2(a1 - mx), jnp.exp2(a2 - mx), jnp.exp2(a3 - mx)
        ys.append((e1 * _token_order(o1, None, slab) + e2 * _token_order(o2, None, slab)
                   + e3 * _token_order(o3, s_o3, slab)) / (e1 + e2 + e3))
    mix_attn = _rms(jnp.concatenate(ys, axis=-1), ga_ref[...]).astype(BF16)

    proj = jnp.dot(mr_ref[0], wo_ref[0:D_RNN, :], preferred_element_type=F32)
    proj = proj + jnp.dot(mix_attn, wo_ref[D_RNN:D_RNN + D_ATTN, :], preferred_element_type=F32)
    x1 = proj + x_ref[0]
    h = _rms(x1, gm_ref[...]).astype(BF16)
    ff = None
    for c in range(D_FF // FF_CHUNK):
        cols = slice(c * FF_CHUNK, (c + 1) * FF_CHUNK)
        z = jnp.dot(h, wu_ref[:, cols], preferred_element_type=F32)
        z = jnp.square(jnp.maximum(z, 0.0)).astype(BF16)
        d = jnp.dot(z, wd_ref[cols, :], preferred_element_type=F32)
        ff = d if ff is None else ff + d
    out_ref[0] = _rms(ff + x1, gf_ref[...])


def _out_mlp(x3, mix_rnn, os_, lses, ga, wo, gm, wu, wd, gf):
    bsz, s, _ = x3.shape
    tm = OUT_TILE
    row = lambda b, i: (b, i, 0)
    const = lambda b, i: (0, 0)
    once = pl.Buffered(1)
    n_slab = D_ATTN // LANES
    attn_specs = [pl.BlockSpec((1, o.shape[1], n_slab, tm // o.shape[1], LANES), lambda b, i: (b, 0, 0, i, 0))
                  for o in os_]
    in_specs = [pl.BlockSpec((1, tm, D_MODEL), row), pl.BlockSpec((1, tm, D_RNN), row)]
    in_specs += attn_specs + attn_specs
    in_specs += [
        pl.BlockSpec((1, D_ATTN), const),
        pl.BlockSpec((D_RNN + D_ATTN, D_MODEL), const, pipeline_mode=once),
        pl.BlockSpec((1, D_MODEL), const),
        pl.BlockSpec((D_MODEL, D_FF), const, pipeline_mode=once),
        pl.BlockSpec((D_FF, D_MODEL), const, pipeline_mode=once),
        pl.BlockSpec((1, D_MODEL), const),
    ]
    return pl.pallas_call(
        _out_mlp_kernel,
        grid=(bsz, s // tm),
        in_specs=in_specs,
        out_specs=pl.BlockSpec((1, tm, D_MODEL), row),
        out_shape=jax.ShapeDtypeStruct((bsz, s, D_MODEL), F32),
        scratch_shapes=[pltpu.VMEM((n_slab, tm, LANES), F32)] * 2,
        compiler_params=pltpu.CompilerParams(
            dimension_semantics=("arbitrary", "arbitrary"), vmem_limit_bytes=BIG_VMEM_LIMIT),
        name="out_mlp",
    )(x3, mix_rnn, *os_, *lses, ga, wo, gm, wu, wd, gf)


def kernel(x, attn_norm_g, w_in, conv_w, conv_b, lru_wa_fwd, lru_ba_fwd, lru_wx_fwd, lru_bx_fwd, lru_lam_fwd, lru_wa_bwd, lru_ba_bwd, lru_wx_bwd, lru_bx_bwd, lru_lam_bwd, rel_bias, norm_rnn_g, norm_attn_g, w_out, mlp_norm_g, w_up, w_down, final_norm_g):
    depth = w_in.shape[0]
    assert depth == 1, "the final RMSNorm is fused into the single layer's last call"
    l = 0
    n_pat = len(PATTERNS)
    row = lambda v: v.reshape(1, -1)
    bias = _bias_tables(rel_bias)
    proj = _in_proj(x, row(attn_norm_g[l]), w_in[l].astype(BF16), (w_out[l], w_up[l], w_down[l]), tm=IN_TILE)
    xr, gate = proj[0], proj[1]
    qs, ks, vs = (proj[2 + a * n_pat:2 + (a + 1) * n_pat] for a in range(3))
    w_out_bf, w_up_bf, w_down_bf = proj[2 + 3 * n_pat:]
    half = lambda v: row(0.5 * v)
    fwd = (conv_w[l], row(conv_b[l]), _gate_weights(lru_wa_fwd[l]), _gate_weights(lru_wx_fwd[l]),
           half(lru_ba_fwd[l]), half(lru_bx_fwd[l]), row(lru_lam_fwd[l]), None)
    bwd = (conv_w[l], row(conv_b[l]), _gate_weights(lru_wa_bwd[l]), _gate_weights(lru_wx_bwd[l]),
           half(lru_ba_bwd[l]), half(lru_bx_bwd[l]), row(lru_lam_bwd[l]), row(norm_rnn_g[l]))
    h_f, xc = _rnn_pass(xr, (), fwd, reverse=False, final=False)
    mix_rnn = _rnn_pass(xc, (h_f, gate), bwd, reverse=True, final=True)
    os_, lses = [], []
    for pat in range(n_pat):
        o, lse = _attn_pattern(qs[pat], ks[pat], vs[pat], bias, pat)
        os_.append(o)
        lses.append(lse)
    return _out_mlp(x, mix_rnn, os_, lses, row(norm_attn_g[l]), w_out_bf,
                    row(mlp_norm_g[l]), w_up_bf, w_down_bf, row(final_norm_g))
```

```python
import functools
import math

import numpy as np
import jax
import jax.numpy as jnp
from jax import lax
from jax.experimental import pallas as pl
from jax.experimental.pallas import tpu as pltpu

F32 = jnp.float32
BF16 = jnp.bfloat16

D_MODEL = 1024
D_RNN = 512
N_RNN_BLOCKS = 8
RNN_BLOCK = D_RNN // N_RNN_BLOCKS
CONV_WIDTH = 4
CONV_LEFT = 2
LRU_C = 8.0
N_HEADS = 8
HEAD_DIM = 64
D_ATTN = N_HEADS * HEAD_DIM
PATTERNS = ((128, 1), (512, 4), (2048, 16))
HALF_STEPS = 64
N_BUCKETS = 32
MAX_DISTANCE = 1024
D_IN = 2 * D_RNN + 3 * D_ATTN
D_FF = 4 * D_MODEL
EPS = 1e-6
NEG_INF = -1e30

LANES = 128
SUBLANES = 8
MXU_DIM = 256

Q_TILE = 128
K_TILE = 64
K_WIN = Q_TILE + 2 * HALF_STEPS
TILES_PER_STEP = 16
CLASS_GROUP = 4
LOG2E = math.log2(math.e)
INTERIOR, FIRST, LAST = 0, 1, 2
GATE_BLK = MXU_DIM

IN_TILE = 1024
RNN_CHUNK = 1024
STRIPE_ROWS = RNN_CHUNK // SUBLANES
OUT_TILE = 512
FF_CHUNK = 1024
assert RNN_CHUNK % IN_TILE == 0 and (CONV_WIDTH, CONV_LEFT) == (4, 2)

VMEM_LIMIT = 48 * 1024 * 1024
BIG_VMEM_LIMIT = 56 * 1024 * 1024
IN_PROJ_VMEM_LIMIT = 60 * 1024 * 1024


def _rms(x, g):
    ms = jnp.mean(x * x, axis=-1, keepdims=True)
    return x * lax.rsqrt(ms + EPS) * g


def _t5_bucket_np(rel):
    nb = N_BUCKETS // 2
    max_exact = nb // 2
    ret = np.where(rel > 0, nb, 0)
    n = np.abs(rel)
    nf = np.maximum(n, 1).astype(np.float32)
    large = max_exact + (np.log(nf / np.float32(max_exact)) / np.float32(math.log(MAX_DISTANCE / max_exact))
                         * np.float32(nb - max_exact)).astype(np.int32)
    large = np.minimum(large, nb - 1)
    return ret + np.where(n < max_exact, n, large)


def _bucket_index_rows():
    step = np.arange(K_WIN) - HALF_STEPS
    rows = []
    for _, dil in PATTERNS:
        b = _t5_bucket_np((step * dil).astype(np.int32))
        rows.append(np.where(np.abs(step) <= HALF_STEPS, b, -1))
    return np.stack(rows, axis=0).astype(np.int32)[:, None, :]


def _bias_table_kernel(idx_ref, rb_ref, out_ref):
    idx = idx_ref[0]
    col = lax.broadcasted_iota(jnp.int32, (Q_TILE, K_WIN), 1)
    hit = [idx == b for b in range(N_BUCKETS)]
    for h in range(N_HEADS):
        first_row = jnp.full(idx.shape, NEG_INF, F32)
        for b in range(N_BUCKETS):
            first_row = jnp.where(hit[b], rb_ref[b, h] * LOG2E, first_row)
        acc = pltpu.roll(jnp.broadcast_to(first_row, (Q_TILE, K_WIN)), 0, 1, stride=1, stride_axis=0)
        out_ref[0, INTERIOR, h] = acc
        out_ref[0, FIRST, h] = jnp.where(col < K_WIN - HALF_STEPS, pltpu.roll(acc, K_WIN - HALF_STEPS, 1), NEG_INF)
        out_ref[0, LAST, h] = jnp.where(col >= HALF_STEPS, pltpu.roll(acc, HALF_STEPS, 1), NEG_INF)


def _bias_tables(rel_bias):
    idx = jnp.asarray(_bucket_index_rows())
    n_pat = idx.shape[0]
    return pl.pallas_call(
        _bias_table_kernel,
        grid=(n_pat,),
        in_specs=[
            pl.BlockSpec((1, 1, K_WIN), lambda g: (g, 0, 0)),
            pl.BlockSpec(memory_space=pltpu.SMEM),
        ],
        out_specs=pl.BlockSpec((1, 3, N_HEADS, Q_TILE, K_WIN), lambda g: (g, 0, 0, 0, 0)),
        out_shape=jax.ShapeDtypeStruct((n_pat, 3, N_HEADS, Q_TILE, K_WIN), F32),
        name="bias_table",
    )(idx, rel_bias)


def _stripe_store(ref, p, part):
    stripes = IN_TILE // STRIPE_ROWS
    for st in range(stripes):
        for j in range(D_RNN // LANES):
            ref[0, j, pl.ds(part * stripes + st, STRIPE_ROWS, stride=SUBLANES), :] = (
                p[st * STRIPE_ROWS:(st + 1) * STRIPE_ROWS, j * LANES:(j + 1) * LANES])


PERM_ROWS = MXU_DIM


def _class_permutations():
    mats = []
    for _, dil in PATTERNS:
        if dil == 1:
            continue
        per = PERM_ROWS // dil
        p = np.zeros((PERM_ROWS, PERM_ROWS), np.float32)
        for r in range(dil):
            for m in range(per):
                p[r * per + m, m * dil + r] = 1.0
        mats.append(p)
    return np.stack(mats)


def _in_proj_kernel(x_ref, g_ref, w_ref, perm_ref, *rest, tm, n_later):
    n_pat = len(PATTERNS)
    later_in, rest = rest[:n_later], rest[n_later:]
    xr_ref, gate_ref, rest = rest[0], rest[1], rest[2:]
    outs = [rest[a * n_pat:(a + 1) * n_pat] for a in range(3)]
    later_out = rest[3 * n_pat:3 * n_pat + n_later]
    h_scr = rest[3 * n_pat + n_later]
    for src, dst in zip(later_in, later_out):
        dst[...] = src[...].astype(BF16)
    w_scr = rest[3 * n_pat + n_later + 1]

    @pl.when((pl.program_id(0) == 0) & (pl.program_id(1) == 0))
    def _():
        w_scr[...] = w_ref[...].astype(BF16)

    h_scr[...] = _rms(x_ref[0], g_ref[...]).astype(BF16)

    def seg(lo, width):
        return jnp.dot(h_scr[...], w_scr[:, lo:lo + width], preferred_element_type=F32)

    for a, scale in enumerate((HEAD_DIM ** -0.5 * LOG2E, 1.0, 1.0)):
        p = seg(2 * D_RNN + a * D_ATTN, D_ATTN)
        if scale != 1.0:
            p = p * scale
        pb = p.astype(BF16)
        dilated = 0
        for (_, dil), o_ref in zip(PATTERNS, outs[a]):
            if dil == 1:
                o_ref[0, 0, 0] = pb
                continue
            per = PERM_ROWS // dil
            for blk in range(tm // PERM_ROWS):
                y = jnp.dot(perm_ref[dilated], pb[blk * PERM_ROWS:(blk + 1) * PERM_ROWS, :],
                            preferred_element_type=F32).astype(BF16)
                groups = o_ref.shape[1]
                for r in range(dil):
                    o_ref[0, r % groups, r // groups, blk * per:(blk + 1) * per, :] = y[r * per:(r + 1) * per, :]
            dilated += 1
    g = seg(D_RNN, D_RNN)
    c = math.sqrt(2.0 / math.pi)
    half_g = 0.5 * g
    gelu = half_g + half_g * jnp.tanh(g * (c + (c * 0.044715) * (g * g)))
    part = pl.program_id(1) % (RNN_CHUNK // IN_TILE)
    _stripe_store(gate_ref, gelu, part)
    _stripe_store(xr_ref, seg(0, D_RNN), part)


def _in_proj(x3, g, w_in_bf, later_weights, tm):
    assert tm == IN_TILE
    bsz, s, _ = x3.shape
    per_seq = s // tm
    n_steps = bsz * per_seq
    row_slice = lambda b, i: (b * per_seq + i, 0)
    later_specs = [pl.BlockSpec((w.shape[0] // n_steps, w.shape[1]), row_slice) for w in later_weights]
    later_shapes = [jax.ShapeDtypeStruct(w.shape, BF16) for w in later_weights]
    row = lambda b, i: (b, i, 0)
    cls = lambda b, i: (b, 0, 0, i, 0)
    chunk = lambda b, i: (b, 0, i // (RNN_CHUNK // IN_TILE), 0)
    const = lambda b, i: (0, 0)
    perms = jnp.asarray(_class_permutations(), BF16)
    qkv_specs, qkv_shapes = [], []
    for _ in range(3):
        for _, dil in PATTERNS:
            groups, per_group = dil // min(dil, CLASS_GROUP), min(dil, CLASS_GROUP)
            qkv_specs.append(pl.BlockSpec((1, groups, per_group, tm // dil, D_ATTN), cls))
            qkv_shapes.append(jax.ShapeDtypeStruct((bsz, groups, per_group, s // dil, D_ATTN), BF16))
    return pl.pallas_call(
        functools.partial(_in_proj_kernel, tm=tm, n_later=len(later_weights)),
        grid=(bsz, per_seq),
        in_specs=[
            pl.BlockSpec((1, tm, D_MODEL), row),
            pl.BlockSpec((1, D_MODEL), const),
            pl.BlockSpec((D_MODEL, D_IN), const, pipeline_mode=pl.Buffered(1)),
            pl.BlockSpec(perms.shape, lambda b, i: (0, 0, 0)),
        ] + later_specs,
        out_specs=[pl.BlockSpec((1, D_RNN // LANES, RNN_CHUNK, LANES), chunk)] * 2 + qkv_specs + later_specs,
        out_shape=[jax.ShapeDtypeStruct((bsz, D_RNN // LANES, s, LANES), F32)] * 2 + qkv_shapes + later_shapes,
        scratch_shapes=[pltpu.VMEM((tm, D_MODEL), BF16), pltpu.VMEM((D_MODEL, D_IN), BF16)],
        compiler_params=pltpu.CompilerParams(
            dimension_semantics=("arbitrary", "arbitrary"), vmem_limit_bytes=IN_PROJ_VMEM_LIMIT),
        name="in_proj",
    )(x3, g, w_in_bf, perms, *later_weights)


def _scan_block(a, b, reverse):
    n = a.shape[0]
    row = lax.broadcasted_iota(jnp.int32, a.shape, 0)
    s = 1
    while s < n:
        if reverse:
            ra = pltpu.roll(a, n - s, 0)
            rb = pltpu.roll(b, n - s, 0)
            m = row < n - s
        else:
            ra = pltpu.roll(a, s, 0)
            rb = pltpu.roll(b, s, 0)
            m = row >= s
        b = jnp.where(m, a * rb + b, b)
        a = jnp.where(m, a * ra, a)
        s *= 2
    return a, b


def _rnn_kernel(*refs, reverse, final, nchunks):
    if final:
        (xc_ref, hf_ref, gate_ref, wa_ref, wx_ref, ba_ref, bx_ref, lam_ref, gn_ref,
         out_ref, a_scr, b_scr, h_scr, carry) = refs
    else:
        (pa_ref, pb_ref, cur_ref, nx_ref, cw_ref, cb_ref, wa_ref, wx_ref, ba_ref, bx_ref, lam_ref,
         out_ref, xc_out_ref, a_scr, b_scr, carry) = refs
        h_scr = None

    step = pl.program_id(1)
    chunk = (nchunks - 1 - step) if reverse else step
    n_slab = D_RNN // LANES
    nv = STRIPE_ROWS
    sub = lax.broadcasted_iota(jnp.int32, (SUBLANES, LANES), 0)
    top, bot = SUBLANES - 1, 0

    @pl.when(step == 0)
    def _():
        carry[...] = jnp.zeros_like(carry)

    if final:
        xc = [xc_ref[0, j] for j in range(n_slab)]
    else:
        xc = []
        for j in range(n_slab):
            lanes = slice(j * LANES, (j + 1) * LANES)
            x = cur_ref[0, j].reshape(nv, SUBLANES, LANES)
            pa = jnp.where(chunk > 0, pa_ref[0, j], 0.0)
            pb = jnp.where(chunk > 0, pb_ref[0, j], 0.0)
            nx = jnp.where(chunk < nchunks - 1, nx_ref[0, j], 0.0)
            xm2 = pltpu.roll(jnp.where(sub == top, pa, x[nv - 2]), 1, 0)
            xm1 = pltpu.roll(jnp.where(sub == top, pb, x[nv - 1]), 1, 0)
            xp1 = pltpu.roll(jnp.where(sub == bot, nx, x[0]), SUBLANES - 1, 0)
            xext = jnp.concatenate([xm2[None], xm1[None], x, xp1[None]], axis=0)
            acc = cb_ref[:, lanes] + xext[0:nv] * cw_ref[0:1, lanes]
            for k in range(1, CONV_WIDTH):
                acc = acc + xext[k:k + nv] * cw_ref[k:k + 1, lanes]
            xc.append(acc.reshape(RNN_CHUNK, LANES))
            xc_out_ref[0, j] = xc[j]

    nlam = -lam_ref[...]
    softplus = jnp.maximum(nlam, 0.0) + jnp.log1p(jnp.exp(-jnp.abs(nlam)))
    half_coef = (0.5 * LRU_C) * softplus
    slabs_per_blk = GATE_BLK // LANES
    for jj in range(D_RNN // GATE_BLK):
        blk = slice(jj * GATE_BLK, (jj + 1) * GATE_BLK)
        xj = jnp.concatenate(xc[jj * slabs_per_blk:(jj + 1) * slabs_per_blk], axis=-1)
        xjb = xj.astype(BF16)
        za = jnp.dot(xjb, wa_ref[jj], preferred_element_type=F32) + ba_ref[:, blk]
        zx = jnp.dot(xjb, wx_ref[jj], preferred_element_type=F32) + bx_ref[:, blk]
        i = 0.5 * jnp.tanh(zx) + 0.5
        neg_log_a = half_coef[:, blk] * jnp.tanh(za) + half_coef[:, blk]
        a = jnp.exp2(neg_log_a * -LOG2E)
        u = jnp.tanh(neg_log_a) * (1.0 + a * a)
        root = jnp.where(u > 0.0, u * lax.rsqrt(u), 0.0)
        b = root * (i * xj)
        for t in range(slabs_per_blk):
            a_scr[jj * slabs_per_blk + t] = a[:, t * LANES:(t + 1) * LANES]
            b_scr[jj * slabs_per_blk + t] = b[:, t * LANES:(t + 1) * LANES]

    order = range(nv - 1, -1, -1) if reverse else range(nv)
    group = lambda v: slice(v * SUBLANES, (v + 1) * SUBLANES)

    h = [jnp.zeros((SUBLANES, LANES), F32)] * n_slab
    p = [jnp.ones((SUBLANES, LANES), F32)] * n_slab
    for v in order:
        for j in range(n_slab):
            av = a_scr[j, group(v), :]
            h[j] = av * h[j] + b_scr[j, group(v), :]
            p[j] = av * p[j]

    start = []
    for j in range(n_slab):
        pc, hc = _scan_block(p[j], h[j], reverse)
        prev = carry[j]
        if reverse:
            seed = jnp.broadcast_to(prev[bot:bot + 1, :], (SUBLANES, LANES))
            ends = hc + pc * seed
            start.append(jnp.where(sub == top, pltpu.roll(prev, SUBLANES - 1, 0),
                                   pltpu.roll(ends, SUBLANES - 1, 0)))
        else:
            seed = jnp.broadcast_to(prev[top:top + 1, :], (SUBLANES, LANES))
            ends = hc + pc * seed
            start.append(jnp.where(sub == bot, pltpu.roll(prev, 1, 0), pltpu.roll(ends, 1, 0)))
        carry[j] = ends

    dst = h_scr if final else None
    h = start
    for v in order:
        for j in range(n_slab):
            h[j] = a_scr[j, group(v), :] * h[j] + b_scr[j, group(v), :]
            if final:
                dst[j, group(v), :] = (hf_ref[0, j, group(v), :] + h[j]) * gate_ref[0, j, group(v), :]
            else:
                out_ref[0, j, group(v), :] = h[j]

    if final:
        ys = [h_scr[j] for j in range(n_slab)]
        ss = ys[0] * ys[0]
        for j in range(1, n_slab):
            ss = ss + ys[j] * ys[j]
        inv = lax.rsqrt(jnp.sum(ss, axis=-1, keepdims=True) * (1.0 / D_RNN) + EPS)
        for j in range(n_slab):
            lanes = slice(j * LANES, (j + 1) * LANES)
            h_scr[j] = ys[j] * inv * gn_ref[:, lanes]
        for st in range(SUBLANES):
            for j in range(n_slab):
                out_ref[0, st * nv:(st + 1) * nv, j * LANES:(j + 1) * LANES] = (
                    h_scr[j, pl.ds(st, nv, stride=SUBLANES), :].astype(BF16))


def _rnn_pass(x4, extra, params, reverse, final):
    bsz, n_slab, s, _ = x4.shape
    nchunks = s // RNN_CHUNK
    hb = RNN_CHUNK // SUBLANES
    nhalo = s // SUBLANES

    def cidx(c):
        return (nchunks - 1 - c) if reverse else c

    cur_map = lambda b, c: (b, 0, cidx(c), 0)
    const2 = lambda b, c: (0, 0)
    const3 = lambda b, c: (0, 0, 0)
    cw, cb, wa, wx, ba, bx, lam, gn = params
    halo = (1, n_slab, SUBLANES, LANES)
    full = (1, n_slab, RNN_CHUNK, LANES)
    gate_specs = [
        pl.BlockSpec((D_RNN // GATE_BLK, GATE_BLK, GATE_BLK), const3),
        pl.BlockSpec((D_RNN // GATE_BLK, GATE_BLK, GATE_BLK), const3),
        pl.BlockSpec((1, D_RNN), const2),
        pl.BlockSpec((1, D_RNN), const2),
        pl.BlockSpec((1, D_RNN), const2),
    ]
    slab_scr = pltpu.VMEM((n_slab, RNN_CHUNK, LANES), F32)
    carry_scr = pltpu.VMEM((n_slab, SUBLANES, LANES), F32)
    if final:
        in_specs = [pl.BlockSpec(full, cur_map)] * 3 + gate_specs + [pl.BlockSpec((1, D_RNN), const2)]
        args = [x4, *extra, wa, wx, ba, bx, lam, gn]
        scratch = [slab_scr, slab_scr, slab_scr, carry_scr]
        out_specs = pl.BlockSpec((1, RNN_CHUNK, D_RNN), lambda b, c: (b, cidx(c), 0))
        out_shape = jax.ShapeDtypeStruct((bsz, s, D_RNN), BF16)
    else:
        pa_map = lambda b, c: (b, 0, jnp.maximum(cidx(c) * hb - 2, 0), 0)
        pb_map = lambda b, c: (b, 0, jnp.maximum(cidx(c) * hb - 1, 0), 0)
        nx_map = lambda b, c: (b, 0, jnp.minimum((cidx(c) + 1) * hb, nhalo - 1), 0)
        in_specs = [pl.BlockSpec(halo, pa_map), pl.BlockSpec(halo, pb_map), pl.BlockSpec(full, cur_map),
                    pl.BlockSpec(halo, nx_map),
                    pl.BlockSpec((CONV_WIDTH, D_RNN), const2), pl.BlockSpec((1, D_RNN), const2)] + gate_specs
        args = [x4, x4, x4, x4, cw, cb, wa, wx, ba, bx, lam]
        scratch = [slab_scr, slab_scr, carry_scr]
        out_specs = [pl.BlockSpec(full, cur_map)] * 2
        out_shape = [jax.ShapeDtypeStruct(x4.shape, F32)] * 2
    return pl.pallas_call(
        functools.partial(_rnn_kernel, reverse=reverse, final=final, nchunks=nchunks),
        grid=(bsz, nchunks),
        in_specs=in_specs,
        out_specs=out_specs,
        out_shape=out_shape,
        scratch_shapes=scratch,
        compiler_params=pltpu.CompilerParams(
            dimension_semantics=("arbitrary", "arbitrary"), vmem_limit_bytes=VMEM_LIMIT),
        name="rnn_bwd" if reverse else "rnn_fwd",
    )(*args)


def _gate_weights(w):
    per = GATE_BLK // RNN_BLOCK
    w5 = (0.5 * w).reshape(D_RNN // GATE_BLK, per, RNN_BLOCK, 1, RNN_BLOCK)
    on_diag = jnp.asarray(np.eye(per, dtype=bool)).reshape(1, per, 1, per, 1)
    dense = jnp.where(on_diag, w5, 0.0)
    return dense.reshape(D_RNN // GATE_BLK, GATE_BLK, GATE_BLK).astype(BF16)


def _attn_kernel(q_ref, kp_ref, k_ref, kn_ref, vp_ref, v_ref, vn_ref, bias_ref, o_ref, lse_ref, kbuf, vbuf,
                 *, sub_len, tiles_per_class):
    i = pl.program_id(2)
    n_tiles = sub_len // Q_TILE
    n_classes = q_ref.shape[1]
    q_rows = tiles_per_class * Q_TILE
    for buf, before, own, after in ((kbuf, kp_ref, k_ref, kn_ref), (vbuf, vp_ref, v_ref, vn_ref)):
        for cl in range(n_classes):
            buf[cl, 0:HALF_STEPS, :] = before[0, cl]
            buf[cl, HALF_STEPS:HALF_STEPS + q_rows, :] = own[0, cl]
            buf[cl, HALF_STEPS + q_rows:, :] = after[0, cl]
    buf_origin = i * q_rows - HALF_STEPS
    lane = lax.broadcasted_iota(jnp.int32, (Q_TILE, LANES), 1)
    heads_per_slab = LANES // HEAD_DIM
    first_half = lane < HEAD_DIM
    for cl in range(n_classes):
        for t in range(tiles_per_class):
            rows = slice(t * Q_TILE, (t + 1) * Q_TILE)
            if n_classes == 1:
                out_rows = rows
            else:
                out_rows = pl.ds(t * Q_TILE * n_classes + cl, Q_TILE, stride=n_classes)
            q = q_ref[0, cl, rows, :]
            tile = i * tiles_per_class + t
            variant = jnp.where(tile == 0, FIRST, jnp.where(tile == n_tiles - 1, LAST, INTERIOR))
            start = jnp.clip(tile * Q_TILE - HALF_STEPS, 0, sub_len - K_WIN) - buf_origin
            start = pl.multiple_of(start, K_TILE)
            kk = kbuf[cl, pl.ds(start, K_WIN), :]
            vv = vbuf[cl, pl.ds(start, K_WIN), :]
            for p in range(D_ATTN // LANES):
                sl = slice(p * LANES, (p + 1) * LANES)
                qp, kp, vp = q[:, sl], kk[:, sl], vv[:, sl]
                zero = jnp.zeros_like(qp)
                q2 = jnp.concatenate([jnp.where(first_half, qp, zero), jnp.where(first_half, zero, qp)], axis=0)
                s = lax.dot_general(q2, kp, (((1,), (1,)), ((), ())), preferred_element_type=F32)
                h0 = p * heads_per_slab
                s = s + bias_ref[variant, h0:h0 + heads_per_slab].reshape(heads_per_slab * Q_TILE, K_WIN)
                m = jnp.max(s, axis=-1, keepdims=True)
                e = jnp.exp2(s - m)
                l = jnp.sum(e, axis=-1, keepdims=True)
                o2 = jnp.dot(e.astype(BF16), vp, preferred_element_type=F32)
                shape = (Q_TILE, LANES)
                pick = lambda a: jnp.where(first_half, jnp.broadcast_to(a[:Q_TILE], shape),
                                           jnp.broadcast_to(a[Q_TILE:], shape))
                l_slab = pick(l)
                o_ref[0, p, out_rows, :] = pick(o2) * (1.0 / l_slab)
                lse_ref[0, p, out_rows, :] = pick(m) + jnp.log(l_slab) * LOG2E


def _attn_pattern(qc, kc, vc, bias, pat):
    bsz, groups, n_classes, sub_len, _ = qc.shape
    tiles_per_class = TILES_PER_STEP // n_classes
    q_rows = tiles_per_class * Q_TILE
    n_slab = D_ATTN // LANES
    qmap = lambda b, g, i: (b, g, 0, i, 0)
    halo_per_step = q_rows // HALF_STEPS
    n_halo = sub_len // HALF_STEPS
    before = lambda b, g, i: (b, g, 0, jnp.maximum(i * halo_per_step - 1, 0), 0)
    after = lambda b, g, i: (b, g, 0, jnp.minimum((i + 1) * halo_per_step, n_halo - 1), 0)
    own_spec = pl.BlockSpec((1, None, n_classes, q_rows, D_ATTN), qmap)
    kv_specs = [pl.BlockSpec((1, None, n_classes, HALF_STEPS, D_ATTN), before), own_spec,
                pl.BlockSpec((1, None, n_classes, HALF_STEPS, D_ATTN), after)]
    in_specs = [own_spec] + kv_specs + kv_specs
    in_specs.append(pl.BlockSpec((None, 3, N_HEADS, Q_TILE, K_WIN), lambda b, g, i: (pat, 0, 0, 0, 0)))
    window_buf = pltpu.VMEM((n_classes, q_rows + 2 * HALF_STEPS, D_ATTN), BF16)
    out_spec = pl.BlockSpec((1, None, n_slab, n_classes * q_rows, LANES), qmap)
    return pl.pallas_call(
        functools.partial(_attn_kernel, sub_len=sub_len, tiles_per_class=tiles_per_class),
        grid=(bsz, groups, sub_len // q_rows),
        in_specs=in_specs,
        out_specs=[out_spec] * 2,
        out_shape=[jax.ShapeDtypeStruct((bsz, groups, n_slab, n_classes * sub_len, LANES), F32)] * 2,
        scratch_shapes=[window_buf, window_buf],
        compiler_params=pltpu.CompilerParams(
            dimension_semantics=("arbitrary",) * 3, vmem_limit_bytes=VMEM_LIMIT),
        name=f"attn_d{groups * n_classes}",
    )(qc, kc, kc, kc, vc, vc, vc, bias)


def _token_order(ref, scr, slab):
    groups = ref.shape[1]
    if groups == 1:
        return ref[0, 0, slab]
    for g in range(groups):
        scr[slab, pl.ds(g, OUT_TILE // groups, stride=groups), :] = ref[0, g, slab]
    return scr[slab]


def _out_mlp_kernel(x_ref, mr_ref, o1, o2, o3, l1, l2, l3, ga_ref, wo_ref, gm_ref, wu_ref, wd_ref,
                    gf_ref, out_ref, s_o3, s_l3):
    ys = []
    for slab in range(D_ATTN // LANES):
        a1 = _token_order(l1, None, slab)
        a2 = _token_order(l2, None, slab)
        a3 = _token_order(l3, s_l3, slab)
        mx = jnp.maximum(jnp.maximum(a1, a2), a3)
        e1, e2, e3 = jnp.exp2(a1 - mx), jnp.exp2(a2 - mx), jnp.exp2(a3 - mx)
        ys.append((e1 * _token_order(o1, None, slab) + e2 * _token_order(o2, None, slab)
                   + e3 * _token_order(o3, s_o3, slab)) / (e1 + e2 + e3))
    mix_attn = _rms(jnp.concatenate(ys, axis=-1), ga_ref[...]).astype(BF16)

    proj = jnp.dot(mr_ref[0], wo_ref[0:D_RNN, :], preferred_element_type=F32)
    proj = proj + jnp.dot(mix_attn, wo_ref[D_RNN:D_RNN + D_ATTN, :], preferred_element_type=F32)
    x1 = proj + x_ref[0]
    h = _rms(x1, gm_ref[...]).astype(BF16)
    ff = None
    for c in range(D_FF // FF_CHUNK):
        cols = slice(c * FF_CHUNK, (c + 1) * FF_CHUNK)
        z = jnp.dot(h, wu_ref[:, cols], preferred_element_type=F32)
        z = jnp.square(jnp.maximum(z, 0.0)).astype(BF16)
        d = jnp.dot(z, wd_ref[cols, :], preferred_element_type=F32)
        ff = d if ff is None else ff + d
    out_ref[0] = _rms(ff + x1, gf_ref[...])


def _out_mlp(x3, mix_rnn, os_, lses, ga, wo, gm, wu, wd, gf):
    bsz, s, _ = x3.shape
    tm = OUT_TILE
    row = lambda b, i: (b, i, 0)
    const = lambda b, i: (0, 0)
    once = pl.Buffered(1)
    n_slab = D_ATTN // LANES
    attn_specs = [pl.BlockSpec((1, o.shape[1], n_slab, tm // o.shape[1], LANES), lambda b, i: (b, 0, 0, i, 0))
                  for o in os_]
    in_specs = [pl.BlockSpec((1, tm, D_MODEL), row), pl.BlockSpec((1, tm, D_RNN), row)]
    in_specs += attn_specs + attn_specs
    in_specs += [
        pl.BlockSpec((1, D_ATTN), const),
        pl.BlockSpec((D_RNN + D_ATTN, D_MODEL), const, pipeline_mode=once),
        pl.BlockSpec((1, D_MODEL), const),
        pl.BlockSpec((D_MODEL, D_FF), const, pipeline_mode=once),
        pl.BlockSpec((D_FF, D_MODEL), const, pipeline_mode=once),
        pl.BlockSpec((1, D_MODEL), const),
    ]
    return pl.pallas_call(
        _out_mlp_kernel,
        grid=(bsz, s // tm),
        in_specs=in_specs,
        out_specs=pl.BlockSpec((1, tm, D_MODEL), row),
        out_shape=jax.ShapeDtypeStruct((bsz, s, D_MODEL), F32),
        scratch_shapes=[pltpu.VMEM((n_slab, tm, LANES), F32)] * 2,
        compiler_params=pltpu.CompilerParams(
            dimension_semantics=("arbitrary", "arbitrary"), vmem_limit_bytes=BIG_VMEM_LIMIT),
        name="out_mlp",
    )(x3, mix_rnn, *os_, *lses, ga, wo, gm, wu, wd, gf)


def kernel(x, attn_norm_g, w_in, conv_w, conv_b, lru_wa_fwd, lru_ba_fwd, lru_wx_fwd, lru_bx_fwd, lru_lam_fwd, lru_wa_bwd, lru_ba_bwd, lru_wx_bwd, lru_bx_bwd, lru_lam_bwd, rel_bias, norm_rnn_g, norm_attn_g, w_out, mlp_norm_g, w_up, w_down, final_norm_g):
    depth = w_in.shape[0]
    assert depth == 1, "the final RMSNorm is fused into the single layer's last call"
    l = 0
    n_pat = len(PATTERNS)
    row = lambda v: v.reshape(1, -1)
    bias = _bias_tables(rel_bias)
    proj = _in_proj(x, row(attn_norm_g[l]), w_in[l], (w_out[l], w_up[l], w_down[l]), tm=IN_TILE)
    xr, gate = proj[0], proj[1]
    qs, ks, vs = (proj[2 + a * n_pat:2 + (a + 1) * n_pat] for a in range(3))
    w_out_bf, w_up_bf, w_down_bf = proj[2 + 3 * n_pat:]
    half = lambda v: row(0.5 * v)
    fwd = (conv_w[l], row(conv_b[l]), _gate_weights(lru_wa_fwd[l]), _gate_weights(lru_wx_fwd[l]),
           half(lru_ba_fwd[l]), half(lru_bx_fwd[l]), row(lru_lam_fwd[l]), None)
    bwd = (conv_w[l], row(conv_b[l]), _gate_weights(lru_wa_bwd[l]), _gate_weights(lru_wx_bwd[l]),
           half(lru_ba_bwd[l]), half(lru_bx_bwd[l]), row(lru_lam_bwd[l]), row(norm_rnn_g[l]))
    h_f, xc = _rnn_pass(xr, (), fwd, reverse=False, final=False)
    mix_rnn = _rnn_pass(xc, (h_f, gate), bwd, reverse=True, final=True)
    os_, lses = [], []
    for pat in range(n_pat):
        o, lse = _attn_pattern(qs[pat], ks[pat], vs[pat], bias, pat)
        os_.append(o)
        lses.append(lse)
    return _out_mlp(x, mix_rnn, os_, lses, row(norm_attn_g[l]), w_out_bf,
                    row(mlp_norm_g[l]), w_up_bf, w_down_bf, row(final_norm_g))
```

```python
import functools
import math

import numpy as np
import jax
import jax.numpy as jnp
from jax import lax
from jax.experimental import pallas as pl
from jax.experimental.pallas import tpu as pltpu

F32 = jnp.float32
BF16 = jnp.bfloat16

D_MODEL = 1024
D_RNN = 512
N_RNN_BLOCKS = 8
RNN_BLOCK = D_RNN // N_RNN_BLOCKS
CONV_WIDTH = 4
CONV_LEFT = 2
LRU_C = 8.0
N_HEADS = 8
HEAD_DIM = 64
D_ATTN = N_HEADS * HEAD_DIM
PATTERNS = ((128, 1), (512, 4), (2048, 16))
HALF_STEPS = 64
N_BUCKETS = 32
MAX_DISTANCE = 1024
D_IN = 2 * D_RNN + 3 * D_ATTN
D_FF = 4 * D_MODEL
EPS = 1e-6
NEG_INF = -1e30

LANES = 128
SUBLANES = 8
MXU_DIM = 256

Q_TILE = 128
K_TILE = 64
K_WIN = Q_TILE + 2 * HALF_STEPS
TILES_PER_STEP = 16
CLASS_GROUP = 4
LOG2E = math.log2(math.e)
INTERIOR, FIRST, LAST = 0, 1, 2
GATE_BLK = MXU_DIM

IN_TILE = 1024
RNN_CHUNK = 1024
CHUNKS_PER_STEP = 2
STRIPE_ROWS = RNN_CHUNK // SUBLANES
OUT_TILE = 512
FF_CHUNK = 1024
assert RNN_CHUNK % IN_TILE == 0 and (CONV_WIDTH, CONV_LEFT) == (4, 2)

VMEM_LIMIT = 48 * 1024 * 1024
BIG_VMEM_LIMIT = 56 * 1024 * 1024
IN_PROJ_VMEM_LIMIT = 60 * 1024 * 1024


def _rms(x, g):
    ms = jnp.mean(x * x, axis=-1, keepdims=True)
    return x * lax.rsqrt(ms + EPS) * g


def _t5_bucket_np(rel):
    nb = N_BUCKETS // 2
    max_exact = nb // 2
    ret = np.where(rel > 0, nb, 0)
    n = np.abs(rel)
    nf = np.maximum(n, 1).astype(np.float32)
    large = max_exact + (np.log(nf / np.float32(max_exact)) / np.float32(math.log(MAX_DISTANCE / max_exact))
                         * np.float32(nb - max_exact)).astype(np.int32)
    large = np.minimum(large, nb - 1)
    return ret + np.where(n < max_exact, n, large)


def _bucket_index_rows():
    step = np.arange(K_WIN) - HALF_STEPS
    rows = []
    for _, dil in PATTERNS:
        b = _t5_bucket_np((step * dil).astype(np.int32))
        rows.append(np.where(np.abs(step) <= HALF_STEPS, b, -1))
    return np.stack(rows, axis=0).astype(np.int32)[:, None, :]


def _bias_table_kernel(idx_ref, rb_ref, out_ref):
    idx = idx_ref[0]
    col = lax.broadcasted_iota(jnp.int32, (Q_TILE, K_WIN), 1)
    hit = [idx == b for b in range(N_BUCKETS)]
    for h in range(N_HEADS):
        first_row = jnp.full(idx.shape, NEG_INF, F32)
        for b in range(N_BUCKETS):
            first_row = jnp.where(hit[b], rb_ref[b, h] * LOG2E, first_row)
        acc = pltpu.roll(jnp.broadcast_to(first_row, (Q_TILE, K_WIN)), 0, 1, stride=1, stride_axis=0)
        out_ref[0, INTERIOR, h] = acc
        out_ref[0, FIRST, h] = jnp.where(col < K_WIN - HALF_STEPS, pltpu.roll(acc, K_WIN - HALF_STEPS, 1), NEG_INF)
        out_ref[0, LAST, h] = jnp.where(col >= HALF_STEPS, pltpu.roll(acc, HALF_STEPS, 1), NEG_INF)


def _bias_tables(rel_bias):
    idx = jnp.asarray(_bucket_index_rows())
    n_pat = idx.shape[0]
    return pl.pallas_call(
        _bias_table_kernel,
        grid=(n_pat,),
        in_specs=[
            pl.BlockSpec((1, 1, K_WIN), lambda g: (g, 0, 0)),
            pl.BlockSpec(memory_space=pltpu.SMEM),
        ],
        out_specs=pl.BlockSpec((1, 3, N_HEADS, Q_TILE, K_WIN), lambda g: (g, 0, 0, 0, 0)),
        out_shape=jax.ShapeDtypeStruct((n_pat, 3, N_HEADS, Q_TILE, K_WIN), F32),
        name="bias_table",
    )(idx, rel_bias)


def _stripe_store(ref, p, part):
    stripes = IN_TILE // STRIPE_ROWS
    for st in range(stripes):
        for j in range(D_RNN // LANES):
            ref[0, j, pl.ds(part * stripes + st, STRIPE_ROWS, stride=SUBLANES), :] = (
                p[st * STRIPE_ROWS:(st + 1) * STRIPE_ROWS, j * LANES:(j + 1) * LANES])


PERM_ROWS = MXU_DIM


def _class_permutations():
    mats = []
    for _, dil in PATTERNS:
        if dil == 1:
            continue
        per = PERM_ROWS // dil
        p = np.zeros((PERM_ROWS, PERM_ROWS), np.float32)
        for r in range(dil):
            for m in range(per):
                p[r * per + m, m * dil + r] = 1.0
        mats.append(p)
    return np.stack(mats)


def _in_proj_kernel(x_ref, g_ref, w_ref, perm_ref, *rest, tm, n_later):
    n_pat = len(PATTERNS)
    later_in, rest = rest[:n_later], rest[n_later:]
    xr_ref, gate_ref, rest = rest[0], rest[1], rest[2:]
    outs = [rest[a * n_pat:(a + 1) * n_pat] for a in range(3)]
    later_out = rest[3 * n_pat:3 * n_pat + n_later]
    h_scr = rest[3 * n_pat + n_later]
    for src, dst in zip(later_in, later_out):
        dst[...] = src[...].astype(BF16)
    w_scr = rest[3 * n_pat + n_later + 1]

    @pl.when((pl.program_id(0) == 0) & (pl.program_id(1) == 0))
    def _():
        w_scr[...] = w_ref[...].astype(BF16)

    h_scr[...] = _rms(x_ref[0], g_ref[...]).astype(BF16)

    def seg(lo, width):
        return jnp.dot(h_scr[...], w_scr[:, lo:lo + width], preferred_element_type=F32)

    for a, scale in enumerate((HEAD_DIM ** -0.5 * LOG2E, 1.0, 1.0)):
        p = seg(2 * D_RNN + a * D_ATTN, D_ATTN)
        if scale != 1.0:
            p = p * scale
        pb = p.astype(BF16)
        dilated = 0
        for (_, dil), o_ref in zip(PATTERNS, outs[a]):
            if dil == 1:
                o_ref[0, 0, 0] = pb
                continue
            per = PERM_ROWS // dil
            for blk in range(tm // PERM_ROWS):
                y = jnp.dot(perm_ref[dilated], pb[blk * PERM_ROWS:(blk + 1) * PERM_ROWS, :],
                            preferred_element_type=F32).astype(BF16)
                groups = o_ref.shape[1]
                for r in range(dil):
                    o_ref[0, r % groups, r // groups, blk * per:(blk + 1) * per, :] = y[r * per:(r + 1) * per, :]
            dilated += 1
    g = seg(D_RNN, D_RNN)
    c = math.sqrt(2.0 / math.pi)
    half_g = 0.5 * g
    gelu = half_g + half_g * jnp.tanh(g * (c + (c * 0.044715) * (g * g)))
    part = pl.program_id(1) % (RNN_CHUNK // IN_TILE)
    _stripe_store(gate_ref, gelu, part)
    _stripe_store(xr_ref, seg(0, D_RNN), part)


def _in_proj(x3, g, w_in_bf, later_weights, tm):
    assert tm == IN_TILE
    bsz, s, _ = x3.shape
    per_seq = s // tm
    n_steps = bsz * per_seq
    row_slice = lambda b, i: (b * per_seq + i, 0)
    later_specs = [pl.BlockSpec((w.shape[0] // n_steps, w.shape[1]), row_slice) for w in later_weights]
    later_shapes = [jax.ShapeDtypeStruct(w.shape, BF16) for w in later_weights]
    row = lambda b, i: (b, i, 0)
    cls = lambda b, i: (b, 0, 0, i, 0)
    chunk = lambda b, i: (b, 0, i // (RNN_CHUNK // IN_TILE), 0)
    const = lambda b, i: (0, 0)
    perms = jnp.asarray(_class_permutations(), BF16)
    qkv_specs, qkv_shapes = [], []
    for _ in range(3):
        for _, dil in PATTERNS:
            groups, per_group = dil // min(dil, CLASS_GROUP), min(dil, CLASS_GROUP)
            qkv_specs.append(pl.BlockSpec((1, groups, per_group, tm // dil, D_ATTN), cls))
            qkv_shapes.append(jax.ShapeDtypeStruct((bsz, groups, per_group, s // dil, D_ATTN), BF16))
    return pl.pallas_call(
        functools.partial(_in_proj_kernel, tm=tm, n_later=len(later_weights)),
        grid=(bsz, per_seq),
        in_specs=[
            pl.BlockSpec((1, tm, D_MODEL), row),
            pl.BlockSpec((1, D_MODEL), const),
            pl.BlockSpec((D_MODEL, D_IN), const, pipeline_mode=pl.Buffered(1)),
            pl.BlockSpec(perms.shape, lambda b, i: (0, 0, 0)),
        ] + later_specs,
        out_specs=[pl.BlockSpec((1, D_RNN // LANES, RNN_CHUNK, LANES), chunk)] * 2 + qkv_specs + later_specs,
        out_shape=[jax.ShapeDtypeStruct((bsz, D_RNN // LANES, s, LANES), F32)] * 2 + qkv_shapes + later_shapes,
        scratch_shapes=[pltpu.VMEM((tm, D_MODEL), BF16), pltpu.VMEM((D_MODEL, D_IN), BF16)],
        compiler_params=pltpu.CompilerParams(
            dimension_semantics=("arbitrary", "arbitrary"), vmem_limit_bytes=IN_PROJ_VMEM_LIMIT),
        name="in_proj",
    )(x3, g, w_in_bf, perms, *later_weights)


def _scan_block(a, b, reverse):
    n = a.shape[0]
    row = lax.broadcasted_iota(jnp.int32, a.shape, 0)
    s = 1
    while s < n:
        if reverse:
            ra = pltpu.roll(a, n - s, 0)
            rb = pltpu.roll(b, n - s, 0)
            m = row < n - s
        else:
            ra = pltpu.roll(a, s, 0)
            rb = pltpu.roll(b, s, 0)
            m = row >= s
        b = jnp.where(m, a * rb + b, b)
        a = jnp.where(m, a * ra, a)
        s *= 2
    return a, b


def _rnn_kernel(*refs, reverse, final, nchunks, sc=None):
    if final:
        (xc_ref, hf_ref, gate_ref, wa_ref, wx_ref, ba_ref, bx_ref, lam_ref, gn_ref,
         out_ref, a_scr, b_scr, h_scr, carry) = refs
    else:
        (pa_ref, pb_ref, cur_ref, nx_ref, cw_ref, cb_ref, wa_ref, wx_ref, ba_ref, bx_ref, lam_ref,
         out_ref, xc_out_ref, a_scr, b_scr, carry) = refs
        h_scr = None

    scan_order = range(CHUNKS_PER_STEP - 1, -1, -1) if reverse else range(CHUNKS_PER_STEP)
    if sc is None:
        for c in scan_order:
            _rnn_kernel(*refs, reverse=reverse, final=final, nchunks=nchunks, sc=c)
        return
    rows = pl.ds(sc * RNN_CHUNK, RNN_CHUNK)
    if final:
        xc_ref, hf_ref, gate_ref = (r.at[:, :, rows] for r in (xc_ref, hf_ref, gate_ref))
        out_ref = out_ref.at[:, rows]
    else:
        step_ref = cur_ref
        cur_ref, out_ref, xc_out_ref = (r.at[:, :, rows] for r in (cur_ref, out_ref, xc_out_ref))

    step = pl.program_id(1)
    block = (nchunks // CHUNKS_PER_STEP - 1 - step) if reverse else step
    chunk = block * CHUNKS_PER_STEP + sc
    n_slab = D_RNN // LANES
    nv = STRIPE_ROWS
    sub = lax.broadcasted_iota(jnp.int32, (SUBLANES, LANES), 0)
    top, bot = SUBLANES - 1, 0

    if sc == scan_order[0]:
        @pl.when(step == 0)
        def _():
            carry[...] = jnp.zeros_like(carry)

    if final:
        xc = [xc_ref[0, j] for j in range(n_slab)]
    else:
        xc = []
        for j in range(n_slab):
            lanes = slice(j * LANES, (j + 1) * LANES)
            x = cur_ref[0, j].reshape(nv, SUBLANES, LANES)
            lo = sc * RNN_CHUNK
            if sc == 0:
                pa = jnp.where(chunk > 0, pa_ref[0, j], 0.0)
                pb = jnp.where(chunk > 0, pb_ref[0, j], 0.0)
            else:
                pa = step_ref[0, j, lo - 2 * SUBLANES:lo - SUBLANES, :]
                pb = step_ref[0, j, lo - SUBLANES:lo, :]
            if sc == CHUNKS_PER_STEP - 1:
                nx = jnp.where(chunk < nchunks - 1, nx_ref[0, j], 0.0)
            else:
                nx = step_ref[0, j, lo + RNN_CHUNK:lo + RNN_CHUNK + SUBLANES, :]
            xm2 = pltpu.roll(jnp.where(sub == top, pa, x[nv - 2]), 1, 0)
            xm1 = pltpu.roll(jnp.where(sub == top, pb, x[nv - 1]), 1, 0)
            xp1 = pltpu.roll(jnp.where(sub == bot, nx, x[0]), SUBLANES - 1, 0)
            xext = jnp.concatenate([xm2[None], xm1[None], x, xp1[None]], axis=0)
            acc = cb_ref[:, lanes] + xext[0:nv] * cw_ref[0:1, lanes]
            for k in range(1, CONV_WIDTH):
                acc = acc + xext[k:k + nv] * cw_ref[k:k + 1, lanes]
            xc.append(acc.reshape(RNN_CHUNK, LANES))
            xc_out_ref[0, j] = xc[j]

    nlam = -lam_ref[...]
    softplus = jnp.maximum(nlam, 0.0) + jnp.log1p(jnp.exp(-jnp.abs(nlam)))
    half_coef = (0.5 * LRU_C) * softplus
    slabs_per_blk = GATE_BLK // LANES
    for jj in range(D_RNN // GATE_BLK):
        blk = slice(jj * GATE_BLK, (jj + 1) * GATE_BLK)
        xj = jnp.concatenate(xc[jj * slabs_per_blk:(jj + 1) * slabs_per_blk], axis=-1)
        xjb = xj.astype(BF16)
        za = jnp.dot(xjb, wa_ref[jj], preferred_element_type=F32) + ba_ref[:, blk]
        zx = jnp.dot(xjb, wx_ref[jj], preferred_element_type=F32) + bx_ref[:, blk]
        i = 0.5 * jnp.tanh(zx) + 0.5
        neg_log_a = half_coef[:, blk] * jnp.tanh(za) + half_coef[:, blk]
        a = jnp.exp2(neg_log_a * -LOG2E)
        u = jnp.tanh(neg_log_a) * (1.0 + a * a)
        root = jnp.where(u > 0.0, u * lax.rsqrt(u), 0.0)
        b = root * (i * xj)
        for t in range(slabs_per_blk):
            a_scr[jj * slabs_per_blk + t] = a[:, t * LANES:(t + 1) * LANES]
            b_scr[jj * slabs_per_blk + t] = b[:, t * LANES:(t + 1) * LANES]

    order = range(nv - 1, -1, -1) if reverse else range(nv)
    group = lambda v: slice(v * SUBLANES, (v + 1) * SUBLANES)

    h = [jnp.zeros((SUBLANES, LANES), F32)] * n_slab
    p = [jnp.ones((SUBLANES, LANES), F32)] * n_slab
    for v in order:
        for j in range(n_slab):
            av = a_scr[j, group(v), :]
            h[j] = av * h[j] + b_scr[j, group(v), :]
            p[j] = av * p[j]

    start = []
    for j in range(n_slab):
        pc, hc = _scan_block(p[j], h[j], reverse)
        prev = carry[j]
        if reverse:
            seed = jnp.broadcast_to(prev[bot:bot + 1, :], (SUBLANES, LANES))
            ends = hc + pc * seed
            start.append(jnp.where(sub == top, pltpu.roll(prev, SUBLANES - 1, 0),
                                   pltpu.roll(ends, SUBLANES - 1, 0)))
        else:
            seed = jnp.broadcast_to(prev[top:top + 1, :], (SUBLANES, LANES))
            ends = hc + pc * seed
            start.append(jnp.where(sub == bot, pltpu.roll(prev, 1, 0), pltpu.roll(ends, 1, 0)))
        carry[j] = ends

    dst = h_scr if final else None
    h = start
    for v in order:
        for j in range(n_slab):
            h[j] = a_scr[j, group(v), :] * h[j] + b_scr[j, group(v), :]
            if final:
                dst[j, group(v), :] = (hf_ref[0, j, group(v), :] + h[j]) * gate_ref[0, j, group(v), :]
            else:
                out_ref[0, j, group(v), :] = h[j]

    if final:
        ys = [h_scr[j] for j in range(n_slab)]
        ss = ys[0] * ys[0]
        for j in range(1, n_slab):
            ss = ss + ys[j] * ys[j]
        inv = lax.rsqrt(jnp.sum(ss, axis=-1, keepdims=True) * (1.0 / D_RNN) + EPS)
        for j in range(n_slab):
            lanes = slice(j * LANES, (j + 1) * LANES)
            h_scr[j] = ys[j] * inv * gn_ref[:, lanes]
        for st in range(SUBLANES):
            for j in range(n_slab):
                out_ref[0, st * nv:(st + 1) * nv, j * LANES:(j + 1) * LANES] = (
                    h_scr[j, pl.ds(st, nv, stride=SUBLANES), :].astype(BF16))


def _rnn_pass(x4, extra, params, reverse, final):
    bsz, n_slab, s, _ = x4.shape
    nchunks = s // RNN_CHUNK
    step_rows = CHUNKS_PER_STEP * RNN_CHUNK
    nsteps = s // step_rows
    hb = step_rows // SUBLANES
    nhalo = s // SUBLANES

    def cidx(c):
        return (nsteps - 1 - c) if reverse else c

    cur_map = lambda b, c: (b, 0, cidx(c), 0)
    const2 = lambda b, c: (0, 0)
    const3 = lambda b, c: (0, 0, 0)
    cw, cb, wa, wx, ba, bx, lam, gn = params
    halo = (1, n_slab, SUBLANES, LANES)
    full = (1, n_slab, step_rows, LANES)
    gate_specs = [
        pl.BlockSpec((D_RNN // GATE_BLK, GATE_BLK, GATE_BLK), const3),
        pl.BlockSpec((D_RNN // GATE_BLK, GATE_BLK, GATE_BLK), const3),
        pl.BlockSpec((1, D_RNN), const2),
        pl.BlockSpec((1, D_RNN), const2),
        pl.BlockSpec((1, D_RNN), const2),
    ]
    slab_scr = pltpu.VMEM((n_slab, RNN_CHUNK, LANES), F32)
    carry_scr = pltpu.VMEM((n_slab, SUBLANES, LANES), F32)
    if final:
        in_specs = [pl.BlockSpec(full, cur_map)] * 3 + gate_specs + [pl.BlockSpec((1, D_RNN), const2)]
        args = [x4, *extra, wa, wx, ba, bx, lam, gn]
        scratch = [slab_scr, slab_scr, slab_scr, carry_scr]
        out_specs = pl.BlockSpec((1, step_rows, D_RNN), lambda b, c: (b, cidx(c), 0))
        out_shape = jax.ShapeDtypeStruct((bsz, s, D_RNN), BF16)
    else:
        pa_map = lambda b, c: (b, 0, jnp.maximum(cidx(c) * hb - 2, 0), 0)
        pb_map = lambda b, c: (b, 0, jnp.maximum(cidx(c) * hb - 1, 0), 0)
        nx_map = lambda b, c: (b, 0, jnp.minimum((cidx(c) + 1) * hb, nhalo - 1), 0)
        in_specs = [pl.BlockSpec(halo, pa_map), pl.BlockSpec(halo, pb_map), pl.BlockSpec(full, cur_map),
                    pl.BlockSpec(halo, nx_map),
                    pl.BlockSpec((CONV_WIDTH, D_RNN), const2), pl.BlockSpec((1, D_RNN), const2)] + gate_specs
        args = [x4, x4, x4, x4, cw, cb, wa, wx, ba, bx, lam]
        scratch = [slab_scr, slab_scr, carry_scr]
        out_specs = [pl.BlockSpec(full, cur_map)] * 2
        out_shape = [jax.ShapeDtypeStruct(x4.shape, F32)] * 2
    return pl.pallas_call(
        functools.partial(_rnn_kernel, reverse=reverse, final=final, nchunks=nchunks),
        grid=(bsz, nsteps),
        in_specs=in_specs,
        out_specs=out_specs,
        out_shape=out_shape,
        scratch_shapes=scratch,
        compiler_params=pltpu.CompilerParams(
            dimension_semantics=("arbitrary", "arbitrary"), vmem_limit_bytes=VMEM_LIMIT),
        name="rnn_bwd" if reverse else "rnn_fwd",
    )(*args)


def _gate_weights(w):
    per = GATE_BLK // RNN_BLOCK
    w5 = (0.5 * w).reshape(D_RNN // GATE_BLK, per, RNN_BLOCK, 1, RNN_BLOCK)
    on_diag = jnp.asarray(np.eye(per, dtype=bool)).reshape(1, per, 1, per, 1)
    dense = jnp.where(on_diag, w5, 0.0)
    return dense.reshape(D_RNN // GATE_BLK, GATE_BLK, GATE_BLK).astype(BF16)


def _attn_kernel(q_ref, kp_ref, k_ref, kn_ref, vp_ref, v_ref, vn_ref, bias_ref, o_ref, lse_ref, kbuf, vbuf,
                 *, sub_len, tiles_per_class):
    i = pl.program_id(2)
    n_tiles = sub_len // Q_TILE
    n_classes = q_ref.shape[1]
    q_rows = tiles_per_class * Q_TILE
    for buf, before, own, after in ((kbuf, kp_ref, k_ref, kn_ref), (vbuf, vp_ref, v_ref, vn_ref)):
        for cl in range(n_classes):
            buf[cl, 0:HALF_STEPS, :] = before[0, cl]
            buf[cl, HALF_STEPS:HALF_STEPS + q_rows, :] = own[0, cl]
            buf[cl, HALF_STEPS + q_rows:, :] = after[0, cl]
    buf_origin = i * q_rows - HALF_STEPS
    lane = lax.broadcasted_iota(jnp.int32, (Q_TILE, LANES), 1)
    heads_per_slab = LANES // HEAD_DIM
    first_half = lane < HEAD_DIM
    for cl in range(n_classes):
        for t in range(tiles_per_class):
            rows = slice(t * Q_TILE, (t + 1) * Q_TILE)
            if n_classes == 1:
                out_rows = rows
            else:
                out_rows = pl.ds(t * Q_TILE * n_classes + cl, Q_TILE, stride=n_classes)
            q = q_ref[0, cl, rows, :]
            tile = i * tiles_per_class + t
            variant = jnp.where(tile == 0, FIRST, jnp.where(tile == n_tiles - 1, LAST, INTERIOR))
            start = jnp.clip(tile * Q_TILE - HALF_STEPS, 0, sub_len - K_WIN) - buf_origin
            start = pl.multiple_of(start, K_TILE)
            kk = kbuf[cl, pl.ds(start, K_WIN), :]
            vv = vbuf[cl, pl.ds(start, K_WIN), :]
            for p in range(D_ATTN // LANES):
                sl = slice(p * LANES, (p + 1) * LANES)
                qp, kp, vp = q[:, sl], kk[:, sl], vv[:, sl]
                zero = jnp.zeros_like(qp)
                q2 = jnp.concatenate([jnp.where(first_half, qp, zero), jnp.where(first_half, zero, qp)], axis=0)
                s = lax.dot_general(q2, kp, (((1,), (1,)), ((), ())), preferred_element_type=F32)
                h0 = p * heads_per_slab
                s = s + bias_ref[variant, h0:h0 + heads_per_slab].reshape(heads_per_slab * Q_TILE, K_WIN)
                m = jnp.max(s, axis=-1, keepdims=True)
                e = jnp.exp2(s - m)
                l = jnp.sum(e, axis=-1, keepdims=True)
                o2 = jnp.dot(e.astype(BF16), vp, preferred_element_type=F32)
                shape = (Q_TILE, LANES)
                pick = lambda a: jnp.where(first_half, jnp.broadcast_to(a[:Q_TILE], shape),
                                           jnp.broadcast_to(a[Q_TILE:], shape))
                l_slab = pick(l)
                o_ref[0, p, out_rows, :] = pick(o2) * (1.0 / l_slab)
                lse_ref[0, p, out_rows, :] = pick(m) + jnp.log(l_slab) * LOG2E


def _attn_pattern(qc, kc, vc, bias, pat):
    bsz, groups, n_classes, sub_len, _ = qc.shape
    tiles_per_class = TILES_PER_STEP // n_classes
    q_rows = tiles_per_class * Q_TILE
    n_slab = D_ATTN // LANES
    qmap = lambda b, g, i: (b, g, 0, i, 0)
    halo_per_step = q_rows // HALF_STEPS
    n_halo = sub_len // HALF_STEPS
    before = lambda b, g, i: (b, g, 0, jnp.maximum(i * halo_per_step - 1, 0), 0)
    after = lambda b, g, i: (b, g, 0, jnp.minimum((i + 1) * halo_per_step, n_halo - 1), 0)
    own_spec = pl.BlockSpec((1, None, n_classes, q_rows, D_ATTN), qmap)
    kv_specs = [pl.BlockSpec((1, None, n_classes, HALF_STEPS, D_ATTN), before), own_spec,
                pl.BlockSpec((1, None, n_classes, HALF_STEPS, D_ATTN), after)]
    in_specs = [own_spec] + kv_specs + kv_specs
    in_specs.append(pl.BlockSpec((None, 3, N_HEADS, Q_TILE, K_WIN), lambda b, g, i: (pat, 0, 0, 0, 0)))
    window_buf = pltpu.VMEM((n_classes, q_rows + 2 * HALF_STEPS, D_ATTN), BF16)
    out_spec = pl.BlockSpec((1, None, n_slab, n_classes * q_rows, LANES), qmap)
    return pl.pallas_call(
        functools.partial(_attn_kernel, sub_len=sub_len, tiles_per_class=tiles_per_class),
        grid=(bsz, groups, sub_len // q_rows),
        in_specs=in_specs,
        out_specs=[out_spec] * 2,
        out_shape=[jax.ShapeDtypeStruct((bsz, groups, n_slab, n_classes * sub_len, LANES), F32)] * 2,
        scratch_shapes=[window_buf, window_buf],
        compiler_params=pltpu.CompilerParams(
            dimension_semantics=("arbitrary",) * 3, vmem_limit_bytes=VMEM_LIMIT),
        name=f"attn_d{groups * n_classes}",
    )(qc, kc, kc, kc, vc, vc, vc, bias)


def _token_order(ref, scr, slab):
    groups = ref.shape[1]
    if groups == 1:
        return ref[0, 0, slab]
    for g in range(groups):
        scr[slab, pl.ds(g, OUT_TILE // groups, stride=groups), :] = ref[0, g, slab]
    return scr[slab]


def _out_mlp_kernel(x_ref, mr_ref, o1, o2, o3, l1, l2, l3, ga_ref, wo_ref, gm_ref, wu_ref, wd_ref,
                    gf_ref, out_ref, s_o3, s_l3):
    ys = []
    for slab in range(D_ATTN // LANES):
        a1 = _token_order(l1, None, slab)
        a2 = _token_order(l2, None, slab)
        a3 = _token_order(l3, s_l3, slab)
        mx = jnp.maximum(jnp.maximum(a1, a2), a3)
        e1, e2, e3 = jnp.exp2(a1 - mx), jnp.exp2(a2 - mx), jnp.exp2(a3 - mx)
        ys.append((e1 * _token_order(o1, None, slab) + e2 * _token_order(o2, None, slab)
                   + e3 * _token_order(o3, s_o3, slab)) / (e1 + e2 + e3))
    mix_attn = _rms(jnp.concatenate(ys, axis=-1), ga_ref[...]).astype(BF16)

    proj = jnp.dot(mr_ref[0], wo_ref[0:D_RNN, :], preferred_element_type=F32)
    proj = proj + jnp.dot(mix_attn, wo_ref[D_RNN:D_RNN + D_ATTN, :], preferred_element_type=F32)
    x1 = proj + x_ref[0]
    h = _rms(x1, gm_ref[...]).astype(BF16)
    ff = None
    for c in range(D_FF // FF_CHUNK):
        cols = slice(c * FF_CHUNK, (c + 1) * FF_CHUNK)
        z = jnp.dot(h, wu_ref[:, cols], preferred_element_type=F32)
        z = jnp.square(jnp.maximum(z, 0.0)).astype(BF16)
        d = jnp.dot(z, wd_ref[cols, :], preferred_element_type=F32)
        ff = d if ff is None else ff + d
    out_ref[0] = _rms(ff + x1, gf_ref[...])


def _out_mlp(x3, mix_rnn, os_, lses, ga, wo, gm, wu, wd, gf):
    bsz, s, _ = x3.shape
    tm = OUT_TILE
    row = lambda b, i: (b, i, 0)
    const = lambda b, i: (0, 0)
    once = pl.Buffered(1)
    n_slab = D_ATTN // LANES
    attn_specs = [pl.BlockSpec((1, o.shape[1], n_slab, tm // o.shape[1], LANES), lambda b, i: (b, 0, 0, i, 0))
                  for o in os_]
    in_specs = [pl.BlockSpec((1, tm, D_MODEL), row), pl.BlockSpec((1, tm, D_RNN), row)]
    in_specs += attn_specs + attn_specs
    in_specs += [
        pl.BlockSpec((1, D_ATTN), const),
        pl.BlockSpec((D_RNN + D_ATTN, D_MODEL), const, pipeline_mode=once),
        pl.BlockSpec((1, D_MODEL), const),
        pl.BlockSpec((D_MODEL, D_FF), const, pipeline_mode=once),
        pl.BlockSpec((D_FF, D_MODEL), const, pipeline_mode=once),
        pl.BlockSpec((1, D_MODEL), const),
    ]
    return pl.pallas_call(
        _out_mlp_kernel,
        grid=(bsz, s // tm),
        in_specs=in_specs,
        out_specs=pl.BlockSpec((1, tm, D_MODEL), row),
        out_shape=jax.ShapeDtypeStruct((bsz, s, D_MODEL), F32),
        scratch_shapes=[pltpu.VMEM((n_slab, tm, LANES), F32)] * 2,
        compiler_params=pltpu.CompilerParams(
            dimension_semantics=("arbitrary", "arbitrary"), vmem_limit_bytes=BIG_VMEM_LIMIT),
        name="out_mlp",
    )(x3, mix_rnn, *os_, *lses, ga, wo, gm, wu, wd, gf)


def kernel(x, attn_norm_g, w_in, conv_w, conv_b, lru_wa_fwd, lru_ba_fwd, lru_wx_fwd, lru_bx_fwd, lru_lam_fwd, lru_wa_bwd, lru_ba_bwd, lru_wx_bwd, lru_bx_bwd, lru_lam_bwd, rel_bias, norm_rnn_g, norm_attn_g, w_out, mlp_norm_g, w_up, w_down, final_norm_g):
    depth = w_in.shape[0]
    assert depth == 1, "the final RMSNorm is fused into the single layer's last call"
    l = 0
    n_pat = len(PATTERNS)
    row = lambda v: v.reshape(1, -1)
    bias = _bias_tables(rel_bias)
    proj = _in_proj(x, row(attn_norm_g[l]), w_in[l], (w_out[l], w_up[l], w_down[l]), tm=IN_TILE)
    xr, gate = proj[0], proj[1]
    qs, ks, vs = (proj[2 + a * n_pat:2 + (a + 1) * n_pat] for a in range(3))
    w_out_bf, w_up_bf, w_down_bf = proj[2 + 3 * n_pat:]
    half = lambda v: row(0.5 * v)
    fwd = (conv_w[l], row(conv_b[l]), _gate_weights(lru_wa_fwd[l]), _gate_weights(lru_wx_fwd[l]),
           half(lru_ba_fwd[l]), half(lru_bx_fwd[l]), row(lru_lam_fwd[l]), None)
    bwd = (conv_w[l], row(conv_b[l]), _gate_weights(lru_wa_bwd[l]), _gate_weights(lru_wx_bwd[l]),
           half(lru_ba_bwd[l]), half(lru_bx_bwd[l]), row(lru_lam_bwd[l]), row(norm_rnn_g[l]))
    h_f, xc = _rnn_pass(xr, (), fwd, reverse=False, final=False)
    mix_rnn = _rnn_pass(xc, (h_f, gate), bwd, reverse=True, final=True)
    os_, lses = [], []
    for pat in range(n_pat):
        o, lse = _attn_pattern(qs[pat], ks[pat], vs[pat], bias, pat)
        os_.append(o)
        lses.append(lse)
    return _out_mlp(x, mix_rnn, os_, lses, row(norm_attn_g[l]), w_out_bf,
                    row(mlp_norm_g[l]), w_up_bf, w_down_bf, row(final_norm_g))
```

```python
import functools
import math

import numpy as np
import jax
import jax.numpy as jnp
from jax import lax
from jax.experimental import pallas as pl
from jax.experimental.pallas import tpu as pltpu

F32 = jnp.float32
BF16 = jnp.bfloat16

D_MODEL = 1024
D_RNN = 512
N_RNN_BLOCKS = 8
RNN_BLOCK = D_RNN // N_RNN_BLOCKS
CONV_WIDTH = 4
CONV_LEFT = 2
LRU_C = 8.0
N_HEADS = 8
HEAD_DIM = 64
D_ATTN = N_HEADS * HEAD_DIM
PATTERNS = ((128, 1), (512, 4), (2048, 16))
HALF_STEPS = 64
N_BUCKETS = 32
MAX_DISTANCE = 1024
D_IN = 2 * D_RNN + 3 * D_ATTN
D_FF = 4 * D_MODEL
EPS = 1e-6
NEG_INF = -1e30

LANES = 128
SUBLANES = 8
MXU_DIM = 256

Q_TILE = 128
K_TILE = 64
K_WIN = Q_TILE + 2 * HALF_STEPS
TILES_PER_STEP = 16
CLASS_GROUP = 4
LOG2E = math.log2(math.e)
INTERIOR, FIRST, LAST = 0, 1, 2
GATE_BLK = MXU_DIM

IN_TILE = 1024
RNN_CHUNK = 1024
STRIPE_ROWS = RNN_CHUNK // SUBLANES
OUT_TILE = 512
FF_CHUNK = 1024
assert RNN_CHUNK % IN_TILE == 0 and (CONV_WIDTH, CONV_LEFT) == (4, 2)

VMEM_LIMIT = 48 * 1024 * 1024
BIG_VMEM_LIMIT = 56 * 1024 * 1024
IN_PROJ_VMEM_LIMIT = 60 * 1024 * 1024


def _rms(x, g):
    ms = jnp.mean(x * x, axis=-1, keepdims=True)
    return x * lax.rsqrt(ms + EPS) * g


def _t5_bucket_np(rel):
    nb = N_BUCKETS // 2
    max_exact = nb // 2
    ret = np.where(rel > 0, nb, 0)
    n = np.abs(rel)
    nf = np.maximum(n, 1).astype(np.float32)
    large = max_exact + (np.log(nf / np.float32(max_exact)) / np.float32(math.log(MAX_DISTANCE / max_exact))
                         * np.float32(nb - max_exact)).astype(np.int32)
    large = np.minimum(large, nb - 1)
    return ret + np.where(n < max_exact, n, large)


def _bucket_index_rows():
    step = np.arange(K_WIN) - HALF_STEPS
    rows = []
    for _, dil in PATTERNS:
        b = _t5_bucket_np((step * dil).astype(np.int32))
        rows.append(np.where(np.abs(step) <= HALF_STEPS, b, -1))
    return np.stack(rows, axis=0).astype(np.int32)[:, None, :]


def _bias_table_kernel(idx_ref, rb_ref, out_ref):
    idx = idx_ref[0]
    col = lax.broadcasted_iota(jnp.int32, (Q_TILE, K_WIN), 1)
    hit = [idx == b for b in range(N_BUCKETS)]
    for h in range(N_HEADS):
        first_row = jnp.full(idx.shape, NEG_INF, F32)
        for b in range(N_BUCKETS):
            first_row = jnp.where(hit[b], rb_ref[b, h] * LOG2E, first_row)
        acc = pltpu.roll(jnp.broadcast_to(first_row, (Q_TILE, K_WIN)), 0, 1, stride=1, stride_axis=0)
        out_ref[0, INTERIOR, h] = acc
        out_ref[0, FIRST, h] = jnp.where(col < K_WIN - HALF_STEPS, pltpu.roll(acc, K_WIN - HALF_STEPS, 1), NEG_INF)
        out_ref[0, LAST, h] = jnp.where(col >= HALF_STEPS, pltpu.roll(acc, HALF_STEPS, 1), NEG_INF)


def _bias_tables(rel_bias):
    idx = jnp.asarray(_bucket_index_rows())
    n_pat = idx.shape[0]
    return pl.pallas_call(
        _bias_table_kernel,
        grid=(n_pat,),
        in_specs=[
            pl.BlockSpec((1, 1, K_WIN), lambda g: (g, 0, 0)),
            pl.BlockSpec(memory_space=pltpu.SMEM),
        ],
        out_specs=pl.BlockSpec((1, 3, N_HEADS, Q_TILE, K_WIN), lambda g: (g, 0, 0, 0, 0)),
        out_shape=jax.ShapeDtypeStruct((n_pat, 3, N_HEADS, Q_TILE, K_WIN), F32),
        name="bias_table",
    )(idx, rel_bias)


def _stripe_store(ref, p, part):
    stripes = IN_TILE // STRIPE_ROWS
    for st in range(stripes):
        for j in range(D_RNN // LANES):
            ref[0, j, pl.ds(part * stripes + st, STRIPE_ROWS, stride=SUBLANES), :] = (
                p[st * STRIPE_ROWS:(st + 1) * STRIPE_ROWS, j * LANES:(j + 1) * LANES])


PERM_ROWS = MXU_DIM


def _class_permutations():
    mats = []
    for _, dil in PATTERNS:
        if dil == 1:
            continue
        per = PERM_ROWS // dil
        p = np.zeros((PERM_ROWS, PERM_ROWS), np.float32)
        for r in range(dil):
            for m in range(per):
                p[r * per + m, m * dil + r] = 1.0
        mats.append(p)
    return np.stack(mats)


def _in_proj_kernel(x_ref, g_ref, w_ref, perm_ref, *rest, tm, n_later):
    n_pat = len(PATTERNS)
    later_in, rest = rest[:n_later], rest[n_later:]
    xr_ref, gate_ref, rest = rest[0], rest[1], rest[2:]
    outs = [rest[a * n_pat:(a + 1) * n_pat] for a in range(3)]
    later_out = rest[3 * n_pat:3 * n_pat + n_later]
    h_scr = rest[3 * n_pat + n_later]
    for src, dst in zip(later_in, later_out):
        dst[...] = src[...].astype(BF16)
    w_scr = rest[3 * n_pat + n_later + 1]

    @pl.when((pl.program_id(0) == 0) & (pl.program_id(1) == 0))
    def _():
        w_scr[...] = w_ref[...].astype(BF16)

    h_scr[...] = _rms(x_ref[0], g_ref[...]).astype(BF16)

    def seg(lo, width):
        return jnp.dot(h_scr[...], w_scr[:, lo:lo + width], preferred_element_type=F32)

    for a, scale in enumerate((HEAD_DIM ** -0.5 * LOG2E, 1.0, 1.0)):
        p = seg(2 * D_RNN + a * D_ATTN, D_ATTN)
        if scale != 1.0:
            p = p * scale
        pb = p.astype(BF16)
        dilated = 0
        for (_, dil), o_ref in zip(PATTERNS, outs[a]):
            if dil == 1:
                o_ref[0, 0, 0] = pb
                continue
            per = PERM_ROWS // dil
            for blk in range(tm // PERM_ROWS):
                y = jnp.dot(perm_ref[dilated], pb[blk * PERM_ROWS:(blk + 1) * PERM_ROWS, :],
                            preferred_element_type=F32).astype(BF16)
                groups = o_ref.shape[1]
                for r in range(dil):
                    o_ref[0, r % groups, r // groups, blk * per:(blk + 1) * per, :] = y[r * per:(r + 1) * per, :]
            dilated += 1
    g = seg(D_RNN, D_RNN)
    c = math.sqrt(2.0 / math.pi)
    half_g = 0.5 * g
    gelu = half_g + half_g * jnp.tanh(g * (c + (c * 0.044715) * (g * g)))
    part = pl.program_id(1) % (RNN_CHUNK // IN_TILE)
    _stripe_store(gate_ref, gelu, part)
    _stripe_store(xr_ref, seg(0, D_RNN), part)


def _in_proj(x3, g, w_in_bf, later_weights, tm):
    assert tm == IN_TILE
    bsz, s, _ = x3.shape
    per_seq = s // tm
    n_steps = bsz * per_seq
    row_slice = lambda b, i: (b * per_seq + i, 0)
    later_specs = [pl.BlockSpec((w.shape[0] // n_steps, w.shape[1]), row_slice) for w in later_weights]
    later_shapes = [jax.ShapeDtypeStruct(w.shape, BF16) for w in later_weights]
    row = lambda b, i: (b, i, 0)
    cls = lambda b, i: (b, 0, 0, i, 0)
    chunk = lambda b, i: (b, 0, i // (RNN_CHUNK // IN_TILE), 0)
    const = lambda b, i: (0, 0)
    perms = jnp.asarray(_class_permutations(), BF16)
    qkv_specs, qkv_shapes = [], []
    for _ in range(3):
        for _, dil in PATTERNS:
            groups, per_group = dil // min(dil, CLASS_GROUP), min(dil, CLASS_GROUP)
            qkv_specs.append(pl.BlockSpec((1, groups, per_group, tm // dil, D_ATTN), cls))
            qkv_shapes.append(jax.ShapeDtypeStruct((bsz, groups, per_group, s // dil, D_ATTN), BF16))
    return pl.pallas_call(
        functools.partial(_in_proj_kernel, tm=tm, n_later=len(later_weights)),
        grid=(bsz, per_seq),
        in_specs=[
            pl.BlockSpec((1, tm, D_MODEL), row),
            pl.BlockSpec((1, D_MODEL), const),
            pl.BlockSpec((D_MODEL, D_IN), const, pipeline_mode=pl.Buffered(1)),
            pl.BlockSpec(perms.shape, lambda b, i: (0, 0, 0)),
        ] + later_specs,
        out_specs=[pl.BlockSpec((1, D_RNN // LANES, RNN_CHUNK, LANES), chunk)] * 2 + qkv_specs + later_specs,
        out_shape=[jax.ShapeDtypeStruct((bsz, D_RNN // LANES, s, LANES), F32)] * 2 + qkv_shapes + later_shapes,
        scratch_shapes=[pltpu.VMEM((tm, D_MODEL), BF16), pltpu.VMEM((D_MODEL, D_IN), BF16)],
        compiler_params=pltpu.CompilerParams(
            dimension_semantics=("arbitrary", "arbitrary"), vmem_limit_bytes=IN_PROJ_VMEM_LIMIT),
        name="in_proj",
    )(x3, g, w_in_bf, perms, *later_weights)


def _scan_block(a, b, reverse):
    n = a.shape[0]
    row = lax.broadcasted_iota(jnp.int32, a.shape, 0)
    s = 1
    while s < n:
        if reverse:
            ra = pltpu.roll(a, n - s, 0)
            rb = pltpu.roll(b, n - s, 0)
            m = row < n - s
        else:
            ra = pltpu.roll(a, s, 0)
            rb = pltpu.roll(b, s, 0)
            m = row >= s
        b = jnp.where(m, a * rb + b, b)
        a = jnp.where(m, a * ra, a)
        s *= 2
    return a, b


def _rnn_kernel(*refs, reverse, final, nchunks):
    if final:
        (xc_ref, hf_ref, gate_ref, wa_ref, wx_ref, ba_ref, bx_ref, lam_ref, gn_ref,
         out_ref, a_scr, b_scr, h_scr, carry) = refs
    else:
        (pa_ref, pb_ref, cur_ref, nx_ref, cw_ref, cb_ref, wa_ref, wx_ref, ba_ref, bx_ref, lam_ref,
         out_ref, xc_out_ref, a_scr, b_scr, carry) = refs
        h_scr = None

    step = pl.program_id(1)
    chunk = (nchunks - 1 - step) if reverse else step
    n_slab = D_RNN // LANES
    nv = STRIPE_ROWS
    sub = lax.broadcasted_iota(jnp.int32, (SUBLANES, LANES), 0)
    top, bot = SUBLANES - 1, 0

    @pl.when(step == 0)
    def _():
        carry[...] = jnp.zeros_like(carry)

    if final:
        xc = [xc_ref[0, j] for j in range(n_slab)]
    else:
        xc = []
        for j in range(n_slab):
            lanes = slice(j * LANES, (j + 1) * LANES)
            x = cur_ref[0, j].reshape(nv, SUBLANES, LANES)
            pa = jnp.where(chunk > 0, pa_ref[0, j], 0.0)
            pb = jnp.where(chunk > 0, pb_ref[0, j], 0.0)
            nx = jnp.where(chunk < nchunks - 1, nx_ref[0, j], 0.0)
            xm2 = pltpu.roll(jnp.where(sub == top, pa, x[nv - 2]), 1, 0)
            xm1 = pltpu.roll(jnp.where(sub == top, pb, x[nv - 1]), 1, 0)
            xp1 = pltpu.roll(jnp.where(sub == bot, nx, x[0]), SUBLANES - 1, 0)
            xext = jnp.concatenate([xm2[None], xm1[None], x, xp1[None]], axis=0)
            acc = cb_ref[:, lanes] + xext[0:nv] * cw_ref[0:1, lanes]
            for k in range(1, CONV_WIDTH):
                acc = acc + xext[k:k + nv] * cw_ref[k:k + 1, lanes]
            xc.append(acc.reshape(RNN_CHUNK, LANES))
            xc_out_ref[0, j] = xc[j]

    nlam = -lam_ref[...]
    softplus = jnp.maximum(nlam, 0.0) + jnp.log1p(jnp.exp(-jnp.abs(nlam)))
    half_coef = (0.5 * LRU_C) * softplus
    slabs_per_blk = GATE_BLK // LANES
    for jj in range(D_RNN // GATE_BLK):
        blk = slice(jj * GATE_BLK, (jj + 1) * GATE_BLK)
        xj = jnp.concatenate(xc[jj * slabs_per_blk:(jj + 1) * slabs_per_blk], axis=-1)
        xjb = xj.astype(BF16)
        za = jnp.dot(xjb, wa_ref[jj], preferred_element_type=F32) + ba_ref[:, blk]
        zx = jnp.dot(xjb, wx_ref[jj], preferred_element_type=F32) + bx_ref[:, blk]
        i = 0.5 * jnp.tanh(zx) + 0.5
        neg_log_a = half_coef[:, blk] * jnp.tanh(za) + half_coef[:, blk]
        a = jnp.exp2(neg_log_a * -LOG2E)
        u = jnp.tanh(neg_log_a) * (1.0 + a * a)
        root = jnp.where(u > 0.0, u * lax.rsqrt(u), 0.0)
        b = root * (i * xj)
        for t in range(slabs_per_blk):
            a_scr[jj * slabs_per_blk + t] = a[:, t * LANES:(t + 1) * LANES]
            b_scr[jj * slabs_per_blk + t] = b[:, t * LANES:(t + 1) * LANES]

    order = range(nv - 1, -1, -1) if reverse else range(nv)
    group = lambda v: slice(v * SUBLANES, (v + 1) * SUBLANES)

    h = [jnp.zeros((SUBLANES, LANES), F32)] * n_slab
    p = [jnp.ones((SUBLANES, LANES), F32)] * n_slab
    for v in order:
        for j in range(n_slab):
            av = a_scr[j, group(v), :]
            h[j] = av * h[j] + b_scr[j, group(v), :]
            p[j] = av * p[j]

    start = []
    for j in range(n_slab):
        pc, hc = _scan_block(p[j], h[j], reverse)
        prev = carry[j]
        if reverse:
            seed = jnp.broadcast_to(prev[bot:bot + 1, :], (SUBLANES, LANES))
            ends = hc + pc * seed
            start.append(jnp.where(sub == top, pltpu.roll(prev, SUBLANES - 1, 0),
                                   pltpu.roll(ends, SUBLANES - 1, 0)))
        else:
            seed = jnp.broadcast_to(prev[top:top + 1, :], (SUBLANES, LANES))
            ends = hc + pc * seed
            start.append(jnp.where(sub == bot, pltpu.roll(prev, 1, 0), pltpu.roll(ends, 1, 0)))
        carry[j] = ends

    dst = h_scr if final else None
    h = start
    for v in order:
        for j in range(n_slab):
            h[j] = a_scr[j, group(v), :] * h[j] + b_scr[j, group(v), :]
            if final:
                dst[j, group(v), :] = (hf_ref[0, j, group(v), :] + h[j]) * gate_ref[0, j, group(v), :]
            else:
                out_ref[0, j, group(v), :] = h[j]

    if final:
        ys = [h_scr[j] for j in range(n_slab)]
        ss = ys[0] * ys[0]
        for j in range(1, n_slab):
            ss = ss + ys[j] * ys[j]
        inv = lax.rsqrt(jnp.sum(ss, axis=-1, keepdims=True) * (1.0 / D_RNN) + EPS)
        for j in range(n_slab):
            lanes = slice(j * LANES, (j + 1) * LANES)
            h_scr[j] = ys[j] * inv * gn_ref[:, lanes]
        for st in range(SUBLANES):
            for j in range(n_slab):
                out_ref[0, st * nv:(st + 1) * nv, j * LANES:(j + 1) * LANES] = (
                    h_scr[j, pl.ds(st, nv, stride=SUBLANES), :].astype(BF16))


def _rnn_pass(x4, extra, params, reverse, final):
    bsz, n_slab, s, _ = x4.shape
    nchunks = s // RNN_CHUNK
    hb = RNN_CHUNK // SUBLANES
    nhalo = s // SUBLANES

    def cidx(c):
        return (nchunks - 1 - c) if reverse else c

    cur_map = lambda b, c: (b, 0, cidx(c), 0)
    const2 = lambda b, c: (0, 0)
    const3 = lambda b, c: (0, 0, 0)
    cw, cb, wa, wx, ba, bx, lam, gn = params
    halo = (1, n_slab, SUBLANES, LANES)
    full = (1, n_slab, RNN_CHUNK, LANES)
    gate_specs = [
        pl.BlockSpec((D_RNN // GATE_BLK, GATE_BLK, GATE_BLK), const3),
        pl.BlockSpec((D_RNN // GATE_BLK, GATE_BLK, GATE_BLK), const3),
        pl.BlockSpec((1, D_RNN), const2),
        pl.BlockSpec((1, D_RNN), const2),
        pl.BlockSpec((1, D_RNN), const2),
    ]
    slab_scr = pltpu.VMEM((n_slab, RNN_CHUNK, LANES), F32)
    carry_scr = pltpu.VMEM((n_slab, SUBLANES, LANES), F32)
    if final:
        in_specs = [pl.BlockSpec(full, cur_map)] * 3 + gate_specs + [pl.BlockSpec((1, D_RNN), const2)]
        args = [x4, *extra, wa, wx, ba, bx, lam, gn]
        scratch = [slab_scr, slab_scr, slab_scr, carry_scr]
        out_specs = pl.BlockSpec((1, RNN_CHUNK, D_RNN), lambda b, c: (b, cidx(c), 0))
        out_shape = jax.ShapeDtypeStruct((bsz, s, D_RNN), BF16)
    else:
        pa_map = lambda b, c: (b, 0, jnp.maximum(cidx(c) * hb - 2, 0), 0)
        pb_map = lambda b, c: (b, 0, jnp.maximum(cidx(c) * hb - 1, 0), 0)
        nx_map = lambda b, c: (b, 0, jnp.minimum((cidx(c) + 1) * hb, nhalo - 1), 0)
        in_specs = [pl.BlockSpec(halo, pa_map), pl.BlockSpec(halo, pb_map), pl.BlockSpec(full, cur_map),
                    pl.BlockSpec(halo, nx_map),
                    pl.BlockSpec((CONV_WIDTH, D_RNN), const2), pl.BlockSpec((1, D_RNN), const2)] + gate_specs
        args = [x4, x4, x4, x4, cw, cb, wa, wx, ba, bx, lam]
        scratch = [slab_scr, slab_scr, carry_scr]
        out_specs = [pl.BlockSpec(full, cur_map)] * 2
        out_shape = [jax.ShapeDtypeStruct(x4.shape, F32)] * 2
    return pl.pallas_call(
        functools.partial(_rnn_kernel, reverse=reverse, final=final, nchunks=nchunks),
        grid=(bsz, nchunks),
        in_specs=in_specs,
        out_specs=out_specs,
        out_shape=out_shape,
        scratch_shapes=scratch,
        compiler_params=pltpu.CompilerParams(
            dimension_semantics=("arbitrary", "arbitrary"), vmem_limit_bytes=VMEM_LIMIT),
        name="rnn_bwd" if reverse else "rnn_fwd",
    )(*args)


def _gate_weights(w):
    per = GATE_BLK // RNN_BLOCK
    w5 = (0.5 * w).reshape(D_RNN // GATE_BLK, per, RNN_BLOCK, 1, RNN_BLOCK)
    on_diag = jnp.asarray(np.eye(per, dtype=bool)).reshape(1, per, 1, per, 1)
    dense = jnp.where(on_diag, w5, 0.0)
    return dense.reshape(D_RNN // GATE_BLK, GATE_BLK, GATE_BLK).astype(BF16)


def _attn_kernel(q_ref, kp_ref, k_ref, kn_ref, vp_ref, v_ref, vn_ref, bias_ref, o_ref, lse_ref, kbuf, vbuf,
                 *, sub_len, tiles_per_class):
    i = pl.program_id(2)
    n_tiles = sub_len // Q_TILE
    n_classes = q_ref.shape[1]
    q_rows = tiles_per_class * Q_TILE
    for buf, before, own, after in ((kbuf, kp_ref, k_ref, kn_ref), (vbuf, vp_ref, v_ref, vn_ref)):
        for cl in range(n_classes):
            buf[cl, 0:HALF_STEPS, :] = before[0, cl]
            buf[cl, HALF_STEPS:HALF_STEPS + q_rows, :] = own[0, cl]
            buf[cl, HALF_STEPS + q_rows:, :] = after[0, cl]
    buf_origin = i * q_rows - HALF_STEPS
    lane = lax.broadcasted_iota(jnp.int32, (Q_TILE, LANES), 1)
    heads_per_slab = LANES // HEAD_DIM
    first_half = lane < HEAD_DIM
    for cl in range(n_classes):
        for t in range(tiles_per_class):
            rows = slice(t * Q_TILE, (t + 1) * Q_TILE)
            if n_classes == 1:
                out_rows = rows
            else:
                out_rows = pl.ds(t * Q_TILE * n_classes + cl, Q_TILE, stride=n_classes)
            q = q_ref[0, cl, rows, :]
            tile = i * tiles_per_class + t
            variant = jnp.where(tile == 0, FIRST, jnp.where(tile == n_tiles - 1, LAST, INTERIOR))
            start = jnp.clip(tile * Q_TILE - HALF_STEPS, 0, sub_len - K_WIN) - buf_origin
            start = pl.multiple_of(start, K_TILE)
            kk = kbuf[cl, pl.ds(start, K_WIN), :]
            vv = vbuf[cl, pl.ds(start, K_WIN), :]
            for p in range(D_ATTN // LANES):
                sl = slice(p * LANES, (p + 1) * LANES)
                qp, kp, vp = q[:, sl], kk[:, sl], vv[:, sl]
                zero = jnp.zeros_like(qp)
                q2 = jnp.concatenate([jnp.where(first_half, qp, zero), jnp.where(first_half, zero, qp)], axis=0)
                s = lax.dot_general(q2, kp, (((1,), (1,)), ((), ())), preferred_element_type=F32)
                h0 = p * heads_per_slab
                s = s + bias_ref[variant, h0:h0 + heads_per_slab].reshape(heads_per_slab * Q_TILE, K_WIN)
                m = jnp.max(s, axis=-1, keepdims=True)
                e = jnp.exp2(s - m)
                l = jnp.sum(e, axis=-1, keepdims=True)
                o2 = jnp.dot(e.astype(BF16), vp, preferred_element_type=F32)
                shape = (Q_TILE, LANES)
                pick = lambda a: jnp.where(first_half, jnp.broadcast_to(a[:Q_TILE], shape),
                                           jnp.broadcast_to(a[Q_TILE:], shape))
                l_slab = pick(l)
                o_ref[0, p, out_rows, :] = pick(o2) * (1.0 / l_slab)
                lse_ref[0, p, out_rows, :] = pick(m) + jnp.log(l_slab) * LOG2E


def _attn_pattern(qc, kc, vc, bias, pat):
    bsz, groups, n_classes, sub_len, _ = qc.shape
    tiles_per_class = TILES_PER_STEP // n_classes
    q_rows = tiles_per_class * Q_TILE
    n_slab = D_ATTN // LANES
    qmap = lambda b, g, i: (b, g, 0, i, 0)
    halo_per_step = q_rows // HALF_STEPS
    n_halo = sub_len // HALF_STEPS
    before = lambda b, g, i: (b, g, 0, jnp.maximum(i * halo_per_step - 1, 0), 0)
    after = lambda b, g, i: (b, g, 0, jnp.minimum((i + 1) * halo_per_step, n_halo - 1), 0)
    own_spec = pl.BlockSpec((1, None, n_classes, q_rows, D_ATTN), qmap)
    kv_specs = [pl.BlockSpec((1, None, n_classes, HALF_STEPS, D_ATTN), before), own_spec,
                pl.BlockSpec((1, None, n_classes, HALF_STEPS, D_ATTN), after)]
    in_specs = [own_spec] + kv_specs + kv_specs
    in_specs.append(pl.BlockSpec((None, 3, N_HEADS, Q_TILE, K_WIN), lambda b, g, i: (pat, 0, 0, 0, 0)))
    window_buf = pltpu.VMEM((n_classes, q_rows + 2 * HALF_STEPS, D_ATTN), BF16)
    out_spec = pl.BlockSpec((1, None, n_slab, n_classes * q_rows, LANES), qmap)
    return pl.pallas_call(
        functools.partial(_attn_kernel, sub_len=sub_len, tiles_per_class=tiles_per_class),
        grid=(bsz, groups, sub_len // q_rows),
        in_specs=in_specs,
        out_specs=[out_spec] * 2,
        out_shape=[jax.ShapeDtypeStruct((bsz, groups, n_slab, n_classes * sub_len, LANES), F32)] * 2,
        scratch_shapes=[window_buf, window_buf],
        compiler_params=pltpu.CompilerParams(
            dimension_semantics=("arbitrary",) * 3, vmem_limit_bytes=VMEM_LIMIT),
        name=f"attn_d{groups * n_classes}",
    )(qc, kc, kc, kc, vc, vc, vc, bias)


def _token_order(ref, scr, slab):
    groups = ref.shape[1]
    if groups == 1:
        return ref[0, 0, slab]
    for g in range(groups):
        scr[slab, pl.ds(g, OUT_TILE // groups, stride=groups), :] = ref[0, g, slab]
    return scr[slab]


def _out_mlp_kernel(x_ref, mr_ref, o1, o2, o3, l1, l2, l3, ga_ref, wo_hbm, gm_ref, wu_hbm, wd_hbm,
                    gf_ref, out_ref, s_o3, s_l3, wo_ref, wu_ref, wd_ref, sem):
    first = (pl.program_id(0) == 0) & (pl.program_id(1) == 0)
    copies = [pltpu.make_async_copy(src, dst, sem.at[n])
              for n, (src, dst) in enumerate(((wo_hbm, wo_ref), (wu_hbm, wu_ref), (wd_hbm, wd_ref)))]

    @pl.when(first)
    def _():
        for cp in copies:
            cp.start()

    ys = []
    for slab in range(D_ATTN // LANES):
        a1 = _token_order(l1, None, slab)
        a2 = _token_order(l2, None, slab)
        a3 = _token_order(l3, s_l3, slab)
        mx = jnp.maximum(jnp.maximum(a1, a2), a3)
        e1, e2, e3 = jnp.exp2(a1 - mx), jnp.exp2(a2 - mx), jnp.exp2(a3 - mx)
        ys.append((e1 * _token_order(o1, None, slab) + e2 * _token_order(o2, None, slab)
                   + e3 * _token_order(o3, s_o3, slab)) / (e1 + e2 + e3))
    mix_attn = _rms(jnp.concatenate(ys, axis=-1), ga_ref[...]).astype(BF16)

    @pl.when(first)
    def _():
        copies[0].wait()

    proj = jnp.dot(mr_ref[0], wo_ref[0:D_RNN, :], preferred_element_type=F32)
    proj = proj + jnp.dot(mix_attn, wo_ref[D_RNN:D_RNN + D_ATTN, :], preferred_element_type=F32)
    x1 = proj + x_ref[0]
    h = _rms(x1, gm_ref[...]).astype(BF16)

    @pl.when(first)
    def _():
        copies[1].wait()
        copies[2].wait()

    ff = None
    for c in range(D_FF // FF_CHUNK):
        cols = slice(c * FF_CHUNK, (c + 1) * FF_CHUNK)
        z = jnp.dot(h, wu_ref[:, cols], preferred_element_type=F32)
        z = jnp.square(jnp.maximum(z, 0.0)).astype(BF16)
        d = jnp.dot(z, wd_ref[cols, :], preferred_element_type=F32)
        ff = d if ff is None else ff + d
    out_ref[0] = _rms(ff + x1, gf_ref[...])


def _out_mlp(x3, mix_rnn, os_, lses, ga, wo, gm, wu, wd, gf):
    bsz, s, _ = x3.shape
    tm = OUT_TILE
    row = lambda b, i: (b, i, 0)
    const = lambda b, i: (0, 0)
    in_hbm = pl.BlockSpec(memory_space=pl.ANY)
    n_slab = D_ATTN // LANES
    attn_specs = [pl.BlockSpec((1, o.shape[1], n_slab, tm // o.shape[1], LANES), lambda b, i: (b, 0, 0, i, 0))
                  for o in os_]
    in_specs = [pl.BlockSpec((1, tm, D_MODEL), row), pl.BlockSpec((1, tm, D_RNN), row)]
    in_specs += attn_specs + attn_specs
    in_specs += [
        pl.BlockSpec((1, D_ATTN), const),
        in_hbm,
        pl.BlockSpec((1, D_MODEL), const),
        in_hbm,
        in_hbm,
        pl.BlockSpec((1, D_MODEL), const),
    ]
    attn_scr = pltpu.VMEM((n_slab, tm, LANES), F32)
    weight_scr = [pltpu.VMEM(w.shape, w.dtype) for w in (wo, wu, wd)]
    return pl.pallas_call(
        _out_mlp_kernel,
        grid=(bsz, s // tm),
        in_specs=in_specs,
        out_specs=pl.BlockSpec((1, tm, D_MODEL), row),
        out_shape=jax.ShapeDtypeStruct((bsz, s, D_MODEL), F32),
        scratch_shapes=[attn_scr, attn_scr, *weight_scr, pltpu.SemaphoreType.DMA((len(weight_scr),))],
        compiler_params=pltpu.CompilerParams(
            dimension_semantics=("arbitrary", "arbitrary"), vmem_limit_bytes=BIG_VMEM_LIMIT),
        name="out_mlp",
    )(x3, mix_rnn, *os_, *lses, ga, wo, gm, wu, wd, gf)


def kernel(x, attn_norm_g, w_in, conv_w, conv_b, lru_wa_fwd, lru_ba_fwd, lru_wx_fwd, lru_bx_fwd, lru_lam_fwd, lru_wa_bwd, lru_ba_bwd, lru_wx_bwd, lru_bx_bwd, lru_lam_bwd, rel_bias, norm_rnn_g, norm_attn_g, w_out, mlp_norm_g, w_up, w_down, final_norm_g):
    depth = w_in.shape[0]
    assert depth == 1, "the final RMSNorm is fused into the single layer's last call"
    l = 0
    n_pat = len(PATTERNS)
    row = lambda v: v.reshape(1, -1)
    bias = _bias_tables(rel_bias)
    proj = _in_proj(x, row(attn_norm_g[l]), w_in[l], (w_out[l], w_up[l], w_down[l]), tm=IN_TILE)
    xr, gate = proj[0], proj[1]
    qs, ks, vs = (proj[2 + a * n_pat:2 + (a + 1) * n_pat] for a in range(3))
    w_out_bf, w_up_bf, w_down_bf = proj[2 + 3 * n_pat:]
    half = lambda v: row(0.5 * v)
    fwd = (conv_w[l], row(conv_b[l]), _gate_weights(lru_wa_fwd[l]), _gate_weights(lru_wx_fwd[l]),
           half(lru_ba_fwd[l]), half(lru_bx_fwd[l]), row(lru_lam_fwd[l]), None)
    bwd = (conv_w[l], row(conv_b[l]), _gate_weights(lru_wa_bwd[l]), _gate_weights(lru_wx_bwd[l]),
           half(lru_ba_bwd[l]), half(lru_bx_bwd[l]), row(lru_lam_bwd[l]), row(norm_rnn_g[l]))
    h_f, xc = _rnn_pass(xr, (), fwd, reverse=False, final=False)
    mix_rnn = _rnn_pass(xc, (h_f, gate), bwd, reverse=True, final=True)
    os_, lses = [], []
    for pat in range(n_pat):
        o, lse = _attn_pattern(qs[pat], ks[pat], vs[pat], bias, pat)
        os_.append(o)
        lses.append(lse)
    return _out_mlp(x, mix_rnn, os_, lses, row(norm_attn_g[l]), w_out_bf,
                    row(mlp_norm_g[l]), w_up_bf, w_down_bf, row(final_norm_g))
```

```python
import functools
import math

import numpy as np
import jax
import jax.numpy as jnp
from jax import lax
from jax.experimental import pallas as pl
from jax.experimental.pallas import tpu as pltpu

F32 = jnp.float32
BF16 = jnp.bfloat16

D_MODEL = 1024
D_RNN = 512
N_RNN_BLOCKS = 8
RNN_BLOCK = D_RNN // N_RNN_BLOCKS
CONV_WIDTH = 4
CONV_LEFT = 2
LRU_C = 8.0
N_HEADS = 8
HEAD_DIM = 64
D_ATTN = N_HEADS * HEAD_DIM
PATTERNS = ((128, 1), (512, 4), (2048, 16))
HALF_STEPS = 64
N_BUCKETS = 32
MAX_DISTANCE = 1024
D_IN = 2 * D_RNN + 3 * D_ATTN
D_FF = 4 * D_MODEL
EPS = 1e-6
NEG_INF = -1e30

LANES = 128
SUBLANES = 8
MXU_DIM = 256

Q_TILE = 128
K_TILE = 64
K_WIN = Q_TILE + 2 * HALF_STEPS
TILES_PER_STEP = 16
CLASS_GROUP = 4
LOG2E = math.log2(math.e)
INTERIOR, FIRST, LAST = 0, 1, 2
GATE_BLK = MXU_DIM

IN_TILE = 1024
RNN_CHUNK = 1024
FWD_CHUNKS_PER_STEP = 1
BWD_CHUNKS_PER_STEP = 2
STRIPE_ROWS = RNN_CHUNK // SUBLANES
OUT_TILE = 512
FF_CHUNK = 1024
assert RNN_CHUNK % IN_TILE == 0 and (CONV_WIDTH, CONV_LEFT) == (4, 2)

VMEM_LIMIT = 48 * 1024 * 1024
BIG_VMEM_LIMIT = 56 * 1024 * 1024
IN_PROJ_VMEM_LIMIT = 60 * 1024 * 1024


def _rms(x, g):
    ms = jnp.mean(x * x, axis=-1, keepdims=True)
    return x * lax.rsqrt(ms + EPS) * g


def _t5_bucket_np(rel):
    nb = N_BUCKETS // 2
    max_exact = nb // 2
    ret = np.where(rel > 0, nb, 0)
    n = np.abs(rel)
    nf = np.maximum(n, 1).astype(np.float32)
    large = max_exact + (np.log(nf / np.float32(max_exact)) / np.float32(math.log(MAX_DISTANCE / max_exact))
                         * np.float32(nb - max_exact)).astype(np.int32)
    large = np.minimum(large, nb - 1)
    return ret + np.where(n < max_exact, n, large)


def _bucket_index_rows():
    step = np.arange(K_WIN) - HALF_STEPS
    rows = []
    for _, dil in PATTERNS:
        b = _t5_bucket_np((step * dil).astype(np.int32))
        rows.append(np.where(np.abs(step) <= HALF_STEPS, b, -1))
    return np.stack(rows, axis=0).astype(np.int32)[:, None, :]


def _bias_table_kernel(idx_ref, rb_ref, out_ref):
    idx = idx_ref[0]
    col = lax.broadcasted_iota(jnp.int32, (Q_TILE, K_WIN), 1)
    hit = [idx == b for b in range(N_BUCKETS)]
    for h in range(N_HEADS):
        first_row = jnp.full(idx.shape, NEG_INF, F32)
        for b in range(N_BUCKETS):
            first_row = jnp.where(hit[b], rb_ref[b, h] * LOG2E, first_row)
        acc = pltpu.roll(jnp.broadcast_to(first_row, (Q_TILE, K_WIN)), 0, 1, stride=1, stride_axis=0)
        out_ref[0, INTERIOR, h] = acc
        out_ref[0, FIRST, h] = jnp.where(col < K_WIN - HALF_STEPS, pltpu.roll(acc, K_WIN - HALF_STEPS, 1), NEG_INF)
        out_ref[0, LAST, h] = jnp.where(col >= HALF_STEPS, pltpu.roll(acc, HALF_STEPS, 1), NEG_INF)


def _bias_tables(rel_bias):
    idx = jnp.asarray(_bucket_index_rows())
    n_pat = idx.shape[0]
    return pl.pallas_call(
        _bias_table_kernel,
        grid=(n_pat,),
        in_specs=[
            pl.BlockSpec((1, 1, K_WIN), lambda g: (g, 0, 0)),
            pl.BlockSpec(memory_space=pltpu.SMEM),
        ],
        out_specs=pl.BlockSpec((1, 3, N_HEADS, Q_TILE, K_WIN), lambda g: (g, 0, 0, 0, 0)),
        out_shape=jax.ShapeDtypeStruct((n_pat, 3, N_HEADS, Q_TILE, K_WIN), F32),
        name="bias_table",
    )(idx, rel_bias)


def _stripe_store(ref, p, part):
    stripes = IN_TILE // STRIPE_ROWS
    for st in range(stripes):
        for j in range(D_RNN // LANES):
            ref[0, j, pl.ds(part * stripes + st, STRIPE_ROWS, stride=SUBLANES), :] = (
                p[st * STRIPE_ROWS:(st + 1) * STRIPE_ROWS, j * LANES:(j + 1) * LANES])


PERM_ROWS = MXU_DIM


def _class_permutations():
    mats = []
    for _, dil in PATTERNS:
        if dil == 1:
            continue
        per = PERM_ROWS // dil
        p = np.zeros((PERM_ROWS, PERM_ROWS), np.float32)
        for r in range(dil):
            for m in range(per):
                p[r * per + m, m * dil + r] = 1.0
        mats.append(p)
    return np.stack(mats)


def _in_proj_kernel(x_ref, g_ref, w_ref, perm_ref, *rest, tm, n_later):
    n_pat = len(PATTERNS)
    later_in, rest = rest[:n_later], rest[n_later:]
    xr_ref, gate_ref, rest = rest[0], rest[1], rest[2:]
    outs = [rest[a * n_pat:(a + 1) * n_pat] for a in range(3)]
    later_out = rest[3 * n_pat:3 * n_pat + n_later]
    h_scr = rest[3 * n_pat + n_later]
    for src, dst in zip(later_in, later_out):
        dst[...] = src[...].astype(BF16)
    w_scr = rest[3 * n_pat + n_later + 1]

    @pl.when((pl.program_id(0) == 0) & (pl.program_id(1) == 0))
    def _():
        w_scr[...] = w_ref[...].astype(BF16)

    h_scr[...] = _rms(x_ref[0], g_ref[...]).astype(BF16)

    def seg(lo, width):
        return jnp.dot(h_scr[...], w_scr[:, lo:lo + width], preferred_element_type=F32)

    for a, scale in enumerate((HEAD_DIM ** -0.5 * LOG2E, 1.0, 1.0)):
        p = seg(2 * D_RNN + a * D_ATTN, D_ATTN)
        if scale != 1.0:
            p = p * scale
        pb = p.astype(BF16)
        dilated = 0
        for (_, dil), o_ref in zip(PATTERNS, outs[a]):
            if dil == 1:
                o_ref[0, 0, 0] = pb
                continue
            per = PERM_ROWS // dil
            for blk in range(tm // PERM_ROWS):
                y = jnp.dot(perm_ref[dilated], pb[blk * PERM_ROWS:(blk + 1) * PERM_ROWS, :],
                            preferred_element_type=F32).astype(BF16)
                groups = o_ref.shape[1]
                for r in range(dil):
                    o_ref[0, r % groups, r // groups, blk * per:(blk + 1) * per, :] = y[r * per:(r + 1) * per, :]
            dilated += 1
    g = seg(D_RNN, D_RNN)
    c = math.sqrt(2.0 / math.pi)
    half_g = 0.5 * g
    gelu = half_g + half_g * jnp.tanh(g * (c + (c * 0.044715) * (g * g)))
    part = pl.program_id(1) % (RNN_CHUNK // IN_TILE)
    _stripe_store(gate_ref, gelu, part)
    _stripe_store(xr_ref, seg(0, D_RNN), part)


def _in_proj(x3, g, w_in_bf, later_weights, tm):
    assert tm == IN_TILE
    bsz, s, _ = x3.shape
    per_seq = s // tm
    n_steps = bsz * per_seq
    row_slice = lambda b, i: (b * per_seq + i, 0)
    later_specs = [pl.BlockSpec((w.shape[0] // n_steps, w.shape[1]), row_slice) for w in later_weights]
    later_shapes = [jax.ShapeDtypeStruct(w.shape, BF16) for w in later_weights]
    row = lambda b, i: (b, i, 0)
    cls = lambda b, i: (b, 0, 0, i, 0)
    chunk = lambda b, i: (b, 0, i // (RNN_CHUNK // IN_TILE), 0)
    const = lambda b, i: (0, 0)
    perms = jnp.asarray(_class_permutations(), BF16)
    qkv_specs, qkv_shapes = [], []
    for _ in range(3):
        for _, dil in PATTERNS:
            groups, per_group = dil // min(dil, CLASS_GROUP), min(dil, CLASS_GROUP)
            qkv_specs.append(pl.BlockSpec((1, groups, per_group, tm // dil, D_ATTN), cls))
            qkv_shapes.append(jax.ShapeDtypeStruct((bsz, groups, per_group, s // dil, D_ATTN), BF16))
    return pl.pallas_call(
        functools.partial(_in_proj_kernel, tm=tm, n_later=len(later_weights)),
        grid=(bsz, per_seq),
        in_specs=[
            pl.BlockSpec((1, tm, D_MODEL), row),
            pl.BlockSpec((1, D_MODEL), const),
            pl.BlockSpec((D_MODEL, D_IN), const, pipeline_mode=pl.Buffered(1)),
            pl.BlockSpec(perms.shape, lambda b, i: (0, 0, 0)),
        ] + later_specs,
        out_specs=[pl.BlockSpec((1, D_RNN // LANES, RNN_CHUNK, LANES), chunk)] * 2 + qkv_specs + later_specs,
        out_shape=[jax.ShapeDtypeStruct((bsz, D_RNN // LANES, s, LANES), F32)] * 2 + qkv_shapes + later_shapes,
        scratch_shapes=[pltpu.VMEM((tm, D_MODEL), BF16), pltpu.VMEM((D_MODEL, D_IN), BF16)],
        compiler_params=pltpu.CompilerParams(
            dimension_semantics=("arbitrary", "arbitrary"), vmem_limit_bytes=IN_PROJ_VMEM_LIMIT),
        name="in_proj",
    )(x3, g, w_in_bf, perms, *later_weights)


def _scan_block(a, b, reverse):
    n = a.shape[0]
    row = lax.broadcasted_iota(jnp.int32, a.shape, 0)
    s = 1
    while s < n:
        if reverse:
            ra = pltpu.roll(a, n - s, 0)
            rb = pltpu.roll(b, n - s, 0)
            m = row < n - s
        else:
            ra = pltpu.roll(a, s, 0)
            rb = pltpu.roll(b, s, 0)
            m = row >= s
        b = jnp.where(m, a * rb + b, b)
        a = jnp.where(m, a * ra, a)
        s *= 2
    return a, b


def _rnn_kernel(*refs, reverse, final, nchunks, sc=None):
    if final:
        (xc_ref, hf_ref, gate_ref, wa_ref, wx_ref, ba_ref, bx_ref, lam_ref, gn_ref,
         out_ref, a_scr, b_scr, h_scr, carry) = refs
    else:
        (pa_ref, pb_ref, cur_ref, nx_ref, cw_ref, cb_ref, wa_ref, wx_ref, ba_ref, bx_ref, lam_ref,
         out_ref, xc_out_ref, a_scr, b_scr, carry) = refs
        h_scr = None

    per_step = BWD_CHUNKS_PER_STEP if final else FWD_CHUNKS_PER_STEP
    scan_order = range(per_step - 1, -1, -1) if reverse else range(per_step)
    if sc is None:
        for c in scan_order:
            _rnn_kernel(*refs, reverse=reverse, final=final, nchunks=nchunks, sc=c)
        return
    rows = pl.ds(sc * RNN_CHUNK, RNN_CHUNK)
    if final:
        xc_ref, hf_ref, gate_ref = (r.at[:, :, rows] for r in (xc_ref, hf_ref, gate_ref))
        out_ref = out_ref.at[:, rows]
    else:
        step_ref = cur_ref
        cur_ref, out_ref, xc_out_ref = (r.at[:, :, rows] for r in (cur_ref, out_ref, xc_out_ref))

    step = pl.program_id(1)
    block = (nchunks // per_step - 1 - step) if reverse else step
    chunk = block * per_step + sc
    n_slab = D_RNN // LANES
    nv = STRIPE_ROWS
    sub = lax.broadcasted_iota(jnp.int32, (SUBLANES, LANES), 0)
    top, bot = SUBLANES - 1, 0

    if sc == scan_order[0]:
        @pl.when(step == 0)
        def _():
            carry[...] = jnp.zeros_like(carry)

    if final:
        xc = [xc_ref[0, j] for j in range(n_slab)]
    else:
        xc = []
        for j in range(n_slab):
            lanes = slice(j * LANES, (j + 1) * LANES)
            x = cur_ref[0, j].reshape(nv, SUBLANES, LANES)
            lo = sc * RNN_CHUNK
            if sc == 0:
                pa = jnp.where(chunk > 0, pa_ref[0, j], 0.0)
                pb = jnp.where(chunk > 0, pb_ref[0, j], 0.0)
            else:
                pa = step_ref[0, j, lo - 2 * SUBLANES:lo - SUBLANES, :]
                pb = step_ref[0, j, lo - SUBLANES:lo, :]
            if sc == per_step - 1:
                nx = jnp.where(chunk < nchunks - 1, nx_ref[0, j], 0.0)
            else:
                nx = step_ref[0, j, lo + RNN_CHUNK:lo + RNN_CHUNK + SUBLANES, :]
            xm2 = pltpu.roll(jnp.where(sub == top, pa, x[nv - 2]), 1, 0)
            xm1 = pltpu.roll(jnp.where(sub == top, pb, x[nv - 1]), 1, 0)
            xp1 = pltpu.roll(jnp.where(sub == bot, nx, x[0]), SUBLANES - 1, 0)
            xext = jnp.concatenate([xm2[None], xm1[None], x, xp1[None]], axis=0)
            acc = cb_ref[:, lanes] + xext[0:nv] * cw_ref[0:1, lanes]
            for k in range(1, CONV_WIDTH):
                acc = acc + xext[k:k + nv] * cw_ref[k:k + 1, lanes]
            xc.append(acc.reshape(RNN_CHUNK, LANES))
            xc_out_ref[0, j] = xc[j]

    nlam = -lam_ref[...]
    softplus = jnp.maximum(nlam, 0.0) + jnp.log1p(jnp.exp(-jnp.abs(nlam)))
    half_coef = (0.5 * LRU_C) * softplus
    slabs_per_blk = GATE_BLK // LANES
    for jj in range(D_RNN // GATE_BLK):
        blk = slice(jj * GATE_BLK, (jj + 1) * GATE_BLK)
        xj = jnp.concatenate(xc[jj * slabs_per_blk:(jj + 1) * slabs_per_blk], axis=-1)
        xjb = xj.astype(BF16)
        za = jnp.dot(xjb, wa_ref[jj], preferred_element_type=F32) + ba_ref[:, blk]
        zx = jnp.dot(xjb, wx_ref[jj], preferred_element_type=F32) + bx_ref[:, blk]
        i = 0.5 * jnp.tanh(zx) + 0.5
        neg_log_a = half_coef[:, blk] * jnp.tanh(za) + half_coef[:, blk]
        a = jnp.exp2(neg_log_a * -LOG2E)
        u = jnp.tanh(neg_log_a) * (1.0 + a * a)
        root = jnp.where(u > 0.0, u * lax.rsqrt(u), 0.0)
        b = root * (i * xj)
        for t in range(slabs_per_blk):
            a_scr[jj * slabs_per_blk + t] = a[:, t * LANES:(t + 1) * LANES]
            b_scr[jj * slabs_per_blk + t] = b[:, t * LANES:(t + 1) * LANES]

    order = range(nv - 1, -1, -1) if reverse else range(nv)
    group = lambda v: slice(v * SUBLANES, (v + 1) * SUBLANES)

    h = [jnp.zeros((SUBLANES, LANES), F32)] * n_slab
    p = [jnp.ones((SUBLANES, LANES), F32)] * n_slab
    for v in order:
        for j in range(n_slab):
            av = a_scr[j, group(v), :]
            h[j] = av * h[j] + b_scr[j, group(v), :]
            p[j] = av * p[j]

    start = []
    for j in range(n_slab):
        pc, hc = _scan_block(p[j], h[j], reverse)
        prev = carry[j]
        if reverse:
            seed = jnp.broadcast_to(prev[bot:bot + 1, :], (SUBLANES, LANES))
            ends = hc + pc * seed
            start.append(jnp.where(sub == top, pltpu.roll(prev, SUBLANES - 1, 0),
                                   pltpu.roll(ends, SUBLANES - 1, 0)))
        else:
            seed = jnp.broadcast_to(prev[top:top + 1, :], (SUBLANES, LANES))
            ends = hc + pc * seed
            start.append(jnp.where(sub == bot, pltpu.roll(prev, 1, 0), pltpu.roll(ends, 1, 0)))
        carry[j] = ends

    dst = h_scr if final else None
    h = start
    for v in order:
        for j in range(n_slab):
            h[j] = a_scr[j, group(v), :] * h[j] + b_scr[j, group(v), :]
            if final:
                dst[j, group(v), :] = (hf_ref[0, j, group(v), :] + h[j]) * gate_ref[0, j, group(v), :]
            else:
                out_ref[0, j, group(v), :] = h[j]

    if final:
        ys = [h_scr[j] for j in range(n_slab)]
        ss = ys[0] * ys[0]
        for j in range(1, n_slab):
            ss = ss + ys[j] * ys[j]
        inv = lax.rsqrt(jnp.sum(ss, axis=-1, keepdims=True) * (1.0 / D_RNN) + EPS)
        for j in range(n_slab):
            lanes = slice(j * LANES, (j + 1) * LANES)
            h_scr[j] = ys[j] * inv * gn_ref[:, lanes]
        for st in range(SUBLANES):
            for j in range(n_slab):
                out_ref[0, st * nv:(st + 1) * nv, j * LANES:(j + 1) * LANES] = (
                    h_scr[j, pl.ds(st, nv, stride=SUBLANES), :].astype(BF16))


def _rnn_pass(x4, extra, params, reverse, final):
    bsz, n_slab, s, _ = x4.shape
    nchunks = s // RNN_CHUNK
    step_rows = (BWD_CHUNKS_PER_STEP if final else FWD_CHUNKS_PER_STEP) * RNN_CHUNK
    nsteps = s // step_rows
    hb = step_rows // SUBLANES
    nhalo = s // SUBLANES

    def cidx(c):
        return (nsteps - 1 - c) if reverse else c

    cur_map = lambda b, c: (b, 0, cidx(c), 0)
    const2 = lambda b, c: (0, 0)
    const3 = lambda b, c: (0, 0, 0)
    cw, cb, wa, wx, ba, bx, lam, gn = params
    halo = (1, n_slab, SUBLANES, LANES)
    full = (1, n_slab, step_rows, LANES)
    gate_specs = [
        pl.BlockSpec((D_RNN // GATE_BLK, GATE_BLK, GATE_BLK), const3),
        pl.BlockSpec((D_RNN // GATE_BLK, GATE_BLK, GATE_BLK), const3),
        pl.BlockSpec((1, D_RNN), const2),
        pl.BlockSpec((1, D_RNN), const2),
        pl.BlockSpec((1, D_RNN), const2),
    ]
    slab_scr = pltpu.VMEM((n_slab, RNN_CHUNK, LANES), F32)
    carry_scr = pltpu.VMEM((n_slab, SUBLANES, LANES), F32)
    if final:
        in_specs = [pl.BlockSpec(full, cur_map)] * 3 + gate_specs + [pl.BlockSpec((1, D_RNN), const2)]
        args = [x4, *extra, wa, wx, ba, bx, lam, gn]
        scratch = [slab_scr, slab_scr, slab_scr, carry_scr]
        out_specs = pl.BlockSpec((1, step_rows, D_RNN), lambda b, c: (b, cidx(c), 0))
        out_shape = jax.ShapeDtypeStruct((bsz, s, D_RNN), BF16)
    else:
        pa_map = lambda b, c: (b, 0, jnp.maximum(cidx(c) * hb - 2, 0), 0)
        pb_map = lambda b, c: (b, 0, jnp.maximum(cidx(c) * hb - 1, 0), 0)
        nx_map = lambda b, c: (b, 0, jnp.minimum((cidx(c) + 1) * hb, nhalo - 1), 0)
        in_specs = [pl.BlockSpec(halo, pa_map), pl.BlockSpec(halo, pb_map), pl.BlockSpec(full, cur_map),
                    pl.BlockSpec(halo, nx_map),
                    pl.BlockSpec((CONV_WIDTH, D_RNN), const2), pl.BlockSpec((1, D_RNN), const2)] + gate_specs
        args = [x4, x4, x4, x4, cw, cb, wa, wx, ba, bx, lam]
        scratch = [slab_scr, slab_scr, carry_scr]
        out_specs = [pl.BlockSpec(full, cur_map)] * 2
        out_shape = [jax.ShapeDtypeStruct(x4.shape, F32)] * 2
    return pl.pallas_call(
        functools.partial(_rnn_kernel, reverse=reverse, final=final, nchunks=nchunks),
        grid=(bsz, nsteps),
        in_specs=in_specs,
        out_specs=out_specs,
        out_shape=out_shape,
        scratch_shapes=scratch,
        compiler_params=pltpu.CompilerParams(
            dimension_semantics=("arbitrary", "arbitrary"), vmem_limit_bytes=VMEM_LIMIT),
        name="rnn_bwd" if reverse else "rnn_fwd",
    )(*args)


def _gate_weights(w):
    per = GATE_BLK // RNN_BLOCK
    w5 = (0.5 * w).reshape(D_RNN // GATE_BLK, per, RNN_BLOCK, 1, RNN_BLOCK)
    on_diag = jnp.asarray(np.eye(per, dtype=bool)).reshape(1, per, 1, per, 1)
    dense = jnp.where(on_diag, w5, 0.0)
    return dense.reshape(D_RNN // GATE_BLK, GATE_BLK, GATE_BLK).astype(BF16)


def _attn_kernel(q_ref, kp_ref, k_ref, kn_ref, vp_ref, v_ref, vn_ref, bias_ref, o_ref, lse_ref, kbuf, vbuf,
                 *, sub_len, tiles_per_class):
    i = pl.program_id(2)
    n_tiles = sub_len // Q_TILE
    n_classes = q_ref.shape[1]
    q_rows = tiles_per_class * Q_TILE
    for buf, before, own, after in ((kbuf, kp_ref, k_ref, kn_ref), (vbuf, vp_ref, v_ref, vn_ref)):
        for cl in range(n_classes):
            buf[cl, 0:HALF_STEPS, :] = before[0, cl]
            buf[cl, HALF_STEPS:HALF_STEPS + q_rows, :] = own[0, cl]
            buf[cl, HALF_STEPS + q_rows:, :] = after[0, cl]
    buf_origin = i * q_rows - HALF_STEPS
    lane = lax.broadcasted_iota(jnp.int32, (Q_TILE, LANES), 1)
    heads_per_slab = LANES // HEAD_DIM
    first_half = lane < HEAD_DIM
    for cl in range(n_classes):
        for t in range(tiles_per_class):
            rows = slice(t * Q_TILE, (t + 1) * Q_TILE)
            if n_classes == 1:
                out_rows = rows
            else:
                out_rows = pl.ds(t * Q_TILE * n_classes + cl, Q_TILE, stride=n_classes)
            q = q_ref[0, cl, rows, :]
            tile = i * tiles_per_class + t
            variant = jnp.where(tile == 0, FIRST, jnp.where(tile == n_tiles - 1, LAST, INTERIOR))
            start = jnp.clip(tile * Q_TILE - HALF_STEPS, 0, sub_len - K_WIN) - buf_origin
            start = pl.multiple_of(start, K_TILE)
            kk = kbuf[cl, pl.ds(start, K_WIN), :]
            vv = vbuf[cl, pl.ds(start, K_WIN), :]
            for p in range(D_ATTN // LANES):
                sl = slice(p * LANES, (p + 1) * LANES)
                qp, kp, vp = q[:, sl], kk[:, sl], vv[:, sl]
                zero = jnp.zeros_like(qp)
                q2 = jnp.concatenate([jnp.where(first_half, qp, zero), jnp.where(first_half, zero, qp)], axis=0)
                s = lax.dot_general(q2, kp, (((1,), (1,)), ((), ())), preferred_element_type=F32)
                h0 = p * heads_per_slab
                s = s + bias_ref[variant, h0:h0 + heads_per_slab].reshape(heads_per_slab * Q_TILE, K_WIN)
                m = jnp.max(s, axis=-1, keepdims=True)
                e = jnp.exp2(s - m)
                l = jnp.sum(e, axis=-1, keepdims=True)
                o2 = jnp.dot(e.astype(BF16), vp, preferred_element_type=F32)
                shape = (Q_TILE, LANES)
                pick = lambda a: jnp.where(first_half, jnp.broadcast_to(a[:Q_TILE], shape),
                                           jnp.broadcast_to(a[Q_TILE:], shape))
                l_slab = pick(l)
                o_ref[0, p, out_rows, :] = pick(o2) * (1.0 / l_slab)
                lse_ref[0, p, out_rows, :] = pick(m) + jnp.log(l_slab) * LOG2E


def _attn_pattern(qc, kc, vc, bias, pat):
    bsz, groups, n_classes, sub_len, _ = qc.shape
    tiles_per_class = TILES_PER_STEP // n_classes
    q_rows = tiles_per_class * Q_TILE
    n_slab = D_ATTN // LANES
    qmap = lambda b, g, i: (b, g, 0, i, 0)
    halo_per_step = q_rows // HALF_STEPS
    n_halo = sub_len // HALF_STEPS
    before = lambda b, g, i: (b, g, 0, jnp.maximum(i * halo_per_step - 1, 0), 0)
    after = lambda b, g, i: (b, g, 0, jnp.minimum((i + 1) * halo_per_step, n_halo - 1), 0)
    own_spec = pl.BlockSpec((1, None, n_classes, q_rows, D_ATTN), qmap)
    kv_specs = [pl.BlockSpec((1, None, n_classes, HALF_STEPS, D_ATTN), before), own_spec,
                pl.BlockSpec((1, None, n_classes, HALF_STEPS, D_ATTN), after)]
    in_specs = [own_spec] + kv_specs + kv_specs
    in_specs.append(pl.BlockSpec((None, 3, N_HEADS, Q_TILE, K_WIN), lambda b, g, i: (pat, 0, 0, 0, 0)))
    window_buf = pltpu.VMEM((n_classes, q_rows + 2 * HALF_STEPS, D_ATTN), BF16)
    out_spec = pl.BlockSpec((1, None, n_slab, n_classes * q_rows, LANES), qmap)
    return pl.pallas_call(
        functools.partial(_attn_kernel, sub_len=sub_len, tiles_per_class=tiles_per_class),
        grid=(bsz, groups, sub_len // q_rows),
        in_specs=in_specs,
        out_specs=[out_spec] * 2,
        out_shape=[jax.ShapeDtypeStruct((bsz, groups, n_slab, n_classes * sub_len, LANES), F32)] * 2,
        scratch_shapes=[window_buf, window_buf],
        compiler_params=pltpu.CompilerParams(
            dimension_semantics=("arbitrary",) * 3, vmem_limit_bytes=VMEM_LIMIT),
        name=f"attn_d{groups * n_classes}",
    )(qc, kc, kc, kc, vc, vc, vc, bias)


def _token_order(ref, scr, slab):
    groups = ref.shape[1]
    if groups == 1:
        return ref[0, 0, slab]
    for g in range(groups):
        scr[slab, pl.ds(g, OUT_TILE // groups, stride=groups), :] = ref[0, g, slab]
    return scr[slab]


def _out_mlp_kernel(x_ref, mr_ref, o1, o2, o3, l1, l2, l3, ga_ref, wo_ref, gm_ref, wu_ref, wd_ref,
                    gf_ref, out_ref, s_o3, s_l3):
    ys = []
    for slab in range(D_ATTN // LANES):
        a1 = _token_order(l1, None, slab)
        a2 = _token_order(l2, None, slab)
        a3 = _token_order(l3, s_l3, slab)
        mx = jnp.maximum(jnp.maximum(a1, a2), a3)
        e1, e2, e3 = jnp.exp2(a1 - mx), jnp.exp2(a2 - mx), jnp.exp2(a3 - mx)
        ys.append((e1 * _token_order(o1, None, slab) + e2 * _token_order(o2, None, slab)
                   + e3 * _token_order(o3, s_o3, slab)) / (e1 + e2 + e3))
    mix_attn = _rms(jnp.concatenate(ys, axis=-1), ga_ref[...]).astype(BF16)

    proj = jnp.dot(mr_ref[0], wo_ref[0:D_RNN, :], preferred_element_type=F32)
    proj = proj + jnp.dot(mix_attn, wo_ref[D_RNN:D_RNN + D_ATTN, :], preferred_element_type=F32)
    x1 = proj + x_ref[0]
    h = _rms(x1, gm_ref[...]).astype(BF16)
    ff = None
    for c in range(D_FF // FF_CHUNK):
        cols = slice(c * FF_CHUNK, (c + 1) * FF_CHUNK)
        z = jnp.dot(h, wu_ref[:, cols], preferred_element_type=F32)
        z = jnp.square(jnp.maximum(z, 0.0)).astype(BF16)
        d = jnp.dot(z, wd_ref[cols, :], preferred_element_type=F32)
        ff = d if ff is None else ff + d
    out_ref[0] = _rms(ff + x1, gf_ref[...])


def _out_mlp(x3, mix_rnn, os_, lses, ga, wo, gm, wu, wd, gf):
    bsz, s, _ = x3.shape
    tm = OUT_TILE
    row = lambda b, i: (b, i, 0)
    const = lambda b, i: (0, 0)
    once = pl.Buffered(1)
    n_slab = D_ATTN // LANES
    attn_specs = [pl.BlockSpec((1, o.shape[1], n_slab, tm // o.shape[1], LANES), lambda b, i: (b, 0, 0, i, 0))
                  for o in os_]
    in_specs = [pl.BlockSpec((1, tm, D_MODEL), row), pl.BlockSpec((1, tm, D_RNN), row)]
    in_specs += attn_specs + attn_specs
    in_specs += [
        pl.BlockSpec((1, D_ATTN), const),
        pl.BlockSpec((D_RNN + D_ATTN, D_MODEL), const, pipeline_mode=once),
        pl.BlockSpec((1, D_MODEL), const),
        pl.BlockSpec((D_MODEL, D_FF), const, pipeline_mode=once),
        pl.BlockSpec((D_FF, D_MODEL), const, pipeline_mode=once),
        pl.BlockSpec((1, D_MODEL), const),
    ]
    return pl.pallas_call(
        _out_mlp_kernel,
        grid=(bsz, s // tm),
        in_specs=in_specs,
        out_specs=pl.BlockSpec((1, tm, D_MODEL), row),
        out_shape=jax.ShapeDtypeStruct((bsz, s, D_MODEL), F32),
        scratch_shapes=[pltpu.VMEM((n_slab, tm, LANES), F32)] * 2,
        compiler_params=pltpu.CompilerParams(
            dimension_semantics=("arbitrary", "arbitrary"), vmem_limit_bytes=BIG_VMEM_LIMIT),
        name="out_mlp",
    )(x3, mix_rnn, *os_, *lses, ga, wo, gm, wu, wd, gf)


def kernel(x, attn_norm_g, w_in, conv_w, conv_b, lru_wa_fwd, lru_ba_fwd, lru_wx_fwd, lru_bx_fwd, lru_lam_fwd, lru_wa_bwd, lru_ba_bwd, lru_wx_bwd, lru_bx_bwd, lru_lam_bwd, rel_bias, norm_rnn_g, norm_attn_g, w_out, mlp_norm_g, w_up, w_down, final_norm_g):
    depth = w_in.shape[0]
    assert depth == 1, "the final RMSNorm is fused into the single layer's last call"
    l = 0
    n_pat = len(PATTERNS)
    row = lambda v: v.reshape(1, -1)
    bias = _bias_tables(rel_bias)
    proj = _in_proj(x, row(attn_norm_g[l]), w_in[l], (w_out[l], w_up[l], w_down[l]), tm=IN_TILE)
    xr, gate = proj[0], proj[1]
    qs, ks, vs = (proj[2 + a * n_pat:2 + (a + 1) * n_pat] for a in range(3))
    w_out_bf, w_up_bf, w_down_bf = proj[2 + 3 * n_pat:]
    half = lambda v: row(0.5 * v)
    fwd = (conv_w[l], row(conv_b[l]), _gate_weights(lru_wa_fwd[l]), _gate_weights(lru_wx_fwd[l]),
           half(lru_ba_fwd[l]), half(lru_bx_fwd[l]), row(lru_lam_fwd[l]), None)
    bwd = (conv_w[l], row(conv_b[l]), _gate_weights(lru_wa_bwd[l]), _gate_weights(lru_wx_bwd[l]),
           half(lru_ba_bwd[l]), half(lru_bx_bwd[l]), row(lru_lam_bwd[l]), row(norm_rnn_g[l]))
    h_f, xc = _rnn_pass(xr, (), fwd, reverse=False, final=False)
    mix_rnn = _rnn_pass(xc, (h_f, gate), bwd, reverse=True, final=True)
    os_, lses = [], []
    for pat in range(n_pat):
        o, lse = _attn_pattern(qs[pat], ks[pat], vs[pat], bias, pat)
        os_.append(o)
        lses.append(lse)
    return _out_mlp(x, mix_rnn, os_, lses, row(norm_attn_g[l]), w_out_bf,
                    row(mlp_norm_g[l]), w_up_bf, w_down_bf, row(final_norm_g))
```
